```python
import jax
import jax.numpy as jnp
from jax import lax
import numpy as np

D_MODEL = 1024
BATCH = 4
SEQ = 8192
DEPTH = 2

GRID_W = 64
CTX_LEN = 256
MIX_WIDTH = D_MODEL
M_HEADS = 4
M_DV = MIX_WIDTH // (2 * M_HEADS)
M_DK = M_DV // 2
M_CHUNK = 128
A_HEADS = 8
A_KV_HEADS = 2
A_GROUP = A_HEADS // A_KV_HEADS
A_DH = (MIX_WIDTH - M_HEADS * M_DV) // A_HEADS
Q_BLOCK = 128
ROPE_THETA = 10000.0
D_FF = ((8 * D_MODEL // 3 + 127) // 128) * 128
N_EXPERTS = 8
TOP_K = 2
D_FF_EXPERT = 7 * D_MODEL // 2
EPS = 1e-6
SPLIT_SIZES = (M_HEADS * M_DK, M_HEADS * M_DK, M_HEADS * M_DV, M_HEADS * M_DV, 4 * M_HEADS,
               A_HEADS * A_DH, A_KV_HEADS * A_DH, A_KV_HEADS * A_DH)
D_IN = sum(SPLIT_SIZES)

kernel_name = 'hybrid_mlstm_gqa_moe_dit_prefix'


def _rms(x, w):
    xf = x.astype(jnp.float32)
    y = xf * lax.rsqrt(jnp.mean(xf * xf, axis=-1, keepdims=True) + EPS)
    return (y * w.astype(jnp.float32)).astype(x.dtype)


def _modulate(h, shift, scale):
    return h * (1.0 + scale) + shift


def _split(p):
    out, o = [], 0
    for s in SPLIT_SIZES:
        out.append(p[..., o:o + s])
        o += s
    return out


def _rope_tables(n_tok):
    rows = n_tok // GRID_W
    row = jnp.broadcast_to(jnp.arange(rows, dtype=jnp.float32)[:, None], (rows, GRID_W)).reshape(n_tok)
    col = jnp.broadcast_to(jnp.arange(GRID_W, dtype=jnp.float32)[None, :], (rows, GRID_W)).reshape(n_tok)
    n_freq = A_DH // 4
    inv_freq = ROPE_THETA ** (-jnp.arange(n_freq, dtype=jnp.float32) / n_freq)
    ang = jnp.concatenate([row[:, None] * inv_freq, col[:, None] * inv_freq], axis=-1)
    return jnp.cos(ang), jnp.sin(ang)


def _rope(x, cos, sin):
    xf = x.astype(jnp.float32).reshape(x.shape[:-1] + (A_DH // 2, 2))
    xe, xo = xf[..., 0], xf[..., 1]
    c, s = cos[None, :, None, :], sin[None, :, None, :]
    out = jnp.stack([xe * c - xo * s, xe * s + xo * c], axis=-1)
    return out.reshape(x.shape).astype(x.dtype)


def _mlstm_prep(mq, mk, mv, mg, gate_b):
    B, T, _ = mq.shape
    heads = lambda t: t.astype(jnp.float32).reshape(B, T, M_HEADS, -1).transpose(0, 2, 1, 3)
    q = heads(mq) * (M_DK ** -0.5)
    g = (mg.astype(jnp.float32) + gate_b.astype(jnp.float32)).reshape(B, T, 4, M_HEADS).transpose(2, 0, 3, 1)
    return (q, heads(mk), heads(mv), g[0], jax.nn.log_sigmoid(g[1]), g[2], jax.nn.log_sigmoid(g[3]))


def _mlstm_states(k, v, ig, lf, state0):
    B, H, T, dk = k.shape
    dv = v.shape[-1]
    nc = T // M_CHUNK
    kc = k.reshape(B, H, nc, M_CHUNK, dk)
    vc = v.reshape(B, H, nc, M_CHUNK, dv)
    igc = ig.reshape(B, H, nc, M_CHUNK)
    b = jnp.cumsum(lf.reshape(B, H, nc, M_CHUNK), axis=-1)
    b_end = b[..., -1]
    w_log = b_end[..., None] - b + igc
    m_loc = jnp.max(w_log, axis=-1)
    w = jnp.exp(w_log - m_loc[..., None])
    c_loc = jnp.einsum('bhcsd,bhcsk->bhcdk', vc * w[..., None], kc)
    n_loc = jnp.einsum('bhcs,bhcsk->bhck', w, kc)

    def step(carry, inp):
        c_prev, n_prev, m_prev = carry
        c_l, n_l, m_l, g = inp
        m_new = jnp.maximum(g + m_prev, m_l)
        a = jnp.exp(g + m_prev - m_new)
        s = jnp.exp(m_l - m_new)
        c_new = a[..., None, None] * c_prev + s[..., None, None] * c_l
        n_new = a[..., None] * n_prev + s[..., None] * n_l
        return (c_new, n_new, m_new), (c_prev, n_prev, m_prev)

    xs = (jnp.moveaxis(c_loc, 2, 0), jnp.moveaxis(n_loc, 2, 0), jnp.moveaxis(m_loc, 2, 0), jnp.moveaxis(b_end, 2, 0))
    final, enter = lax.scan(step, state0, xs)
    enter = (jnp.moveaxis(enter[0], 0, 2), jnp.moveaxis(enter[1], 0, 2), jnp.moveaxis(enter[2], 0, 2))
    return enter, final


def _mlstm_outputs(q, k, v, ig, lf, enter):
    c_e, n_e, m_e = enter
    B, H, T, dk = q.shape
    dv = v.shape[-1]
    nc = T // M_CHUNK
    qc = q.reshape(B, H, nc, M_CHUNK, dk)
    kc = k.reshape(B, H, nc, M_CHUNK, dk)
    vc = v.reshape(B, H, nc, M_CHUNK, dv)
    igc = ig.reshape(B, H, nc, M_CHUNK)
    b = jnp.cumsum(lf.reshape(B, H, nc, M_CHUNK), axis=-1)
    lower = jnp.tril(jnp.ones((M_CHUNK, M_CHUNK), dtype=bool))
    d_log = jnp.where(lower, b[..., :, None] - b[..., None, :] + igc[..., None, :], -jnp.inf)
    inter = b + m_e[..., None]
    m_t = jnp.maximum(inter, jnp.max(d_log, axis=-1))
    a = jnp.exp(inter - m_t)
    s = jnp.einsum('bhctk,bhcsk->bhcts', qc, kc) * jnp.exp(d_log - m_t[..., None])
    num = jnp.einsum('bhcts,bhcsd->bhctd', s, vc) + a[..., None] * jnp.einsum('bhcdk,bhctk->bhctd', c_e, qc)
    den = jnp.sum(s, axis=-1) + a * jnp.einsum('bhck,bhctk->bhct', n_e, qc)
    h = num / jnp.maximum(jnp.abs(den), jnp.exp(-m_t))[..., None]
    return h.reshape(B, H, T, dv)


def _mlstm_direction(ctx_feats, lat_feats, need_ctx):
    q_c, k_c, v_c, i_c, f_c = ctx_feats
    q_l, k_l, v_l, i_l, f_l = lat_feats
    B, H, _, dk = k_c.shape
    dv = v_c.shape[-1]
    zero = (jnp.zeros((B, H, dv, dk), jnp.float32), jnp.zeros((B, H, dk), jnp.float32), jnp.zeros((B, H), jnp.float32))
    ctx_enter, ctx_final = _mlstm_states(k_c, v_c, i_c, f_c, zero)
    lat_enter, _ = _mlstm_states(k_l, v_l, i_l, f_l, ctx_final)
    h_l = _mlstm_outputs(q_l, k_l, v_l, i_l, f_l, lat_enter)
    h_c = _mlstm_outputs(q_c, k_c, v_c, i_c, f_c, ctx_enter) if need_ctx else None
    return h_l, h_c


def _mlstm_merge(h, mo, m_norm_w):
    B, H, T, dv = h.shape
    hn = _rms(h, m_norm_w.reshape(H, 1, dv))
    return hn.transpose(0, 2, 1, 3).reshape(B, T, H * dv) * jax.nn.sigmoid(mo.astype(jnp.float32))


def _fwd(f):
    return (f[0], f[1], f[2], f[3], f[4])


def _bwd(f):
    return tuple(jnp.flip(t, axis=2) for t in (f[0], f[1], f[2], f[5], f[6]))


def _attn_q(aq, q_norm_w, rope):
    B, T, _ = aq.shape
    q = _rms(aq.reshape(B, T, A_HEADS, A_DH), q_norm_w)
    if rope is not None:
        q = _rope(q, *rope)
    q = q * (A_DH ** -0.5)
    return q.reshape(B, T, A_KV_HEADS, A_GROUP, A_DH).transpose(0, 2, 3, 1, 4)


def _attn_kv(ak, av, k_norm_w, rope):
    B, T, _ = ak.shape
    k = _rms(ak.reshape(B, T, A_KV_HEADS, A_DH), k_norm_w)
    if rope is not None:
        k = _rope(k, *rope)
    v = av.reshape(B, T, A_KV_HEADS, A_DH)
    return k.transpose(0, 2, 1, 3), v.transpose(0, 2, 1, 3)


def _attend(q, k, v):
    s = jnp.einsum('bkgqd,bksd->bkgqs', q, k, preferred_element_type=jnp.float32)
    p = jax.nn.softmax(s, axis=-1)
    return jnp.einsum('bkgqs,bksd->bkgqd', p.astype(v.dtype), v)


def _merge_heads(o):
    B, K, G, T, d = o.shape
    return o.transpose(0, 3, 1, 2, 4).reshape(B, T, K * G * d)


def _latent_attention(q, k_all, v_all):
    B, K, G, T, d = q.shape
    nb = T // Q_BLOCK
    qb = jnp.moveaxis(q.reshape(B, K, G, nb, Q_BLOCK, d), 3, 0)
    ob = lax.map(lambda blk: _attend(blk, k_all, v_all), qb)
    return _merge_heads(jnp.moveaxis(ob, 0, 3).reshape(B, K, G, T, d))


def _mixer(px, pc, rope, gate_b, m_norm_w, q_norm_w, k_norm_w, need_ctx):
    mq_x, mk_x, mv_x, mo_x, mg_x, aq_x, ak_x, av_x = _split(px)
    mq_c, mk_c, mv_c, mo_c, mg_c, aq_c, ak_c, av_c = _split(pc)
    lat = _mlstm_prep(mq_x, mk_x, mv_x, mg_x, gate_b)
    ctf = _mlstm_prep(mq_c, mk_c, mv_c, mg_c, gate_b)
    hf_x, hf_c = _mlstm_direction(_fwd(ctf), _fwd(lat), need_ctx)
    hb_x, hb_c = _mlstm_direction(_bwd(ctf), _bwd(lat), need_ctx)
    m_x = _mlstm_merge(hf_x + jnp.flip(hb_x, axis=2), mo_x, m_norm_w)
    k_x, v_x = _attn_kv(ak_x, av_x, k_norm_w, rope)
    k_c, v_c = _attn_kv(ak_c, av_c, k_norm_w, None)
    k_all = jnp.concatenate([k_c, k_x], axis=2)
    v_all = jnp.concatenate([v_c, v_x], axis=2)
    a_x = _latent_attention(_attn_q(aq_x, q_norm_w, rope), k_all, v_all)
    y_x = jnp.concatenate([m_x.astype(px.dtype), a_x.astype(px.dtype)], axis=-1)
    if not need_ctx:
        return y_x, None
    m_c = _mlstm_merge(hf_c + jnp.flip(hb_c, axis=2), mo_c, m_norm_w)
    a_c = _merge_heads(_attend(_attn_q(aq_c, q_norm_w, None), k_c, v_c))
    y_c = jnp.concatenate([m_c.astype(pc.dtype), a_c.astype(pc.dtype)], axis=-1)
    return y_x, y_c


def _swiglu(h, wg, wu, wd):
    return (jax.nn.silu(h @ wg) * (h @ wu)) @ wd


def _moe(h, router, wg, wu, wd):
    logits = (h @ router).astype(jnp.float32)
    top_v, top_i = lax.top_k(logits, TOP_K)
    gates = jax.nn.softmax(top_v, axis=-1)
    y = jnp.zeros_like(h)
    for e in range(N_EXPERTS):
        w_e = jnp.sum(jnp.where(top_i == e, gates, 0.0), axis=-1).astype(h.dtype)
        y = y + w_e[..., None] * _swiglu(h, wg[e], wu[e], wd[e])
    return y


def setup_inputs(seed: int = 0) -> dict:
    key = jax.random.key(seed)
    ks = jax.random.split(key, 26)
    D = D_MODEL
    n_dense = (DEPTH + 1) // 2
    n_moe = DEPTH // 2

    def nrm(k, shape, scale):
        return jax.random.normal(k, shape, jnp.float32) * scale

    f_bias = jnp.linspace(3.0, 6.0, M_HEADS, dtype=jnp.float32)
    mlstm_gate_b = jnp.concatenate([
        nrm(ks[8], (DEPTH, M_HEADS), 0.1),
        f_bias + nrm(ks[9], (DEPTH, M_HEADS), 0.1),
        nrm(ks[10], (DEPTH, M_HEADS), 0.1),
        f_bias + nrm(ks[11], (DEPTH, M_HEADS), 0.1)], axis=-1)
    return {
        'x': nrm(ks[0], (BATCH, SEQ, D), 1.0),
        'c': nrm(ks[1], (BATCH, D), 1.0),
        'ctx': nrm(ks[2], (BATCH, CTX_LEN, D), 1.0),
        'c_ctx': nrm(ks[3], (D,), 1.0),
        'ada_w': nrm(ks[4], (DEPTH, D, 6 * D), 0.5 * D ** -0.5),
        'ada_b': nrm(ks[5], (DEPTH, 6 * D), 0.02),
        'norm1_w': 1.0 + nrm(ks[6], (DEPTH, D), 0.02),
        'norm2_w': 1.0 + nrm(ks[7], (DEPTH, D), 0.02),
        'w_in': nrm(ks[12], (DEPTH, D, D_IN), D ** -0.5),
        'mlstm_gate_b': mlstm_gate_b,
        'mlstm_norm_w': 1.0 + nrm(ks[13], (DEPTH, M_HEADS * M_DV), 0.02),
        'q_norm_w': 1.0 + nrm(ks[14], (DEPTH, A_DH), 0.02),
        'k_norm_w': 1.0 + nrm(ks[15], (DEPTH, A_DH), 0.02),
        'w_out': nrm(ks[16], (DEPTH, MIX_WIDTH, D), MIX_WIDTH ** -0.5),
        'ffn_w_gate': nrm(ks[17], (n_dense, D, D_FF), D ** -0.5),
        'ffn_w_up': nrm(ks[18], (n_dense, D, D_FF), D ** -0.5),
        'ffn_w_down': nrm(ks[19], (n_dense, D_FF, D), D_FF ** -0.5),
        'moe_router': nrm(ks[20], (n_moe, D, N_EXPERTS), D ** -0.5),
        'moe_w_gate': nrm(ks[21], (n_moe, N_EXPERTS, D, D_FF_EXPERT), D ** -0.5),
        'moe_w_up': nrm(ks[22], (n_moe, N_EXPERTS, D, D_FF_EXPERT), D ** -0.5),
        'moe_w_down': nrm(ks[23], (n_moe, N_EXPERTS, D_FF_EXPERT, D), D_FF_EXPERT ** -0.5),
        'final_norm_w': 1.0 + nrm(ks[24], (D,), 0.02),
    }


def reference(x, c, ctx, c_ctx, ada_w, ada_b, norm1_w, norm2_w, w_in, mlstm_gate_b, mlstm_norm_w,
              q_norm_w, k_norm_w, w_out, ffn_w_gate, ffn_w_up, ffn_w_down, moe_router, moe_w_gate,
              moe_w_up, moe_w_down, final_norm_w):
    rope = _rope_tables(x.shape[1])
    silu_c = jax.nn.silu(c)
    silu_cc = jax.nn.silu(c_ctx)
    for i in range(DEPTH):
        need_ctx = i < DEPTH - 1
        mod_x = [m[:, None, :] for m in jnp.split(silu_c @ ada_w[i] + ada_b[i], 6, axis=-1)]
        mod_c = jnp.split(silu_cc @ ada_w[i] + ada_b[i], 6, axis=-1)
        hx = _modulate(_rms(x, norm1_w[i]), mod_x[0], mod_x[1])
        hc = _modulate(_rms(ctx, norm1_w[i]), mod_c[0], mod_c[1])
        y_x, y_c = _mixer(hx @ w_in[i], hc @ w_in[i], rope, mlstm_gate_b[i], mlstm_norm_w[i],
                          q_norm_w[i], k_norm_w[i], need_ctx)
        x = x + mod_x[2] * (y_x @ w_out[i])
        if i % 2 == 0:
            ffn = lambda h, j=i // 2: _swiglu(h, ffn_w_gate[j], ffn_w_up[j], ffn_w_down[j])
        else:
            ffn = lambda h, j=i // 2: _moe(h, moe_router[j], moe_w_gate[j], moe_w_up[j], moe_w_down[j])
        x = x + mod_x[5] * ffn(_modulate(_rms(x, norm2_w[i]), mod_x[3], mod_x[4]))
        if need_ctx:
            ctx = ctx + mod_c[2] * (y_c @ w_out[i])
            ctx = ctx + mod_c[5] * ffn(_modulate(_rms(ctx, norm2_w[i]), mod_c[3], mod_c[4]))
    return _rms(x, final_norm_w)
```

```python
import functools
import math

import numpy as np
import jax
import jax.numpy as jnp
from jax import lax
from jax.experimental import pallas as pl
from jax.experimental.pallas import tpu as pltpu

F32 = jnp.float32
BF16 = jnp.bfloat16
HIGHEST = lax.Precision.HIGHEST

D_MODEL = 1024
GRID_W = 64
M_HEADS = 4
M_DV = 128
M_DK = 64
M_CHUNK = 128
A_HEADS = 8
A_KV_HEADS = 2
A_GROUP = A_HEADS // A_KV_HEADS
A_DH = 64
ROPE_THETA = 10000.0
N_EXPERTS = 8
EPS = 1e-6

LANES = 128
ROW_TILE = 256
VMEM_LIMIT = 56 * 1024 * 1024

P_QK = 0
P_MV = 512
P_MO = 1024
P_AQ = 1536
P_AKV = 2048
P_WIDTH = 2304
G_WIDTH = LANES


def _params(sem, vmem=VMEM_LIMIT):
    return pltpu.CompilerParams(dimension_semantics=sem, vmem_limit_bytes=vmem)


def _sigmoid(x):
    return 1.0 / (1.0 + jnp.exp(-x))


def _rms_rows(x):
    return x * lax.rsqrt(jnp.mean(x * x, axis=-1, keepdims=True) + EPS)


def _ada_kernel(c_ref, w_ref, b_ref, o_ref):
    c = c_ref[...]
    s = c * _sigmoid(c)
    o_ref[...] = jnp.dot(s, w_ref[...], precision=HIGHEST, preferred_element_type=F32) + b_ref[...]


def _ada(cvec, w, b):
    n = w.shape[1]
    bn = 1536
    return pl.pallas_call(
        _ada_kernel,
        out_shape=jax.ShapeDtypeStruct((cvec.shape[0], n), F32),
        grid=(n // bn,),
        in_specs=[pl.BlockSpec(cvec.shape, lambda j: (0, 0)),
                  pl.BlockSpec((w.shape[0], bn), lambda j: (0, j)),
                  pl.BlockSpec((1, bn), lambda j: (0, j))],
        out_specs=pl.BlockSpec((cvec.shape[0], bn), lambda j: (0, j)),
        compiler_params=_params(("arbitrary",)),
        name="ada_mod",
    )(cvec, w, b.reshape(1, n))


def _mod_spec(off=0):
    return pl.BlockSpec((1, 1, 6, D_MODEL), lambda b, t: (b, jnp.minimum(t + off, 1), 0, 0))


def _in_proj_kernel(x_ref, mod_ref, nw_ref, w_ref, p_ref, g_ref):
    mod = mod_ref[0, 0]
    h = _rms_rows(x_ref[0]) * nw_ref[...] * (1.0 + mod[1:2]) + mod[0:1]
    r = jnp.dot(h.astype(BF16), w_ref[...], preferred_element_type=F32)
    p_ref[0] = r[:, :P_WIDTH].astype(BF16)
    g_ref[0] = r[:, P_WIDTH:]


def _in_proj(xa, mods, nw, wp):
    B, L, D = xa.shape
    row = lambda w: pl.BlockSpec((1, ROW_TILE, w), lambda b, t: (b, t, 0))
    return pl.pallas_call(
        _in_proj_kernel,
        out_shape=(jax.ShapeDtypeStruct((B, L, P_WIDTH), BF16),
                   jax.ShapeDtypeStruct((B, L, G_WIDTH), F32)),
        grid=(B, L // ROW_TILE),
        in_specs=[row(D), _mod_spec(),
                  pl.BlockSpec((1, D), lambda b, t: (0, 0)),
                  pl.BlockSpec(wp.shape, lambda b, t: (0, 0))],
        out_specs=(row(P_WIDTH), row(G_WIDTH)),
        compiler_params=_params(("parallel", "arbitrary")),
        name="in_proj",
    )(xa, mods, nw.reshape(1, D), wp)


def _mlstm_kernel(qkf_ref, vf_ref, gf_ref, qkb_ref, vb_ref, gb_ref, bias_ref,
                  hf_ref, hb_ref, ct_ref, n_ref, m_ref):
    @pl.when(pl.program_id(1) == 0)
    def _():
        ct_ref[...] = jnp.zeros_like(ct_ref)
        n_ref[...] = jnp.zeros_like(n_ref)
        m_ref[...] = jnp.zeros_like(m_ref)

    row = lax.broadcasted_iota(jnp.int32, (M_CHUNK, M_CHUNK), 0)
    col = lax.broadcasted_iota(jnp.int32, (M_CHUNK, M_CHUNK), 1)
    dirs = ((qkf_ref, vf_ref, gf_ref, hf_ref, col <= row),
            (qkb_ref, vb_ref, gb_ref, hb_ref, col >= row))
    for d, (qk_ref, v_ref, g_ref, h_ref, allowed) in enumerate(dirs):
        g = g_ref[0] + bias_ref[...]
        lf = jnp.minimum(g, 0.0) - jnp.log1p(jnp.exp(-jnp.abs(g)))
        b_col = jnp.dot(allowed.astype(F32), lf, precision=HIGHEST, preferred_element_type=F32)
        b_row = b_col.T
        g_row = g.T
        b_tot = jnp.sum(lf, axis=0, keepdims=True)
        for h in range(M_HEADS):
            idx = d * M_HEADS + h
            ic, fc = 8 * d + h, 8 * d + 4 + h
            b_c, b_r = b_col[:, fc:fc + 1], b_row[fc:fc + 1, :]
            i_c, i_r = g[:, ic:ic + 1], g_row[ic:ic + 1, :]
            b_end = b_tot[:, fc:fc + 1]
            m_prev = m_ref[idx][:, 0:1]
            ct = ct_ref[idx]
            n = n_ref[idx]
            qk = qk_ref[0, :, h * LANES:(h + 1) * LANES]
            q, k = qk[:, :M_DK], qk[:, M_DK:]
            v = v_ref[0, :, h * M_DV:(h + 1) * M_DV]

            d_log = jnp.where(allowed, b_c - b_r + i_r, -jnp.inf)
            inter = b_c + m_prev
            m_t = jnp.maximum(inter, jnp.max(d_log, axis=-1, keepdims=True))
            a = jnp.exp(inter - m_t)
            s = lax.dot_general(q, k, (((1,), (1,)), ((), ())), preferred_element_type=F32)
            s = s * jnp.exp(d_log - m_t)
            num = (jnp.dot(s.astype(BF16), v, preferred_element_type=F32)
                   + a * jnp.dot(q, ct.astype(BF16), preferred_element_type=F32))
            qn = jnp.sum(q.astype(F32) * n, axis=-1, keepdims=True)
            den = jnp.sum(s, axis=-1, keepdims=True) + a * qn
            h_ref[0, :, h * M_DV:(h + 1) * M_DV] = num / jnp.maximum(jnp.abs(den), jnp.exp(-m_t))

            m_loc = jnp.max(b_end - b_r + i_r, axis=-1, keepdims=True)
            w_col = jnp.exp(b_end - b_c + i_c - m_loc)
            vw = (v.astype(F32) * w_col).astype(BF16)
            c_loc = lax.dot_general(k, vw, (((0,), (0,)), ((), ())), preferred_element_type=F32)
            n_loc = jnp.sum(k.astype(F32) * w_col, axis=0, keepdims=True)
            m_new = jnp.maximum(b_end + m_prev, m_loc)
            a_s = jnp.exp(b_end + m_prev - m_new)
            s_s = jnp.exp(m_loc - m_new)
            ct_ref[idx] = a_s * ct + s_s * c_loc
            n_ref[idx] = a_s * n + s_s * n_loc
            m_ref[idx] = jnp.broadcast_to(m_new, (1, LANES))


def _mlstm(p, g, gate_b, ctx_chunks):
    B, L, _ = p.shape
    nc = L // M_CHUNK
    fwd = lambda j: j
    bwd = lambda j: jnp.where(j < ctx_chunks, ctx_chunks - 1 - j, nc - 1 + ctx_chunks - j)
    width = M_HEADS * M_DV
    blk = lambda cm, colblk, w=width: pl.BlockSpec((1, M_CHUNK, w), lambda b, j: (b, cm(j), colblk))
    bias = jnp.zeros((1, G_WIDTH), F32).at[0, :16].set(gate_b)
    return pl.pallas_call(
        _mlstm_kernel,
        out_shape=(jax.ShapeDtypeStruct((B, L, width), F32),) * 2,
        grid=(B, nc),
        in_specs=[blk(fwd, P_QK // width), blk(fwd, P_MV // width), blk(fwd, 0, G_WIDTH),
                  blk(bwd, P_QK // width), blk(bwd, P_MV // width), blk(bwd, 0, G_WIDTH),
                  pl.BlockSpec((1, G_WIDTH), lambda b, j: (0, 0))],
        out_specs=(blk(fwd, 0), blk(bwd, 0)),
        scratch_shapes=[pltpu.VMEM((2 * M_HEADS, M_DK, M_DV), F32),
                        pltpu.VMEM((2 * M_HEADS, 1, M_DK), F32),
                        pltpu.VMEM((2 * M_HEADS, 1, LANES), F32)],
        compiler_params=_params(("parallel", "arbitrary")),
        name="mlstm",
    )(p, p, g, p, p, g, bias)


def _head_norm_rope(x, w, cos, sin, bd):
    ms = jnp.dot(x * x, bd, precision=HIGHEST, preferred_element_type=F32)
    y = x * lax.rsqrt(ms + EPS) * w
    lane = lax.broadcasted_iota(jnp.int32, y.shape, 1)
    partner = jnp.where(lane % A_DH < A_DH // 2,
                        pltpu.roll(y, LANES - A_DH // 2, axis=1), pltpu.roll(y, A_DH // 2, axis=1))
    return y * cos + partner * sin


def _attn_prep_kernel(q_ref, kv_ref, cos_ref, sin_ref, qw_ref, kw_ref, qt_ref, k_ref, vt_ref, *, q_scale):
    r = lax.broadcasted_iota(jnp.int32, (LANES, LANES), 0) // A_DH
    c = lax.broadcasted_iota(jnp.int32, (LANES, LANES), 1) // A_DH
    bd = jnp.where(r == c, 1.0 / A_DH, 0.0).astype(F32)
    cos, sin = cos_ref[...], sin_ref[...]
    for pair in range(A_HEADS // 2):
        x = q_ref[0, :, pair * LANES:(pair + 1) * LANES].astype(F32)
        y = _head_norm_rope(x, qw_ref[...], cos, sin, bd) * q_scale
        qt_ref[0, pair * LANES:(pair + 1) * LANES, :] = y.T.astype(BF16)
    kv = kv_ref[0].astype(F32)
    k = _head_norm_rope(kv[:, :LANES], kw_ref[...], cos, sin, bd).astype(BF16)
    for kvh in range(A_KV_HEADS):
        k_ref[0, kvh, 0] = k[:, kvh * A_DH:(kvh + 1) * A_DH]
    vt = kv[:, LANES:].T.astype(BF16)
    for kvh in range(A_KV_HEADS):
        vt_ref[0, kvh, 0] = vt[kvh * A_DH:(kvh + 1) * A_DH, :]


def _attn_prep(p, cos, sin, qw, kw, ts):
    B, L, _ = p.shape
    per = ts // ROW_TILE
    nblk = L // ts
    q_scale = A_DH ** -0.5 * math.log2(math.e)
    return pl.pallas_call(
        functools.partial(_attn_prep_kernel, q_scale=q_scale),
        out_shape=(jax.ShapeDtypeStruct((B, A_HEADS * A_DH, L), BF16),
                   jax.ShapeDtypeStruct((B, A_KV_HEADS, nblk, ts, A_DH), BF16),
                   jax.ShapeDtypeStruct((B, A_KV_HEADS, nblk, A_DH, ts), BF16)),
        grid=(B, L // ROW_TILE),
        in_specs=[pl.BlockSpec((1, ROW_TILE, A_HEADS * A_DH), lambda b, t: (b, t, P_AQ // (A_HEADS * A_DH))),
                  pl.BlockSpec((1, ROW_TILE, 2 * LANES), lambda b, t: (b, t, P_AKV // (2 * LANES))),
                  pl.BlockSpec((ROW_TILE, LANES), lambda b, t: (t, 0)),
                  pl.BlockSpec((ROW_TILE, LANES), lambda b, t: (t, 0)),
                  pl.BlockSpec((1, LANES), lambda b, t: (0, 0)),
                  pl.BlockSpec((1, LANES), lambda b, t: (0, 0))],
        out_specs=(pl.BlockSpec((1, A_HEADS * A_DH, ROW_TILE), lambda b, t: (b, 0, t)),
                   pl.BlockSpec((1, A_KV_HEADS, 1, ROW_TILE, A_DH), lambda b, t: (b, 0, t // per, t % per, 0)),
                   pl.BlockSpec((1, A_KV_HEADS, 1, A_DH, ROW_TILE), lambda b, t: (b, 0, t // per, 0, t % per))),
        compiler_params=_params(("parallel", "arbitrary")),
        name="attn_prep",
    )(p, p, cos, sin, qw, kw)


def _attn_kernel(*refs, nblk, tq, aliased):
    qt_ref, k_ref, vt_ref = refs[:3]
    o_ref = refs[-1]
    qt = qt_ref[0]
    q = jnp.concatenate([qt[g * A_DH:(g + 1) * A_DH, :] for g in range(A_GROUP)], axis=1)
    n = A_GROUP * tq

    def body(i, carry):
        m, l, acc = carry
        s = jnp.dot(k_ref[0, 0, i], q, preferred_element_type=F32)
        m_new = jnp.maximum(m, jnp.max(s, axis=0, keepdims=True))
        alpha = jnp.exp2(m - m_new)
        p = jnp.exp2(s - m_new)
        l = alpha * l + jnp.sum(p, axis=0, keepdims=True)
        acc = alpha * acc + jnp.dot(vt_ref[0, 0, i], p.astype(BF16), preferred_element_type=F32)
        return m_new, l, acc

    init = (jnp.full((1, n), -jnp.inf, F32), jnp.zeros((1, n), F32), jnp.zeros((A_DH, n), F32))
    _, l, acc = lax.fori_loop(0, nblk, body, init)
    o = acc / l
    o = jnp.concatenate([o[:, g * tq:(g + 1) * tq] for g in range(A_GROUP)], axis=0)
    o_ref[0] = o.T.astype(BF16)


def _attention(qt, k, vt, *, q_tile0, n_qt, tq, nblk, ts, out_tile0, out_rows, out=None):
    B = qt.shape[0]
    width = A_GROUP * A_DH
    in_specs = [pl.BlockSpec((1, width, tq), lambda b, kv, t: (b, kv, t + q_tile0)),
                pl.BlockSpec((1, 1, nblk, ts, A_DH), lambda b, kv, t: (b, kv, 0, 0, 0)),
                pl.BlockSpec((1, 1, nblk, A_DH, ts), lambda b, kv, t: (b, kv, 0, 0, 0))]
    args = [qt, k, vt]
    aliases = {}
    if out is not None:
        in_specs.append(pl.BlockSpec(memory_space=pl.ANY))
        args.append(out)
        aliases = {3: 0}
    return pl.pallas_call(
        functools.partial(_attn_kernel, nblk=nblk, tq=tq, aliased=out is not None),
        out_shape=jax.ShapeDtypeStruct((B, out_rows, A_HEADS * A_DH), BF16),
        grid=(B, A_KV_HEADS, n_qt),
        in_specs=in_specs,
        out_specs=pl.BlockSpec((1, tq, width), lambda b, kv, t: (b, t + out_tile0, kv)),
        input_output_aliases=aliases,
        compiler_params=_params(("parallel", "parallel", "arbitrary")),
        name="attention",
    )(*args)


def _mixer_out_kernel(*refs, with_router):
    (hf_ref, hb_ref, mo_ref, a_ref, x_ref, mod_ref, mnw_ref, n2w_ref, wout_ref) = refs[:9]
    hs = hf_ref[0] + hb_ref[0]
    hn = jnp.concatenate([_rms_rows(hs[:, h * M_DV:(h + 1) * M_DV]) for h in range(M_HEADS)], axis=1)
    m = hn * mnw_ref[...] * _sigmoid(mo_ref[0].astype(F32))
    y_in = jnp.concatenate([m.astype(BF16), a_ref[0]], axis=1)
    mod = mod_ref[0, 0]
    x1 = x_ref[0] + mod[2:3] * jnp.dot(y_in, wout_ref[...], preferred_element_type=F32)
    h2 = _rms_rows(x1) * n2w_ref[...] * (1.0 + mod[4:5]) + mod[3:4]
    if not with_router:
        x1_ref, h2_ref = refs[9:]
        x1_ref[0] = x1
        h2_ref[0] = h2.astype(BF16)
        return
    router_ref, x1_ref, h2_ref, wts_ref = refs[9:]
    x1_ref[0] = x1
    h2_ref[0] = h2.astype(BF16)
    logits = jnp.dot(h2, router_ref[...], precision=HIGHEST, preferred_element_type=F32)
    lane = lax.broadcasted_iota(jnp.int32, logits.shape, 1)
    logits = jnp.where(lane < N_EXPERTS, logits, -jnp.inf)
    m1 = jnp.max(logits, axis=-1, keepdims=True)
    i1 = jnp.min(jnp.where(logits == m1, lane, LANES), axis=-1, keepdims=True)
    rest = jnp.where(lane == i1, -jnp.inf, logits)
    m2 = jnp.max(rest, axis=-1, keepdims=True)
    i2 = jnp.min(jnp.where(rest == m2, lane, LANES), axis=-1, keepdims=True)
    e2 = jnp.exp(m2 - m1)
    g1 = 1.0 / (1.0 + e2)
    wts_ref[0] = jnp.where(lane == i1, g1, 0.0) + jnp.where(lane == i2, e2 * g1, 0.0)


def _mixer_out(hf, hb, p, a, xa, mods, mnw, n2w, wout, router=None, row_off=0):
    B, L, D = xa.shape
    nt = L // ROW_TILE - row_off
    rin = lambda w, cb=0: pl.BlockSpec((1, ROW_TILE, w), lambda b, t: (b, t + row_off, cb))
    rout = lambda w: pl.BlockSpec((1, ROW_TILE, w), lambda b, t: (b, t, 0))
    full = lambda arr: pl.BlockSpec(arr.shape, lambda b, t: (0,) * arr.ndim)
    mw = M_HEADS * M_DV
    a_off = row_off - (L - a.shape[1]) // ROW_TILE
    a_spec = pl.BlockSpec((1, ROW_TILE, A_HEADS * A_DH), lambda b, t: (b, t + a_off, 0))
    in_specs = [rin(mw), rin(mw), rin(mw, P_MO // mw), a_spec, rin(D), _mod_spec(row_off),
                pl.BlockSpec((1, mw), lambda b, t: (0, 0)), pl.BlockSpec((1, D), lambda b, t: (0, 0)),
                full(wout)]
    args = [hf, hb, p, a, xa, mods, mnw.reshape(1, mw), n2w.reshape(1, D), wout]
    out_shape = [jax.ShapeDtypeStruct((B, nt * ROW_TILE, D), F32), jax.ShapeDtypeStruct((B, nt * ROW_TILE, D), BF16)]
    out_specs = [rout(D), rout(D)]
    if router is not None:
        in_specs.append(full(router))
        args.append(router)
        out_shape.append(jax.ShapeDtypeStruct((B, nt * ROW_TILE, LANES), F32))
        out_specs.append(rout(LANES))
    return pl.pallas_call(
        functools.partial(_mixer_out_kernel, with_router=router is not None),
        out_shape=tuple(out_shape),
        grid=(B, nt),
        in_specs=in_specs,
        out_specs=tuple(out_specs),
        compiler_params=_params(("parallel", "arbitrary")),
        name="mixer_out",
    )(*args)


def _ffn_kernel(h_ref, x_ref, mod_ref, wg_ref, wu_ref, wd_ref, o_ref):
    h = h_ref[0]
    g = jnp.dot(h, wg_ref[...], preferred_element_type=F32)
    u = jnp.dot(h, wu_ref[...], preferred_element_type=F32)
    act = (g * _sigmoid(g) * u).astype(BF16)
    y = jnp.dot(act, wd_ref[...], preferred_element_type=F32)
    o_ref[0] = x_ref[0] + mod_ref[0, 0][5:6] * y


def _ffn(h2, x1, mods, wg, wu, wd):
    B, L, D = x1.shape
    row = pl.BlockSpec((1, ROW_TILE, D), lambda b, t: (b, t, 0))
    const = lambda arr: pl.BlockSpec(arr.shape, lambda b, t: (0, 0), pipeline_mode=pl.Buffered(1))
    return pl.pallas_call(
        _ffn_kernel,
        out_shape=jax.ShapeDtypeStruct((B, L, D), F32),
        grid=(B, L // ROW_TILE),
        in_specs=[row, row, _mod_spec(), const(wg), const(wu), const(wd)],
        out_specs=row,
        compiler_params=_params(("parallel", "arbitrary")),
        name="ffn_swiglu",
    )(h2, x1, mods, wg, wu, wd)


MOE_ROWS = 1024
MOE_FF = 512


def _moe_kernel(h_ref, wts_ref, x_ref, mod_ref, wg_ref, wu_ref, wd_ref, fw_ref, o_ref, acc_ref):
    e, f = pl.program_id(2), pl.program_id(3)

    @pl.when(jnp.logical_and(e == 0, f == 0))
    def _():
        acc_ref[...] = jnp.zeros_like(acc_ref)

    h = h_ref[0]
    g = jnp.dot(h, wg_ref[0], preferred_element_type=F32)
    u = jnp.dot(h, wu_ref[0], preferred_element_type=F32)
    act = (g * _sigmoid(g) * u).astype(BF16)
    lane = lax.broadcasted_iota(jnp.int32, wts_ref.shape[1:], 1)
    w_e = jnp.sum(jnp.where(lane == e, wts_ref[0], 0.0), axis=-1, keepdims=True)
    acc_ref[...] += w_e * jnp.dot(act, wd_ref[0], preferred_element_type=F32)

    @pl.when(jnp.logical_and(e == pl.num_programs(2) - 1, f == pl.num_programs(3) - 1))
    def _():
        x2 = x_ref[0] + mod_ref[0, 0][5:6] * acc_ref[...]
        o_ref[0] = _rms_rows(x2) * fw_ref[...]


def _moe(h2, wts, x1, mods, wg, wu, wd, fw):
    B, T, D = x1.shape
    E, _, FF = wg.shape
    row = lambda w: pl.BlockSpec((1, MOE_ROWS, w), lambda b, t, e, f: (b, t, 0))
    return pl.pallas_call(
        _moe_kernel,
        out_shape=jax.ShapeDtypeStruct((B, T, D), F32),
        grid=(B, T // MOE_ROWS, E, FF // MOE_FF),
        in_specs=[row(D), row(LANES), row(D),
                  pl.BlockSpec((1, 1, 6, D), lambda b, t, e, f: (b, 1, 0, 0)),
                  pl.BlockSpec((1, D, MOE_FF), lambda b, t, e, f: (e, 0, f)),
                  pl.BlockSpec((1, D, MOE_FF), lambda b, t, e, f: (e, 0, f)),
                  pl.BlockSpec((1, MOE_FF, D), lambda b, t, e, f: (e, f, 0)),
                  pl.BlockSpec((1, D), lambda b, t, e, f: (0, 0))],
        out_specs=row(D),
        scratch_shapes=[pltpu.VMEM((MOE_ROWS, D), F32)],
        compiler_params=_params(("parallel", "parallel", "arbitrary", "arbitrary")),
        name="moe_experts",
    )(h2, wts, x1, mods, wg, wu, wd, fw.reshape(1, D))


_ROT_PERM = np.concatenate([np.arange(0, A_DH, 2), np.arange(1, A_DH, 2)])


def _prep_w_in(w):
    o = np.cumsum([0, M_HEADS * M_DK, M_HEADS * M_DK, M_HEADS * M_DV, M_HEADS * M_DV, 4 * M_HEADS,
                   A_HEADS * A_DH, A_KV_HEADS * A_DH, A_KV_HEADS * A_DH])
    mq, mk, mv, mo, mg, aq, ak, av = [w[:, o[i]:o[i + 1]] for i in range(8)]
    qk = jnp.concatenate([jnp.concatenate([mq[:, h * M_DK:(h + 1) * M_DK] * (M_DK ** -0.5),
                                           mk[:, h * M_DK:(h + 1) * M_DK]], axis=1) for h in range(M_HEADS)], axis=1)
    perm_q = np.concatenate([h * A_DH + _ROT_PERM for h in range(A_HEADS)])
    perm_k = np.concatenate([h * A_DH + _ROT_PERM for h in range(A_KV_HEADS)])
    pad = jnp.zeros((w.shape[0], G_WIDTH - 4 * M_HEADS), w.dtype)
    return jnp.concatenate([qk, mv, mo, aq[:, perm_q], ak[:, perm_k], av, mg, pad], axis=1).astype(BF16)


def _rope_tables(n_tok, n_ctx):
    rows = n_tok // GRID_W
    row = jnp.broadcast_to(jnp.arange(rows, dtype=F32)[:, None], (rows, GRID_W)).reshape(n_tok)
    col = jnp.broadcast_to(jnp.arange(GRID_W, dtype=F32)[None, :], (rows, GRID_W)).reshape(n_tok)
    n_freq = A_DH // 4
    inv_freq = ROPE_THETA ** (-jnp.arange(n_freq, dtype=F32) / n_freq)
    ang = jnp.concatenate([row[:, None] * inv_freq, col[:, None] * inv_freq], axis=-1)
    cos, sin = jnp.cos(ang), jnp.sin(ang)
    cos = jnp.concatenate([jnp.ones((n_ctx, A_DH // 2), F32), cos], axis=0)
    sin = jnp.concatenate([jnp.zeros((n_ctx, A_DH // 2), F32), sin], axis=0)
    return jnp.tile(cos, (1, 4)), jnp.tile(jnp.concatenate([-sin, sin], axis=1), (1, 2))


def kernel(x, c, ctx, c_ctx, ada_w, ada_b, norm1_w, norm2_w, w_in, mlstm_gate_b, mlstm_norm_w, q_norm_w, k_norm_w,
           w_out, ffn_w_gate, ffn_w_up, ffn_w_down, moe_router, moe_w_gate, moe_w_up, moe_w_down, final_norm_w):
    B, T, D = x.shape
    n_ctx = ctx.shape[1]
    L = n_ctx + T
    depth = w_in.shape[0]
    assert D == D_MODEL and n_ctx == ROW_TILE and T % MOE_ROWS == 0 and depth == 2
    ts = 3 * ROW_TILE
    assert L % ts == 0
    ctx_tiles = n_ctx // ROW_TILE
    tq = ROW_TILE

    xa = jnp.concatenate([ctx, x], axis=1)
    cvec = jnp.concatenate([c, c_ctx[None], jnp.zeros((8 - B - 1, D), F32)], axis=0)
    cos, sin = _rope_tables(T, n_ctx)
    out = None
    for i in range(depth):
        last = i == depth - 1
        modraw = _ada(cvec, ada_w[i], ada_b[i])
        mods = jnp.stack([jnp.broadcast_to(modraw[B].reshape(1, 6, D), (B, 6, D)),
                          modraw[:B].reshape(B, 6, D)], axis=1)
        p, g = _in_proj(xa, mods, norm1_w[i], _prep_w_in(w_in[i]))
        hf, hb = _mlstm(p, g, mlstm_gate_b[i], n_ctx // M_CHUNK)
        qw = jnp.tile(q_norm_w[i][_ROT_PERM], 2).reshape(1, LANES)
        kw = jnp.tile(k_norm_w[i][_ROT_PERM], 2).reshape(1, LANES)
        qt, k, vt = _attn_prep(p, cos, sin, qw, kw, ts)
        a = _attention(qt, k, vt, q_tile0=ctx_tiles, n_qt=T // tq, tq=tq, nblk=L // ts, ts=ts,
                       out_tile0=0 if last else ctx_tiles, out_rows=T if last else L)
        wout = w_out[i].astype(BF16)
        if not last:
            a = _attention(qt, k, vt, q_tile0=0, n_qt=ctx_tiles, tq=tq, nblk=1, ts=n_ctx,
                           out_tile0=0, out_rows=L, out=a)
            x1, h2 = _mixer_out(hf, hb, p, a, xa, mods, mlstm_norm_w[i], norm2_w[i], wout)
            j = i // 2
            xa = _ffn(h2, x1, mods, ffn_w_gate[j].astype(BF16), ffn_w_up[j].astype(BF16),
                      ffn_w_down[j].astype(BF16))
        else:
            j = i // 2
            router = jnp.zeros((D, LANES), F32).at[:, :N_EXPERTS].set(moe_router[j])
            x1, h2, wts = _mixer_out(hf, hb, p, a, xa, mods, mlstm_norm_w[i], norm2_w[i], wout,
                                     router=router, row_off=ctx_tiles)
            out = _moe(h2, wts, x1, mods, moe_w_gate[j].astype(BF16), moe_w_up[j].astype(BF16),
                       moe_w_down[j].astype(BF16), final_norm_w)
    return out
```

```python
import functools
import math

import numpy as np
import jax
import jax.numpy as jnp
from jax import lax
from jax.experimental import pallas as pl
from jax.experimental.pallas import tpu as pltpu

F32 = jnp.float32
BF16 = jnp.bfloat16
HIGHEST = lax.Precision.HIGHEST

D_MODEL = 1024
GRID_W = 64
M_HEADS = 4
M_DV = 128
M_DK = 64
M_CHUNK = 128
A_HEADS = 8
A_KV_HEADS = 2
A_GROUP = A_HEADS // A_KV_HEADS
A_DH = 64
ROPE_THETA = 10000.0
N_EXPERTS = 8
EPS = 1e-6

LANES = 128
SUBLANES = 8
ROW_TILE = 256
VMEM_LIMIT = 56 * 1024 * 1024

P_QK = 0
P_MV = 512
P_MO = 1024
P_AQ = 1536
P_AKV = 2048
P_WIDTH = 2304
G_WIDTH = LANES


def _params(sem, vmem=VMEM_LIMIT):
    return pltpu.CompilerParams(dimension_semantics=sem, vmem_limit_bytes=vmem)


def _sigmoid(x):
    return 1.0 / (1.0 + jnp.exp(-x))


def _rms_rows(x):
    return x * lax.rsqrt(jnp.mean(x * x, axis=-1, keepdims=True) + EPS)


def _ada_kernel(c_ref, w_ref, b_ref, o_ref):
    c = c_ref[...]
    s = c * _sigmoid(c)
    o_ref[...] = jnp.dot(s, w_ref[...], precision=HIGHEST, preferred_element_type=F32) + b_ref[...]


def _ada(cvec, w, b):
    n = w.shape[1]
    bn = 1536
    return pl.pallas_call(
        _ada_kernel,
        out_shape=jax.ShapeDtypeStruct((cvec.shape[0], n), F32),
        grid=(n // bn,),
        in_specs=[pl.BlockSpec(cvec.shape, lambda j: (0, 0)),
                  pl.BlockSpec((w.shape[0], bn), lambda j: (0, j)),
                  pl.BlockSpec((1, bn), lambda j: (0, j))],
        out_specs=pl.BlockSpec((cvec.shape[0], bn), lambda j: (0, j)),
        compiler_params=_params(("arbitrary",)),
        name="ada_mod",
    )(cvec, w, b.reshape(1, n))


def _mod_spec(off=0):
    return pl.BlockSpec((1, 1, 6, D_MODEL), lambda b, t: (b, jnp.minimum(t + off, 1), 0, 0))


def _in_proj_kernel(x_ref, mod_ref, nw_ref, w_ref, p_ref, g_ref):
    mod = mod_ref[0, 0]
    h = _rms_rows(x_ref[0]) * nw_ref[...] * (1.0 + mod[1:2]) + mod[0:1]
    r = jnp.dot(h.astype(BF16), w_ref[...], preferred_element_type=F32)
    p_ref[0] = r[:, :P_WIDTH].astype(BF16)
    g_ref[0] = r[:, P_WIDTH:]


def _in_proj(xa, mods, nw, wp):
    B, L, D = xa.shape
    row = lambda w: pl.BlockSpec((1, ROW_TILE, w), lambda b, t: (b, t, 0))
    return pl.pallas_call(
        _in_proj_kernel,
        out_shape=(jax.ShapeDtypeStruct((B, L, P_WIDTH), BF16),
                   jax.ShapeDtypeStruct((B, L, G_WIDTH), F32)),
        grid=(B, L // ROW_TILE),
        in_specs=[row(D), _mod_spec(),
                  pl.BlockSpec((1, D), lambda b, t: (0, 0)),
                  pl.BlockSpec(wp.shape, lambda b, t: (0, 0))],
        out_specs=(row(P_WIDTH), row(G_WIDTH)),
        compiler_params=_params(("parallel", "arbitrary")),
        name="in_proj",
    )(xa, mods, nw.reshape(1, D), wp)


def _mlstm_kernel(qkf_ref, vf_ref, gf_ref, qkb_ref, vb_ref, gb_ref, bias_ref,
                  hf_ref, hb_ref, ct_ref, n_ref, m_ref):
    @pl.when(pl.program_id(1) == 0)
    def _():
        ct_ref[...] = jnp.zeros_like(ct_ref)
        n_ref[...] = jnp.zeros_like(n_ref)
        m_ref[...] = jnp.zeros_like(m_ref)

    row = lax.broadcasted_iota(jnp.int32, (M_CHUNK, M_CHUNK), 0)
    col = lax.broadcasted_iota(jnp.int32, (M_CHUNK, M_CHUNK), 1)
    dirs = ((qkf_ref, vf_ref, gf_ref, hf_ref, col <= row),
            (qkb_ref, vb_ref, gb_ref, hb_ref, col >= row))
    for d, (qk_ref, v_ref, g_ref, h_ref, allowed) in enumerate(dirs):
        g = g_ref[0] + bias_ref[...]
        lf = jnp.minimum(g, 0.0) - jnp.log1p(jnp.exp(-jnp.abs(g)))
        b_col = jnp.dot(allowed.astype(F32), lf, precision=HIGHEST, preferred_element_type=F32)
        b_row = b_col.T
        g_row = g.T
        b_tot = jnp.sum(lf, axis=0, keepdims=True)
        for h in range(M_HEADS):
            idx = d * M_HEADS + h
            ic, fc = 8 * d + h, 8 * d + 4 + h
            b_c, b_r = b_col[:, fc:fc + 1], b_row[fc:fc + 1, :]
            i_c, i_r = g[:, ic:ic + 1], g_row[ic:ic + 1, :]
            b_end = b_tot[:, fc:fc + 1]
            m_prev = m_ref[idx][:, 0:1]
            ct = ct_ref[idx]
            n = n_ref[idx]
            qk = qk_ref[0, :, h * LANES:(h + 1) * LANES]
            q, k = qk[:, :M_DK], qk[:, M_DK:]
            v = v_ref[0, :, h * M_DV:(h + 1) * M_DV]

            d_log = jnp.where(allowed, b_c - b_r + i_r, -jnp.inf)
            inter = b_c + m_prev
            m_t = jnp.maximum(inter, jnp.max(d_log, axis=-1, keepdims=True))
            a = jnp.exp(inter - m_t)
            s = lax.dot_general(q, k, (((1,), (1,)), ((), ())), preferred_element_type=F32)
            s = s * jnp.exp(d_log - m_t)
            num = (jnp.dot(s.astype(BF16), v, preferred_element_type=F32)
                   + a * jnp.dot(q, ct.astype(BF16), preferred_element_type=F32))
            qn = jnp.sum(q.astype(F32) * n, axis=-1, keepdims=True)
            den = jnp.sum(s, axis=-1, keepdims=True) + a * qn
            h_ref[0, :, h * M_DV:(h + 1) * M_DV] = num / jnp.maximum(jnp.abs(den), jnp.exp(-m_t))

            m_loc = jnp.max(b_end - b_r + i_r, axis=-1, keepdims=True)
            w_col = jnp.exp(b_end - b_c + i_c - m_loc)
            vw = (v.astype(F32) * w_col).astype(BF16)
            c_loc = lax.dot_general(k, vw, (((0,), (0,)), ((), ())), preferred_element_type=F32)
            n_loc = jnp.sum(k.astype(F32) * w_col, axis=0, keepdims=True)
            m_new = jnp.maximum(b_end + m_prev, m_loc)
            a_s = jnp.exp(b_end + m_prev - m_new)
            s_s = jnp.exp(m_loc - m_new)
            ct_ref[idx] = a_s * ct + s_s * c_loc
            n_ref[idx] = a_s * n + s_s * n_loc
            m_ref[idx] = jnp.broadcast_to(m_new, (1, LANES))


def _mlstm(p, g, gate_b, ctx_chunks):
    B, L, _ = p.shape
    nc = L // M_CHUNK
    fwd = lambda j: j
    bwd = lambda j: jnp.where(j < ctx_chunks, ctx_chunks - 1 - j, nc - 1 + ctx_chunks - j)
    width = M_HEADS * M_DV
    blk = lambda cm, colblk, w=width: pl.BlockSpec((1, M_CHUNK, w), lambda b, j: (b, cm(j), colblk))
    bias = jnp.zeros((1, G_WIDTH), F32).at[0, :16].set(gate_b)
    return pl.pallas_call(
        _mlstm_kernel,
        out_shape=(jax.ShapeDtypeStruct((B, L, width), F32),) * 2,
        grid=(B, nc),
        in_specs=[blk(fwd, P_QK // width), blk(fwd, P_MV // width), blk(fwd, 0, G_WIDTH),
                  blk(bwd, P_QK // width), blk(bwd, P_MV // width), blk(bwd, 0, G_WIDTH),
                  pl.BlockSpec((1, G_WIDTH), lambda b, j: (0, 0))],
        out_specs=(blk(fwd, 0), blk(bwd, 0)),
        scratch_shapes=[pltpu.VMEM((2 * M_HEADS, M_DK, M_DV), F32),
                        pltpu.VMEM((2 * M_HEADS, 1, M_DK), F32),
                        pltpu.VMEM((2 * M_HEADS, 1, LANES), F32)],
        compiler_params=_params(("parallel", "arbitrary")),
        name="mlstm",
    )(p, p, g, p, p, g, bias)


def _head_norm_rope(x, w, cos, sin, bd):
    ms = jnp.dot(x * x, bd, precision=HIGHEST, preferred_element_type=F32)
    y = x * lax.rsqrt(ms + EPS) * w
    lane = lax.broadcasted_iota(jnp.int32, y.shape, 1)
    partner = jnp.where(lane % A_DH < A_DH // 2,
                        pltpu.roll(y, LANES - A_DH // 2, axis=1), pltpu.roll(y, A_DH // 2, axis=1))
    return y * cos + partner * sin


def _attn_prep_kernel(q_ref, kv_ref, cos_ref, sin_ref, qw_ref, kw_ref, qt_ref, k_ref, vt_ref, *, q_scale):
    r = lax.broadcasted_iota(jnp.int32, (LANES, LANES), 0) // A_DH
    c = lax.broadcasted_iota(jnp.int32, (LANES, LANES), 1) // A_DH
    bd = jnp.where(r == c, 1.0 / A_DH, 0.0).astype(F32)
    cos, sin = cos_ref[...], sin_ref[...]
    for pair in range(A_HEADS // 2):
        x = q_ref[0, :, pair * LANES:(pair + 1) * LANES].astype(F32)
        y = _head_norm_rope(x, qw_ref[...], cos, sin, bd) * q_scale
        qt_ref[0, pair * LANES:(pair + 1) * LANES, :] = y.T.astype(BF16)
    kv = kv_ref[0].astype(F32)
    k = _head_norm_rope(kv[:, :LANES], kw_ref[...], cos, sin, bd).astype(BF16)
    for kvh in range(A_KV_HEADS):
        k_ref[0, kvh, 0] = k[:, kvh * A_DH:(kvh + 1) * A_DH]
    vt = kv[:, LANES:].T.astype(BF16)
    for kvh in range(A_KV_HEADS):
        vt_ref[0, kvh, 0] = vt[kvh * A_DH:(kvh + 1) * A_DH, :]


def _attn_prep(p, cos, sin, qw, kw, ts):
    B, L, _ = p.shape
    per = ts // ROW_TILE
    nblk = L // ts
    q_scale = A_DH ** -0.5 * math.log2(math.e)
    return pl.pallas_call(
        functools.partial(_attn_prep_kernel, q_scale=q_scale),
        out_shape=(jax.ShapeDtypeStruct((B, A_HEADS * A_DH, L), BF16),
                   jax.ShapeDtypeStruct((B, A_KV_HEADS, nblk, ts, A_DH), BF16),
                   jax.ShapeDtypeStruct((B, A_KV_HEADS, nblk, A_DH, ts), BF16)),
        grid=(B, L // ROW_TILE),
        in_specs=[pl.BlockSpec((1, ROW_TILE, A_HEADS * A_DH), lambda b, t: (b, t, P_AQ // (A_HEADS * A_DH))),
                  pl.BlockSpec((1, ROW_TILE, 2 * LANES), lambda b, t: (b, t, P_AKV // (2 * LANES))),
                  pl.BlockSpec((ROW_TILE, LANES), lambda b, t: (t, 0)),
                  pl.BlockSpec((ROW_TILE, LANES), lambda b, t: (t, 0)),
                  pl.BlockSpec((1, LANES), lambda b, t: (0, 0)),
                  pl.BlockSpec((1, LANES), lambda b, t: (0, 0))],
        out_specs=(pl.BlockSpec((1, A_HEADS * A_DH, ROW_TILE), lambda b, t: (b, 0, t)),
                   pl.BlockSpec((1, A_KV_HEADS, 1, ROW_TILE, A_DH), lambda b, t: (b, 0, t // per, t % per, 0)),
                   pl.BlockSpec((1, A_KV_HEADS, 1, A_DH, ROW_TILE), lambda b, t: (b, 0, t // per, 0, t % per))),
        compiler_params=_params(("parallel", "arbitrary")),
        name="attn_prep",
    )(p, p, cos, sin, qw, kw)


def _attn_kernel(qt_ref, k_ref, vt_ref, o_ref, *, nblk, tq, ctx_tiles, n_ctx, q_tile0):
    qt = qt_ref[0]
    q = jnp.concatenate([qt[g * A_DH:(g + 1) * A_DH, :] for g in range(A_GROUP)], axis=1)
    n = A_GROUP * tq

    def block(i, carry, n_keys):
        m, l, acc = carry
        s = jnp.dot(k_ref[0, 0, i], q, preferred_element_type=F32)
        if n_keys is not None:
            s = jnp.where(lax.broadcasted_iota(jnp.int32, s.shape, 0) < n_keys, s, -jnp.inf)
        m_new = jnp.maximum(m, jnp.max(s, axis=0, keepdims=True))
        alpha = jnp.exp2(m - m_new)
        p = jnp.exp2(s - m_new)
        l = alpha * l + jnp.sum(p, axis=0, keepdims=True)
        acc = alpha * acc + jnp.dot(vt_ref[0, 0, i], p.astype(BF16), preferred_element_type=F32)
        return m_new, l, acc

    def finish(carry):
        _, l, acc = carry
        o = acc / l
        o = jnp.concatenate([o[:, g * tq:(g + 1) * tq] for g in range(A_GROUP)], axis=0)
        o_ref[0] = o.T.astype(BF16)

    init = (jnp.full((1, n), -jnp.inf, F32), jnp.zeros((1, n), F32), jnp.zeros((A_DH, n), F32))
    is_ctx = pl.program_id(2) + q_tile0 < ctx_tiles

    @pl.when(is_ctx)
    def _():
        finish(block(0, init, n_ctx))

    @pl.when(jnp.logical_not(is_ctx))
    def _():
        finish(lax.fori_loop(0, nblk, lambda i, c: block(i, c, None), init))


def _attention(qt, k, vt, *, q_tile0, n_ctx, tq):
    B, _, L = qt.shape
    nblk, ts = k.shape[2], k.shape[3]
    assert n_ctx <= ts and n_ctx % tq == 0
    width = A_GROUP * A_DH
    return pl.pallas_call(
        functools.partial(_attn_kernel, nblk=nblk, tq=tq, ctx_tiles=n_ctx // tq, n_ctx=n_ctx, q_tile0=q_tile0),
        out_shape=jax.ShapeDtypeStruct((B, L - q_tile0 * tq, A_HEADS * A_DH), BF16),
        grid=(B, A_KV_HEADS, L // tq - q_tile0),
        in_specs=[pl.BlockSpec((1, width, tq), lambda b, kv, t: (b, kv, t + q_tile0)),
                  pl.BlockSpec((1, 1, nblk, ts, A_DH), lambda b, kv, t: (b, kv, 0, 0, 0)),
                  pl.BlockSpec((1, 1, nblk, A_DH, ts), lambda b, kv, t: (b, kv, 0, 0, 0))],
        out_specs=pl.BlockSpec((1, tq, width), lambda b, kv, t: (b, t, kv)),
        compiler_params=_params(("parallel", "parallel", "arbitrary")),
        name="attention",
    )(qt, k, vt)


def _mixer_out_kernel(*refs, with_router):
    (hf_ref, hb_ref, mo_ref, a_ref, x_ref, mod_ref, mnw_ref, n2w_ref, wout_ref) = refs[:9]
    hs = hf_ref[0] + hb_ref[0]
    hn = jnp.concatenate([_rms_rows(hs[:, h * M_DV:(h + 1) * M_DV]) for h in range(M_HEADS)], axis=1)
    m = hn * mnw_ref[...] * _sigmoid(mo_ref[0].astype(F32))
    y_in = jnp.concatenate([m.astype(BF16), a_ref[0]], axis=1)
    mod = mod_ref[0, 0]
    x1 = x_ref[0] + mod[2:3] * jnp.dot(y_in, wout_ref[...], preferred_element_type=F32)
    h2 = _rms_rows(x1) * n2w_ref[...] * (1.0 + mod[4:5]) + mod[3:4]
    if not with_router:
        x1_ref, h2_ref = refs[9:]
        x1_ref[0] = x1
        h2_ref[0] = h2.astype(BF16)
        return
    router_ref, x1_ref, h2_ref, ids_ref, gates_ref = refs[9:]
    x1_ref[0] = x1
    h2_ref[0] = h2
    logits = jnp.dot(h2, router_ref[...], precision=HIGHEST, preferred_element_type=F32)
    lane = lax.broadcasted_iota(jnp.int32, logits.shape, 1)
    logits = jnp.where(lane < N_EXPERTS, logits, -jnp.inf)
    m1 = jnp.max(logits, axis=-1, keepdims=True)
    i1 = jnp.min(jnp.where(logits == m1, lane, LANES), axis=-1, keepdims=True)
    rest = jnp.where(lane == i1, -jnp.inf, logits)
    m2 = jnp.max(rest, axis=-1, keepdims=True)
    i2 = jnp.min(jnp.where(rest == m2, lane, LANES), axis=-1, keepdims=True)
    e2 = jnp.exp(m2 - m1)
    g1 = 1.0 / (1.0 + e2)
    ids_ref[0] = jnp.where(lane == 0, i1, jnp.where(lane == 1, i2, -1))
    gates_ref[0] = jnp.where(lane == 0, g1, jnp.where(lane == 1, e2 * g1, 0.0))


def _mixer_out(hf, hb, p, a, xa, mods, mnw, n2w, wout, router=None, row_off=0):
    B, L, D = xa.shape
    nt = L // ROW_TILE - row_off
    rin = lambda w, cb=0: pl.BlockSpec((1, ROW_TILE, w), lambda b, t: (b, t + row_off, cb))
    rout = lambda w: pl.BlockSpec((1, ROW_TILE, w), lambda b, t: (b, t, 0))
    full = lambda arr: pl.BlockSpec(arr.shape, lambda b, t: (0,) * arr.ndim)
    mw = M_HEADS * M_DV
    a_off = row_off - (L - a.shape[1]) // ROW_TILE
    a_spec = pl.BlockSpec((1, ROW_TILE, A_HEADS * A_DH), lambda b, t: (b, t + a_off, 0))
    in_specs = [rin(mw), rin(mw), rin(mw, P_MO // mw), a_spec, rin(D), _mod_spec(row_off),
                pl.BlockSpec((1, mw), lambda b, t: (0, 0)), pl.BlockSpec((1, D), lambda b, t: (0, 0)),
                full(wout)]
    args = [hf, hb, p, a, xa, mods, mnw.reshape(1, mw), n2w.reshape(1, D), wout]
    rows = nt * ROW_TILE
    out_shape = [jax.ShapeDtypeStruct((B, rows, D), F32),
                 jax.ShapeDtypeStruct((B, rows, D), BF16 if router is None else F32)]
    out_specs = [rout(D), rout(D)]
    if router is not None:
        in_specs.append(full(router))
        args.append(router)
        out_shape += [jax.ShapeDtypeStruct((B, rows, LANES), jnp.int32), jax.ShapeDtypeStruct((B, rows, LANES), F32)]
        out_specs += [rout(LANES), rout(LANES)]
    return pl.pallas_call(
        functools.partial(_mixer_out_kernel, with_router=router is not None),
        out_shape=tuple(out_shape),
        grid=(B, nt),
        in_specs=in_specs,
        out_specs=tuple(out_specs),
        compiler_params=_params(("parallel", "arbitrary")),
        name="mixer_out",
    )(*args)


def _ffn_kernel(h_ref, x_ref, mod_ref, wg_ref, wu_ref, wd_ref, o_ref):
    h = h_ref[0]
    g = jnp.dot(h, wg_ref[...], preferred_element_type=F32)
    u = jnp.dot(h, wu_ref[...], preferred_element_type=F32)
    act = (g * _sigmoid(g) * u).astype(BF16)
    y = jnp.dot(act, wd_ref[...], preferred_element_type=F32)
    o_ref[0] = x_ref[0] + mod_ref[0, 0][5:6] * y


def _ffn(h2, x1, mods, wg, wu, wd):
    B, L, D = x1.shape
    row = pl.BlockSpec((1, ROW_TILE, D), lambda b, t: (b, t, 0))
    const = lambda arr: pl.BlockSpec(arr.shape, lambda b, t: (0, 0), pipeline_mode=pl.Buffered(1))
    return pl.pallas_call(
        _ffn_kernel,
        out_shape=jax.ShapeDtypeStruct((B, L, D), F32),
        grid=(B, L // ROW_TILE),
        in_specs=[row, row, _mod_spec(), const(wg), const(wu), const(wd)],
        out_specs=row,
        compiler_params=_params(("parallel", "arbitrary")),
        name="ffn_swiglu",
    )(h2, x1, mods, wg, wu, wd)


MOE_TM = 512
MOE_FF = 512
MOE_TD = 256
RANK_TILE = 512


def _moe_rank_kernel(ids_ref, rank_ref, cnt_ref, carry_ref):
    @pl.when(pl.program_id(0) == 0)
    def _():
        carry_ref[...] = jnp.zeros_like(carry_ref)

    ids = ids_ref[...]
    lane = lax.broadcasted_iota(jnp.int32, ids.shape, 1)
    onehot = jnp.where(jnp.logical_or(lane == ids[:, 0:1], lane == ids[:, 1:2]), 1.0, 0.0)
    r = lax.broadcasted_iota(jnp.int32, (RANK_TILE, RANK_TILE), 0)
    c = lax.broadcasted_iota(jnp.int32, (RANK_TILE, RANK_TILE), 1)
    before = jnp.where(c < r, 1.0, 0.0).astype(BF16)
    rank_ref[...] = jnp.dot(before, onehot.astype(BF16), preferred_element_type=F32) + carry_ref[...]
    carry_ref[...] += jnp.sum(onehot, axis=0, keepdims=True)
    cnt_ref[...] = carry_ref[...]


def _moe_rank(ids):
    n = ids.shape[0]
    return pl.pallas_call(
        _moe_rank_kernel,
        out_shape=(jax.ShapeDtypeStruct((n, LANES), F32), jax.ShapeDtypeStruct((1, LANES), F32)),
        grid=(n // RANK_TILE,),
        in_specs=[pl.BlockSpec((RANK_TILE, LANES), lambda t: (t, 0))],
        out_specs=(pl.BlockSpec((RANK_TILE, LANES), lambda t: (t, 0)), pl.BlockSpec((1, LANES), lambda t: (0, 0))),
        scratch_shapes=[pltpu.VMEM((1, LANES), F32)],
        compiler_params=_params(("arbitrary",)),
        name="moe_rank",
    )(ids)


def _moe_pos_kernel(ids_ref, rank_ref, start_ref, pos_ref):
    ids = ids_ref[...]
    lane = lax.broadcasted_iota(jnp.int32, ids.shape, 1)
    tgt = start_ref[...] + rank_ref[...]
    p0 = jnp.sum(jnp.where(lane == ids[:, 0:1], tgt, 0.0), axis=-1, keepdims=True)
    p1 = jnp.sum(jnp.where(lane == ids[:, 1:2], tgt, 0.0), axis=-1, keepdims=True)
    pos_ref[...] = jnp.where(lane == 0, p0, jnp.where(lane == 1, p1, 0.0)).astype(jnp.int32)


def _moe_pos(ids, rank, start_row):
    n = ids.shape[0]
    blk = pl.BlockSpec((RANK_TILE, LANES), lambda t: (t, 0))
    return pl.pallas_call(
        _moe_pos_kernel,
        out_shape=jax.ShapeDtypeStruct((n, LANES), jnp.int32),
        grid=(n // RANK_TILE,),
        in_specs=[blk, blk, pl.BlockSpec((1, LANES), lambda t: (0, 0))],
        out_specs=blk,
        compiler_params=_params(("parallel",)),
        name="moe_pos",
    )(ids, rank, start_row)


def _row_copy(src, src_row, dst, dst_row, sem):
    return pltpu.make_async_copy(src.at[pl.ds(src_row, 1), :], dst.at[pl.ds(dst_row, 1), :], sem)


def _moe_dispatch_kernel(pad_ref, pos_ref, h_ref, xs_ref, zero_ref, sem):
    @pl.when(pl.program_id(0) == 0)
    def _():
        zero_ref[...] = jnp.zeros_like(zero_ref)
        fills = [pltpu.make_async_copy(
            zero_ref, xs_ref.at[pl.ds(pl.multiple_of(pad_ref[e], SUBLANES), MOE_TM + SUBLANES), :], sem)
            for e in range(N_EXPERTS)]
        for cp in fills:
            cp.start()
        for cp in fills:
            cp.wait()

        def fill_tile(j, carry):
            cp = pltpu.make_async_copy(zero_ref.at[pl.ds(0, MOE_TM), :],
                                       xs_ref.at[pl.ds(pl.multiple_of(j * MOE_TM, MOE_TM), MOE_TM), :], sem)
            cp.start()
            cp.wait()
            return carry

        lax.fori_loop(pad_ref[N_EXPERTS], xs_ref.shape[0] // MOE_TM, fill_tile, 0)

    def issue(r, carry):
        for k in range(2):
            _row_copy(h_ref, r, xs_ref, pos_ref[0, 0, 2 * r + k], sem).start()
        return carry

    lax.fori_loop(0, MOE_TD, issue, 0)
    for k in range(2):
        pltpu.make_async_copy(h_ref, xs_ref.at[pl.ds(0, MOE_TD), :], sem).wait()


def _moe_dispatch(h, pos, fill_meta, ns):
    n, d = h.shape
    return pl.pallas_call(
        _moe_dispatch_kernel,
        out_shape=jax.ShapeDtypeStruct((ns, d), F32),
        grid_spec=pltpu.PrefetchScalarGridSpec(
            num_scalar_prefetch=1,
            grid=(n // MOE_TD,),
            in_specs=[pl.BlockSpec((1, 1, 2 * MOE_TD), lambda t, pad: (t, 0, 0), memory_space=pltpu.SMEM),
                      pl.BlockSpec((MOE_TD, d), lambda t, pad: (t, 0))],
            out_specs=pl.BlockSpec(memory_space=pl.ANY),
            scratch_shapes=[pltpu.VMEM((MOE_TM + SUBLANES, d), F32), pltpu.SemaphoreType.DMA(())]),
        compiler_params=_params(("arbitrary",)),
        name="moe_dispatch",
    )(fill_meta, pos, h)


def _moe_group_kernel(te_ref, nv_ref, xs_ref, wg_ref, wu_ref, wd_ref, ys_ref, acc_ref):
    i, f = pl.program_id(0), pl.program_id(1)
    last = pl.num_programs(1) - 1
    valid = i < nv_ref[0]

    @pl.when(valid)
    def _():
        @pl.when(f == 0)
        def _():
            acc_ref[...] = jnp.zeros_like(acc_ref)

        x = xs_ref[...].astype(BF16)
        g = jnp.dot(x, wg_ref[0], preferred_element_type=F32)
        u = jnp.dot(x, wu_ref[0], preferred_element_type=F32)
        act = (g * _sigmoid(g) * u).astype(BF16)
        acc_ref[...] += jnp.dot(act, wd_ref[0], preferred_element_type=F32)

        @pl.when(f == last)
        def _():
            ys_ref[...] = acc_ref[...]

    @pl.when(jnp.logical_and(jnp.logical_not(valid), f == last))
    def _():
        ys_ref[...] = jnp.zeros_like(ys_ref)


def _moe_group(xs, tile_expert, n_valid, wg, wu, wd):
    ns, d = xs.shape
    n_tiles = ns // MOE_TM - 1
    ff = wg.shape[2]
    live = lambda i, nv: i < nv[0]
    return pl.pallas_call(
        _moe_group_kernel,
        out_shape=jax.ShapeDtypeStruct((n_tiles * MOE_TM, d), F32),
        grid_spec=pltpu.PrefetchScalarGridSpec(
            num_scalar_prefetch=2,
            grid=(n_tiles, ff // MOE_FF),
            in_specs=[pl.BlockSpec((MOE_TM, d), lambda i, f, te, nv: (jnp.where(live(i, nv), i, 0), 0)),
                      pl.BlockSpec((1, d, MOE_FF), lambda i, f, te, nv: (te[i], 0, f)),
                      pl.BlockSpec((1, d, MOE_FF), lambda i, f, te, nv: (te[i], 0, f)),
                      pl.BlockSpec((1, MOE_FF, d), lambda i, f, te, nv: (te[i], f, 0))],
            out_specs=pl.BlockSpec((MOE_TM, d), lambda i, f, te, nv: (i, 0)),
            scratch_shapes=[pltpu.VMEM((MOE_TM, d), F32)]),
        compiler_params=_params(("arbitrary", "arbitrary")),
        name="moe_group",
    )(tile_expert, n_valid, xs, wg, wu, wd)


def _moe_combine_kernel(pos_ref, x_ref, gates_ref, mod_ref, fw_ref, ys_ref, o_ref, ybuf, sem):
    def issue(r, carry):
        for k in range(2):
            _row_copy(ys_ref, pos_ref[0, 0, 2 * r + k], ybuf.at[k], r, sem).start()
        return carry

    lax.fori_loop(0, MOE_TD, issue, 0)
    for k in range(2):
        pltpu.make_async_copy(ys_ref.at[pl.ds(0, MOE_TD), :], ybuf.at[k], sem).wait()
    gates = gates_ref[...]
    y = gates[:, 0:1] * ybuf[0] + gates[:, 1:2] * ybuf[1]
    x2 = x_ref[...] + mod_ref[0, 0][5:6] * y
    o_ref[...] = _rms_rows(x2) * fw_ref[...]


def _moe_combine(pos, x1, gates, mods, fw, ys, tokens_per_sample):
    n, d = x1.shape
    per = tokens_per_sample // MOE_TD
    return pl.pallas_call(
        _moe_combine_kernel,
        out_shape=jax.ShapeDtypeStruct((n, d), F32),
        grid=(n // MOE_TD,),
        in_specs=[pl.BlockSpec((1, 1, 2 * MOE_TD), lambda t: (t, 0, 0), memory_space=pltpu.SMEM),
                  pl.BlockSpec((MOE_TD, d), lambda t: (t, 0)),
                  pl.BlockSpec((MOE_TD, LANES), lambda t: (t, 0)),
                  pl.BlockSpec((1, 1, 6, d), lambda t: (t // per, 1, 0, 0)),
                  pl.BlockSpec((1, d), lambda t: (0, 0)),
                  pl.BlockSpec(memory_space=pl.ANY)],
        out_specs=pl.BlockSpec((MOE_TD, d), lambda t: (t, 0)),
        scratch_shapes=[pltpu.VMEM((2, MOE_TD, d), F32), pltpu.SemaphoreType.DMA(())],
        compiler_params=_params(("arbitrary",)),
        name="moe_combine",
    )(pos, x1, gates, mods, fw.reshape(1, d), ys)


def _moe(h2, ids, gates, x1, mods, wg, wu, wd, fw):
    B, T, D = x1.shape
    n = B * T
    ids, gates = ids.reshape(n, LANES), gates.reshape(n, LANES)
    rank, cnt = _moe_rank(ids)
    cnt = cnt[0, :N_EXPERTS].astype(jnp.int32)
    padded = (cnt + MOE_TM - 1) // MOE_TM * MOE_TM
    end = jnp.cumsum(padded)
    start = end - padded
    n_tiles = 2 * n // MOE_TM + N_EXPERTS
    tile_expert = jnp.minimum(jnp.sum(jnp.arange(n_tiles)[:, None] >= (end // MOE_TM)[None, :], axis=1),
                              N_EXPERTS - 1).astype(jnp.int32)
    n_valid = (end[-1:] // MOE_TM).astype(jnp.int32)
    start_row = jnp.zeros((1, LANES), F32).at[0, :N_EXPERTS].set(start.astype(F32))
    pos = _moe_pos(ids, rank, start_row)
    pos = pos[:, :2].reshape(n // MOE_TD, 1, 2 * MOE_TD)
    fill_meta = jnp.concatenate([(start + cnt) // SUBLANES * SUBLANES, n_valid]).astype(jnp.int32)
    xs = _moe_dispatch(h2.reshape(n, D), pos, fill_meta, (n_tiles + 1) * MOE_TM)
    ys = _moe_group(xs, tile_expert, n_valid, wg, wu, wd)
    return _moe_combine(pos, x1.reshape(n, D), gates, mods, fw, ys, T).reshape(B, T, D)


_ROT_PERM = np.concatenate([np.arange(0, A_DH, 2), np.arange(1, A_DH, 2)])


def _prep_w_in(w):
    o = np.cumsum([0, M_HEADS * M_DK, M_HEADS * M_DK, M_HEADS * M_DV, M_HEADS * M_DV, 4 * M_HEADS,
                   A_HEADS * A_DH, A_KV_HEADS * A_DH, A_KV_HEADS * A_DH])
    mq, mk, mv, mo, mg, aq, ak, av = [w[:, o[i]:o[i + 1]] for i in range(8)]
    qk = jnp.concatenate([jnp.concatenate([mq[:, h * M_DK:(h + 1) * M_DK] * (M_DK ** -0.5),
                                           mk[:, h * M_DK:(h + 1) * M_DK]], axis=1) for h in range(M_HEADS)], axis=1)
    perm_q = np.concatenate([h * A_DH + _ROT_PERM for h in range(A_HEADS)])
    perm_k = np.concatenate([h * A_DH + _ROT_PERM for h in range(A_KV_HEADS)])
    pad = jnp.zeros((w.shape[0], G_WIDTH - 4 * M_HEADS), w.dtype)
    return jnp.concatenate([qk, mv, mo, aq[:, perm_q], ak[:, perm_k], av, mg, pad], axis=1).astype(BF16)


def _rope_tables(n_tok, n_ctx):
    rows = n_tok // GRID_W
    row = jnp.broadcast_to(jnp.arange(rows, dtype=F32)[:, None], (rows, GRID_W)).reshape(n_tok)
    col = jnp.broadcast_to(jnp.arange(GRID_W, dtype=F32)[None, :], (rows, GRID_W)).reshape(n_tok)
    n_freq = A_DH // 4
    inv_freq = ROPE_THETA ** (-jnp.arange(n_freq, dtype=F32) / n_freq)
    ang = jnp.concatenate([row[:, None] * inv_freq, col[:, None] * inv_freq], axis=-1)
    cos, sin = jnp.cos(ang), jnp.sin(ang)
    cos = jnp.concatenate([jnp.ones((n_ctx, A_DH // 2), F32), cos], axis=0)
    sin = jnp.concatenate([jnp.zeros((n_ctx, A_DH // 2), F32), sin], axis=0)
    return jnp.tile(cos, (1, 4)), jnp.tile(jnp.concatenate([-sin, sin], axis=1), (1, 2))


def kernel(x, c, ctx, c_ctx, ada_w, ada_b, norm1_w, norm2_w, w_in, mlstm_gate_b, mlstm_norm_w, q_norm_w, k_norm_w,
           w_out, ffn_w_gate, ffn_w_up, ffn_w_down, moe_router, moe_w_gate, moe_w_up, moe_w_down, final_norm_w):
    B, T, D = x.shape
    n_ctx = ctx.shape[1]
    L = n_ctx + T
    depth = w_in.shape[0]
    assert D == D_MODEL and n_ctx == ROW_TILE and T % RANK_TILE == 0 and depth == 2
    ts = 3 * ROW_TILE
    assert L % ts == 0
    ctx_tiles = n_ctx // ROW_TILE
    tq = ROW_TILE

    xa = jnp.concatenate([ctx, x], axis=1)
    cvec = jnp.concatenate([c, c_ctx[None], jnp.zeros((8 - B - 1, D), F32)], axis=0)
    cos, sin = _rope_tables(T, n_ctx)
    out = None
    for i in range(depth):
        last = i == depth - 1
        modraw = _ada(cvec, ada_w[i], ada_b[i])
        mods = jnp.stack([jnp.broadcast_to(modraw[B].reshape(1, 6, D), (B, 6, D)),
                          modraw[:B].reshape(B, 6, D)], axis=1)
        p, g = _in_proj(xa, mods, norm1_w[i], _prep_w_in(w_in[i]))
        hf, hb = _mlstm(p, g, mlstm_gate_b[i], n_ctx // M_CHUNK)
        qw = jnp.tile(q_norm_w[i][_ROT_PERM], 2).reshape(1, LANES)
        kw = jnp.tile(k_norm_w[i][_ROT_PERM], 2).reshape(1, LANES)
        qt, k, vt = _attn_prep(p, cos, sin, qw, kw, ts)
        a = _attention(qt, k, vt, q_tile0=ctx_tiles if last else 0, n_ctx=n_ctx, tq=tq)
        wout = w_out[i].astype(BF16)
        if not last:
            x1, h2 = _mixer_out(hf, hb, p, a, xa, mods, mlstm_norm_w[i], norm2_w[i], wout)
            j = i // 2
            xa = _ffn(h2, x1, mods, ffn_w_gate[j].astype(BF16), ffn_w_up[j].astype(BF16),
                      ffn_w_down[j].astype(BF16))
        else:
            j = i // 2
            router = jnp.zeros((D, LANES), F32).at[:, :N_EXPERTS].set(moe_router[j])
            x1, h2, ids, gates = _mixer_out(hf, hb, p, a, xa, mods, mlstm_norm_w[i], norm2_w[i], wout,
                                            router=router, row_off=ctx_tiles)
            out = _moe(h2, ids, gates, x1, mods, moe_w_gate[j].astype(BF16), moe_w_up[j].astype(BF16),
                       moe_w_down[j].astype(BF16), final_norm_w)
    return out
```

```python
import functools
import math

import numpy as np
import jax
import jax.numpy as jnp
from jax import lax
from jax.experimental import pallas as pl
from jax.experimental.pallas import tpu as pltpu

F32 = jnp.float32
BF16 = jnp.bfloat16
HIGHEST = lax.Precision.HIGHEST

D_MODEL = 1024
GRID_W = 64
M_HEADS = 4
M_DV = 128
M_DK = 64
M_CHUNK = 128
A_HEADS = 8
A_KV_HEADS = 2
A_GROUP = A_HEADS // A_KV_HEADS
A_DH = 64
ROPE_THETA = 10000.0
N_EXPERTS = 8
EPS = 1e-6

LANES = 128
SUBLANES = 8
ROW_TILE = 256
VMEM_LIMIT = 56 * 1024 * 1024

P_QK = 0
P_MV = 512
P_MO = 1024
P_AQ = 1536
P_AKV = 2048
P_WIDTH = 2304
G_WIDTH = LANES


def _params(sem, vmem=VMEM_LIMIT):
    return pltpu.CompilerParams(dimension_semantics=sem, vmem_limit_bytes=vmem)


def _sigmoid(x):
    return 1.0 / (1.0 + jnp.exp(-x))


def _rms_rows(x):
    return x * lax.rsqrt(jnp.mean(x * x, axis=-1, keepdims=True) + EPS)


def _ada_kernel(c_ref, w_ref, b_ref, o_ref):
    c = c_ref[...]
    s = c * _sigmoid(c)
    o_ref[...] = jnp.dot(s, w_ref[...], precision=HIGHEST, preferred_element_type=F32) + b_ref[...]


def _ada(cvec, w, b):
    n = w.shape[1]
    bn = 1536
    return pl.pallas_call(
        _ada_kernel,
        out_shape=jax.ShapeDtypeStruct((cvec.shape[0], n), F32),
        grid=(n // bn,),
        in_specs=[pl.BlockSpec(cvec.shape, lambda j: (0, 0)),
                  pl.BlockSpec((w.shape[0], bn), lambda j: (0, j)),
                  pl.BlockSpec((1, bn), lambda j: (0, j))],
        out_specs=pl.BlockSpec((cvec.shape[0], bn), lambda j: (0, j)),
        compiler_params=_params(("arbitrary",)),
        name="ada_mod",
    )(cvec, w, b.reshape(1, n))


def _mod_spec(off=0):
    return pl.BlockSpec((1, 1, 6, D_MODEL), lambda b, t: (b, jnp.minimum(t + off, 1), 0, 0))


def _in_proj_kernel(x_ref, mod_ref, nw_ref, w_ref, p_ref, g_ref):
    mod = mod_ref[0, 0]
    h = _rms_rows(x_ref[0]) * nw_ref[...] * (1.0 + mod[1:2]) + mod[0:1]
    r = jnp.dot(h.astype(BF16), w_ref[...], preferred_element_type=F32)
    p_ref[0] = r[:, :P_WIDTH].astype(BF16)
    g_ref[0] = r[:, P_WIDTH:]


def _in_proj(xa, mods, nw, wp):
    B, L, D = xa.shape
    row = lambda w: pl.BlockSpec((1, ROW_TILE, w), lambda b, t: (b, t, 0))
    return pl.pallas_call(
        _in_proj_kernel,
        out_shape=(jax.ShapeDtypeStruct((B, L, P_WIDTH), BF16),
                   jax.ShapeDtypeStruct((B, L, G_WIDTH), F32)),
        grid=(B, L // ROW_TILE),
        in_specs=[row(D), _mod_spec(),
                  pl.BlockSpec((1, D), lambda b, t: (0, 0)),
                  pl.BlockSpec(wp.shape, lambda b, t: (0, 0))],
        out_specs=(row(P_WIDTH), row(G_WIDTH)),
        compiler_params=_params(("parallel", "arbitrary")),
        name="in_proj",
    )(xa, mods, nw.reshape(1, D), wp)


def _scan_lanes(x, reverse):
    lane = lax.broadcasted_iota(jnp.int32, x.shape, 1)
    k = 1
    while k < M_CHUNK:
        if reverse:
            x = x + jnp.where(lane < M_CHUNK - k, pltpu.roll(x, M_CHUNK - k, axis=1), 0.0)
        else:
            x = x + jnp.where(lane >= k, pltpu.roll(x, k, axis=1), 0.0)
        k *= 2
    return x


def _mlstm_intra_kernel(qk_ref, v_ref, g_ref, bias_ref, numf_ref, numb_ref, vecf_ref, vecb_ref,
                        clf_ref, clb_ref, smf_ref, smb_ref):
    row = lax.broadcasted_iota(jnp.int32, (M_CHUNK, M_CHUNK), 0)
    col = lax.broadcasted_iota(jnp.int32, (M_CHUNK, M_CHUNK), 1)
    lane_row = lax.broadcasted_iota(jnp.int32, (1, LANES), 1)
    g = g_ref[0] + bias_ref[...]
    g_row = g.T[0:16, :]
    lf_row = jnp.minimum(g_row, 0.0) - jnp.log1p(jnp.exp(-jnp.abs(g_row)))
    pre_row, suf_row = _scan_lanes(lf_row, False), _scan_lanes(lf_row, True)
    b_cols = jnp.concatenate([pre_row, suf_row, jnp.zeros((M_CHUNK - 32, M_CHUNK), F32)], axis=0).T
    outs = ((numf_ref, vecf_ref, clf_ref, smf_ref, col <= row, pre_row, 0, M_CHUNK - 1),
            (numb_ref, vecb_ref, clb_ref, smb_ref, col >= row, suf_row, 16, 0))
    vecs = [jnp.zeros((M_CHUNK, LANES), F32), jnp.zeros((M_CHUNK, LANES), F32)]
    for h in range(M_HEADS):
        qk = qk_ref[0, :, h * LANES:(h + 1) * LANES]
        q, k = qk[:, :M_DK], qk[:, M_DK:]
        v = v_ref[0, :, h * M_DV:(h + 1) * M_DV]
        s_raw = lax.dot_general(q, k, (((1,), (1,)), ((), ())), preferred_element_type=F32)
        kf, vf = k.astype(F32), v.astype(F32)
        for d, (num_ref, _, cl_ref, sm_ref, allowed, b_rows, col0, last) in enumerate(outs):
            ic, fc = 8 * d + h, 8 * d + 4 + h
            b_r, b_c = b_rows[fc:fc + 1, :], b_cols[:, col0 + fc:col0 + fc + 1]
            i_r, i_c = g_row[ic:ic + 1, :], g[:, ic:ic + 1]
            b_end = b_r[:, last:last + 1]
            d_log = jnp.where(allowed, b_c - b_r + i_r, -jnp.inf)
            m_intra = jnp.max(d_log, axis=-1, keepdims=True)
            s = s_raw * jnp.exp(d_log - m_intra)
            num_ref[0, :, h * M_DV:(h + 1) * M_DV] = jnp.dot(s.astype(BF16), v, preferred_element_type=F32)
            den = jnp.sum(s, axis=-1, keepdims=True)
            lane = lax.broadcasted_iota(jnp.int32, (M_CHUNK, LANES), 1)
            vecs[d] = jnp.where(lane == 3 * h, den,
                                jnp.where(lane == 3 * h + 1, m_intra, jnp.where(lane == 3 * h + 2, b_c, vecs[d])))
            m_loc = jnp.max(b_end - b_r + i_r, axis=-1, keepdims=True)
            w_col = jnp.exp(b_end - b_c + i_c - m_loc)
            cl_ref[0, 0, h] = lax.dot_general(k, (vf * w_col).astype(BF16), (((0,), (0,)), ((), ())),
                                              preferred_element_type=F32)
            n_loc = jnp.sum(kf * w_col, axis=0, keepdims=True)
            small = jnp.concatenate([n_loc, jnp.zeros((1, LANES - M_DK), F32)], axis=1)
            sm_ref[0, 0, h] = jnp.where(lane_row == M_DK, m_loc, jnp.where(lane_row == M_DK + 1, b_end, small))
    vecf_ref[0] = vecs[0]
    vecb_ref[0] = vecs[1]


def _mlstm_scan_kernel(*refs, n_batch):
    ins, (hf_ref, hb_ref, ct_ref, n_ref, m_ref) = refs[:10], refs[10:]

    @pl.when(pl.program_id(0) == 0)
    def _():
        ct_ref[...] = jnp.zeros_like(ct_ref)
        n_ref[...] = jnp.zeros_like(n_ref)
        m_ref[...] = jnp.zeros_like(m_ref)

    for d, h_ref in enumerate((hf_ref, hb_ref)):
        qk_ref, num_ref, vec_ref, cl_ref, sm_ref = ins[5 * d:5 * d + 5]
        for b in range(n_batch):
            vec = vec_ref[b]
            for h in range(M_HEADS):
                idx = (d * n_batch + b) * M_HEADS + h
                q = qk_ref[b, :, h * LANES:h * LANES + M_DK]
                den_i, m_i, b_c = vec[:, 3 * h:3 * h + 1], vec[:, 3 * h + 1:3 * h + 2], vec[:, 3 * h + 2:3 * h + 3]
                small = sm_ref[b, 0, h]
                n_loc, m_loc, b_end = small[:, :M_DK], small[:, M_DK:M_DK + 1], small[:, M_DK + 1:M_DK + 2]
                m_prev = m_ref[idx][:, 0:1]
                ct = ct_ref[idx]
                n = n_ref[idx]

                inter = b_c + m_prev
                m_t = jnp.maximum(inter, m_i)
                a = jnp.exp(inter - m_t)
                e = jnp.exp(m_i - m_t)
                num = (e * num_ref[b, :, h * M_DV:(h + 1) * M_DV]
                       + a * jnp.dot(q, ct.astype(BF16), preferred_element_type=F32))
                den = e * den_i + a * jnp.sum(q.astype(F32) * n, axis=-1, keepdims=True)
                h_ref[b, :, h * M_DV:(h + 1) * M_DV] = num / jnp.maximum(jnp.abs(den), jnp.exp(-m_t))

                m_new = jnp.maximum(b_end + m_prev, m_loc)
                a_s = jnp.exp(b_end + m_prev - m_new)
                s_s = jnp.exp(m_loc - m_new)
                ct_ref[idx] = a_s * ct + s_s * cl_ref[b, 0, h]
                n_ref[idx] = a_s * n + s_s * n_loc
                m_ref[idx] = jnp.broadcast_to(m_new, (1, LANES))


def _mlstm(p, g, gate_b, ctx_chunks):
    B, L, _ = p.shape
    nc = L // M_CHUNK
    width = M_HEADS * M_DV
    bias = jnp.zeros((1, G_WIDTH), F32).at[0, :16].set(gate_b)
    tok = lambda w, cb=0: pl.BlockSpec((1, M_CHUNK, w), lambda b, c: (b, c, cb))
    per_chunk = lambda r: pl.BlockSpec((1, 1, M_HEADS, r, M_DV), lambda b, c: (b, c, 0, 0, 0))
    f32 = lambda *s: jax.ShapeDtypeStruct(s, F32)
    numf, numb, vecf, vecb, clf, clb, smf, smb = pl.pallas_call(
        _mlstm_intra_kernel,
        out_shape=(f32(B, L, width),) * 2 + (f32(B, L, LANES),) * 2
                  + (f32(B, nc, M_HEADS, M_DK, M_DV),) * 2 + (f32(B, nc, M_HEADS, 1, LANES),) * 2,
        grid=(B, nc),
        in_specs=[tok(width, P_QK // width), tok(width, P_MV // width), tok(G_WIDTH),
                  pl.BlockSpec((1, G_WIDTH), lambda b, c: (0, 0))],
        out_specs=(tok(width),) * 2 + (tok(LANES),) * 2 + (per_chunk(M_DK),) * 2 + (per_chunk(1),) * 2,
        compiler_params=_params(("parallel", "parallel")),
        name="mlstm_intra",
    )(p, p, g, bias)

    fwd = lambda j: j
    bwd = lambda j: jnp.where(j < ctx_chunks, ctx_chunks - 1 - j, nc - 1 + ctx_chunks - j)
    stok = lambda cm, w, cb=0: pl.BlockSpec((B, M_CHUNK, w), lambda j: (0, cm(j), cb))
    schunk = lambda cm, r: pl.BlockSpec((B, 1, M_HEADS, r, M_DV), lambda j: (0, cm(j), 0, 0, 0))
    side = lambda cm: [stok(cm, width, P_QK // width), stok(cm, width), stok(cm, LANES),
                       schunk(cm, M_DK), schunk(cm, 1)]
    chains = 2 * B * M_HEADS
    return pl.pallas_call(
        functools.partial(_mlstm_scan_kernel, n_batch=B),
        out_shape=(f32(B, L, width),) * 2,
        grid=(nc,),
        in_specs=side(fwd) + side(bwd),
        out_specs=(stok(fwd, width), stok(bwd, width)),
        scratch_shapes=[pltpu.VMEM((chains, M_DK, M_DV), F32),
                        pltpu.VMEM((chains, 1, M_DK), F32),
                        pltpu.VMEM((chains, 1, LANES), F32)],
        compiler_params=_params(("arbitrary",)),
        name="mlstm_scan",
    )(p, numf, vecf, clf, smf, p, numb, vecb, clb, smb)


def _head_norm_rope(x, w, cos, sin, bd):
    ms = jnp.dot(x * x, bd, precision=HIGHEST, preferred_element_type=F32)
    y = x * lax.rsqrt(ms + EPS) * w
    lane = lax.broadcasted_iota(jnp.int32, y.shape, 1)
    partner = jnp.where(lane % A_DH < A_DH // 2,
                        pltpu.roll(y, LANES - A_DH // 2, axis=1), pltpu.roll(y, A_DH // 2, axis=1))
    return y * cos + partner * sin


def _attn_prep_kernel(q_ref, kv_ref, cos_ref, sin_ref, qw_ref, kw_ref, qt_ref, k_ref, vt_ref, *, q_scale):
    r = lax.broadcasted_iota(jnp.int32, (LANES, LANES), 0) // A_DH
    c = lax.broadcasted_iota(jnp.int32, (LANES, LANES), 1) // A_DH
    bd = jnp.where(r == c, 1.0 / A_DH, 0.0).astype(F32)
    cos, sin = cos_ref[...], sin_ref[...]
    for pair in range(A_HEADS // 2):
        x = q_ref[0, :, pair * LANES:(pair + 1) * LANES].astype(F32)
        y = _head_norm_rope(x, qw_ref[...], cos, sin, bd) * q_scale
        qt_ref[0, pair * LANES:(pair + 1) * LANES, :] = y.T.astype(BF16)
    kv = kv_ref[0].astype(F32)
    k = _head_norm_rope(kv[:, :LANES], kw_ref[...], cos, sin, bd).astype(BF16)
    for kvh in range(A_KV_HEADS):
        k_ref[0, kvh, 0] = k[:, kvh * A_DH:(kvh + 1) * A_DH]
    vt = kv[:, LANES:].T.astype(BF16)
    ones = jnp.ones((VT_ROWS - A_DH, vt.shape[1]), BF16)
    for kvh in range(A_KV_HEADS):
        vt_ref[0, kvh, 0] = jnp.concatenate([vt[kvh * A_DH:(kvh + 1) * A_DH, :], ones], axis=0)


def _attn_prep(p, cos, sin, qw, kw, ts):
    B, L, _ = p.shape
    per = ts // ROW_TILE
    nblk = L // ts
    q_scale = A_DH ** -0.5 * math.log2(math.e)
    return pl.pallas_call(
        functools.partial(_attn_prep_kernel, q_scale=q_scale),
        out_shape=(jax.ShapeDtypeStruct((B, A_HEADS * A_DH, L), BF16),
                   jax.ShapeDtypeStruct((B, A_KV_HEADS, nblk, ts, A_DH), BF16),
                   jax.ShapeDtypeStruct((B, A_KV_HEADS, nblk, VT_ROWS, ts), BF16)),
        grid=(B, L // ROW_TILE),
        in_specs=[pl.BlockSpec((1, ROW_TILE, A_HEADS * A_DH), lambda b, t: (b, t, P_AQ // (A_HEADS * A_DH))),
                  pl.BlockSpec((1, ROW_TILE, 2 * LANES), lambda b, t: (b, t, P_AKV // (2 * LANES))),
                  pl.BlockSpec((ROW_TILE, LANES), lambda b, t: (t, 0)),
                  pl.BlockSpec((ROW_TILE, LANES), lambda b, t: (t, 0)),
                  pl.BlockSpec((1, LANES), lambda b, t: (0, 0)),
                  pl.BlockSpec((1, LANES), lambda b, t: (0, 0))],
        out_specs=(pl.BlockSpec((1, A_HEADS * A_DH, ROW_TILE), lambda b, t: (b, 0, t)),
                   pl.BlockSpec((1, A_KV_HEADS, 1, ROW_TILE, A_DH), lambda b, t: (b, 0, t // per, t % per, 0)),
                   pl.BlockSpec((1, A_KV_HEADS, 1, VT_ROWS, ROW_TILE), lambda b, t: (b, 0, t // per, 0, t % per))),
        compiler_params=_params(("parallel", "arbitrary")),
        name="attn_prep",
    )(p, p, cos, sin, qw, kw)


ATT_SUB = 256
VT_ROWS = A_DH + 16


def _attn_kernel(qt_ref, k_ref, vt_ref, o_ref, sa_ref, sb_ref, ma_ref, mb_ref, acc_ref,
                 *, nblk, nsub, tq, ctx_tiles, ctx_sub, q_tile0):
    qt = qt_ref[0]
    qs = [qt[g * A_DH:(g + 1) * A_DH, :] for g in range(A_GROUP)]
    n = A_GROUP * tq
    lanes = lambda g: slice(g * tq, (g + 1) * tq)
    keys = lambda r: pl.ds(r * ATT_SUB, ATT_SUB)

    def step(nxt, cur, ms, subs):
        out = []
        for g in range(A_GROUP):
            if cur is not None:
                ci, cs_ref, cm_ref = cur
                m_new = jnp.maximum(ms[g], cm_ref[:, lanes(g)])
                alpha = jnp.exp2(ms[g] - m_new)
            best, pv = None, None
            for r in range(subs):
                if nxt is not None:
                    ni, ns_ref, _ = nxt
                    s = jnp.dot(k_ref[0, 0, ni, keys(r), :], qs[g], preferred_element_type=F32)
                    ns_ref[keys(r), lanes(g)] = s
                    top = jnp.max(s, axis=0, keepdims=True)
                    best = top if best is None else jnp.maximum(best, top)
                if cur is not None:
                    p = jnp.exp2(cs_ref[keys(r), lanes(g)] - m_new)
                    d = jnp.dot(vt_ref[0, 0, ci, :, keys(r)], p.astype(BF16), preferred_element_type=F32)
                    pv = d if pv is None else pv + d
            if nxt is not None:
                nxt[2][:, lanes(g)] = best
            if cur is not None:
                acc_ref[:, lanes(g)] = alpha * acc_ref[:, lanes(g)] + pv
                out.append(m_new)
            else:
                out.append(ms[g])
        return tuple(out)

    def finish():
        o = acc_ref[0:A_DH, :] / acc_ref[A_DH:A_DH + 1, :]
        o = jnp.concatenate([o[:, lanes(g)] for g in range(A_GROUP)], axis=0)
        o_ref[0] = o.T.astype(BF16)

    acc_ref[...] = jnp.zeros_like(acc_ref)
    init = (jnp.full((1, tq), -jnp.inf, F32),) * A_GROUP
    is_ctx = pl.program_id(2) + q_tile0 < ctx_tiles
    buf_a, buf_b = (sa_ref, ma_ref), (sb_ref, mb_ref)

    @pl.when(is_ctx)
    def _():
        step((0, *buf_a), None, init, ctx_sub)
        step(None, (0, *buf_a), init, ctx_sub)
        finish()

    @pl.when(jnp.logical_not(is_ctx))
    def _():
        step((0, *buf_a), None, init, nsub)

        def pair(j, ms):
            i = 2 * j
            ms = step((i + 1, *buf_b), (i, *buf_a), ms, nsub)
            return step((i + 2, *buf_a), (i + 1, *buf_b), ms, nsub)

        ms = lax.fori_loop(0, (nblk - 1) // 2, pair, init)
        step(None, (nblk - 1, *buf_a), ms, nsub)
        finish()


def _attention(qt, k, vt, *, q_tile0, n_ctx, tq):
    B, _, L = qt.shape
    nblk, ts = k.shape[2], k.shape[3]
    assert n_ctx <= ts and n_ctx % tq == 0 and n_ctx % ATT_SUB == 0 and ts % ATT_SUB == 0 and nblk % 2 == 1
    width = A_GROUP * A_DH
    n = A_GROUP * tq
    s_buf, m_buf = pltpu.VMEM((ts, n), F32), pltpu.VMEM((1, n), F32)
    return pl.pallas_call(
        functools.partial(_attn_kernel, nblk=nblk, nsub=ts // ATT_SUB, tq=tq, ctx_tiles=n_ctx // tq,
                          ctx_sub=n_ctx // ATT_SUB, q_tile0=q_tile0),
        out_shape=jax.ShapeDtypeStruct((B, L - q_tile0 * tq, A_HEADS * A_DH), BF16),
        grid=(B, A_KV_HEADS, L // tq - q_tile0),
        in_specs=[pl.BlockSpec((1, width, tq), lambda b, kv, t: (b, kv, t + q_tile0)),
                  pl.BlockSpec((1, 1, nblk, ts, A_DH), lambda b, kv, t: (b, kv, 0, 0, 0)),
                  pl.BlockSpec((1, 1, nblk, VT_ROWS, ts), lambda b, kv, t: (b, kv, 0, 0, 0))],
        out_specs=pl.BlockSpec((1, tq, width), lambda b, kv, t: (b, t, kv)),
        scratch_shapes=[s_buf, s_buf, m_buf, m_buf, pltpu.VMEM((VT_ROWS, n), F32)],
        compiler_params=_params(("parallel", "parallel", "arbitrary")),
        name="attention",
    )(qt, k, vt)


def _mixer_out_kernel(*refs, with_router):
    (hf_ref, hb_ref, mo_ref, a_ref, x_ref, mod_ref, mnw_ref, n2w_ref, wout_ref) = refs[:9]
    hs = hf_ref[0] + hb_ref[0]
    hn = jnp.concatenate([_rms_rows(hs[:, h * M_DV:(h + 1) * M_DV]) for h in range(M_HEADS)], axis=1)
    m = hn * mnw_ref[...] * _sigmoid(mo_ref[0].astype(F32))
    y_in = jnp.concatenate([m.astype(BF16), a_ref[0]], axis=1)
    mod = mod_ref[0, 0]
    x1 = x_ref[0] + mod[2:3] * jnp.dot(y_in, wout_ref[...], preferred_element_type=F32)
    h2 = _rms_rows(x1) * n2w_ref[...] * (1.0 + mod[4:5]) + mod[3:4]
    if not with_router:
        x1_ref, h2_ref = refs[9:]
        x1_ref[0] = x1
        h2_ref[0] = h2.astype(BF16)
        return
    router_ref, x1_ref, h2_ref, ids_ref, gates_ref = refs[9:]
    x1_ref[0] = x1
    h2_ref[0] = h2
    h_hi = h2.astype(BF16)
    h_lo = (h2 - h_hi.astype(F32)).astype(BF16)
    logits = (jnp.dot(h_hi, router_ref[0], preferred_element_type=F32)
              + jnp.dot(h_lo, router_ref[0], preferred_element_type=F32)
              + jnp.dot(h_hi, router_ref[1], preferred_element_type=F32))
    lane = lax.broadcasted_iota(jnp.int32, logits.shape, 1)
    logits = jnp.where(lane < N_EXPERTS, logits, -jnp.inf)
    m1 = jnp.max(logits, axis=-1, keepdims=True)
    i1 = jnp.min(jnp.where(logits == m1, lane, LANES), axis=-1, keepdims=True)
    rest = jnp.where(lane == i1, -jnp.inf, logits)
    m2 = jnp.max(rest, axis=-1, keepdims=True)
    i2 = jnp.min(jnp.where(rest == m2, lane, LANES), axis=-1, keepdims=True)
    e2 = jnp.exp(m2 - m1)
    g1 = 1.0 / (1.0 + e2)
    ids_ref[0] = jnp.where(lane == 0, i1, jnp.where(lane == 1, i2, -1))
    gates_ref[0] = jnp.where(lane == 0, g1, jnp.where(lane == 1, e2 * g1, 0.0))


def _mixer_out(hf, hb, p, a, xa, mods, mnw, n2w, wout, router=None, row_off=0):
    B, L, D = xa.shape
    nt = L // ROW_TILE - row_off
    rin = lambda w, cb=0: pl.BlockSpec((1, ROW_TILE, w), lambda b, t: (b, t + row_off, cb))
    rout = lambda w: pl.BlockSpec((1, ROW_TILE, w), lambda b, t: (b, t, 0))
    full = lambda arr: pl.BlockSpec(arr.shape, lambda b, t: (0,) * arr.ndim)
    mw = M_HEADS * M_DV
    a_off = row_off - (L - a.shape[1]) // ROW_TILE
    a_spec = pl.BlockSpec((1, ROW_TILE, A_HEADS * A_DH), lambda b, t: (b, t + a_off, 0))
    in_specs = [rin(mw), rin(mw), rin(mw, P_MO // mw), a_spec, rin(D), _mod_spec(row_off),
                pl.BlockSpec((1, mw), lambda b, t: (0, 0)), pl.BlockSpec((1, D), lambda b, t: (0, 0)),
                full(wout)]
    args = [hf, hb, p, a, xa, mods, mnw.reshape(1, mw), n2w.reshape(1, D), wout]
    rows = nt * ROW_TILE
    out_shape = [jax.ShapeDtypeStruct((B, rows, D), F32),
                 jax.ShapeDtypeStruct((B, rows, D), BF16 if router is None else F32)]
    out_specs = [rout(D), rout(D)]
    if router is not None:
        in_specs.append(full(router))
        args.append(router)
        out_shape += [jax.ShapeDtypeStruct((B, rows, LANES), jnp.int32), jax.ShapeDtypeStruct((B, rows, LANES), F32)]
        out_specs += [rout(LANES), rout(LANES)]
    return pl.pallas_call(
        functools.partial(_mixer_out_kernel, with_router=router is not None),
        out_shape=tuple(out_shape),
        grid=(B, nt),
        in_specs=in_specs,
        out_specs=tuple(out_specs),
        compiler_params=_params(("parallel", "arbitrary")),
        name="mixer_out",
    )(*args)


def _ffn_kernel(h_ref, x_ref, mod_ref, wg_ref, wu_ref, wd_ref, o_ref):
    h = h_ref[0]
    g = jnp.dot(h, wg_ref[...], preferred_element_type=F32)
    u = jnp.dot(h, wu_ref[...], preferred_element_type=F32)
    act = (g * _sigmoid(g) * u).astype(BF16)
    y = jnp.dot(act, wd_ref[...], preferred_element_type=F32)
    o_ref[0] = x_ref[0] + mod_ref[0, 0][5:6] * y


def _ffn(h2, x1, mods, wg, wu, wd):
    B, L, D = x1.shape
    row = pl.BlockSpec((1, ROW_TILE, D), lambda b, t: (b, t, 0))
    const = lambda arr: pl.BlockSpec(arr.shape, lambda b, t: (0, 0), pipeline_mode=pl.Buffered(1))
    return pl.pallas_call(
        _ffn_kernel,
        out_shape=jax.ShapeDtypeStruct((B, L, D), F32),
        grid=(B, L // ROW_TILE),
        in_specs=[row, row, _mod_spec(), const(wg), const(wu), const(wd)],
        out_specs=row,
        compiler_params=_params(("parallel", "arbitrary")),
        name="ffn_swiglu",
    )(h2, x1, mods, wg, wu, wd)


MOE_TM = 512
MOE_FF = 1792
MOE_TD = 256
RANK_TILE = 512


def _moe_rank_kernel(ids_ref, rank_ref, cnt_ref, carry_ref):
    @pl.when(pl.program_id(0) == 0)
    def _():
        carry_ref[...] = jnp.zeros_like(carry_ref)

    ids = ids_ref[...]
    lane = lax.broadcasted_iota(jnp.int32, ids.shape, 1)
    onehot = jnp.where(jnp.logical_or(lane == ids[:, 0:1], lane == ids[:, 1:2]), 1.0, 0.0)
    r = lax.broadcasted_iota(jnp.int32, (RANK_TILE, RANK_TILE), 0)
    c = lax.broadcasted_iota(jnp.int32, (RANK_TILE, RANK_TILE), 1)
    before = jnp.where(c < r, 1.0, 0.0).astype(BF16)
    rank_ref[...] = jnp.dot(before, onehot.astype(BF16), preferred_element_type=F32) + carry_ref[...]
    carry_ref[...] += jnp.sum(onehot, axis=0, keepdims=True)
    cnt_ref[...] = carry_ref[...]


def _moe_rank(ids):
    n = ids.shape[0]
    return pl.pallas_call(
        _moe_rank_kernel,
        out_shape=(jax.ShapeDtypeStruct((n, LANES), F32), jax.ShapeDtypeStruct((1, LANES), F32)),
        grid=(n // RANK_TILE,),
        in_specs=[pl.BlockSpec((RANK_TILE, LANES), lambda t: (t, 0))],
        out_specs=(pl.BlockSpec((RANK_TILE, LANES), lambda t: (t, 0)), pl.BlockSpec((1, LANES), lambda t: (0, 0))),
        scratch_shapes=[pltpu.VMEM((1, LANES), F32)],
        compiler_params=_params(("arbitrary",)),
        name="moe_rank",
    )(ids)


def _moe_pos_kernel(ids_ref, rank_ref, start_ref, pos_ref):
    ids = ids_ref[...]
    lane = lax.broadcasted_iota(jnp.int32, ids.shape, 1)
    tgt = start_ref[...] + rank_ref[...]
    p0 = jnp.sum(jnp.where(lane == ids[:, 0:1], tgt, 0.0), axis=-1, keepdims=True)
    p1 = jnp.sum(jnp.where(lane == ids[:, 1:2], tgt, 0.0), axis=-1, keepdims=True)
    pos_ref[...] = jnp.where(lane == 0, p0, jnp.where(lane == 1, p1, 0.0)).astype(jnp.int32)


def _moe_pos(ids, rank, start_row):
    n = ids.shape[0]
    blk = pl.BlockSpec((RANK_TILE, LANES), lambda t: (t, 0))
    return pl.pallas_call(
        _moe_pos_kernel,
        out_shape=jax.ShapeDtypeStruct((n, LANES), jnp.int32),
        grid=(n // RANK_TILE,),
        in_specs=[blk, blk, pl.BlockSpec((1, LANES), lambda t: (0, 0))],
        out_specs=blk,
        compiler_params=_params(("parallel",)),
        name="moe_pos",
    )(ids, rank, start_row)


def _row_copy(src, src_row, dst, dst_row, sem):
    return pltpu.make_async_copy(src.at[pl.ds(src_row, 1), :], dst.at[pl.ds(dst_row, 1), :], sem)


def _moe_dispatch_kernel(pad_ref, pos_ref, h_ref, xs_ref, zero_ref, sem):
    @pl.when(pl.program_id(0) == 0)
    def _():
        zero_ref[...] = jnp.zeros_like(zero_ref)
        fills = [pltpu.make_async_copy(
            zero_ref, xs_ref.at[pl.ds(pl.multiple_of(pad_ref[e], SUBLANES), MOE_TM + SUBLANES), :], sem)
            for e in range(N_EXPERTS)]
        for cp in fills:
            cp.start()
        for cp in fills:
            cp.wait()

        def fill_tile(j, carry):
            cp = pltpu.make_async_copy(zero_ref.at[pl.ds(0, MOE_TM), :],
                                       xs_ref.at[pl.ds(pl.multiple_of(j * MOE_TM, MOE_TM), MOE_TM), :], sem)
            cp.start()
            cp.wait()
            return carry

        lax.fori_loop(pad_ref[N_EXPERTS], xs_ref.shape[0] // MOE_TM, fill_tile, 0)

    def issue(r, carry):
        for k in range(2):
            _row_copy(h_ref, r, xs_ref, pos_ref[0, 0, 2 * r + k], sem).start()
        return carry

    lax.fori_loop(0, MOE_TD, issue, 0)
    for k in range(2):
        pltpu.make_async_copy(h_ref, xs_ref.at[pl.ds(0, MOE_TD), :], sem).wait()


def _moe_dispatch(h, pos, fill_meta, ns):
    n, d = h.shape
    return pl.pallas_call(
        _moe_dispatch_kernel,
        out_shape=jax.ShapeDtypeStruct((ns, d), F32),
        grid_spec=pltpu.PrefetchScalarGridSpec(
            num_scalar_prefetch=1,
            grid=(n // MOE_TD,),
            in_specs=[pl.BlockSpec((1, 1, 2 * MOE_TD), lambda t, pad: (t, 0, 0), memory_space=pltpu.SMEM),
                      pl.BlockSpec((MOE_TD, d), lambda t, pad: (t, 0))],
            out_specs=pl.BlockSpec(memory_space=pl.ANY),
            scratch_shapes=[pltpu.VMEM((MOE_TM + SUBLANES, d), F32), pltpu.SemaphoreType.DMA(())]),
        compiler_params=_params(("arbitrary",)),
        name="moe_dispatch",
    )(fill_meta, pos, h)


def _moe_group_kernel(te_ref, nv_ref, xs_ref, wg_ref, wu_ref, wd_ref, ys_ref, acc_ref):
    i, f = pl.program_id(0), pl.program_id(1)
    last = pl.num_programs(1) - 1
    valid = i < nv_ref[0]

    @pl.when(valid)
    def _():
        @pl.when(f == 0)
        def _():
            acc_ref[...] = jnp.zeros_like(acc_ref)

        x = xs_ref[...].astype(BF16)
        g = jnp.dot(x, wg_ref[0], preferred_element_type=F32)
        u = jnp.dot(x, wu_ref[0], preferred_element_type=F32)
        act = (g * _sigmoid(g) * u).astype(BF16)
        acc_ref[...] += jnp.dot(act, wd_ref[0], preferred_element_type=F32)

        @pl.when(f == last)
        def _():
            ys_ref[...] = acc_ref[...]

    @pl.when(jnp.logical_and(jnp.logical_not(valid), f == last))
    def _():
        ys_ref[...] = jnp.zeros_like(ys_ref)


def _moe_group(xs, tile_expert, n_valid, wg, wu, wd):
    ns, d = xs.shape
    n_tiles = ns // MOE_TM - 1
    ff = wg.shape[2]
    live = lambda i, nv: i < nv[0]
    return pl.pallas_call(
        _moe_group_kernel,
        out_shape=jax.ShapeDtypeStruct((n_tiles * MOE_TM, d), F32),
        grid_spec=pltpu.PrefetchScalarGridSpec(
            num_scalar_prefetch=2,
            grid=(n_tiles, ff // MOE_FF),
            in_specs=[pl.BlockSpec((MOE_TM, d), lambda i, f, te, nv: (jnp.where(live(i, nv), i, 0), 0)),
                      pl.BlockSpec((1, d, MOE_FF), lambda i, f, te, nv: (te[i], 0, f)),
                      pl.BlockSpec((1, d, MOE_FF), lambda i, f, te, nv: (te[i], 0, f)),
                      pl.BlockSpec((1, MOE_FF, d), lambda i, f, te, nv: (te[i], f, 0))],
            out_specs=pl.BlockSpec((MOE_TM, d), lambda i, f, te, nv: (i, 0)),
            scratch_shapes=[pltpu.VMEM((MOE_TM, d), F32)]),
        compiler_params=_params(("arbitrary", "arbitrary")),
        name="moe_group",
    )(tile_expert, n_valid, xs, wg, wu, wd)


def _moe_combine_kernel(pos_ref, x_ref, gates_ref, mod_ref, fw_ref, ys_ref, o_ref, ybuf, sem):
    def issue(r, carry):
        for k in range(2):
            _row_copy(ys_ref, pos_ref[0, 0, 2 * r + k], ybuf.at[k], r, sem).start()
        return carry

    lax.fori_loop(0, MOE_TD, issue, 0)
    for k in range(2):
        pltpu.make_async_copy(ys_ref.at[pl.ds(0, MOE_TD), :], ybuf.at[k], sem).wait()
    gates = gates_ref[...]
    y = gates[:, 0:1] * ybuf[0] + gates[:, 1:2] * ybuf[1]
    x2 = x_ref[...] + mod_ref[0, 0][5:6] * y
    o_ref[...] = _rms_rows(x2) * fw_ref[...]


def _moe_combine(pos, x1, gates, mods, fw, ys, tokens_per_sample):
    n, d = x1.shape
    per = tokens_per_sample // MOE_TD
    return pl.pallas_call(
        _moe_combine_kernel,
        out_shape=jax.ShapeDtypeStruct((n, d), F32),
        grid=(n // MOE_TD,),
        in_specs=[pl.BlockSpec((1, 1, 2 * MOE_TD), lambda t: (t, 0, 0), memory_space=pltpu.SMEM),
                  pl.BlockSpec((MOE_TD, d), lambda t: (t, 0)),
                  pl.BlockSpec((MOE_TD, LANES), lambda t: (t, 0)),
                  pl.BlockSpec((1, 1, 6, d), lambda t: (t // per, 1, 0, 0)),
                  pl.BlockSpec((1, d), lambda t: (0, 0)),
                  pl.BlockSpec(memory_space=pl.ANY)],
        out_specs=pl.BlockSpec((MOE_TD, d), lambda t: (t, 0)),
        scratch_shapes=[pltpu.VMEM((2, MOE_TD, d), F32), pltpu.SemaphoreType.DMA(())],
        compiler_params=_params(("arbitrary",)),
        name="moe_combine",
    )(pos, x1, gates, mods, fw.reshape(1, d), ys)


def _moe(h2, ids, gates, x1, mods, wg, wu, wd, fw):
    B, T, D = x1.shape
    n = B * T
    ids, gates = ids.reshape(n, LANES), gates.reshape(n, LANES)
    rank, cnt = _moe_rank(ids)
    cnt = cnt[0, :N_EXPERTS].astype(jnp.int32)
    padded = (cnt + MOE_TM - 1) // MOE_TM * MOE_TM
    end = jnp.cumsum(padded)
    start = end - padded
    n_tiles = 2 * n // MOE_TM + N_EXPERTS
    tile_expert = jnp.minimum(jnp.sum(jnp.arange(n_tiles)[:, None] >= (end // MOE_TM)[None, :], axis=1),
                              N_EXPERTS - 1).astype(jnp.int32)
    n_valid = (end[-1:] // MOE_TM).astype(jnp.int32)
    start_row = jnp.zeros((1, LANES), F32).at[0, :N_EXPERTS].set(start.astype(F32))
    pos = _moe_pos(ids, rank, start_row)
    pos = pos[:, :2].reshape(n // MOE_TD, 1, 2 * MOE_TD)
    fill_meta = jnp.concatenate([(start + cnt) // SUBLANES * SUBLANES, n_valid]).astype(jnp.int32)
    xs = _moe_dispatch(h2.reshape(n, D), pos, fill_meta, (n_tiles + 1) * MOE_TM)
    ys = _moe_group(xs, tile_expert, n_valid, wg, wu, wd)
    return _moe_combine(pos, x1.reshape(n, D), gates, mods, fw, ys, T).reshape(B, T, D)


_ROT_PERM = np.concatenate([np.arange(0, A_DH, 2), np.arange(1, A_DH, 2)])


def _prep_w_in(w):
    o = np.cumsum([0, M_HEADS * M_DK, M_HEADS * M_DK, M_HEADS * M_DV, M_HEADS * M_DV, 4 * M_HEADS,
                   A_HEADS * A_DH, A_KV_HEADS * A_DH, A_KV_HEADS * A_DH])
    mq, mk, mv, mo, mg, aq, ak, av = [w[:, o[i]:o[i + 1]] for i in range(8)]
    qk = jnp.concatenate([jnp.concatenate([mq[:, h * M_DK:(h + 1) * M_DK] * (M_DK ** -0.5),
                                           mk[:, h * M_DK:(h + 1) * M_DK]], axis=1) for h in range(M_HEADS)], axis=1)
    perm_q = np.concatenate([h * A_DH + _ROT_PERM for h in range(A_HEADS)])
    perm_k = np.concatenate([h * A_DH + _ROT_PERM for h in range(A_KV_HEADS)])
    pad = jnp.zeros((w.shape[0], G_WIDTH - 4 * M_HEADS), w.dtype)
    return jnp.concatenate([qk, mv, mo, aq[:, perm_q], ak[:, perm_k], av, mg, pad], axis=1).astype(BF16)


def _rope_tables(n_tok, n_ctx):
    rows = n_tok // GRID_W
    row = jnp.broadcast_to(jnp.arange(rows, dtype=F32)[:, None], (rows, GRID_W)).reshape(n_tok)
    col = jnp.broadcast_to(jnp.arange(GRID_W, dtype=F32)[None, :], (rows, GRID_W)).reshape(n_tok)
    n_freq = A_DH // 4
    inv_freq = ROPE_THETA ** (-jnp.arange(n_freq, dtype=F32) / n_freq)
    ang = jnp.concatenate([row[:, None] * inv_freq, col[:, None] * inv_freq], axis=-1)
    cos, sin = jnp.cos(ang), jnp.sin(ang)
    cos = jnp.concatenate([jnp.ones((n_ctx, A_DH // 2), F32), cos], axis=0)
    sin = jnp.concatenate([jnp.zeros((n_ctx, A_DH // 2), F32), sin], axis=0)
    return jnp.tile(cos, (1, 4)), jnp.tile(jnp.concatenate([-sin, sin], axis=1), (1, 2))


def kernel(x, c, ctx, c_ctx, ada_w, ada_b, norm1_w, norm2_w, w_in, mlstm_gate_b, mlstm_norm_w, q_norm_w, k_norm_w,
           w_out, ffn_w_gate, ffn_w_up, ffn_w_down, moe_router, moe_w_gate, moe_w_up, moe_w_down, final_norm_w):
    B, T, D = x.shape
    n_ctx = ctx.shape[1]
    L = n_ctx + T
    depth = w_in.shape[0]
    assert D == D_MODEL and n_ctx == ROW_TILE and T % RANK_TILE == 0 and depth == 2
    ts = 3 * ROW_TILE
    assert L % ts == 0
    ctx_tiles = n_ctx // ROW_TILE
    tq = ROW_TILE

    xa = jnp.concatenate([ctx, x], axis=1)
    cvec = jnp.concatenate([c, c_ctx[None], jnp.zeros((8 - B - 1, D), F32)], axis=0)
    cos, sin = _rope_tables(T, n_ctx)
    out = None
    for i in range(depth):
        last = i == depth - 1
        modraw = _ada(cvec, ada_w[i], ada_b[i])
        mods = jnp.stack([jnp.broadcast_to(modraw[B].reshape(1, 6, D), (B, 6, D)),
                          modraw[:B].reshape(B, 6, D)], axis=1)
        p, g = _in_proj(xa, mods, norm1_w[i], _prep_w_in(w_in[i]))
        hf, hb = _mlstm(p, g, mlstm_gate_b[i], n_ctx // M_CHUNK)
        qw = jnp.tile(q_norm_w[i][_ROT_PERM], 2).reshape(1, LANES)
        kw = jnp.tile(k_norm_w[i][_ROT_PERM], 2).reshape(1, LANES)
        qt, k, vt = _attn_prep(p, cos, sin, qw, kw, ts)
        a = _attention(qt, k, vt, q_tile0=ctx_tiles if last else 0, n_ctx=n_ctx, tq=tq)
        wout = w_out[i].astype(BF16)
        if not last:
            x1, h2 = _mixer_out(hf, hb, p, a, xa, mods, mlstm_norm_w[i], norm2_w[i], wout)
            j = i // 2
            xa = _ffn(h2, x1, mods, ffn_w_gate[j].astype(BF16), ffn_w_up[j].astype(BF16),
                      ffn_w_down[j].astype(BF16))
        else:
            j = i // 2
            router = jnp.zeros((D, LANES), F32).at[:, :N_EXPERTS].set(moe_router[j])
            router_hi = router.astype(BF16)
            router = jnp.stack([router_hi, (router - router_hi.astype(F32)).astype(BF16)])
            x1, h2, ids, gates = _mixer_out(hf, hb, p, a, xa, mods, mlstm_norm_w[i], norm2_w[i], wout,
                                            router=router, row_off=ctx_tiles)
            out = _moe(h2, ids, gates, x1, mods, moe_w_gate[j].astype(BF16), moe_w_up[j].astype(BF16),
                       moe_w_down[j].astype(BF16), final_norm_w)
    return out
```

```python
import functools
import math

import numpy as np
import jax
import jax.numpy as jnp
from jax import lax
from jax.experimental import pallas as pl
from jax.experimental.pallas import tpu as pltpu

F32 = jnp.float32
BF16 = jnp.bfloat16
HIGHEST = lax.Precision.HIGHEST

D_MODEL = 1024
GRID_W = 64
M_HEADS = 4
M_DV = 128
M_DK = 64
M_CHUNK = 128
A_HEADS = 8
A_KV_HEADS = 2
A_GROUP = A_HEADS // A_KV_HEADS
A_DH = 64
ROPE_THETA = 10000.0
N_EXPERTS = 8
EPS = 1e-6

LANES = 128
SUBLANES = 8
ROW_TILE = 256
VMEM_LIMIT = 56 * 1024 * 1024

P_QK = 0
P_MV = 512
P_MO = 1024
P_AQ = 1536
P_AKV = 2048
P_WIDTH = 2304
G_WIDTH = LANES


def _params(sem, vmem=VMEM_LIMIT):
    return pltpu.CompilerParams(dimension_semantics=sem, vmem_limit_bytes=vmem)


def _sigmoid(x):
    return 1.0 / (1.0 + jnp.exp(-x))


def _rms_rows(x):
    return x * lax.rsqrt(jnp.mean(x * x, axis=-1, keepdims=True) + EPS)


def _ada_kernel(c_ref, w_ref, b_ref, o_ref):
    c = c_ref[...]
    s = c * _sigmoid(c)
    o_ref[...] = jnp.dot(s, w_ref[...], precision=HIGHEST, preferred_element_type=F32) + b_ref[...]


def _ada(cvec, w, b):
    n = w.shape[1]
    bn = 1536
    return pl.pallas_call(
        _ada_kernel,
        out_shape=jax.ShapeDtypeStruct((cvec.shape[0], n), F32),
        grid=(n // bn,),
        in_specs=[pl.BlockSpec(cvec.shape, lambda j: (0, 0)),
                  pl.BlockSpec((w.shape[0], bn), lambda j: (0, j)),
                  pl.BlockSpec((1, bn), lambda j: (0, j))],
        out_specs=pl.BlockSpec((cvec.shape[0], bn), lambda j: (0, j)),
        compiler_params=_params(("arbitrary",)),
        name="ada_mod",
    )(cvec, w, b.reshape(1, n))


def _mod_spec(off=0):
    return pl.BlockSpec((1, 1, 6, D_MODEL), lambda b, t: (b, jnp.minimum(t + off, 1), 0, 0))


def _in_proj_kernel(x_ref, mod_ref, nw_ref, w_ref, p_ref, g_ref):
    mod = mod_ref[0, 0]
    h = _rms_rows(x_ref[0]) * nw_ref[...] * (1.0 + mod[1:2]) + mod[0:1]
    r = jnp.dot(h.astype(BF16), w_ref[...], preferred_element_type=F32)
    p_ref[0] = r[:, :P_WIDTH].astype(BF16)
    g_ref[0] = r[:, P_WIDTH:]


def _in_proj(xa, mods, nw, wp):
    B, L, D = xa.shape
    row = lambda w: pl.BlockSpec((1, ROW_TILE, w), lambda b, t: (b, t, 0))
    return pl.pallas_call(
        _in_proj_kernel,
        out_shape=(jax.ShapeDtypeStruct((B, L, P_WIDTH), BF16),
                   jax.ShapeDtypeStruct((B, L, G_WIDTH), F32)),
        grid=(B, L // ROW_TILE),
        in_specs=[row(D), _mod_spec(),
                  pl.BlockSpec((1, D), lambda b, t: (0, 0)),
                  pl.BlockSpec(wp.shape, lambda b, t: (0, 0))],
        out_specs=(row(P_WIDTH), row(G_WIDTH)),
        compiler_params=_params(("parallel", "arbitrary")),
        name="in_proj",
    )(xa, mods, nw.reshape(1, D), wp)


C_ROWS = M_DV + 16
VEC_ROWS = 24
INTRA_CHUNKS = 2

def _scan_lanes(x, reverse):
    lane = lax.broadcasted_iota(jnp.int32, x.shape, 1)
    k = 1
    while k < M_CHUNK:
        if reverse:
            x = x + jnp.where(lane < M_CHUNK - k, pltpu.roll(x, M_CHUNK - k, axis=1), 0.0)
        else:
            x = x + jnp.where(lane >= k, pltpu.roll(x, k, axis=1), 0.0)
        k *= 2
    return x


def _mlstm_intra_kernel(*refs):
    for c in range(INTRA_CHUNKS):
        _mlstm_intra_chunk(c, *refs)


def _mlstm_intra_chunk(c, qk_ref, v_ref, g_ref, bias_ref, numf_ref, numb_ref, vecf_ref, vecb_ref,
                       clf_ref, clb_ref):
    toks = pl.ds(c * M_CHUNK, M_CHUNK)
    row = lax.broadcasted_iota(jnp.int32, (M_CHUNK, M_CHUNK), 0)
    col = lax.broadcasted_iota(jnp.int32, (M_CHUNK, M_CHUNK), 1)
    g = g_ref[0, toks, :] + bias_ref[...]
    g_row = g.T[0:16, :]
    lf_row = jnp.minimum(g_row, 0.0) - jnp.log1p(jnp.exp(-jnp.abs(g_row)))
    scans = (_scan_lanes(lf_row, False), _scan_lanes(lf_row, True))
    gate_row = lax.broadcasted_iota(jnp.int32, (16, M_CHUNK), 0)
    gaps = g_row - pltpu.roll(jnp.where(gate_row < 8, scans[0], scans[1]), 12, axis=0)
    gap_cols = jnp.concatenate([gaps, jnp.zeros((M_CHUNK - 16, M_CHUNK), F32)], axis=0).T
    outs = ((numf_ref, vecf_ref, clf_ref, row <= col, M_CHUNK - 1),
            (numb_ref, vecb_ref, clb_ref, row >= col, 0))
    for vec_ref in (vecf_ref, vecb_ref):
        vec_ref[0, c,12 + 2 * M_HEADS:VEC_ROWS, :] = jnp.zeros((VEC_ROWS - 12 - 2 * M_HEADS, LANES), F32)
    tail_row = lax.broadcasted_iota(jnp.int32, (C_ROWS - M_DV, M_CHUNK), 0)
    for h in range(M_HEADS):
        qk = qk_ref[0, toks, h * LANES:(h + 1) * LANES]
        q, k = qk[:, :M_DK], qk[:, M_DK:]
        vt = v_ref[0, toks, h * M_DV:(h + 1) * M_DV].astype(F32).T
        vt_bf = vt.astype(BF16)
        s_raw = lax.dot_general(k, q, (((1,), (1,)), ((), ())), preferred_element_type=F32)
        for d, (num_ref, vec_ref, cl_ref, allowed, last) in enumerate(outs):
            b_r = scans[d][8 * d + 4 + h:8 * d + 5 + h, :]
            i_r = g_row[8 * d + h:8 * d + h + 1, :]
            j = 8 * d + h
            b_end = b_r[:, last:last + 1]
            d_log = jnp.where(allowed, b_r + gap_cols[:, j:j + 1], -jnp.inf)
            m_intra = jnp.max(d_log, axis=0, keepdims=True)
            s = s_raw * jnp.exp(d_log - m_intra)
            num_ref[0, c, h] = jnp.dot(vt_bf, s.astype(BF16), preferred_element_type=F32)
            vec_ref[0, c,3 * h:3 * h + 1, :] = jnp.sum(s, axis=0, keepdims=True)
            vec_ref[0, c,3 * h + 1:3 * h + 2, :] = m_intra
            vec_ref[0, c,3 * h + 2:3 * h + 3, :] = b_r
            w_log = b_end - b_r + i_r
            m_loc = jnp.max(w_log, axis=-1, keepdims=True)
            w_row = jnp.exp(w_log - m_loc)
            vw = jnp.concatenate([vt * w_row, jnp.where(tail_row == 0, w_row, 0.0)], axis=0).astype(BF16)
            cl_ref[0, c, h] = jnp.dot(vw, k, preferred_element_type=F32)
            vec_ref[0, c,12 + 2 * h:13 + 2 * h, :] = jnp.broadcast_to(m_loc, (1, LANES))
            vec_ref[0, c,13 + 2 * h:14 + 2 * h, :] = jnp.broadcast_to(b_end, (1, LANES))


def _mlstm_scan_kernel(*refs, n_batch):
    ins, (hf_ref, hb_ref, cn_ref, m_ref) = refs[:8], refs[8:]

    @pl.when(pl.program_id(0) == 0)
    def _():
        cn_ref[...] = jnp.zeros_like(cn_ref)
        m_ref[...] = jnp.zeros_like(m_ref)

    for d, h_ref in enumerate((hf_ref, hb_ref)):
        qk_ref, num_ref, vec_ref, cl_ref = ins[4 * d:4 * d + 4]
        for b in range(n_batch):
            for h in range(M_HEADS):
                idx = (d * n_batch + b) * M_HEADS + h
                q = qk_ref[b, :, h * LANES:h * LANES + M_DK]
                row = lambda r: vec_ref[b, 0, r:r + 1, :]
                den_i, m_i, b_r = row(3 * h), row(3 * h + 1), row(3 * h + 2)
                m_loc, b_end = row(12 + 2 * h), row(13 + 2 * h)
                m_prev = m_ref[idx]
                cn = cn_ref[idx]

                inter = b_r + m_prev
                m_t = jnp.maximum(inter, m_i)
                a = jnp.exp(inter - m_t)
                e = jnp.exp(m_i - m_t)
                cq = lax.dot_general(cn.astype(BF16), q, (((1,), (1,)), ((), ())),
                                     preferred_element_type=F32)
                den = e * den_i + a * cq[M_DV:M_DV + 1, :]
                scale = 1.0 / jnp.maximum(jnp.abs(den), jnp.exp(-m_t))
                ht = (e * num_ref[b, 0, h] + a * cq[0:M_DV, :]) * scale
                h_ref[b, :, h * M_DV:(h + 1) * M_DV] = ht.T

                m_new = jnp.maximum(b_end + m_prev, m_loc)
                a_s = jnp.exp(b_end + m_prev - m_new)
                s_s = jnp.exp(m_loc - m_new)
                cn_ref[idx] = a_s[:, :M_DK] * cn + s_s[:, :M_DK] * cl_ref[b, 0, h]
                m_ref[idx] = m_new


def _mlstm(p, g, gate_b, ctx_chunks):
    B, L, _ = p.shape
    nc = L // M_CHUNK
    width = M_HEADS * M_DV
    bias = jnp.zeros((1, G_WIDTH), F32).at[0, :16].set(gate_b)
    assert nc % INTRA_CHUNKS == 0
    tok = lambda w, cb=0: pl.BlockSpec((1, INTRA_CHUNKS * M_CHUNK, w), lambda b, c: (b, c, cb))
    num_shape, vec_shape, cl_shape = (M_HEADS, M_DV, M_CHUNK), (VEC_ROWS, LANES), (M_HEADS, C_ROWS, M_DK)
    per_chunk = lambda s: pl.BlockSpec((1, INTRA_CHUNKS) + s, lambda b, c: (b, c) + (0,) * len(s))
    f32 = lambda *s: jax.ShapeDtypeStruct(s, F32)
    numf, numb, vecf, vecb, clf, clb = pl.pallas_call(
        _mlstm_intra_kernel,
        out_shape=(f32(B, nc, *num_shape),) * 2 + (f32(B, nc, *vec_shape),) * 2 + (f32(B, nc, *cl_shape),) * 2,
        grid=(B, nc // INTRA_CHUNKS),
        in_specs=[tok(width, P_QK // width), tok(width, P_MV // width), tok(G_WIDTH),
                  pl.BlockSpec((1, G_WIDTH), lambda b, c: (0, 0))],
        out_specs=(per_chunk(num_shape),) * 2 + (per_chunk(vec_shape),) * 2 + (per_chunk(cl_shape),) * 2,
        compiler_params=_params(("parallel", "parallel")),
        name="mlstm_intra",
    )(p, p, g, bias)

    fwd = lambda j: j
    bwd = lambda j: jnp.where(j < ctx_chunks, ctx_chunks - 1 - j, nc - 1 + ctx_chunks - j)
    stok = lambda cm, w, cb=0: pl.BlockSpec((B, M_CHUNK, w), lambda j: (0, cm(j), cb))
    schunk = lambda cm, s: pl.BlockSpec((B, 1) + s, lambda j: (0, cm(j)) + (0,) * len(s))
    side = lambda cm: [stok(cm, width, P_QK // width), schunk(cm, num_shape), schunk(cm, vec_shape),
                       schunk(cm, cl_shape)]
    chains = 2 * B * M_HEADS
    return pl.pallas_call(
        functools.partial(_mlstm_scan_kernel, n_batch=B),
        out_shape=(f32(B, L, width),) * 2,
        grid=(nc,),
        in_specs=side(fwd) + side(bwd),
        out_specs=(stok(fwd, width), stok(bwd, width)),
        scratch_shapes=[pltpu.VMEM((chains, C_ROWS, M_DK), F32),
                        pltpu.VMEM((chains, 1, LANES), F32)],
        compiler_params=_params(("arbitrary",)),
        name="mlstm_scan",
    )(p, numf, vecf, clf, p, numb, vecb, clb)


def _head_norm_rope(x, w, cos, sin, bd):
    ms = jnp.dot(x * x, bd, precision=HIGHEST, preferred_element_type=F32)
    y = x * lax.rsqrt(ms + EPS) * w
    lane = lax.broadcasted_iota(jnp.int32, y.shape, 1)
    partner = jnp.where(lane % A_DH < A_DH // 2,
                        pltpu.roll(y, LANES - A_DH // 2, axis=1), pltpu.roll(y, A_DH // 2, axis=1))
    return y * cos + partner * sin


def _attn_prep_kernel(q_ref, kv_ref, cos_ref, sin_ref, qw_ref, kw_ref, qt_ref, k_ref, vt_ref, *, q_scale):
    r = lax.broadcasted_iota(jnp.int32, (LANES, LANES), 0) // A_DH
    c = lax.broadcasted_iota(jnp.int32, (LANES, LANES), 1) // A_DH
    bd = jnp.where(r == c, 1.0 / A_DH, 0.0).astype(F32)
    cos, sin = cos_ref[...], sin_ref[...]
    for pair in range(A_HEADS // 2):
        x = q_ref[0, :, pair * LANES:(pair + 1) * LANES].astype(F32)
        y = _head_norm_rope(x, qw_ref[...], cos, sin, bd) * q_scale
        qt_ref[0, pair * LANES:(pair + 1) * LANES, :] = y.T.astype(BF16)
    kv = kv_ref[0].astype(F32)
    k = _head_norm_rope(kv[:, :LANES], kw_ref[...], cos, sin, bd).astype(BF16)
    for kvh in range(A_KV_HEADS):
        k_ref[0, kvh, 0] = k[:, kvh * A_DH:(kvh + 1) * A_DH]
    vt = kv[:, LANES:].T.astype(BF16)
    ones = jnp.ones((VT_ROWS - A_DH, vt.shape[1]), BF16)
    for kvh in range(A_KV_HEADS):
        vt_ref[0, kvh, 0] = jnp.concatenate([vt[kvh * A_DH:(kvh + 1) * A_DH, :], ones], axis=0)


def _attn_prep(p, cos, sin, qw, kw, ts):
    B, L, _ = p.shape
    per = ts // ROW_TILE
    nblk = L // ts
    q_scale = A_DH ** -0.5 * math.log2(math.e)
    return pl.pallas_call(
        functools.partial(_attn_prep_kernel, q_scale=q_scale),
        out_shape=(jax.ShapeDtypeStruct((B, A_HEADS * A_DH, L), BF16),
                   jax.ShapeDtypeStruct((B, A_KV_HEADS, nblk, ts, A_DH), BF16),
                   jax.ShapeDtypeStruct((B, A_KV_HEADS, nblk, VT_ROWS, ts), BF16)),
        grid=(B, L // ROW_TILE),
        in_specs=[pl.BlockSpec((1, ROW_TILE, A_HEADS * A_DH), lambda b, t: (b, t, P_AQ // (A_HEADS * A_DH))),
                  pl.BlockSpec((1, ROW_TILE, 2 * LANES), lambda b, t: (b, t, P_AKV // (2 * LANES))),
                  pl.BlockSpec((ROW_TILE, LANES), lambda b, t: (t, 0)),
                  pl.BlockSpec((ROW_TILE, LANES), lambda b, t: (t, 0)),
                  pl.BlockSpec((1, LANES), lambda b, t: (0, 0)),
                  pl.BlockSpec((1, LANES), lambda b, t: (0, 0))],
        out_specs=(pl.BlockSpec((1, A_HEADS * A_DH, ROW_TILE), lambda b, t: (b, 0, t)),
                   pl.BlockSpec((1, A_KV_HEADS, 1, ROW_TILE, A_DH), lambda b, t: (b, 0, t // per, t % per, 0)),
                   pl.BlockSpec((1, A_KV_HEADS, 1, VT_ROWS, ROW_TILE), lambda b, t: (b, 0, t // per, 0, t % per))),
        compiler_params=_params(("parallel", "arbitrary")),
        name="attn_prep",
    )(p, p, cos, sin, qw, kw)


ATT_SUB = 256
VT_ROWS = A_DH + 16


def _attn_kernel(qt_ref, k_ref, vt_ref, o_ref, sa_ref, sb_ref, ma_ref, mb_ref, acc_ref,
                 *, nblk, nsub, tq, ctx_tiles, ctx_sub, q_tile0):
    qt = qt_ref[0]
    qs = [qt[g * A_DH:(g + 1) * A_DH, :] for g in range(A_GROUP)]
    n = A_GROUP * tq
    lanes = lambda g: slice(g * tq, (g + 1) * tq)
    keys = lambda r: pl.ds(r * ATT_SUB, ATT_SUB)

    def step(nxt, cur, ms, subs):
        out = []
        for g in range(A_GROUP):
            if cur is not None:
                ci, cs_ref, cm_ref = cur
                m_new = jnp.maximum(ms[g], cm_ref[:, lanes(g)])
                alpha = jnp.exp2(ms[g] - m_new)
            best, pv = None, None
            for r in range(subs):
                if nxt is not None:
                    ni, ns_ref, _ = nxt
                    s = jnp.dot(k_ref[0, 0, ni, keys(r), :], qs[g], preferred_element_type=F32)
                    ns_ref[keys(r), lanes(g)] = s
                    top = jnp.max(s, axis=0, keepdims=True)
                    best = top if best is None else jnp.maximum(best, top)
                if cur is not None:
                    p = jnp.exp2(cs_ref[keys(r), lanes(g)] - m_new)
                    d = jnp.dot(vt_ref[0, 0, ci, :, keys(r)], p.astype(BF16), preferred_element_type=F32)
                    pv = d if pv is None else pv + d
            if nxt is not None:
                nxt[2][:, lanes(g)] = best
            if cur is not None:
                acc_ref[:, lanes(g)] = alpha * acc_ref[:, lanes(g)] + pv
                out.append(m_new)
            else:
                out.append(ms[g])
        return tuple(out)

    def finish():
        o = acc_ref[0:A_DH, :] / acc_ref[A_DH:A_DH + 1, :]
        o = jnp.concatenate([o[:, lanes(g)] for g in range(A_GROUP)], axis=0)
        o_ref[0] = o.T.astype(BF16)

    acc_ref[...] = jnp.zeros_like(acc_ref)
    init = (jnp.full((1, tq), -jnp.inf, F32),) * A_GROUP
    is_ctx = pl.program_id(2) + q_tile0 < ctx_tiles
    buf_a, buf_b = (sa_ref, ma_ref), (sb_ref, mb_ref)

    @pl.when(is_ctx)
    def _():
        step((0, *buf_a), None, init, ctx_sub)
        step(None, (0, *buf_a), init, ctx_sub)
        finish()

    @pl.when(jnp.logical_not(is_ctx))
    def _():
        step((0, *buf_a), None, init, nsub)

        def pair(j, ms):
            i = 2 * j
            ms = step((i + 1, *buf_b), (i, *buf_a), ms, nsub)
            return step((i + 2, *buf_a), (i + 1, *buf_b), ms, nsub)

        ms = lax.fori_loop(0, (nblk - 1) // 2, pair, init)
        step(None, (nblk - 1, *buf_a), ms, nsub)
        finish()


def _attention(qt, k, vt, *, q_tile0, n_ctx, tq):
    B, _, L = qt.shape
    nblk, ts = k.shape[2], k.shape[3]
    assert n_ctx <= ts and n_ctx % tq == 0 and n_ctx % ATT_SUB == 0 and ts % ATT_SUB == 0 and nblk % 2 == 1
    width = A_GROUP * A_DH
    n = A_GROUP * tq
    s_buf, m_buf = pltpu.VMEM((ts, n), F32), pltpu.VMEM((1, n), F32)
    return pl.pallas_call(
        functools.partial(_attn_kernel, nblk=nblk, nsub=ts // ATT_SUB, tq=tq, ctx_tiles=n_ctx // tq,
                          ctx_sub=n_ctx // ATT_SUB, q_tile0=q_tile0),
        out_shape=jax.ShapeDtypeStruct((B, L - q_tile0 * tq, A_HEADS * A_DH), BF16),
        grid=(B, A_KV_HEADS, L // tq - q_tile0),
        in_specs=[pl.BlockSpec((1, width, tq), lambda b, kv, t: (b, kv, t + q_tile0)),
                  pl.BlockSpec((1, 1, nblk, ts, A_DH), lambda b, kv, t: (b, kv, 0, 0, 0)),
                  pl.BlockSpec((1, 1, nblk, VT_ROWS, ts), lambda b, kv, t: (b, kv, 0, 0, 0))],
        out_specs=pl.BlockSpec((1, tq, width), lambda b, kv, t: (b, t, kv)),
        scratch_shapes=[s_buf, s_buf, m_buf, m_buf, pltpu.VMEM((VT_ROWS, n), F32)],
        compiler_params=_params(("parallel", "parallel", "arbitrary")),
        name="attention",
    )(qt, k, vt)


def _mixer_out_kernel(*refs, with_router):
    (hf_ref, hb_ref, mo_ref, a_ref, x_ref, mod_ref, mnw_ref, n2w_ref, wout_ref) = refs[:9]
    hs = hf_ref[0] + hb_ref[0]
    hn = jnp.concatenate([_rms_rows(hs[:, h * M_DV:(h + 1) * M_DV]) for h in range(M_HEADS)], axis=1)
    m = hn * mnw_ref[...] * _sigmoid(mo_ref[0].astype(F32))
    y_in = jnp.concatenate([m.astype(BF16), a_ref[0]], axis=1)
    mod = mod_ref[0, 0]
    x1 = x_ref[0] + mod[2:3] * jnp.dot(y_in, wout_ref[...], preferred_element_type=F32)
    h2 = _rms_rows(x1) * n2w_ref[...] * (1.0 + mod[4:5]) + mod[3:4]
    if not with_router:
        x1_ref, h2_ref = refs[9:]
        x1_ref[0] = x1
        h2_ref[0] = h2.astype(BF16)
        return
    router_ref, x1_ref, h2_ref, ids_ref, gates_ref = refs[9:]
    x1_ref[0] = x1
    h2_ref[0] = h2
    h_hi = h2.astype(BF16)
    h_lo = (h2 - h_hi.astype(F32)).astype(BF16)
    logits = (jnp.dot(h_hi, router_ref[0], preferred_element_type=F32)
              + jnp.dot(h_lo, router_ref[0], preferred_element_type=F32)
              + jnp.dot(h_hi, router_ref[1], preferred_element_type=F32))
    lane = lax.broadcasted_iota(jnp.int32, logits.shape, 1)
    logits = jnp.where(lane < N_EXPERTS, logits, -jnp.inf)
    m1 = jnp.max(logits, axis=-1, keepdims=True)
    i1 = jnp.min(jnp.where(logits == m1, lane, LANES), axis=-1, keepdims=True)
    rest = jnp.where(lane == i1, -jnp.inf, logits)
    m2 = jnp.max(rest, axis=-1, keepdims=True)
    i2 = jnp.min(jnp.where(rest == m2, lane, LANES), axis=-1, keepdims=True)
    e2 = jnp.exp(m2 - m1)
    g1 = 1.0 / (1.0 + e2)
    ids_ref[0] = jnp.where(lane == 0, i1, jnp.where(lane == 1, i2, -1))
    gates_ref[0] = jnp.where(lane == 0, g1, jnp.where(lane == 1, e2 * g1, 0.0))


def _mixer_out(hf, hb, p, a, xa, mods, mnw, n2w, wout, router=None, row_off=0):
    B, L, D = xa.shape
    nt = L // ROW_TILE - row_off
    rin = lambda w, cb=0: pl.BlockSpec((1, ROW_TILE, w), lambda b, t: (b, t + row_off, cb))
    rout = lambda w: pl.BlockSpec((1, ROW_TILE, w), lambda b, t: (b, t, 0))
    full = lambda arr: pl.BlockSpec(arr.shape, lambda b, t: (0,) * arr.ndim)
    mw = M_HEADS * M_DV
    a_off = row_off - (L - a.shape[1]) // ROW_TILE
    a_spec = pl.BlockSpec((1, ROW_TILE, A_HEADS * A_DH), lambda b, t: (b, t + a_off, 0))
    in_specs = [rin(mw), rin(mw), rin(mw, P_MO // mw), a_spec, rin(D), _mod_spec(row_off),
                pl.BlockSpec((1, mw), lambda b, t: (0, 0)), pl.BlockSpec((1, D), lambda b, t: (0, 0)),
                full(wout)]
    args = [hf, hb, p, a, xa, mods, mnw.reshape(1, mw), n2w.reshape(1, D), wout]
    rows = nt * ROW_TILE
    out_shape = [jax.ShapeDtypeStruct((B, rows, D), F32),
                 jax.ShapeDtypeStruct((B, rows, D), BF16 if router is None else F32)]
    out_specs = [rout(D), rout(D)]
    if router is not None:
        in_specs.append(full(router))
        args.append(router)
        out_shape += [jax.ShapeDtypeStruct((B, rows, LANES), jnp.int32), jax.ShapeDtypeStruct((B, rows, LANES), F32)]
        out_specs += [rout(LANES), rout(LANES)]
    return pl.pallas_call(
        functools.partial(_mixer_out_kernel, with_router=router is not None),
        out_shape=tuple(out_shape),
        grid=(B, nt),
        in_specs=in_specs,
        out_specs=tuple(out_specs),
        compiler_params=_params(("parallel", "arbitrary")),
        name="mixer_out",
    )(*args)


def _ffn_kernel(h_ref, x_ref, mod_ref, wg_ref, wu_ref, wd_ref, o_ref):
    h = h_ref[0]
    g = jnp.dot(h, wg_ref[...], preferred_element_type=F32)
    u = jnp.dot(h, wu_ref[...], preferred_element_type=F32)
    act = (g * _sigmoid(g) * u).astype(BF16)
    y = jnp.dot(act, wd_ref[...], preferred_element_type=F32)
    o_ref[0] = x_ref[0] + mod_ref[0, 0][5:6] * y


def _ffn(h2, x1, mods, wg, wu, wd):
    B, L, D = x1.shape
    row = pl.BlockSpec((1, ROW_TILE, D), lambda b, t: (b, t, 0))
    const = lambda arr: pl.BlockSpec(arr.shape, lambda b, t: (0, 0), pipeline_mode=pl.Buffered(1))
    return pl.pallas_call(
        _ffn_kernel,
        out_shape=jax.ShapeDtypeStruct((B, L, D), F32),
        grid=(B, L // ROW_TILE),
        in_specs=[row, row, _mod_spec(), const(wg), const(wu), const(wd)],
        out_specs=row,
        compiler_params=_params(("parallel", "arbitrary")),
        name="ffn_swiglu",
    )(h2, x1, mods, wg, wu, wd)


MOE_TM = 512
MOE_FF = 1792
MOE_TD = 256
RANK_TILE = 512


def _moe_rank_kernel(ids_ref, rank_ref, cnt_ref, carry_ref):
    @pl.when(pl.program_id(0) == 0)
    def _():
        carry_ref[...] = jnp.zeros_like(carry_ref)

    ids = ids_ref[...]
    lane = lax.broadcasted_iota(jnp.int32, ids.shape, 1)
    onehot = jnp.where(jnp.logical_or(lane == ids[:, 0:1], lane == ids[:, 1:2]), 1.0, 0.0)
    r = lax.broadcasted_iota(jnp.int32, (RANK_TILE, RANK_TILE), 0)
    c = lax.broadcasted_iota(jnp.int32, (RANK_TILE, RANK_TILE), 1)
    before = jnp.where(c < r, 1.0, 0.0).astype(BF16)
    rank_ref[...] = jnp.dot(before, onehot.astype(BF16), preferred_element_type=F32) + carry_ref[...]
    carry_ref[...] += jnp.sum(onehot, axis=0, keepdims=True)
    cnt_ref[...] = carry_ref[...]


def _moe_rank(ids):
    n = ids.shape[0]
    return pl.pallas_call(
        _moe_rank_kernel,
        out_shape=(jax.ShapeDtypeStruct((n, LANES), F32), jax.ShapeDtypeStruct((1, LANES), F32)),
        grid=(n // RANK_TILE,),
        in_specs=[pl.BlockSpec((RANK_TILE, LANES), lambda t: (t, 0))],
        out_specs=(pl.BlockSpec((RANK_TILE, LANES), lambda t: (t, 0)), pl.BlockSpec((1, LANES), lambda t: (0, 0))),
        scratch_shapes=[pltpu.VMEM((1, LANES), F32)],
        compiler_params=_params(("arbitrary",)),
        name="moe_rank",
    )(ids)


def _moe_pos_kernel(ids_ref, rank_ref, start_ref, pos_ref):
    ids = ids_ref[...]
    lane = lax.broadcasted_iota(jnp.int32, ids.shape, 1)
    tgt = start_ref[...] + rank_ref[...]
    p0 = jnp.sum(jnp.where(lane == ids[:, 0:1], tgt, 0.0), axis=-1, keepdims=True)
    p1 = jnp.sum(jnp.where(lane == ids[:, 1:2], tgt, 0.0), axis=-1, keepdims=True)
    pos_ref[...] = jnp.where(lane == 0, p0, jnp.where(lane == 1, p1, 0.0)).astype(jnp.int32)


def _moe_pos(ids, rank, start_row):
    n = ids.shape[0]
    blk = pl.BlockSpec((RANK_TILE, LANES), lambda t: (t, 0))
    return pl.pallas_call(
        _moe_pos_kernel,
        out_shape=jax.ShapeDtypeStruct((n, LANES), jnp.int32),
        grid=(n // RANK_TILE,),
        in_specs=[blk, blk, pl.BlockSpec((1, LANES), lambda t: (0, 0))],
        out_specs=blk,
        compiler_params=_params(("parallel",)),
        name="moe_pos",
    )(ids, rank, start_row)


def _row_copy(src, src_row, dst, dst_row, sem):
    return pltpu.make_async_copy(src.at[pl.ds(src_row, 1), :], dst.at[pl.ds(dst_row, 1), :], sem)


def _moe_dispatch_kernel(pad_ref, pos_ref, h_ref, xs_ref, zero_ref, sem):
    @pl.when(pl.program_id(0) == 0)
    def _():
        zero_ref[...] = jnp.zeros_like(zero_ref)
        fills = [pltpu.make_async_copy(
            zero_ref, xs_ref.at[pl.ds(pl.multiple_of(pad_ref[e], SUBLANES), MOE_TM + SUBLANES), :], sem)
            for e in range(N_EXPERTS)]
        for cp in fills:
            cp.start()
        for cp in fills:
            cp.wait()

        def fill_tile(j, carry):
            cp = pltpu.make_async_copy(zero_ref.at[pl.ds(0, MOE_TM), :],
                                       xs_ref.at[pl.ds(pl.multiple_of(j * MOE_TM, MOE_TM), MOE_TM), :], sem)
            cp.start()
            cp.wait()
            return carry

        lax.fori_loop(pad_ref[N_EXPERTS], xs_ref.shape[0] // MOE_TM, fill_tile, 0)

    def issue(r, carry):
        for k in range(2):
            _row_copy(h_ref, r, xs_ref, pos_ref[0, 0, 2 * r + k], sem).start(priority=k)
        return carry

    lax.fori_loop(0, MOE_TD, issue, 0)
    for k in range(2):
        pltpu.make_async_copy(h_ref, xs_ref.at[pl.ds(0, MOE_TD), :], sem).wait()


def _moe_dispatch(h, pos, fill_meta, ns):
    n, d = h.shape
    return pl.pallas_call(
        _moe_dispatch_kernel,
        out_shape=jax.ShapeDtypeStruct((ns, d), F32),
        grid_spec=pltpu.PrefetchScalarGridSpec(
            num_scalar_prefetch=1,
            grid=(n // MOE_TD,),
            in_specs=[pl.BlockSpec((1, 1, 2 * MOE_TD), lambda t, pad: (t, 0, 0), memory_space=pltpu.SMEM),
                      pl.BlockSpec((MOE_TD, d), lambda t, pad: (t, 0))],
            out_specs=pl.BlockSpec(memory_space=pl.ANY),
            scratch_shapes=[pltpu.VMEM((MOE_TM + SUBLANES, d), F32), pltpu.SemaphoreType.DMA(())]),
        compiler_params=_params(("arbitrary",)),
        name="moe_dispatch",
    )(fill_meta, pos, h)


def _moe_group_kernel(te_ref, nv_ref, xs_ref, wg_ref, wu_ref, wd_ref, ys_ref, acc_ref):
    i, f = pl.program_id(0), pl.program_id(1)
    last = pl.num_programs(1) - 1
    valid = i < nv_ref[0]

    @pl.when(valid)
    def _():
        @pl.when(f == 0)
        def _():
            acc_ref[...] = jnp.zeros_like(acc_ref)

        x = xs_ref[...].astype(BF16)
        g = jnp.dot(x, wg_ref[0], preferred_element_type=F32)
        u = jnp.dot(x, wu_ref[0], preferred_element_type=F32)
        act = (g * _sigmoid(g) * u).astype(BF16)
        acc_ref[...] += jnp.dot(act, wd_ref[0], preferred_element_type=F32)

        @pl.when(f == last)
        def _():
            ys_ref[...] = acc_ref[...]

    @pl.when(jnp.logical_and(jnp.logical_not(valid), f == last))
    def _():
        ys_ref[...] = jnp.zeros_like(ys_ref)


def _moe_group(xs, tile_expert, n_valid, wg, wu, wd):
    ns, d = xs.shape
    n_tiles = ns // MOE_TM - 1
    ff = wg.shape[2]
    live = lambda i, nv: i < nv[0]
    return pl.pallas_call(
        _moe_group_kernel,
        out_shape=jax.ShapeDtypeStruct((n_tiles * MOE_TM, d), F32),
        grid_spec=pltpu.PrefetchScalarGridSpec(
            num_scalar_prefetch=2,
            grid=(n_tiles, ff // MOE_FF),
            in_specs=[pl.BlockSpec((MOE_TM, d), lambda i, f, te, nv: (jnp.where(live(i, nv), i, 0), 0)),
                      pl.BlockSpec((1, d, MOE_FF), lambda i, f, te, nv: (te[i], 0, f)),
                      pl.BlockSpec((1, d, MOE_FF), lambda i, f, te, nv: (te[i], 0, f)),
                      pl.BlockSpec((1, MOE_FF, d), lambda i, f, te, nv: (te[i], f, 0))],
            out_specs=pl.BlockSpec((MOE_TM, d), lambda i, f, te, nv: (i, 0)),
            scratch_shapes=[pltpu.VMEM((MOE_TM, d), F32)]),
        compiler_params=_params(("arbitrary", "arbitrary")),
        name="moe_group",
    )(tile_expert, n_valid, xs, wg, wu, wd)


def _moe_combine_kernel(pos_ref, x_ref, gates_ref, mod_ref, fw_ref, ys_ref, o_ref, ybuf, sem):
    def issue(r, carry):
        for k in range(2):
            _row_copy(ys_ref, pos_ref[0, 0, 2 * r + k], ybuf.at[k], r, sem).start(priority=k)
        return carry

    lax.fori_loop(0, MOE_TD, issue, 0)
    for k in range(2):
        pltpu.make_async_copy(ys_ref.at[pl.ds(0, MOE_TD), :], ybuf.at[k], sem).wait()
    gates = gates_ref[...]
    y = gates[:, 0:1] * ybuf[0] + gates[:, 1:2] * ybuf[1]
    x2 = x_ref[...] + mod_ref[0, 0][5:6] * y
    o_ref[...] = _rms_rows(x2) * fw_ref[...]


def _moe_combine(pos, x1, gates, mods, fw, ys, tokens_per_sample):
    n, d = x1.shape
    per = tokens_per_sample // MOE_TD
    return pl.pallas_call(
        _moe_combine_kernel,
        out_shape=jax.ShapeDtypeStruct((n, d), F32),
        grid=(n // MOE_TD,),
        in_specs=[pl.BlockSpec((1, 1, 2 * MOE_TD), lambda t: (t, 0, 0), memory_space=pltpu.SMEM),
                  pl.BlockSpec((MOE_TD, d), lambda t: (t, 0)),
                  pl.BlockSpec((MOE_TD, LANES), lambda t: (t, 0)),
                  pl.BlockSpec((1, 1, 6, d), lambda t: (t // per, 1, 0, 0)),
                  pl.BlockSpec((1, d), lambda t: (0, 0)),
                  pl.BlockSpec(memory_space=pl.ANY)],
        out_specs=pl.BlockSpec((MOE_TD, d), lambda t: (t, 0)),
        scratch_shapes=[pltpu.VMEM((2, MOE_TD, d), F32), pltpu.SemaphoreType.DMA(())],
        compiler_params=_params(("arbitrary",)),
        name="moe_combine",
    )(pos, x1, gates, mods, fw.reshape(1, d), ys)


def _moe(h2, ids, gates, x1, mods, wg, wu, wd, fw):
    B, T, D = x1.shape
    n = B * T
    ids, gates = ids.reshape(n, LANES), gates.reshape(n, LANES)
    rank, cnt = _moe_rank(ids)
    cnt = cnt[0, :N_EXPERTS].astype(jnp.int32)
    padded = (cnt + MOE_TM - 1) // MOE_TM * MOE_TM
    end = jnp.cumsum(padded)
    start = end - padded
    n_tiles = 2 * n // MOE_TM + N_EXPERTS
    tile_expert = jnp.minimum(jnp.sum(jnp.arange(n_tiles)[:, None] >= (end // MOE_TM)[None, :], axis=1),
                              N_EXPERTS - 1).astype(jnp.int32)
    n_valid = (end[-1:] // MOE_TM).astype(jnp.int32)
    start_row = jnp.zeros((1, LANES), F32).at[0, :N_EXPERTS].set(start.astype(F32))
    pos = _moe_pos(ids, rank, start_row)
    pos = pos[:, :2].reshape(n // MOE_TD, 1, 2 * MOE_TD)
    fill_meta = jnp.concatenate([(start + cnt) // SUBLANES * SUBLANES, n_valid]).astype(jnp.int32)
    xs = _moe_dispatch(h2.reshape(n, D), pos, fill_meta, (n_tiles + 1) * MOE_TM)
    ys = _moe_group(xs, tile_expert, n_valid, wg, wu, wd)
    return _moe_combine(pos, x1.reshape(n, D), gates, mods, fw, ys, T).reshape(B, T, D)


_ROT_PERM = np.concatenate([np.arange(0, A_DH, 2), np.arange(1, A_DH, 2)])


def _prep_w_in(w):
    o = np.cumsum([0, M_HEADS * M_DK, M_HEADS * M_DK, M_HEADS * M_DV, M_HEADS * M_DV, 4 * M_HEADS,
                   A_HEADS * A_DH, A_KV_HEADS * A_DH, A_KV_HEADS * A_DH])
    mq, mk, mv, mo, mg, aq, ak, av = [w[:, o[i]:o[i + 1]] for i in range(8)]
    qk = jnp.concatenate([jnp.concatenate([mq[:, h * M_DK:(h + 1) * M_DK] * (M_DK ** -0.5),
                                           mk[:, h * M_DK:(h + 1) * M_DK]], axis=1) for h in range(M_HEADS)], axis=1)
    perm_q = np.concatenate([h * A_DH + _ROT_PERM for h in range(A_HEADS)])
    perm_k = np.concatenate([h * A_DH + _ROT_PERM for h in range(A_KV_HEADS)])
    pad = jnp.zeros((w.shape[0], G_WIDTH - 4 * M_HEADS), w.dtype)
    return jnp.concatenate([qk, mv, mo, aq[:, perm_q], ak[:, perm_k], av, mg, pad], axis=1).astype(BF16)


def _rope_tables(n_tok, n_ctx):
    rows = n_tok // GRID_W
    row = jnp.broadcast_to(jnp.arange(rows, dtype=F32)[:, None], (rows, GRID_W)).reshape(n_tok)
    col = jnp.broadcast_to(jnp.arange(GRID_W, dtype=F32)[None, :], (rows, GRID_W)).reshape(n_tok)
    n_freq = A_DH // 4
    inv_freq = ROPE_THETA ** (-jnp.arange(n_freq, dtype=F32) / n_freq)
    ang = jnp.concatenate([row[:, None] * inv_freq, col[:, None] * inv_freq], axis=-1)
    cos, sin = jnp.cos(ang), jnp.sin(ang)
    cos = jnp.concatenate([jnp.ones((n_ctx, A_DH // 2), F32), cos], axis=0)
    sin = jnp.concatenate([jnp.zeros((n_ctx, A_DH // 2), F32), sin], axis=0)
    return jnp.tile(cos, (1, 4)), jnp.tile(jnp.concatenate([-sin, sin], axis=1), (1, 2))


def kernel(x, c, ctx, c_ctx, ada_w, ada_b, norm1_w, norm2_w, w_in, mlstm_gate_b, mlstm_norm_w, q_norm_w, k_norm_w,
           w_out, ffn_w_gate, ffn_w_up, ffn_w_down, moe_router, moe_w_gate, moe_w_up, moe_w_down, final_norm_w):
    B, T, D = x.shape
    n_ctx = ctx.shape[1]
    L = n_ctx + T
    depth = w_in.shape[0]
    assert D == D_MODEL and n_ctx == ROW_TILE and T % RANK_TILE == 0 and depth == 2
    ts = 3 * ROW_TILE
    assert L % ts == 0
    ctx_tiles = n_ctx // ROW_TILE
    tq = ROW_TILE

    xa = jnp.concatenate([ctx, x], axis=1)
    cvec = jnp.concatenate([c, c_ctx[None], jnp.zeros((8 - B - 1, D), F32)], axis=0)
    cos, sin = _rope_tables(T, n_ctx)
    out = None
    for i in range(depth):
        last = i == depth - 1
        modraw = _ada(cvec, ada_w[i], ada_b[i])
        mods = jnp.stack([jnp.broadcast_to(modraw[B].reshape(1, 6, D), (B, 6, D)),
                          modraw[:B].reshape(B, 6, D)], axis=1)
        p, g = _in_proj(xa, mods, norm1_w[i], _prep_w_in(w_in[i]))
        hf, hb = _mlstm(p, g, mlstm_gate_b[i], n_ctx // M_CHUNK)
        qw = jnp.tile(q_norm_w[i][_ROT_PERM], 2).reshape(1, LANES)
        kw = jnp.tile(k_norm_w[i][_ROT_PERM], 2).reshape(1, LANES)
        qt, k, vt = _attn_prep(p, cos, sin, qw, kw, ts)
        a = _attention(qt, k, vt, q_tile0=ctx_tiles if last else 0, n_ctx=n_ctx, tq=tq)
        wout = w_out[i].astype(BF16)
        if not last:
            x1, h2 = _mixer_out(hf, hb, p, a, xa, mods, mlstm_norm_w[i], norm2_w[i], wout)
            j = i // 2
            xa = _ffn(h2, x1, mods, ffn_w_gate[j].astype(BF16), ffn_w_up[j].astype(BF16),
                      ffn_w_down[j].astype(BF16))
        else:
            j = i // 2
            router = jnp.zeros((D, LANES), F32).at[:, :N_EXPERTS].set(moe_router[j])
            router_hi = router.astype(BF16)
            router = jnp.stack([router_hi, (router - router_hi.astype(F32)).astype(BF16)])
            x1, h2, ids, gates = _mixer_out(hf, hb, p, a, xa, mods, mlstm_norm_w[i], norm2_w[i], wout,
                                            router=router, row_off=ctx_tiles)
            out = _moe(h2, ids, gates, x1, mods, moe_w_gate[j].astype(BF16), moe_w_up[j].astype(BF16),
                       moe_w_down[j].astype(BF16), final_norm_w)
    return out
```

```python
import functools
import math

import numpy as np
import jax
import jax.numpy as jnp
from jax import lax
from jax.experimental import pallas as pl
from jax.experimental.pallas import tpu as pltpu

F32 = jnp.float32
BF16 = jnp.bfloat16
HIGHEST = lax.Precision.HIGHEST

D_MODEL = 1024
GRID_W = 64
M_HEADS = 4
M_DV = 128
M_DK = 64
M_CHUNK = 128
A_HEADS = 8
A_KV_HEADS = 2
A_GROUP = A_HEADS // A_KV_HEADS
A_DH = 64
ROPE_THETA = 10000.0
N_EXPERTS = 8
EPS = 1e-6

LANES = 128
SUBLANES = 8
ROW_TILE = 256
VMEM_LIMIT = 56 * 1024 * 1024

P_QK = 0
P_MV = 512
P_MO = 1024
P_AQ = 1536
P_AKV = 2048
P_WIDTH = 2304
G_WIDTH = LANES


def _params(sem, vmem=VMEM_LIMIT, flags=None):
    return pltpu.CompilerParams(dimension_semantics=sem, vmem_limit_bytes=vmem, flags=flags)


def _sigmoid(x):
    return 1.0 / (1.0 + jnp.exp(-x))


def _rms_rows(x):
    return x * lax.rsqrt(jnp.mean(x * x, axis=-1, keepdims=True) + EPS)


def _ada_kernel(c_ref, w_ref, b_ref, o_ref):
    c = c_ref[...]
    s = c * _sigmoid(c)
    o_ref[...] = jnp.dot(s, w_ref[...], precision=HIGHEST, preferred_element_type=F32) + b_ref[...]


def _ada(cvec, w, b):
    n = w.shape[1]
    bn = 1536
    return pl.pallas_call(
        _ada_kernel,
        out_shape=jax.ShapeDtypeStruct((cvec.shape[0], n), F32),
        grid=(n // bn,),
        in_specs=[pl.BlockSpec(cvec.shape, lambda j: (0, 0)),
                  pl.BlockSpec((w.shape[0], bn), lambda j: (0, j)),
                  pl.BlockSpec((1, bn), lambda j: (0, j))],
        out_specs=pl.BlockSpec((cvec.shape[0], bn), lambda j: (0, j)),
        compiler_params=_params(("arbitrary",)),
        name="ada_mod",
    )(cvec, w, b.reshape(1, n))


def _mod_spec(off=0):
    return pl.BlockSpec((1, 1, 6, D_MODEL), lambda b, t: (b, jnp.minimum(t + off, 1), 0, 0))


def _in_proj_kernel(x_ref, mod_ref, nw_ref, w_ref, p_ref, g_ref):
    mod = mod_ref[0, 0]
    h = _rms_rows(x_ref[0]) * nw_ref[...] * (1.0 + mod[1:2]) + mod[0:1]
    r = jnp.dot(h.astype(BF16), w_ref[...], preferred_element_type=F32)
    p_ref[0] = r[:, :P_WIDTH].astype(BF16)
    g_ref[0] = r[:, P_WIDTH:]


def _in_proj(xa, mods, nw, wp):
    B, L, D = xa.shape
    row = lambda w: pl.BlockSpec((1, ROW_TILE, w), lambda b, t: (b, t, 0))
    return pl.pallas_call(
        _in_proj_kernel,
        out_shape=(jax.ShapeDtypeStruct((B, L, P_WIDTH), BF16),
                   jax.ShapeDtypeStruct((B, L, G_WIDTH), F32)),
        grid=(B, L // ROW_TILE),
        in_specs=[row(D), _mod_spec(),
                  pl.BlockSpec((1, D), lambda b, t: (0, 0)),
                  pl.BlockSpec(wp.shape, lambda b, t: (0, 0))],
        out_specs=(row(P_WIDTH), row(G_WIDTH)),
        compiler_params=_params(("parallel", "arbitrary")),
        name="in_proj",
    )(xa, mods, nw.reshape(1, D), wp)


C_ROWS = M_DV + 16
VEC_ROWS = 24
INTRA_CHUNKS = 2

def _scan_lanes(x, reverse):
    lane = lax.broadcasted_iota(jnp.int32, x.shape, 1)
    k = 1
    while k < M_CHUNK:
        if reverse:
            x = x + jnp.where(lane < M_CHUNK - k, pltpu.roll(x, M_CHUNK - k, axis=1), 0.0)
        else:
            x = x + jnp.where(lane >= k, pltpu.roll(x, k, axis=1), 0.0)
        k *= 2
    return x


def _mlstm_intra_kernel(*refs):
    for c in range(INTRA_CHUNKS):
        _mlstm_intra_chunk(c, *refs)


def _mlstm_intra_chunk(c, qk_ref, v_ref, g_ref, bias_ref, numf_ref, numb_ref, vecf_ref, vecb_ref,
                       clf_ref, clb_ref):
    toks = pl.ds(c * M_CHUNK, M_CHUNK)
    row = lax.broadcasted_iota(jnp.int32, (M_CHUNK, M_CHUNK), 0)
    col = lax.broadcasted_iota(jnp.int32, (M_CHUNK, M_CHUNK), 1)
    g = g_ref[0, toks, :] + bias_ref[...]
    g_row = g.T[0:16, :]
    lf_row = jnp.minimum(g_row, 0.0) - jnp.log1p(jnp.exp(-jnp.abs(g_row)))
    scans = (_scan_lanes(lf_row, False), _scan_lanes(lf_row, True))
    gate_row = lax.broadcasted_iota(jnp.int32, (16, M_CHUNK), 0)
    gaps = g_row - pltpu.roll(jnp.where(gate_row < 8, scans[0], scans[1]), 12, axis=0)
    gap_cols = jnp.concatenate([gaps, jnp.zeros((M_CHUNK - 16, M_CHUNK), F32)], axis=0).T
    outs = ((numf_ref, vecf_ref, clf_ref, row <= col, M_CHUNK - 1),
            (numb_ref, vecb_ref, clb_ref, row >= col, 0))
    for vec_ref in (vecf_ref, vecb_ref):
        vec_ref[0, c,12 + 2 * M_HEADS:VEC_ROWS, :] = jnp.zeros((VEC_ROWS - 12 - 2 * M_HEADS, LANES), F32)
    tail_row = lax.broadcasted_iota(jnp.int32, (C_ROWS - M_DV, M_CHUNK), 0)
    for h in range(M_HEADS):
        qk = qk_ref[0, toks, h * LANES:(h + 1) * LANES]
        q, k = qk[:, :M_DK], qk[:, M_DK:]
        vt = v_ref[0, toks, h * M_DV:(h + 1) * M_DV].astype(F32).T
        vt_bf = vt.astype(BF16)
        s_raw = lax.dot_general(k, q, (((1,), (1,)), ((), ())), preferred_element_type=F32)
        for d, (num_ref, vec_ref, cl_ref, allowed, last) in enumerate(outs):
            b_r = scans[d][8 * d + 4 + h:8 * d + 5 + h, :]
            i_r = g_row[8 * d + h:8 * d + h + 1, :]
            j = 8 * d + h
            b_end = b_r[:, last:last + 1]
            d_log = jnp.where(allowed, b_r + gap_cols[:, j:j + 1], -jnp.inf)
            m_intra = jnp.max(d_log, axis=0, keepdims=True)
            s = s_raw * jnp.exp(d_log - m_intra)
            num_ref[0, c, h] = jnp.dot(vt_bf, s.astype(BF16), preferred_element_type=F32)
            vec_ref[0, c,3 * h:3 * h + 1, :] = jnp.sum(s, axis=0, keepdims=True)
            vec_ref[0, c,3 * h + 1:3 * h + 2, :] = m_intra
            vec_ref[0, c,3 * h + 2:3 * h + 3, :] = b_r
            w_log = b_end - b_r + i_r
            m_loc = jnp.max(w_log, axis=-1, keepdims=True)
            w_row = jnp.exp(w_log - m_loc)
            vw = jnp.concatenate([vt * w_row, jnp.where(tail_row == 0, w_row, 0.0)], axis=0).astype(BF16)
            cl_ref[0, c, h] = jnp.dot(vw, k, preferred_element_type=F32)
            vec_ref[0, c,12 + 2 * h:13 + 2 * h, :] = jnp.broadcast_to(m_loc, (1, LANES))
            vec_ref[0, c,13 + 2 * h:14 + 2 * h, :] = jnp.broadcast_to(b_end, (1, LANES))


def _mlstm_scan_kernel(*refs, n_batch):
    ins, (hf_ref, hb_ref, cn_ref, m_ref) = refs[:8], refs[8:]

    @pl.when(pl.program_id(0) == 0)
    def _():
        cn_ref[...] = jnp.zeros_like(cn_ref)
        m_ref[...] = jnp.zeros_like(m_ref)

    for d, h_ref in enumerate((hf_ref, hb_ref)):
        qk_ref, num_ref, vec_ref, cl_ref = ins[4 * d:4 * d + 4]
        for b in range(n_batch):
            for h in range(M_HEADS):
                idx = (d * n_batch + b) * M_HEADS + h
                q = qk_ref[b, :, h * LANES:h * LANES + M_DK]
                row = lambda r: vec_ref[b, 0, r:r + 1, :]
                den_i, m_i, b_r = row(3 * h), row(3 * h + 1), row(3 * h + 2)
                m_loc, b_end = row(12 + 2 * h), row(13 + 2 * h)
                m_prev = m_ref[idx]
                cn = cn_ref[idx]

                inter = b_r + m_prev
                m_t = jnp.maximum(inter, m_i)
                a = jnp.exp(inter - m_t)
                e = jnp.exp(m_i - m_t)
                cq = lax.dot_general(cn.astype(BF16), q, (((1,), (1,)), ((), ())),
                                     preferred_element_type=F32)
                den = e * den_i + a * cq[M_DV:M_DV + 1, :]
                scale = 1.0 / jnp.maximum(jnp.abs(den), jnp.exp(-m_t))
                ht = (e * num_ref[b, 0, h] + a * cq[0:M_DV, :]) * scale
                h_ref[b, :, h * M_DV:(h + 1) * M_DV] = ht.T

                m_new = jnp.maximum(b_end + m_prev, m_loc)
                a_s = jnp.exp(b_end + m_prev - m_new)
                s_s = jnp.exp(m_loc - m_new)
                cn_ref[idx] = a_s[:, :M_DK] * cn + s_s[:, :M_DK] * cl_ref[b, 0, h]
                m_ref[idx] = m_new


def _mlstm(p, g, gate_b, ctx_chunks):
    B, L, _ = p.shape
    nc = L // M_CHUNK
    width = M_HEADS * M_DV
    bias = jnp.zeros((1, G_WIDTH), F32).at[0, :16].set(gate_b)
    assert nc % INTRA_CHUNKS == 0
    tok = lambda w, cb=0: pl.BlockSpec((1, INTRA_CHUNKS * M_CHUNK, w), lambda b, c: (b, c, cb))
    num_shape, vec_shape, cl_shape = (M_HEADS, M_DV, M_CHUNK), (VEC_ROWS, LANES), (M_HEADS, C_ROWS, M_DK)
    per_chunk = lambda s: pl.BlockSpec((1, INTRA_CHUNKS) + s, lambda b, c: (b, c) + (0,) * len(s))
    f32 = lambda *s: jax.ShapeDtypeStruct(s, F32)
    numf, numb, vecf, vecb, clf, clb = pl.pallas_call(
        _mlstm_intra_kernel,
        out_shape=(f32(B, nc, *num_shape),) * 2 + (f32(B, nc, *vec_shape),) * 2 + (f32(B, nc, *cl_shape),) * 2,
        grid=(B, nc // INTRA_CHUNKS),
        in_specs=[tok(width, P_QK // width), tok(width, P_MV // width), tok(G_WIDTH),
                  pl.BlockSpec((1, G_WIDTH), lambda b, c: (0, 0))],
        out_specs=(per_chunk(num_shape),) * 2 + (per_chunk(vec_shape),) * 2 + (per_chunk(cl_shape),) * 2,
        compiler_params=_params(("parallel", "parallel")),
        name="mlstm_intra",
    )(p, p, g, bias)

    fwd = lambda j: j
    bwd = lambda j: jnp.where(j < ctx_chunks, ctx_chunks - 1 - j, nc - 1 + ctx_chunks - j)
    stok = lambda cm, w, cb=0: pl.BlockSpec((B, M_CHUNK, w), lambda j: (0, cm(j), cb))
    schunk = lambda cm, s: pl.BlockSpec((B, 1) + s, lambda j: (0, cm(j)) + (0,) * len(s))
    side = lambda cm: [stok(cm, width, P_QK // width), schunk(cm, num_shape), schunk(cm, vec_shape),
                       schunk(cm, cl_shape)]
    chains = 2 * B * M_HEADS
    return pl.pallas_call(
        functools.partial(_mlstm_scan_kernel, n_batch=B),
        out_shape=(f32(B, L, width),) * 2,
        grid=(nc,),
        in_specs=side(fwd) + side(bwd),
        out_specs=(stok(fwd, width), stok(bwd, width)),
        scratch_shapes=[pltpu.VMEM((chains, C_ROWS, M_DK), F32),
                        pltpu.VMEM((chains, 1, LANES), F32)],
        compiler_params=_params(("arbitrary",)),
        name="mlstm_scan",
    )(p, numf, vecf, clf, p, numb, vecb, clb)


def _head_norm_rope(x, w, cos, sin, bd):
    ms = jnp.dot(x * x, bd, precision=HIGHEST, preferred_element_type=F32)
    y = x * lax.rsqrt(ms + EPS) * w
    lane = lax.broadcasted_iota(jnp.int32, y.shape, 1)
    partner = jnp.where(lane % A_DH < A_DH // 2,
                        pltpu.roll(y, LANES - A_DH // 2, axis=1), pltpu.roll(y, A_DH // 2, axis=1))
    return y * cos + partner * sin


def _attn_prep_kernel(q_ref, kv_ref, cos_ref, sin_ref, qw_ref, kw_ref, qt_ref, k_ref, vt_ref, *, q_scale):
    r = lax.broadcasted_iota(jnp.int32, (LANES, LANES), 0) // A_DH
    c = lax.broadcasted_iota(jnp.int32, (LANES, LANES), 1) // A_DH
    bd = jnp.where(r == c, 1.0 / A_DH, 0.0).astype(F32)
    cos, sin = cos_ref[...], sin_ref[...]
    for pair in range(A_HEADS // 2):
        x = q_ref[0, :, pair * LANES:(pair + 1) * LANES].astype(F32)
        y = _head_norm_rope(x, qw_ref[...], cos, sin, bd) * q_scale
        qt_ref[0, pair * LANES:(pair + 1) * LANES, :] = y.T.astype(BF16)
    kv = kv_ref[0].astype(F32)
    k = _head_norm_rope(kv[:, :LANES], kw_ref[...], cos, sin, bd).astype(BF16)
    for kvh in range(A_KV_HEADS):
        k_ref[0, kvh, 0] = k[:, kvh * A_DH:(kvh + 1) * A_DH]
    vt = kv[:, LANES:].T.astype(BF16)
    ones = jnp.ones((VT_ROWS - A_DH, vt.shape[1]), BF16)
    for kvh in range(A_KV_HEADS):
        vt_ref[0, kvh, 0] = jnp.concatenate([vt[kvh * A_DH:(kvh + 1) * A_DH, :], ones], axis=0)


def _attn_prep(p, cos, sin, qw, kw, ts):
    B, L, _ = p.shape
    per = ts // ROW_TILE
    nblk = L // ts
    q_scale = A_DH ** -0.5 * math.log2(math.e)
    return pl.pallas_call(
        functools.partial(_attn_prep_kernel, q_scale=q_scale),
        out_shape=(jax.ShapeDtypeStruct((B, A_HEADS * A_DH, L), BF16),
                   jax.ShapeDtypeStruct((B, A_KV_HEADS, nblk, ts, A_DH), BF16),
                   jax.ShapeDtypeStruct((B, A_KV_HEADS, nblk, VT_ROWS, ts), BF16)),
        grid=(B, L // ROW_TILE),
        in_specs=[pl.BlockSpec((1, ROW_TILE, A_HEADS * A_DH), lambda b, t: (b, t, P_AQ // (A_HEADS * A_DH))),
                  pl.BlockSpec((1, ROW_TILE, 2 * LANES), lambda b, t: (b, t, P_AKV // (2 * LANES))),
                  pl.BlockSpec((ROW_TILE, LANES), lambda b, t: (t, 0)),
                  pl.BlockSpec((ROW_TILE, LANES), lambda b, t: (t, 0)),
                  pl.BlockSpec((1, LANES), lambda b, t: (0, 0)),
                  pl.BlockSpec((1, LANES), lambda b, t: (0, 0))],
        out_specs=(pl.BlockSpec((1, A_HEADS * A_DH, ROW_TILE), lambda b, t: (b, 0, t)),
                   pl.BlockSpec((1, A_KV_HEADS, 1, ROW_TILE, A_DH), lambda b, t: (b, 0, t // per, t % per, 0)),
                   pl.BlockSpec((1, A_KV_HEADS, 1, VT_ROWS, ROW_TILE), lambda b, t: (b, 0, t // per, 0, t % per))),
        compiler_params=_params(("parallel", "arbitrary")),
        name="attn_prep",
    )(p, p, cos, sin, qw, kw)


ATT_SUB = 256
ATT_PIECE = 128
VT_ROWS = A_DH + 16


def _attn_kernel(qt_ref, k_ref, vt_ref, o_ref, sa_ref, sb_ref, ma_ref, mb_ref, acc_ref,
                 *, nblk, nsub, tq, ctx_tiles, ctx_sub, q_tile0):
    qt = qt_ref[0]
    qs = [qt[g * A_DH:(g + 1) * A_DH, :] for g in range(A_GROUP)]
    n = A_GROUP * tq
    lanes = lambda g: slice(g * tq, (g + 1) * tq)
    keys = lambda r: pl.ds(r * ATT_SUB, ATT_SUB)

    def step(nxt, cur, ms, subs):
        out = []
        for g in range(A_GROUP):
            if cur is not None:
                ci, cs_ref, cm_ref = cur
                m_new = jnp.maximum(ms[g], cm_ref[:, lanes(g)])
                alpha = jnp.exp2(ms[g] - m_new)
            best, pv = None, None
            for r in range(subs):
                parts = []
                for piece in range(ATT_SUB // ATT_PIECE):
                    rows = pl.ds(r * ATT_SUB + piece * ATT_PIECE, ATT_PIECE)
                    if nxt is not None:
                        ni, ns_ref, _ = nxt
                        s = jnp.dot(k_ref[0, 0, ni, rows, :], qs[g], preferred_element_type=F32)
                        ns_ref[rows, lanes(g)] = s
                        top = jnp.max(s, axis=0, keepdims=True)
                        best = top if best is None else jnp.maximum(best, top)
                    if cur is not None:
                        parts.append(jnp.exp2((cs_ref[rows, lanes(g)] - m_new).astype(BF16)))
                if cur is not None:
                    p = jnp.concatenate(parts, axis=0)
                    d = jnp.dot(vt_ref[0, 0, ci, :, keys(r)], p, preferred_element_type=F32)
                    pv = d if pv is None else pv + d
            if nxt is not None:
                nxt[2][:, lanes(g)] = best
            if cur is not None:
                acc_ref[:, lanes(g)] = alpha * acc_ref[:, lanes(g)] + pv
                out.append(m_new)
            else:
                out.append(ms[g])
        return tuple(out)

    def finish():
        o = acc_ref[0:A_DH, :] / acc_ref[A_DH:A_DH + 1, :]
        o = jnp.concatenate([o[:, lanes(g)] for g in range(A_GROUP)], axis=0)
        o_ref[0] = o.T.astype(BF16)

    acc_ref[...] = jnp.zeros_like(acc_ref)
    init = (jnp.full((1, tq), -jnp.inf, F32),) * A_GROUP
    is_ctx = pl.program_id(2) + q_tile0 < ctx_tiles
    buf_a, buf_b = (sa_ref, ma_ref), (sb_ref, mb_ref)

    @pl.when(is_ctx)
    def _():
        step((0, *buf_a), None, init, ctx_sub)
        step(None, (0, *buf_a), init, ctx_sub)
        finish()

    @pl.when(jnp.logical_not(is_ctx))
    def _():
        step((0, *buf_a), None, init, nsub)

        def pair(j, ms):
            i = 2 * j
            ms = step((i + 1, *buf_b), (i, *buf_a), ms, nsub)
            return step((i + 2, *buf_a), (i + 1, *buf_b), ms, nsub)

        ms = lax.fori_loop(0, (nblk - 1) // 2, pair, init)
        step(None, (nblk - 1, *buf_a), ms, nsub)
        finish()


def _attention(qt, k, vt, *, q_tile0, n_ctx, tq):
    B, _, L = qt.shape
    nblk, ts = k.shape[2], k.shape[3]
    assert n_ctx <= ts and n_ctx % tq == 0 and n_ctx % ATT_SUB == 0 and ts % ATT_SUB == 0 and nblk % 2 == 1
    width = A_GROUP * A_DH
    n = A_GROUP * tq
    s_buf, m_buf = pltpu.VMEM((ts, n), F32), pltpu.VMEM((1, n), F32)
    return pl.pallas_call(
        functools.partial(_attn_kernel, nblk=nblk, nsub=ts // ATT_SUB, tq=tq, ctx_tiles=n_ctx // tq,
                          ctx_sub=n_ctx // ATT_SUB, q_tile0=q_tile0),
        out_shape=jax.ShapeDtypeStruct((B, L - q_tile0 * tq, A_HEADS * A_DH), BF16),
        grid=(B, A_KV_HEADS, L // tq - q_tile0),
        in_specs=[pl.BlockSpec((1, width, tq), lambda b, kv, t: (b, kv, t + q_tile0)),
                  pl.BlockSpec((1, 1, nblk, ts, A_DH), lambda b, kv, t: (b, kv, 0, 0, 0)),
                  pl.BlockSpec((1, 1, nblk, VT_ROWS, ts), lambda b, kv, t: (b, kv, 0, 0, 0))],
        out_specs=pl.BlockSpec((1, tq, width), lambda b, kv, t: (b, t, kv)),
        scratch_shapes=[s_buf, s_buf, m_buf, m_buf, pltpu.VMEM((VT_ROWS, n), F32)],
        compiler_params=_params(("parallel", "parallel", "arbitrary")),
        name="attention",
    )(qt, k, vt)


def _mixer_out_kernel(*refs, with_router):
    (hf_ref, hb_ref, mo_ref, a_ref, x_ref, mod_ref, mnw_ref, n2w_ref, wout_ref) = refs[:9]
    hs = hf_ref[0] + hb_ref[0]
    hn = jnp.concatenate([_rms_rows(hs[:, h * M_DV:(h + 1) * M_DV]) for h in range(M_HEADS)], axis=1)
    m = hn * mnw_ref[...] * _sigmoid(mo_ref[0].astype(F32))
    y_in = jnp.concatenate([m.astype(BF16), a_ref[0]], axis=1)
    mod = mod_ref[0, 0]
    x1 = x_ref[0] + mod[2:3] * jnp.dot(y_in, wout_ref[...], preferred_element_type=F32)
    h2 = _rms_rows(x1) * n2w_ref[...] * (1.0 + mod[4:5]) + mod[3:4]
    if not with_router:
        x1_ref, h2_ref = refs[9:]
        x1_ref[0] = x1
        h2_ref[0] = h2.astype(BF16)
        return
    router_ref, x1_ref, h2_ref, ids_ref, gates_ref = refs[9:]
    x1_ref[0] = x1
    h2_ref[0] = h2
    h_hi = h2.astype(BF16)
    h_lo = (h2 - h_hi.astype(F32)).astype(BF16)
    logits = (jnp.dot(h_hi, router_ref[0], preferred_element_type=F32)
              + jnp.dot(h_lo, router_ref[0], preferred_element_type=F32)
              + jnp.dot(h_hi, router_ref[1], preferred_element_type=F32))
    lane = lax.broadcasted_iota(jnp.int32, logits.shape, 1)
    logits = jnp.where(lane < N_EXPERTS, logits, -jnp.inf)
    m1 = jnp.max(logits, axis=-1, keepdims=True)
    i1 = jnp.min(jnp.where(logits == m1, lane, LANES), axis=-1, keepdims=True)
    rest = jnp.where(lane == i1, -jnp.inf, logits)
    m2 = jnp.max(rest, axis=-1, keepdims=True)
    i2 = jnp.min(jnp.where(rest == m2, lane, LANES), axis=-1, keepdims=True)
    e2 = jnp.exp(m2 - m1)
    g1 = 1.0 / (1.0 + e2)
    ids_ref[0] = jnp.where(lane == 0, i1, jnp.where(lane == 1, i2, -1))
    gates_ref[0] = jnp.where(lane == 0, g1, jnp.where(lane == 1, e2 * g1, 0.0))


def _mixer_out(hf, hb, p, a, xa, mods, mnw, n2w, wout, router=None, row_off=0):
    B, L, D = xa.shape
    nt = L // ROW_TILE - row_off
    rin = lambda w, cb=0: pl.BlockSpec((1, ROW_TILE, w), lambda b, t: (b, t + row_off, cb))
    rout = lambda w: pl.BlockSpec((1, ROW_TILE, w), lambda b, t: (b, t, 0))
    full = lambda arr: pl.BlockSpec(arr.shape, lambda b, t: (0,) * arr.ndim)
    mw = M_HEADS * M_DV
    a_off = row_off - (L - a.shape[1]) // ROW_TILE
    a_spec = pl.BlockSpec((1, ROW_TILE, A_HEADS * A_DH), lambda b, t: (b, t + a_off, 0))
    in_specs = [rin(mw), rin(mw), rin(mw, P_MO // mw), a_spec, rin(D), _mod_spec(row_off),
                pl.BlockSpec((1, mw), lambda b, t: (0, 0)), pl.BlockSpec((1, D), lambda b, t: (0, 0)),
                full(wout)]
    args = [hf, hb, p, a, xa, mods, mnw.reshape(1, mw), n2w.reshape(1, D), wout]
    rows = nt * ROW_TILE
    out_shape = [jax.ShapeDtypeStruct((B, rows, D), F32),
                 jax.ShapeDtypeStruct((B, rows, D), BF16 if router is None else F32)]
    out_specs = [rout(D), rout(D)]
    if router is not None:
        in_specs.append(full(router))
        args.append(router)
        out_shape += [jax.ShapeDtypeStruct((B, rows, LANES), jnp.int32), jax.ShapeDtypeStruct((B, rows, LANES), F32)]
        out_specs += [rout(LANES), rout(LANES)]
    return pl.pallas_call(
        functools.partial(_mixer_out_kernel, with_router=router is not None),
        out_shape=tuple(out_shape),
        grid=(B, nt),
        in_specs=in_specs,
        out_specs=tuple(out_specs),
        compiler_params=_params(("parallel", "arbitrary")),
        name="mixer_out",
    )(*args)


def _ffn_kernel(h_ref, x_ref, mod_ref, wg_ref, wu_ref, wd_ref, o_ref):
    h = h_ref[0]
    g = jnp.dot(h, wg_ref[...], preferred_element_type=F32)
    u = jnp.dot(h, wu_ref[...], preferred_element_type=F32)
    act = (g * _sigmoid(g) * u).astype(BF16)
    y = jnp.dot(act, wd_ref[...], preferred_element_type=F32)
    o_ref[0] = x_ref[0] + mod_ref[0, 0][5:6] * y


def _ffn(h2, x1, mods, wg, wu, wd):
    B, L, D = x1.shape
    row = pl.BlockSpec((1, ROW_TILE, D), lambda b, t: (b, t, 0))
    const = lambda arr: pl.BlockSpec(arr.shape, lambda b, t: (0, 0), pipeline_mode=pl.Buffered(1))
    return pl.pallas_call(
        _ffn_kernel,
        out_shape=jax.ShapeDtypeStruct((B, L, D), F32),
        grid=(B, L // ROW_TILE),
        in_specs=[row, row, _mod_spec(), const(wg), const(wu), const(wd)],
        out_specs=row,
        compiler_params=_params(("parallel", "arbitrary")),
        name="ffn_swiglu",
    )(h2, x1, mods, wg, wu, wd)


MOE_TM = 512
MOE_FF = 1792
MOE_TD = 256
RANK_TILE = 512
ISSUE_UNROLL = 8


def _moe_rank_kernel(ids_ref, rank_ref, cnt_ref, carry_ref):
    @pl.when(pl.program_id(0) == 0)
    def _():
        carry_ref[...] = jnp.zeros_like(carry_ref)

    ids = ids_ref[...]
    lane = lax.broadcasted_iota(jnp.int32, ids.shape, 1)
    onehot = jnp.where(jnp.logical_or(lane == ids[:, 0:1], lane == ids[:, 1:2]), 1.0, 0.0)
    r = lax.broadcasted_iota(jnp.int32, (RANK_TILE, RANK_TILE), 0)
    c = lax.broadcasted_iota(jnp.int32, (RANK_TILE, RANK_TILE), 1)
    before = jnp.where(c < r, 1.0, 0.0).astype(BF16)
    rank_ref[...] = jnp.dot(before, onehot.astype(BF16), preferred_element_type=F32) + carry_ref[...]
    carry_ref[...] += jnp.sum(onehot, axis=0, keepdims=True)
    cnt_ref[...] = carry_ref[...]


def _moe_rank(ids):
    n = ids.shape[0]
    return pl.pallas_call(
        _moe_rank_kernel,
        out_shape=(jax.ShapeDtypeStruct((n, LANES), F32), jax.ShapeDtypeStruct((1, LANES), F32)),
        grid=(n // RANK_TILE,),
        in_specs=[pl.BlockSpec((RANK_TILE, LANES), lambda t: (t, 0))],
        out_specs=(pl.BlockSpec((RANK_TILE, LANES), lambda t: (t, 0)), pl.BlockSpec((1, LANES), lambda t: (0, 0))),
        scratch_shapes=[pltpu.VMEM((1, LANES), F32)],
        compiler_params=_params(("arbitrary",)),
        name="moe_rank",
    )(ids)


def _moe_pos_kernel(ids_ref, rank_ref, start_ref, pos_ref):
    ids = ids_ref[...]
    lane = lax.broadcasted_iota(jnp.int32, ids.shape, 1)
    tgt = start_ref[...] + rank_ref[...]
    p0 = jnp.sum(jnp.where(lane == ids[:, 0:1], tgt, 0.0), axis=-1, keepdims=True)
    p1 = jnp.sum(jnp.where(lane == ids[:, 1:2], tgt, 0.0), axis=-1, keepdims=True)
    pos_ref[...] = jnp.where(lane == 0, p0, jnp.where(lane == 1, p1, 0.0)).astype(jnp.int32)


def _moe_pos(ids, rank, start_row):
    n = ids.shape[0]
    blk = pl.BlockSpec((RANK_TILE, LANES), lambda t: (t, 0))
    return pl.pallas_call(
        _moe_pos_kernel,
        out_shape=jax.ShapeDtypeStruct((n, LANES), jnp.int32),
        grid=(n // RANK_TILE,),
        in_specs=[blk, blk, pl.BlockSpec((1, LANES), lambda t: (0, 0))],
        out_specs=blk,
        compiler_params=_params(("parallel",)),
        name="moe_pos",
    )(ids, rank, start_row)


def _row_copy(src, src_row, dst, dst_row, sem):
    return pltpu.make_async_copy(src.at[pl.ds(src_row, 1), :], dst.at[pl.ds(dst_row, 1), :], sem)


def _moe_dispatch_kernel(pad_ref, pos_ref, h_ref, xs_ref, zero_ref, sem):
    @pl.when(pl.program_id(0) == 0)
    def _():
        zero_ref[...] = jnp.zeros_like(zero_ref)
        fills = [pltpu.make_async_copy(
            zero_ref, xs_ref.at[pl.ds(pl.multiple_of(pad_ref[e], SUBLANES), MOE_TM + SUBLANES), :], sem)
            for e in range(N_EXPERTS)]
        for cp in fills:
            cp.start()
        for cp in fills:
            cp.wait()

        def fill_tile(j, carry):
            cp = pltpu.make_async_copy(zero_ref.at[pl.ds(0, MOE_TM), :],
                                       xs_ref.at[pl.ds(pl.multiple_of(j * MOE_TM, MOE_TM), MOE_TM), :], sem)
            cp.start()
            cp.wait()
            return carry

        lax.fori_loop(pad_ref[N_EXPERTS], xs_ref.shape[0] // MOE_TM, fill_tile, 0)

    def issue(r, carry):
        for k in range(2):
            _row_copy(h_ref, r, xs_ref, pos_ref[0, 0, 2 * r + k], sem).start(priority=k)
        return carry

    lax.fori_loop(0, MOE_TD, issue, 0, unroll=ISSUE_UNROLL)
    for k in range(2):
        pltpu.make_async_copy(h_ref, xs_ref.at[pl.ds(0, MOE_TD), :], sem).wait()


def _moe_dispatch(h, pos, fill_meta, ns):
    n, d = h.shape
    return pl.pallas_call(
        _moe_dispatch_kernel,
        out_shape=jax.ShapeDtypeStruct((ns, d), F32),
        grid_spec=pltpu.PrefetchScalarGridSpec(
            num_scalar_prefetch=1,
            grid=(n // MOE_TD,),
            in_specs=[pl.BlockSpec((1, 1, 2 * MOE_TD), lambda t, pad: (t, 0, 0), memory_space=pltpu.SMEM),
                      pl.BlockSpec((MOE_TD, d), lambda t, pad: (t, 0))],
            out_specs=pl.BlockSpec(memory_space=pl.ANY),
            scratch_shapes=[pltpu.VMEM((MOE_TM + SUBLANES, d), F32), pltpu.SemaphoreType.DMA(())]),
        compiler_params=_params(("arbitrary",)),
        name="moe_dispatch",
    )(fill_meta, pos, h)


def _moe_group_kernel(te_ref, nv_ref, xs_ref, wg_ref, wu_ref, wd_ref, ys_ref, acc_ref):
    i, f = pl.program_id(0), pl.program_id(1)
    last = pl.num_programs(1) - 1
    valid = i < nv_ref[0]

    @pl.when(valid)
    def _():
        @pl.when(f == 0)
        def _():
            acc_ref[...] = jnp.zeros_like(acc_ref)

        x = xs_ref[...].astype(BF16)
        g = jnp.dot(x, wg_ref[0], preferred_element_type=F32)
        u = jnp.dot(x, wu_ref[0], preferred_element_type=F32)
        act = (g * _sigmoid(g) * u).astype(BF16)
        acc_ref[...] += jnp.dot(act, wd_ref[0], preferred_element_type=F32)

        @pl.when(f == last)
        def _():
            ys_ref[...] = acc_ref[...]

    @pl.when(jnp.logical_and(jnp.logical_not(valid), f == last))
    def _():
        ys_ref[...] = jnp.zeros_like(ys_ref)


def _moe_group(xs, tile_expert, n_valid, wg, wu, wd):
    ns, d = xs.shape
    n_tiles = ns // MOE_TM - 1
    ff = wg.shape[2]
    live = lambda i, nv: i < nv[0]
    return pl.pallas_call(
        _moe_group_kernel,
        out_shape=jax.ShapeDtypeStruct((n_tiles * MOE_TM, d), F32),
        grid_spec=pltpu.PrefetchScalarGridSpec(
            num_scalar_prefetch=2,
            grid=(n_tiles, ff // MOE_FF),
            in_specs=[pl.BlockSpec((MOE_TM, d), lambda i, f, te, nv: (jnp.where(live(i, nv), i, 0), 0)),
                      pl.BlockSpec((1, d, MOE_FF), lambda i, f, te, nv: (te[i], 0, f)),
                      pl.BlockSpec((1, d, MOE_FF), lambda i, f, te, nv: (te[i], 0, f)),
                      pl.BlockSpec((1, MOE_FF, d), lambda i, f, te, nv: (te[i], f, 0))],
            out_specs=pl.BlockSpec((MOE_TM, d), lambda i, f, te, nv: (i, 0)),
            scratch_shapes=[pltpu.VMEM((MOE_TM, d), F32)]),
        compiler_params=_params(("arbitrary", "arbitrary")),
        name="moe_group",
    )(tile_expert, n_valid, xs, wg, wu, wd)


def _moe_combine_kernel(pos_ref, x_ref, gates_ref, mod_ref, fw_ref, ys_ref, o_ref, ybuf, sem):
    def issue(r, carry):
        for k in range(2):
            _row_copy(ys_ref, pos_ref[0, 0, 2 * r + k], ybuf.at[k], r, sem).start(priority=k)
        return carry

    lax.fori_loop(0, MOE_TD, issue, 0, unroll=ISSUE_UNROLL)
    for k in range(2):
        pltpu.make_async_copy(ys_ref.at[pl.ds(0, MOE_TD), :], ybuf.at[k], sem).wait()
    gates = gates_ref[...]
    y = gates[:, 0:1] * ybuf[0] + gates[:, 1:2] * ybuf[1]
    x2 = x_ref[...] + mod_ref[0, 0][5:6] * y
    o_ref[...] = _rms_rows(x2) * fw_ref[...]


def _moe_combine(pos, x1, gates, mods, fw, ys, tokens_per_sample):
    n, d = x1.shape
    per = tokens_per_sample // MOE_TD
    return pl.pallas_call(
        _moe_combine_kernel,
        out_shape=jax.ShapeDtypeStruct((n, d), F32),
        grid=(n // MOE_TD,),
        in_specs=[pl.BlockSpec((1, 1, 2 * MOE_TD), lambda t: (t, 0, 0), memory_space=pltpu.SMEM),
                  pl.BlockSpec((MOE_TD, d), lambda t: (t, 0)),
                  pl.BlockSpec((MOE_TD, LANES), lambda t: (t, 0)),
                  pl.BlockSpec((1, 1, 6, d), lambda t: (t // per, 1, 0, 0)),
                  pl.BlockSpec((1, d), lambda t: (0, 0)),
                  pl.BlockSpec(memory_space=pl.ANY)],
        out_specs=pl.BlockSpec((MOE_TD, d), lambda t: (t, 0)),
        scratch_shapes=[pltpu.VMEM((2, MOE_TD, d), F32), pltpu.SemaphoreType.DMA(())],
        compiler_params=_params(("arbitrary",)),
        name="moe_combine",
    )(pos, x1, gates, mods, fw.reshape(1, d), ys)


def _moe(h2, ids, gates, x1, mods, wg, wu, wd, fw):
    B, T, D = x1.shape
    n = B * T
    ids, gates = ids.reshape(n, LANES), gates.reshape(n, LANES)
    rank, cnt = _moe_rank(ids)
    cnt = cnt[0, :N_EXPERTS].astype(jnp.int32)
    padded = (cnt + MOE_TM - 1) // MOE_TM * MOE_TM
    end = jnp.cumsum(padded)
    start = end - padded
    n_tiles = 2 * n // MOE_TM + N_EXPERTS
    tile_expert = jnp.minimum(jnp.sum(jnp.arange(n_tiles)[:, None] >= (end // MOE_TM)[None, :], axis=1),
                              N_EXPERTS - 1).astype(jnp.int32)
    n_valid = (end[-1:] // MOE_TM).astype(jnp.int32)
    start_row = jnp.zeros((1, LANES), F32).at[0, :N_EXPERTS].set(start.astype(F32))
    pos = _moe_pos(ids, rank, start_row)
    pos = pos[:, :2].reshape(n // MOE_TD, 1, 2 * MOE_TD)
    fill_meta = jnp.concatenate([(start + cnt) // SUBLANES * SUBLANES, n_valid]).astype(jnp.int32)
    xs = _moe_dispatch(h2.reshape(n, D), pos, fill_meta, (n_tiles + 1) * MOE_TM)
    ys = _moe_group(xs, tile_expert, n_valid, wg, wu, wd)
    return _moe_combine(pos, x1.reshape(n, D), gates, mods, fw, ys, T).reshape(B, T, D)


_ROT_PERM = np.concatenate([np.arange(0, A_DH, 2), np.arange(1, A_DH, 2)])


def _prep_w_in(w):
    o = np.cumsum([0, M_HEADS * M_DK, M_HEADS * M_DK, M_HEADS * M_DV, M_HEADS * M_DV, 4 * M_HEADS,
                   A_HEADS * A_DH, A_KV_HEADS * A_DH, A_KV_HEADS * A_DH])
    mq, mk, mv, mo, mg, aq, ak, av = [w[:, o[i]:o[i + 1]] for i in range(8)]
    qk = jnp.concatenate([jnp.concatenate([mq[:, h * M_DK:(h + 1) * M_DK] * (M_DK ** -0.5),
                                           mk[:, h * M_DK:(h + 1) * M_DK]], axis=1) for h in range(M_HEADS)], axis=1)
    perm_q = np.concatenate([h * A_DH + _ROT_PERM for h in range(A_HEADS)])
    perm_k = np.concatenate([h * A_DH + _ROT_PERM for h in range(A_KV_HEADS)])
    pad = jnp.zeros((w.shape[0], G_WIDTH - 4 * M_HEADS), w.dtype)
    return jnp.concatenate([qk, mv, mo, aq[:, perm_q], ak[:, perm_k], av, mg, pad], axis=1).astype(BF16)


def _rope_tables(n_tok, n_ctx):
    rows = n_tok // GRID_W
    row = jnp.broadcast_to(jnp.arange(rows, dtype=F32)[:, None], (rows, GRID_W)).reshape(n_tok)
    col = jnp.broadcast_to(jnp.arange(GRID_W, dtype=F32)[None, :], (rows, GRID_W)).reshape(n_tok)
    n_freq = A_DH // 4
    inv_freq = ROPE_THETA ** (-jnp.arange(n_freq, dtype=F32) / n_freq)
    ang = jnp.concatenate([row[:, None] * inv_freq, col[:, None] * inv_freq], axis=-1)
    cos, sin = jnp.cos(ang), jnp.sin(ang)
    cos = jnp.concatenate([jnp.ones((n_ctx, A_DH // 2), F32), cos], axis=0)
    sin = jnp.concatenate([jnp.zeros((n_ctx, A_DH // 2), F32), sin], axis=0)
    return jnp.tile(cos, (1, 4)), jnp.tile(jnp.concatenate([-sin, sin], axis=1), (1, 2))


def kernel(x, c, ctx, c_ctx, ada_w, ada_b, norm1_w, norm2_w, w_in, mlstm_gate_b, mlstm_norm_w, q_norm_w, k_norm_w,
           w_out, ffn_w_gate, ffn_w_up, ffn_w_down, moe_router, moe_w_gate, moe_w_up, moe_w_down, final_norm_w):
    B, T, D = x.shape
    n_ctx = ctx.shape[1]
    L = n_ctx + T
    depth = w_in.shape[0]
    assert D == D_MODEL and n_ctx == ROW_TILE and T % RANK_TILE == 0 and depth == 2
    ts = 3 * ROW_TILE
    assert L % ts == 0
    ctx_tiles = n_ctx // ROW_TILE
    tq = ROW_TILE

    xa = jnp.concatenate([ctx, x], axis=1)
    cvec = jnp.concatenate([c, c_ctx[None], jnp.zeros((8 - B - 1, D), F32)], axis=0)
    cos, sin = _rope_tables(T, n_ctx)
    out = None
    for i in range(depth):
        last = i == depth - 1
        modraw = _ada(cvec, ada_w[i], ada_b[i])
        mods = jnp.stack([jnp.broadcast_to(modraw[B].reshape(1, 6, D), (B, 6, D)),
                          modraw[:B].reshape(B, 6, D)], axis=1)
        p, g = _in_proj(xa, mods, norm1_w[i], _prep_w_in(w_in[i]))
        hf, hb = _mlstm(p, g, mlstm_gate_b[i], n_ctx // M_CHUNK)
        qw = jnp.tile(q_norm_w[i][_ROT_PERM], 2).reshape(1, LANES)
        kw = jnp.tile(k_norm_w[i][_ROT_PERM], 2).reshape(1, LANES)
        qt, k, vt = _attn_prep(p, cos, sin, qw, kw, ts)
        a = _attention(qt, k, vt, q_tile0=ctx_tiles if last else 0, n_ctx=n_ctx, tq=tq)
        wout = w_out[i].astype(BF16)
        if not last:
            x1, h2 = _mixer_out(hf, hb, p, a, xa, mods, mlstm_norm_w[i], norm2_w[i], wout)
            j = i // 2
            xa = _ffn(h2, x1, mods, ffn_w_gate[j].astype(BF16), ffn_w_up[j].astype(BF16),
                      ffn_w_down[j].astype(BF16))
        else:
            j = i // 2
            router = jnp.zeros((D, LANES), F32).at[:, :N_EXPERTS].set(moe_router[j])
            router_hi = router.astype(BF16)
            router = jnp.stack([router_hi, (router - router_hi.astype(F32)).astype(BF16)])
            x1, h2, ids, gates = _mixer_out(hf, hb, p, a, xa, mods, mlstm_norm_w[i], norm2_w[i], wout,
                                            router=router, row_off=ctx_tiles)
            out = _moe(h2, ids, gates, x1, mods, moe_w_gate[j].astype(BF16), moe_w_up[j].astype(BF16),
                       moe_w_down[j].astype(BF16), final_norm_w)
    return out
```

```python
import functools
import math

import numpy as np
import jax
import jax.numpy as jnp
from jax import lax
from jax.experimental import pallas as pl
from jax.experimental.pallas import tpu as pltpu

F32 = jnp.float32
BF16 = jnp.bfloat16
HIGHEST = lax.Precision.HIGHEST

D_MODEL = 1024
GRID_W = 64
M_HEADS = 4
M_DV = 128
M_DK = 64
M_CHUNK = 128
A_HEADS = 8
A_KV_HEADS = 2
A_GROUP = A_HEADS // A_KV_HEADS
A_DH = 64
ROPE_THETA = 10000.0
N_EXPERTS = 8
EPS = 1e-6

LANES = 128
SUBLANES = 8
ROW_TILE = 256
VMEM_LIMIT = 56 * 1024 * 1024

P_QK = 0
P_MV = 512
P_MO = 1024
P_AQ = 1536
P_AKV = 2048
P_WIDTH = 2304
G_WIDTH = LANES


def _params(sem, vmem=VMEM_LIMIT, flags=None):
    return pltpu.CompilerParams(dimension_semantics=sem, vmem_limit_bytes=vmem, flags=flags)


def _sigmoid(x):
    return 1.0 / (1.0 + jnp.exp(-x))


def _rms_rows(x):
    return x * lax.rsqrt(jnp.mean(x * x, axis=-1, keepdims=True) + EPS)


def _ada_kernel(c_ref, w_ref, b_ref, o_ref):
    c = c_ref[...]
    s = c * _sigmoid(c)
    o_ref[...] = jnp.dot(s, w_ref[...], precision=HIGHEST, preferred_element_type=F32) + b_ref[...]


def _ada(cvec, w, b):
    n = w.shape[1]
    bn = 1536
    return pl.pallas_call(
        _ada_kernel,
        out_shape=jax.ShapeDtypeStruct((cvec.shape[0], n), F32),
        grid=(n // bn,),
        in_specs=[pl.BlockSpec(cvec.shape, lambda j: (0, 0)),
                  pl.BlockSpec((w.shape[0], bn), lambda j: (0, j)),
                  pl.BlockSpec((1, bn), lambda j: (0, j))],
        out_specs=pl.BlockSpec((cvec.shape[0], bn), lambda j: (0, j)),
        compiler_params=_params(("arbitrary",)),
        name="ada_mod",
    )(cvec, w, b.reshape(1, n))


def _mod_spec(off=0):
    return pl.BlockSpec((1, 1, 6, D_MODEL), lambda b, t: (b, jnp.minimum(t + off, 1), 0, 0))


def _in_proj_kernel(x_ref, mod_ref, nw_ref, w_ref, p_ref, g_ref):
    mod = mod_ref[0, 0]
    h = _rms_rows(x_ref[0]) * nw_ref[...] * (1.0 + mod[1:2]) + mod[0:1]
    r = jnp.dot(h.astype(BF16), w_ref[...], preferred_element_type=F32)
    p_ref[0] = r[:, :P_WIDTH].astype(BF16)
    g_ref[0] = r[:, P_WIDTH:]


def _in_proj(xa, mods, nw, wp):
    B, L, D = xa.shape
    row = lambda w: pl.BlockSpec((1, ROW_TILE, w), lambda b, t: (b, t, 0))
    return pl.pallas_call(
        _in_proj_kernel,
        out_shape=(jax.ShapeDtypeStruct((B, L, P_WIDTH), BF16),
                   jax.ShapeDtypeStruct((B, L, G_WIDTH), F32)),
        grid=(B, L // ROW_TILE),
        in_specs=[row(D), _mod_spec(),
                  pl.BlockSpec((1, D), lambda b, t: (0, 0)),
                  pl.BlockSpec(wp.shape, lambda b, t: (0, 0))],
        out_specs=(row(P_WIDTH), row(G_WIDTH)),
        compiler_params=_params(("parallel", "arbitrary")),
        name="in_proj",
    )(xa, mods, nw.reshape(1, D), wp)


C_ROWS = M_DV + 16
VEC_ROWS = 24
INTRA_CHUNKS = 2

def _scan_lanes(x, reverse):
    lane = lax.broadcasted_iota(jnp.int32, x.shape, 1)
    k = 1
    while k < M_CHUNK:
        if reverse:
            x = x + jnp.where(lane < M_CHUNK - k, pltpu.roll(x, M_CHUNK - k, axis=1), 0.0)
        else:
            x = x + jnp.where(lane >= k, pltpu.roll(x, k, axis=1), 0.0)
        k *= 2
    return x


def _mlstm_intra_kernel(*refs):
    for c in range(INTRA_CHUNKS):
        _mlstm_intra_chunk(c, *refs)


def _mlstm_intra_chunk(c, qk_ref, v_ref, g_ref, bias_ref, numf_ref, numb_ref, vecf_ref, vecb_ref,
                       clf_ref, clb_ref):
    toks = pl.ds(c * M_CHUNK, M_CHUNK)
    row = lax.broadcasted_iota(jnp.int32, (M_CHUNK, M_CHUNK), 0)
    col = lax.broadcasted_iota(jnp.int32, (M_CHUNK, M_CHUNK), 1)
    g = g_ref[0, toks, :] + bias_ref[...]
    g_row = g.T[0:16, :]
    lf_row = jnp.minimum(g_row, 0.0) - jnp.log1p(jnp.exp(-jnp.abs(g_row)))
    scans = (_scan_lanes(lf_row, False), _scan_lanes(lf_row, True))
    gate_row = lax.broadcasted_iota(jnp.int32, (16, M_CHUNK), 0)
    gaps = g_row - pltpu.roll(jnp.where(gate_row < 8, scans[0], scans[1]), 12, axis=0)
    gap_cols = jnp.concatenate([gaps, jnp.zeros((M_CHUNK - 16, M_CHUNK), F32)], axis=0).T
    outs = ((numf_ref, vecf_ref, clf_ref, row <= col, M_CHUNK - 1),
            (numb_ref, vecb_ref, clb_ref, row >= col, 0))
    for vec_ref in (vecf_ref, vecb_ref):
        vec_ref[0, c,12 + 2 * M_HEADS:VEC_ROWS, :] = jnp.zeros((VEC_ROWS - 12 - 2 * M_HEADS, LANES), F32)
    tail_row = lax.broadcasted_iota(jnp.int32, (C_ROWS - M_DV, M_CHUNK), 0)
    for h in range(M_HEADS):
        qk = qk_ref[0, toks, h * LANES:(h + 1) * LANES]
        q, k = qk[:, :M_DK], qk[:, M_DK:]
        vt = v_ref[0, toks, h * M_DV:(h + 1) * M_DV].astype(F32).T
        vt_bf = vt.astype(BF16)
        s_raw = lax.dot_general(k, q, (((1,), (1,)), ((), ())), preferred_element_type=F32)
        for d, (num_ref, vec_ref, cl_ref, allowed, last) in enumerate(outs):
            b_r = scans[d][8 * d + 4 + h:8 * d + 5 + h, :]
            i_r = g_row[8 * d + h:8 * d + h + 1, :]
            j = 8 * d + h
            b_end = b_r[:, last:last + 1]
            d_log = jnp.where(allowed, b_r + gap_cols[:, j:j + 1], -jnp.inf)
            m_intra = jnp.max(d_log, axis=0, keepdims=True)
            s = s_raw * jnp.exp(d_log - m_intra)
            num_ref[0, c, h] = jnp.dot(vt_bf, s.astype(BF16), preferred_element_type=F32)
            vec_ref[0, c,3 * h:3 * h + 1, :] = jnp.sum(s, axis=0, keepdims=True)
            vec_ref[0, c,3 * h + 1:3 * h + 2, :] = m_intra
            vec_ref[0, c,3 * h + 2:3 * h + 3, :] = b_r
            w_log = b_end - b_r + i_r
            m_loc = jnp.max(w_log, axis=-1, keepdims=True)
            w_row = jnp.exp(w_log - m_loc)
            vw = jnp.concatenate([vt * w_row, jnp.where(tail_row == 0, w_row, 0.0)], axis=0).astype(BF16)
            cl_ref[0, c, h] = jnp.dot(vw, k, preferred_element_type=F32)
            vec_ref[0, c,12 + 2 * h:13 + 2 * h, :] = jnp.broadcast_to(m_loc, (1, LANES))
            vec_ref[0, c,13 + 2 * h:14 + 2 * h, :] = jnp.broadcast_to(b_end, (1, LANES))


def _mlstm_scan_kernel(*refs, n_batch):
    ins, (hf_ref, hb_ref, cn_ref, m_ref) = refs[:8], refs[8:]

    @pl.when(pl.program_id(0) == 0)
    def _():
        cn_ref[...] = jnp.zeros_like(cn_ref)
        m_ref[...] = jnp.zeros_like(m_ref)

    for d, h_ref in enumerate((hf_ref, hb_ref)):
        qk_ref, num_ref, vec_ref, cl_ref = ins[4 * d:4 * d + 4]
        for b in range(n_batch):
            for h in range(M_HEADS):
                idx = (d * n_batch + b) * M_HEADS + h
                q = qk_ref[b, :, h * LANES:h * LANES + M_DK]
                row = lambda r: vec_ref[b, 0, r:r + 1, :]
                den_i, m_i, b_r = row(3 * h), row(3 * h + 1), row(3 * h + 2)
                m_loc, b_end = row(12 + 2 * h), row(13 + 2 * h)
                m_prev = m_ref[idx]
                cn = cn_ref[idx]

                inter = b_r + m_prev
                m_t = jnp.maximum(inter, m_i)
                a = jnp.exp(inter - m_t)
                e = jnp.exp(m_i - m_t)
                cq = lax.dot_general(cn.astype(BF16), q, (((1,), (1,)), ((), ())),
                                     preferred_element_type=F32)
                den = e * den_i + a * cq[M_DV:M_DV + 1, :]
                scale = 1.0 / jnp.maximum(jnp.abs(den), jnp.exp(-m_t))
                ht = (e * num_ref[b, 0, h] + a * cq[0:M_DV, :]) * scale
                h_ref[b, :, h * M_DV:(h + 1) * M_DV] = ht.T

                m_new = jnp.maximum(b_end + m_prev, m_loc)
                a_s = jnp.exp(b_end + m_prev - m_new)
                s_s = jnp.exp(m_loc - m_new)
                cn_ref[idx] = a_s[:, :M_DK] * cn + s_s[:, :M_DK] * cl_ref[b, 0, h]
                m_ref[idx] = m_new


def _mlstm(p, g, gate_b, ctx_chunks):
    B, L, _ = p.shape
    nc = L // M_CHUNK
    width = M_HEADS * M_DV
    bias = jnp.zeros((1, G_WIDTH), F32).at[0, :16].set(gate_b)
    assert nc % INTRA_CHUNKS == 0
    tok = lambda w, cb=0: pl.BlockSpec((1, INTRA_CHUNKS * M_CHUNK, w), lambda b, c: (b, c, cb))
    num_shape, vec_shape, cl_shape = (M_HEADS, M_DV, M_CHUNK), (VEC_ROWS, LANES), (M_HEADS, C_ROWS, M_DK)
    per_chunk = lambda s: pl.BlockSpec((1, INTRA_CHUNKS) + s, lambda b, c: (b, c) + (0,) * len(s))
    f32 = lambda *s: jax.ShapeDtypeStruct(s, F32)
    numf, numb, vecf, vecb, clf, clb = pl.pallas_call(
        _mlstm_intra_kernel,
        out_shape=(f32(B, nc, *num_shape),) * 2 + (f32(B, nc, *vec_shape),) * 2 + (f32(B, nc, *cl_shape),) * 2,
        grid=(B, nc // INTRA_CHUNKS),
        in_specs=[tok(width, P_QK // width), tok(width, P_MV // width), tok(G_WIDTH),
                  pl.BlockSpec((1, G_WIDTH), lambda b, c: (0, 0))],
        out_specs=(per_chunk(num_shape),) * 2 + (per_chunk(vec_shape),) * 2 + (per_chunk(cl_shape),) * 2,
        compiler_params=_params(("parallel", "parallel")),
        name="mlstm_intra",
    )(p, p, g, bias)

    fwd = lambda j: j
    bwd = lambda j: jnp.where(j < ctx_chunks, ctx_chunks - 1 - j, nc - 1 + ctx_chunks - j)
    stok = lambda cm, w, cb=0: pl.BlockSpec((B, M_CHUNK, w), lambda j: (0, cm(j), cb))
    schunk = lambda cm, s: pl.BlockSpec((B, 1) + s, lambda j: (0, cm(j)) + (0,) * len(s))
    side = lambda cm: [stok(cm, width, P_QK // width), schunk(cm, num_shape), schunk(cm, vec_shape),
                       schunk(cm, cl_shape)]
    chains = 2 * B * M_HEADS
    return pl.pallas_call(
        functools.partial(_mlstm_scan_kernel, n_batch=B),
        out_shape=(f32(B, L, width),) * 2,
        grid=(nc,),
        in_specs=side(fwd) + side(bwd),
        out_specs=(stok(fwd, width), stok(bwd, width)),
        scratch_shapes=[pltpu.VMEM((chains, C_ROWS, M_DK), F32),
                        pltpu.VMEM((chains, 1, LANES), F32)],
        compiler_params=_params(("arbitrary",)),
        name="mlstm_scan",
    )(p, numf, vecf, clf, p, numb, vecb, clb)


def _head_norm_rope(x, w, cos, sin, bd):
    sq = x * x
    hi = sq.astype(BF16)
    lo = (sq - hi.astype(F32)).astype(BF16)
    ms = jnp.dot(hi, bd, preferred_element_type=F32) + jnp.dot(lo, bd, preferred_element_type=F32)
    y = x * lax.rsqrt(ms + EPS) * w
    lane = lax.broadcasted_iota(jnp.int32, y.shape, 1)
    partner = jnp.where(lane % A_DH < A_DH // 2,
                        pltpu.roll(y, LANES - A_DH // 2, axis=1), pltpu.roll(y, A_DH // 2, axis=1))
    return y * cos + partner * sin


def _attn_prep_kernel(q_ref, kv_ref, cos_ref, sin_ref, qw_ref, kw_ref, qt_ref, k_ref, vt_ref, *, q_scale):
    r = lax.broadcasted_iota(jnp.int32, (LANES, LANES), 0) // A_DH
    c = lax.broadcasted_iota(jnp.int32, (LANES, LANES), 1) // A_DH
    bd = jnp.where(r == c, 1.0 / A_DH, 0.0).astype(BF16)
    cos, sin = cos_ref[...], sin_ref[...]
    for pair in range(A_HEADS // 2):
        x = q_ref[0, :, pair * LANES:(pair + 1) * LANES].astype(F32)
        y = _head_norm_rope(x, qw_ref[...], cos, sin, bd) * q_scale
        qt_ref[0, pair * LANES:(pair + 1) * LANES, :] = y.T.astype(BF16)
    kv = kv_ref[0].astype(F32)
    k = _head_norm_rope(kv[:, :LANES], kw_ref[...], cos, sin, bd).astype(BF16)
    for kvh in range(A_KV_HEADS):
        k_ref[0, kvh, 0] = k[:, kvh * A_DH:(kvh + 1) * A_DH]
    vt = kv[:, LANES:].T.astype(BF16)
    ones = jnp.ones((VT_ROWS - A_DH, vt.shape[1]), BF16)
    for kvh in range(A_KV_HEADS):
        vt_ref[0, kvh, 0] = jnp.concatenate([vt[kvh * A_DH:(kvh + 1) * A_DH, :], ones], axis=0)


def _attn_prep(p, cos, sin, qw, kw, ts):
    B, L, _ = p.shape
    per = ts // ROW_TILE
    nblk = L // ts
    q_scale = A_DH ** -0.5 * math.log2(math.e)
    return pl.pallas_call(
        functools.partial(_attn_prep_kernel, q_scale=q_scale),
        out_shape=(jax.ShapeDtypeStruct((B, A_HEADS * A_DH, L), BF16),
                   jax.ShapeDtypeStruct((B, A_KV_HEADS, nblk, ts, A_DH), BF16),
                   jax.ShapeDtypeStruct((B, A_KV_HEADS, nblk, VT_ROWS, ts), BF16)),
        grid=(B, L // ROW_TILE),
        in_specs=[pl.BlockSpec((1, ROW_TILE, A_HEADS * A_DH), lambda b, t: (b, t, P_AQ // (A_HEADS * A_DH))),
                  pl.BlockSpec((1, ROW_TILE, 2 * LANES), lambda b, t: (b, t, P_AKV // (2 * LANES))),
                  pl.BlockSpec((ROW_TILE, LANES), lambda b, t: (t, 0)),
                  pl.BlockSpec((ROW_TILE, LANES), lambda b, t: (t, 0)),
                  pl.BlockSpec((1, LANES), lambda b, t: (0, 0)),
                  pl.BlockSpec((1, LANES), lambda b, t: (0, 0))],
        out_specs=(pl.BlockSpec((1, A_HEADS * A_DH, ROW_TILE), lambda b, t: (b, 0, t)),
                   pl.BlockSpec((1, A_KV_HEADS, 1, ROW_TILE, A_DH), lambda b, t: (b, 0, t // per, t % per, 0)),
                   pl.BlockSpec((1, A_KV_HEADS, 1, VT_ROWS, ROW_TILE), lambda b, t: (b, 0, t // per, 0, t % per))),
        compiler_params=_params(("parallel", "arbitrary")),
        name="attn_prep",
    )(p, p, cos, sin, qw, kw)


ATT_SUB = 256
ATT_PIECE = 128
VT_ROWS = A_DH + 16


def _attn_kernel(qt_ref, k_ref, vt_ref, o_ref, sa_ref, sb_ref, ma_ref, mb_ref, acc_ref,
                 *, nblk, nsub, tq, ctx_tiles, ctx_sub, q_tile0):
    q_of = lambda g: qt_ref[0, g * A_DH:(g + 1) * A_DH, :]
    n = A_GROUP * tq
    lanes = lambda g: slice(g * tq, (g + 1) * tq)
    keys = lambda r: pl.ds(r * ATT_SUB, ATT_SUB)

    def step(nxt, cur, ms, subs):
        out = []
        for g in range(A_GROUP):
            if cur is not None:
                ci, cs_ref, cm_ref = cur
                m_new = jnp.maximum(ms[g], cm_ref[:, lanes(g)])
                alpha = jnp.exp2(ms[g] - m_new)
            best, pv = None, None
            for r in range(subs):
                parts = []
                for piece in range(ATT_SUB // ATT_PIECE):
                    rows = pl.ds(r * ATT_SUB + piece * ATT_PIECE, ATT_PIECE)
                    if nxt is not None:
                        ni, ns_ref, _ = nxt
                        s = jnp.dot(k_ref[0, 0, ni, rows, :], q_of(g), preferred_element_type=F32)
                        ns_ref[g, rows, :] = s
                        top = jnp.max(s, axis=0, keepdims=True)
                        best = top if best is None else jnp.maximum(best, top)
                    if cur is not None:
                        parts.append(jnp.exp2(cs_ref[g, rows, :] - m_new).astype(BF16))
                if cur is not None:
                    p = jnp.concatenate(parts, axis=0)
                    d = jnp.dot(vt_ref[0, 0, ci, :, keys(r)], p, preferred_element_type=F32)
                    pv = d if pv is None else pv + d
            if nxt is not None:
                nxt[2][:, lanes(g)] = best
            if cur is not None:
                acc_ref[:, lanes(g)] = alpha * acc_ref[:, lanes(g)] + pv
                out.append(m_new)
            else:
                out.append(ms[g])
        return tuple(out)

    def finish():
        o = acc_ref[0:A_DH, :] / acc_ref[A_DH:A_DH + 1, :]
        o = jnp.concatenate([o[:, lanes(g)] for g in range(A_GROUP)], axis=0)
        o_ref[0] = o.T.astype(BF16)

    acc_ref[...] = jnp.zeros_like(acc_ref)
    init = (jnp.full((1, tq), -jnp.inf, F32),) * A_GROUP
    is_ctx = pl.program_id(2) + q_tile0 < ctx_tiles
    buf_a, buf_b = (sa_ref, ma_ref), (sb_ref, mb_ref)

    @pl.when(is_ctx)
    def _():
        step((0, *buf_a), None, init, ctx_sub)
        step(None, (0, *buf_a), init, ctx_sub)
        finish()

    @pl.when(jnp.logical_not(is_ctx))
    def _():
        step((0, *buf_a), None, init, nsub)

        def pair(j, ms):
            i = 2 * j
            ms = step((i + 1, *buf_b), (i, *buf_a), ms, nsub)
            return step((i + 2, *buf_a), (i + 1, *buf_b), ms, nsub)

        ms = lax.fori_loop(0, (nblk - 1) // 2, pair, init)
        step(None, (nblk - 1, *buf_a), ms, nsub)
        finish()


def _attention(qt, k, vt, *, q_tile0, n_ctx, tq):
    B, _, L = qt.shape
    nblk, ts = k.shape[2], k.shape[3]
    assert n_ctx <= ts and n_ctx % tq == 0 and n_ctx % ATT_SUB == 0 and ts % ATT_SUB == 0 and nblk % 2 == 1
    width = A_GROUP * A_DH
    n = A_GROUP * tq
    s_buf, m_buf = pltpu.VMEM((A_GROUP, ts, tq), F32), pltpu.VMEM((1, n), F32)
    return pl.pallas_call(
        functools.partial(_attn_kernel, nblk=nblk, nsub=ts // ATT_SUB, tq=tq, ctx_tiles=n_ctx // tq,
                          ctx_sub=n_ctx // ATT_SUB, q_tile0=q_tile0),
        out_shape=jax.ShapeDtypeStruct((B, L - q_tile0 * tq, A_HEADS * A_DH), BF16),
        grid=(B, A_KV_HEADS, L // tq - q_tile0),
        in_specs=[pl.BlockSpec((1, width, tq), lambda b, kv, t: (b, kv, t + q_tile0)),
                  pl.BlockSpec((1, 1, nblk, ts, A_DH), lambda b, kv, t: (b, kv, 0, 0, 0)),
                  pl.BlockSpec((1, 1, nblk, VT_ROWS, ts), lambda b, kv, t: (b, kv, 0, 0, 0))],
        out_specs=pl.BlockSpec((1, tq, width), lambda b, kv, t: (b, t, kv)),
        scratch_shapes=[s_buf, s_buf, m_buf, m_buf, pltpu.VMEM((VT_ROWS, n), F32)],
        compiler_params=_params(("parallel", "parallel", "arbitrary")),
        name="attention",
    )(qt, k, vt)


def _mixer_out_kernel(*refs, with_router):
    (hf_ref, hb_ref, mo_ref, a_ref, x_ref, mod_ref, mnw_ref, n2w_ref, wout_ref) = refs[:9]
    hs = hf_ref[0] + hb_ref[0]
    hn = jnp.concatenate([_rms_rows(hs[:, h * M_DV:(h + 1) * M_DV]) for h in range(M_HEADS)], axis=1)
    m = hn * mnw_ref[...] * _sigmoid(mo_ref[0].astype(F32))
    y_in = jnp.concatenate([m.astype(BF16), a_ref[0]], axis=1)
    mod = mod_ref[0, 0]
    x1 = x_ref[0] + mod[2:3] * jnp.dot(y_in, wout_ref[...], preferred_element_type=F32)
    h2 = _rms_rows(x1) * n2w_ref[...] * (1.0 + mod[4:5]) + mod[3:4]
    if not with_router:
        x1_ref, h2_ref = refs[9:]
        x1_ref[0] = x1
        h2_ref[0] = h2.astype(BF16)
        return
    router_ref, x1_ref, h2_ref, ids_ref, gates_ref = refs[9:]
    x1_ref[0] = x1
    h2_ref[0] = h2
    h_hi = h2.astype(BF16)
    h_lo = (h2 - h_hi.astype(F32)).astype(BF16)
    logits = (jnp.dot(h_hi, router_ref[0], preferred_element_type=F32)
              + jnp.dot(h_lo, router_ref[0], preferred_element_type=F32)
              + jnp.dot(h_hi, router_ref[1], preferred_element_type=F32))
    lane = lax.broadcasted_iota(jnp.int32, logits.shape, 1)
    logits = jnp.where(lane < N_EXPERTS, logits, -jnp.inf)
    m1 = jnp.max(logits, axis=-1, keepdims=True)
    i1 = jnp.min(jnp.where(logits == m1, lane, LANES), axis=-1, keepdims=True)
    rest = jnp.where(lane == i1, -jnp.inf, logits)
    m2 = jnp.max(rest, axis=-1, keepdims=True)
    i2 = jnp.min(jnp.where(rest == m2, lane, LANES), axis=-1, keepdims=True)
    e2 = jnp.exp(m2 - m1)
    g1 = 1.0 / (1.0 + e2)
    ids_ref[0] = jnp.where(lane == 0, i1, jnp.where(lane == 1, i2, -1))
    gates_ref[0] = jnp.where(lane == 0, g1, jnp.where(lane == 1, e2 * g1, 0.0))


def _mixer_out(hf, hb, p, a, xa, mods, mnw, n2w, wout, router=None, row_off=0):
    B, L, D = xa.shape
    nt = L // ROW_TILE - row_off
    rin = lambda w, cb=0: pl.BlockSpec((1, ROW_TILE, w), lambda b, t: (b, t + row_off, cb))
    rout = lambda w: pl.BlockSpec((1, ROW_TILE, w), lambda b, t: (b, t, 0))
    full = lambda arr: pl.BlockSpec(arr.shape, lambda b, t: (0,) * arr.ndim)
    mw = M_HEADS * M_DV
    a_off = row_off - (L - a.shape[1]) // ROW_TILE
    a_spec = pl.BlockSpec((1, ROW_TILE, A_HEADS * A_DH), lambda b, t: (b, t + a_off, 0))
    in_specs = [rin(mw), rin(mw), rin(mw, P_MO // mw), a_spec, rin(D), _mod_spec(row_off),
                pl.BlockSpec((1, mw), lambda b, t: (0, 0)), pl.BlockSpec((1, D), lambda b, t: (0, 0)),
                full(wout)]
    args = [hf, hb, p, a, xa, mods, mnw.reshape(1, mw), n2w.reshape(1, D), wout]
    rows = nt * ROW_TILE
    out_shape = [jax.ShapeDtypeStruct((B, rows, D), F32),
                 jax.ShapeDtypeStruct((B, rows, D), BF16 if router is None else F32)]
    out_specs = [rout(D), rout(D)]
    if router is not None:
        in_specs.append(full(router))
        args.append(router)
        out_shape += [jax.ShapeDtypeStruct((B, rows, LANES), jnp.int32), jax.ShapeDtypeStruct((B, rows, LANES), F32)]
        out_specs += [rout(LANES), rout(LANES)]
    return pl.pallas_call(
        functools.partial(_mixer_out_kernel, with_router=router is not None),
        out_shape=tuple(out_shape),
        grid=(B, nt),
        in_specs=in_specs,
        out_specs=tuple(out_specs),
        compiler_params=_params(("parallel", "arbitrary")),
        name="mixer_out",
    )(*args)


def _ffn_kernel(h_ref, x_ref, mod_ref, wg_ref, wu_ref, wd_ref, o_ref):
    h = h_ref[0]
    g = jnp.dot(h, wg_ref[...], preferred_element_type=F32)
    u = jnp.dot(h, wu_ref[...], preferred_element_type=F32)
    act = (g * _sigmoid(g) * u).astype(BF16)
    y = jnp.dot(act, wd_ref[...], preferred_element_type=F32)
    o_ref[0] = x_ref[0] + mod_ref[0, 0][5:6] * y


def _ffn(h2, x1, mods, wg, wu, wd):
    B, L, D = x1.shape
    row = pl.BlockSpec((1, ROW_TILE, D), lambda b, t: (b, t, 0))
    const = lambda arr: pl.BlockSpec(arr.shape, lambda b, t: (0, 0), pipeline_mode=pl.Buffered(1))
    return pl.pallas_call(
        _ffn_kernel,
        out_shape=jax.ShapeDtypeStruct((B, L, D), F32),
        grid=(B, L // ROW_TILE),
        in_specs=[row, row, _mod_spec(), const(wg), const(wu), const(wd)],
        out_specs=row,
        compiler_params=_params(("parallel", "arbitrary")),
        name="ffn_swiglu",
    )(h2, x1, mods, wg, wu, wd)


MOE_TM = 512
MOE_FF = 1792
MOE_TD = 256
RANK_TILE = 1024
POS_TILE = 2048
ISSUE_UNROLL = 8


def _moe_rank_kernel(ids_ref, rank_ref, cnt_ref, carry_ref, before_ref):
    @pl.when(pl.program_id(0) == 0)
    def _():
        carry_ref[...] = jnp.zeros_like(carry_ref)
        r = lax.broadcasted_iota(jnp.int32, (RANK_TILE, RANK_TILE), 0)
        c = lax.broadcasted_iota(jnp.int32, (RANK_TILE, RANK_TILE), 1)
        before_ref[...] = jnp.where(c < r, 1.0, 0.0).astype(BF16)

    ids = ids_ref[...]
    lane = lax.broadcasted_iota(jnp.int32, ids.shape, 1)
    onehot = jnp.where(jnp.logical_or(lane == ids[:, 0:1], lane == ids[:, 1:2]), 1.0, 0.0)
    rank_ref[...] = jnp.dot(before_ref[...], onehot.astype(BF16), preferred_element_type=F32) + carry_ref[...]
    carry_ref[...] += jnp.sum(onehot, axis=0, keepdims=True)
    cnt_ref[...] = carry_ref[...]


def _moe_rank(ids):
    n = ids.shape[0]
    return pl.pallas_call(
        _moe_rank_kernel,
        out_shape=(jax.ShapeDtypeStruct((n, LANES), F32), jax.ShapeDtypeStruct((1, LANES), F32)),
        grid=(n // RANK_TILE,),
        in_specs=[pl.BlockSpec((RANK_TILE, LANES), lambda t: (t, 0))],
        out_specs=(pl.BlockSpec((RANK_TILE, LANES), lambda t: (t, 0)), pl.BlockSpec((1, LANES), lambda t: (0, 0))),
        scratch_shapes=[pltpu.VMEM((1, LANES), F32), pltpu.VMEM((RANK_TILE, RANK_TILE), BF16)],
        compiler_params=_params(("arbitrary",)),
        name="moe_rank",
    )(ids)


def _moe_pos_kernel(ids_ref, rank_ref, start_ref, pos_ref):
    ids = ids_ref[...]
    lane = lax.broadcasted_iota(jnp.int32, ids.shape, 1)
    tgt = start_ref[...] + rank_ref[...]
    p0 = jnp.sum(jnp.where(lane == ids[:, 0:1], tgt, 0.0), axis=-1, keepdims=True)
    p1 = jnp.sum(jnp.where(lane == ids[:, 1:2], tgt, 0.0), axis=-1, keepdims=True)
    pos_ref[...] = jnp.where(lane == 0, p0, jnp.where(lane == 1, p1, 0.0)).astype(jnp.int32)


def _moe_pos(ids, rank, start_row):
    n = ids.shape[0]
    blk = pl.BlockSpec((POS_TILE, LANES), lambda t: (t, 0))
    return pl.pallas_call(
        _moe_pos_kernel,
        out_shape=jax.ShapeDtypeStruct((n, LANES), jnp.int32),
        grid=(n // POS_TILE,),
        in_specs=[blk, blk, pl.BlockSpec((1, LANES), lambda t: (0, 0))],
        out_specs=blk,
        compiler_params=_params(("parallel",)),
        name="moe_pos",
    )(ids, rank, start_row)


def _row_copy(src, src_row, dst, dst_row, sem):
    return pltpu.make_async_copy(src.at[pl.ds(src_row, 1), :], dst.at[pl.ds(dst_row, 1), :], sem)


def _moe_dispatch_kernel(pad_ref, pos_ref, h_ref, xs_ref, zero_ref, sem):
    @pl.when(pl.program_id(0) == 0)
    def _():
        zero_ref[...] = jnp.zeros_like(zero_ref)
        fills = [pltpu.make_async_copy(
            zero_ref, xs_ref.at[pl.ds(pl.multiple_of(pad_ref[e], SUBLANES), MOE_TM + SUBLANES), :], sem)
            for e in range(N_EXPERTS)]
        for cp in fills:
            cp.start()
        for cp in fills:
            cp.wait()

        def fill_tile(j, carry):
            cp = pltpu.make_async_copy(zero_ref.at[pl.ds(0, MOE_TM), :],
                                       xs_ref.at[pl.ds(pl.multiple_of(j * MOE_TM, MOE_TM), MOE_TM), :], sem)
            cp.start()
            cp.wait()
            return carry

        lax.fori_loop(pad_ref[N_EXPERTS], xs_ref.shape[0] // MOE_TM, fill_tile, 0)

    def issue(r, carry):
        for k in range(2):
            _row_copy(h_ref, r, xs_ref, pos_ref[0, 0, 2 * r + k], sem).start(priority=k)
        return carry

    lax.fori_loop(0, MOE_TD, issue, 0, unroll=ISSUE_UNROLL)
    for k in range(2):
        pltpu.make_async_copy(h_ref, xs_ref.at[pl.ds(0, MOE_TD), :], sem).wait()


def _moe_dispatch(h, pos, fill_meta, ns):
    n, d = h.shape
    return pl.pallas_call(
        _moe_dispatch_kernel,
        out_shape=jax.ShapeDtypeStruct((ns, d), F32),
        grid_spec=pltpu.PrefetchScalarGridSpec(
            num_scalar_prefetch=1,
            grid=(n // MOE_TD,),
            in_specs=[pl.BlockSpec((1, 1, 2 * MOE_TD), lambda t, pad: (t, 0, 0), memory_space=pltpu.SMEM),
                      pl.BlockSpec((MOE_TD, d), lambda t, pad: (t, 0))],
            out_specs=pl.BlockSpec(memory_space=pl.ANY),
            scratch_shapes=[pltpu.VMEM((MOE_TM + SUBLANES, d), F32), pltpu.SemaphoreType.DMA(())]),
        compiler_params=_params(("arbitrary",)),
        name="moe_dispatch",
    )(fill_meta, pos, h)


def _moe_group_kernel(te_ref, nv_ref, xs_ref, wg_ref, wu_ref, wd_ref, ys_ref, acc_ref):
    i, f = pl.program_id(0), pl.program_id(1)
    last = pl.num_programs(1) - 1
    valid = i < nv_ref[0]

    @pl.when(valid)
    def _():
        @pl.when(f == 0)
        def _():
            acc_ref[...] = jnp.zeros_like(acc_ref)

        x = xs_ref[...].astype(BF16)
        g = jnp.dot(x, wg_ref[0], preferred_element_type=F32)
        u = jnp.dot(x, wu_ref[0], preferred_element_type=F32)
        act = (g * _sigmoid(g) * u).astype(BF16)
        acc_ref[...] += jnp.dot(act, wd_ref[0], preferred_element_type=F32)

        @pl.when(f == last)
        def _():
            ys_ref[...] = acc_ref[...]

    @pl.when(jnp.logical_and(jnp.logical_not(valid), f == last))
    def _():
        ys_ref[...] = jnp.zeros_like(ys_ref)


def _moe_group(xs, tile_expert, n_valid, wg, wu, wd):
    ns, d = xs.shape
    n_tiles = ns // MOE_TM - 1
    ff = wg.shape[2]
    live = lambda i, nv: i < nv[0]
    return pl.pallas_call(
        _moe_group_kernel,
        out_shape=jax.ShapeDtypeStruct((n_tiles * MOE_TM, d), F32),
        grid_spec=pltpu.PrefetchScalarGridSpec(
            num_scalar_prefetch=2,
            grid=(n_tiles, ff // MOE_FF),
            in_specs=[pl.BlockSpec((MOE_TM, d), lambda i, f, te, nv: (jnp.where(live(i, nv), i, 0), 0)),
                      pl.BlockSpec((1, d, MOE_FF), lambda i, f, te, nv: (te[i], 0, f)),
                      pl.BlockSpec((1, d, MOE_FF), lambda i, f, te, nv: (te[i], 0, f)),
                      pl.BlockSpec((1, MOE_FF, d), lambda i, f, te, nv: (te[i], f, 0))],
            out_specs=pl.BlockSpec((MOE_TM, d), lambda i, f, te, nv: (i, 0)),
            scratch_shapes=[pltpu.VMEM((MOE_TM, d), F32)]),
        compiler_params=_params(("arbitrary", "arbitrary")),
        name="moe_group",
    )(tile_expert, n_valid, xs, wg, wu, wd)


def _moe_combine_kernel(pos_ref, x_ref, gates_ref, mod_ref, fw_ref, ys_ref, o_ref, ybuf, sem):
    def issue(r, carry):
        for k in range(2):
            _row_copy(ys_ref, pos_ref[0, 0, 2 * r + k], ybuf.at[k], r, sem).start(priority=k)
        return carry

    lax.fori_loop(0, MOE_TD, issue, 0, unroll=ISSUE_UNROLL)
    for k in range(2):
        pltpu.make_async_copy(ys_ref.at[pl.ds(0, MOE_TD), :], ybuf.at[k], sem).wait()
    gates = gates_ref[...]
    y = gates[:, 0:1] * ybuf[0] + gates[:, 1:2] * ybuf[1]
    x2 = x_ref[...] + mod_ref[0, 0][5:6] * y
    o_ref[...] = _rms_rows(x2) * fw_ref[...]


def _moe_combine(pos, x1, gates, mods, fw, ys, tokens_per_sample):
    n, d = x1.shape
    per = tokens_per_sample // MOE_TD
    return pl.pallas_call(
        _moe_combine_kernel,
        out_shape=jax.ShapeDtypeStruct((n, d), F32),
        grid=(n // MOE_TD,),
        in_specs=[pl.BlockSpec((1, 1, 2 * MOE_TD), lambda t: (t, 0, 0), memory_space=pltpu.SMEM),
                  pl.BlockSpec((MOE_TD, d), lambda t: (t, 0)),
                  pl.BlockSpec((MOE_TD, LANES), lambda t: (t, 0)),
                  pl.BlockSpec((1, 1, 6, d), lambda t: (t // per, 1, 0, 0)),
                  pl.BlockSpec((1, d), lambda t: (0, 0)),
                  pl.BlockSpec(memory_space=pl.ANY)],
        out_specs=pl.BlockSpec((MOE_TD, d), lambda t: (t, 0)),
        scratch_shapes=[pltpu.VMEM((2, MOE_TD, d), F32), pltpu.SemaphoreType.DMA(())],
        compiler_params=_params(("arbitrary",)),
        name="moe_combine",
    )(pos, x1, gates, mods, fw.reshape(1, d), ys)


def _moe(h2, ids, gates, x1, mods, wg, wu, wd, fw):
    B, T, D = x1.shape
    n = B * T
    ids, gates = ids.reshape(n, LANES), gates.reshape(n, LANES)
    rank, cnt = _moe_rank(ids)
    cnt = cnt[0, :N_EXPERTS].astype(jnp.int32)
    padded = (cnt + MOE_TM - 1) // MOE_TM * MOE_TM
    end = jnp.cumsum(padded)
    start = end - padded
    n_tiles = 2 * n // MOE_TM + N_EXPERTS
    tile_expert = jnp.minimum(jnp.sum(jnp.arange(n_tiles)[:, None] >= (end // MOE_TM)[None, :], axis=1),
                              N_EXPERTS - 1).astype(jnp.int32)
    n_valid = (end[-1:] // MOE_TM).astype(jnp.int32)
    start_row = jnp.zeros((1, LANES), F32).at[0, :N_EXPERTS].set(start.astype(F32))
    pos = _moe_pos(ids, rank, start_row)
    pos = pos[:, :2].reshape(n // MOE_TD, 1, 2 * MOE_TD)
    fill_meta = jnp.concatenate([(start + cnt) // SUBLANES * SUBLANES, n_valid]).astype(jnp.int32)
    xs = _moe_dispatch(h2.reshape(n, D), pos, fill_meta, (n_tiles + 1) * MOE_TM)
    ys = _moe_group(xs, tile_expert, n_valid, wg, wu, wd)
    return _moe_combine(pos, x1.reshape(n, D), gates, mods, fw, ys, T).reshape(B, T, D)


_ROT_PERM = np.concatenate([np.arange(0, A_DH, 2), np.arange(1, A_DH, 2)])


def _prep_w_in(w):
    o = np.cumsum([0, M_HEADS * M_DK, M_HEADS * M_DK, M_HEADS * M_DV, M_HEADS * M_DV, 4 * M_HEADS,
                   A_HEADS * A_DH, A_KV_HEADS * A_DH, A_KV_HEADS * A_DH])
    mq, mk, mv, mo, mg, aq, ak, av = [w[:, o[i]:o[i + 1]] for i in range(8)]
    qk = jnp.concatenate([jnp.concatenate([mq[:, h * M_DK:(h + 1) * M_DK] * (M_DK ** -0.5),
                                           mk[:, h * M_DK:(h + 1) * M_DK]], axis=1) for h in range(M_HEADS)], axis=1)
    perm_q = np.concatenate([h * A_DH + _ROT_PERM for h in range(A_HEADS)])
    perm_k = np.concatenate([h * A_DH + _ROT_PERM for h in range(A_KV_HEADS)])
    pad = jnp.zeros((w.shape[0], G_WIDTH - 4 * M_HEADS), w.dtype)
    return jnp.concatenate([qk, mv, mo, aq[:, perm_q], ak[:, perm_k], av, mg, pad], axis=1).astype(BF16)


def _rope_tables(n_tok, n_ctx):
    rows = n_tok // GRID_W
    row = jnp.broadcast_to(jnp.arange(rows, dtype=F32)[:, None], (rows, GRID_W)).reshape(n_tok)
    col = jnp.broadcast_to(jnp.arange(GRID_W, dtype=F32)[None, :], (rows, GRID_W)).reshape(n_tok)
    n_freq = A_DH // 4
    inv_freq = ROPE_THETA ** (-jnp.arange(n_freq, dtype=F32) / n_freq)
    ang = jnp.concatenate([row[:, None] * inv_freq, col[:, None] * inv_freq], axis=-1)
    cos, sin = jnp.cos(ang), jnp.sin(ang)
    cos = jnp.concatenate([jnp.ones((n_ctx, A_DH // 2), F32), cos], axis=0)
    sin = jnp.concatenate([jnp.zeros((n_ctx, A_DH // 2), F32), sin], axis=0)
    return jnp.tile(cos, (1, 4)), jnp.tile(jnp.concatenate([-sin, sin], axis=1), (1, 2))


def kernel(x, c, ctx, c_ctx, ada_w, ada_b, norm1_w, norm2_w, w_in, mlstm_gate_b, mlstm_norm_w, q_norm_w, k_norm_w,
           w_out, ffn_w_gate, ffn_w_up, ffn_w_down, moe_router, moe_w_gate, moe_w_up, moe_w_down, final_norm_w):
    B, T, D = x.shape
    n_ctx = ctx.shape[1]
    L = n_ctx + T
    depth = w_in.shape[0]
    assert D == D_MODEL and n_ctx == ROW_TILE and T % RANK_TILE == 0 and depth == 2
    ts = 3 * ROW_TILE
    assert L % ts == 0
    ctx_tiles = n_ctx // ROW_TILE
    tq = ROW_TILE

    xa = jnp.concatenate([ctx, x], axis=1)
    cvec = jnp.concatenate([c, c_ctx[None], jnp.zeros((8 - B - 1, D), F32)], axis=0)
    cos, sin = _rope_tables(T, n_ctx)
    out = None
    for i in range(depth):
        last = i == depth - 1
        modraw = _ada(cvec, ada_w[i], ada_b[i])
        mods = jnp.stack([jnp.broadcast_to(modraw[B].reshape(1, 6, D), (B, 6, D)),
                          modraw[:B].reshape(B, 6, D)], axis=1)
        p, g = _in_proj(xa, mods, norm1_w[i], _prep_w_in(w_in[i]))
        hf, hb = _mlstm(p, g, mlstm_gate_b[i], n_ctx // M_CHUNK)
        qw = jnp.tile(q_norm_w[i][_ROT_PERM], 2).reshape(1, LANES)
        kw = jnp.tile(k_norm_w[i][_ROT_PERM], 2).reshape(1, LANES)
        qt, k, vt = _attn_prep(p, cos, sin, qw, kw, ts)
        a = _attention(qt, k, vt, q_tile0=ctx_tiles if last else 0, n_ctx=n_ctx, tq=tq)
        wout = w_out[i].astype(BF16)
        if not last:
            x1, h2 = _mixer_out(hf, hb, p, a, xa, mods, mlstm_norm_w[i], norm2_w[i], wout)
            j = i // 2
            xa = _ffn(h2, x1, mods, ffn_w_gate[j].astype(BF16), ffn_w_up[j].astype(BF16),
                      ffn_w_down[j].astype(BF16))
        else:
            j = i // 2
            router = jnp.zeros((D, LANES), F32).at[:, :N_EXPERTS].set(moe_router[j])
            router_hi = router.astype(BF16)
            router = jnp.stack([router_hi, (router - router_hi.astype(F32)).astype(BF16)])
            x1, h2, ids, gates = _mixer_out(hf, hb, p, a, xa, mods, mlstm_norm_w[i], norm2_w[i], wout,
                                            router=router, row_off=ctx_tiles)
            out = _moe(h2, ids, gates, x1, mods, moe_w_gate[j].astype(BF16), moe_w_up[j].astype(BF16),
                       moe_w_down[j].astype(BF16), final_norm_w)
    return out
```

```python
import functools
import math

import numpy as np
import jax
import jax.numpy as jnp
from jax import lax
from jax.experimental import pallas as pl
from jax.experimental.pallas import tpu as pltpu

F32 = jnp.float32
BF16 = jnp.bfloat16
HIGHEST = lax.Precision.HIGHEST

D_MODEL = 1024
GRID_W = 64
M_HEADS = 4
M_DV = 128
M_DK = 64
M_CHUNK = 128
A_HEADS = 8
A_KV_HEADS = 2
A_GROUP = A_HEADS // A_KV_HEADS
A_DH = 64
ROPE_THETA = 10000.0
N_EXPERTS = 8
EPS = 1e-6

LANES = 128
SUBLANES = 8
ROW_TILE = 256
VMEM_LIMIT = 56 * 1024 * 1024

P_QK = 0
P_MV = 512
P_MO = 1024
P_AQ = 1536
P_AKV = 2048
P_WIDTH = 2304
G_WIDTH = LANES


def _params(sem, vmem=VMEM_LIMIT, flags=None):
    return pltpu.CompilerParams(dimension_semantics=sem, vmem_limit_bytes=vmem, flags=flags)


def _sigmoid(x):
    return 1.0 / (1.0 + jnp.exp(-x))


def _rms_rows(x):
    return x * lax.rsqrt(jnp.mean(x * x, axis=-1, keepdims=True) + EPS)


def _ada_kernel(c_ref, w_ref, b_ref, o_ref):
    c = c_ref[...]
    s = c * _sigmoid(c)
    o_ref[...] = jnp.dot(s, w_ref[...], precision=HIGHEST, preferred_element_type=F32) + b_ref[...]


def _ada(cvec, w, b):
    n = w.shape[1]
    bn = 1536
    return pl.pallas_call(
        _ada_kernel,
        out_shape=jax.ShapeDtypeStruct((cvec.shape[0], n), F32),
        grid=(n // bn,),
        in_specs=[pl.BlockSpec(cvec.shape, lambda j: (0, 0)),
                  pl.BlockSpec((w.shape[0], bn), lambda j: (0, j)),
                  pl.BlockSpec((1, bn), lambda j: (0, j))],
        out_specs=pl.BlockSpec((cvec.shape[0], bn), lambda j: (0, j)),
        compiler_params=_params(("arbitrary",)),
        name="ada_mod",
    )(cvec, w, b.reshape(1, n))


def _mod_spec(off=0):
    return pl.BlockSpec((1, 1, 6, D_MODEL), lambda b, t: (b, jnp.minimum(t + off, 1), 0, 0))


def _stream_specs(stream, off=0):
    tile = (1, ROW_TILE, D_MODEL)
    if not isinstance(stream, tuple):
        return [pl.BlockSpec(tile, lambda b, t: (b, t + off, 0))], [stream]
    return ([pl.BlockSpec(tile, lambda b, t: (b, 0, 0)),
             pl.BlockSpec(tile, lambda b, t: (b, jnp.maximum(t + off - 1, 0), 0))], list(stream))


def _stream_tile(refs, off=0):
    if len(refs) == 1:
        return refs[0][0]
    return jnp.where(pl.program_id(1) + off == 0, refs[0][0], refs[1][0])


def _in_proj_kernel(*refs, n_stream):
    mod_ref, nw_ref, w_ref, p_ref, g_ref = refs[n_stream:]
    mod = mod_ref[0, 0]
    h = _rms_rows(_stream_tile(refs[:n_stream])) * nw_ref[...] * (1.0 + mod[1:2]) + mod[0:1]
    r = jnp.dot(h.astype(BF16), w_ref[...], preferred_element_type=F32)
    p_ref[0] = r[:, :P_WIDTH].astype(BF16)
    g_ref[0] = r[:, P_WIDTH:]


def _in_proj(stream, mods, nw, wp):
    s_specs, s_args = _stream_specs(stream)
    B = s_args[0].shape[0]
    L = sum(s.shape[1] for s in s_args)
    D = D_MODEL
    row = lambda w: pl.BlockSpec((1, ROW_TILE, w), lambda b, t: (b, t, 0))
    return pl.pallas_call(
        functools.partial(_in_proj_kernel, n_stream=len(s_args)),
        out_shape=(jax.ShapeDtypeStruct((B, L, P_WIDTH), BF16),
                   jax.ShapeDtypeStruct((B, L, G_WIDTH), F32)),
        grid=(B, L // ROW_TILE),
        in_specs=[*s_specs, _mod_spec(),
                  pl.BlockSpec((1, D), lambda b, t: (0, 0)),
                  pl.BlockSpec(wp.shape, lambda b, t: (0, 0), pipeline_mode=pl.Buffered(1))],
        out_specs=(row(P_WIDTH), row(G_WIDTH)),
        compiler_params=_params(("parallel", "arbitrary")),
        name="in_proj",
    )(*s_args, mods, nw.reshape(1, D), wp)


C_ROWS = M_DV + 16
VEC_ROWS = 24
INTRA_CHUNKS = 2

def _scan_lanes(x, reverse):
    lane = lax.broadcasted_iota(jnp.int32, x.shape, 1)
    k = 1
    while k < M_CHUNK:
        if reverse:
            x = x + jnp.where(lane < M_CHUNK - k, pltpu.roll(x, M_CHUNK - k, axis=1), 0.0)
        else:
            x = x + jnp.where(lane >= k, pltpu.roll(x, k, axis=1), 0.0)
        k *= 2
    return x


def _mlstm_intra_kernel(*refs):
    for c in range(INTRA_CHUNKS):
        _mlstm_intra_chunk(c, *refs)


def _mlstm_intra_chunk(c, qk_ref, v_ref, g_ref, bias_ref, numf_ref, numb_ref, vecf_ref, vecb_ref,
                       clf_ref, clb_ref):
    toks = pl.ds(c * M_CHUNK, M_CHUNK)
    row = lax.broadcasted_iota(jnp.int32, (M_CHUNK, M_CHUNK), 0)
    col = lax.broadcasted_iota(jnp.int32, (M_CHUNK, M_CHUNK), 1)
    g = g_ref[0, toks, :] + bias_ref[...]
    g_row = g.T[0:16, :]
    lf_row = jnp.minimum(g_row, 0.0) - jnp.log1p(jnp.exp(-jnp.abs(g_row)))
    scans = (_scan_lanes(lf_row, False), _scan_lanes(lf_row, True))
    gate_row = lax.broadcasted_iota(jnp.int32, (16, M_CHUNK), 0)
    gaps = g_row - pltpu.roll(jnp.where(gate_row < 8, scans[0], scans[1]), 12, axis=0)
    gap_cols = jnp.concatenate([gaps, jnp.zeros((M_CHUNK - 16, M_CHUNK), F32)], axis=0).T
    outs = ((numf_ref, vecf_ref, clf_ref, row <= col, M_CHUNK - 1),
            (numb_ref, vecb_ref, clb_ref, row >= col, 0))
    for vec_ref in (vecf_ref, vecb_ref):
        vec_ref[0, c,12 + 2 * M_HEADS:VEC_ROWS, :] = jnp.zeros((VEC_ROWS - 12 - 2 * M_HEADS, LANES), F32)
    tail_row = lax.broadcasted_iota(jnp.int32, (C_ROWS - M_DV, M_CHUNK), 0)
    for h in range(M_HEADS):
        qk = qk_ref[0, toks, h * LANES:(h + 1) * LANES]
        q, k = qk[:, :M_DK], qk[:, M_DK:]
        vt = v_ref[0, toks, h * M_DV:(h + 1) * M_DV].astype(F32).T
        vt_bf = vt.astype(BF16)
        s_raw = lax.dot_general(k, q, (((1,), (1,)), ((), ())), preferred_element_type=F32)
        for d, (num_ref, vec_ref, cl_ref, allowed, last) in enumerate(outs):
            b_r = scans[d][8 * d + 4 + h:8 * d + 5 + h, :]
            i_r = g_row[8 * d + h:8 * d + h + 1, :]
            j = 8 * d + h
            b_end = b_r[:, last:last + 1]
            d_log = jnp.where(allowed, b_r + gap_cols[:, j:j + 1], -jnp.inf)
            m_intra = jnp.max(d_log, axis=0, keepdims=True)
            s = s_raw * jnp.exp(d_log - m_intra)
            num_ref[0, c, h] = jnp.dot(vt_bf, s.astype(BF16), preferred_element_type=F32)
            vec_ref[0, c,3 * h:3 * h + 1, :] = jnp.sum(s, axis=0, keepdims=True)
            vec_ref[0, c,3 * h + 1:3 * h + 2, :] = m_intra
            vec_ref[0, c,3 * h + 2:3 * h + 3, :] = b_r
            w_log = b_end - b_r + i_r
            m_loc = jnp.max(w_log, axis=-1, keepdims=True)
            w_row = jnp.exp(w_log - m_loc)
            vw = jnp.concatenate([vt * w_row, jnp.where(tail_row == 0, w_row, 0.0)], axis=0).astype(BF16)
            cl_ref[0, c, h] = jnp.dot(vw, k, preferred_element_type=F32)
            vec_ref[0, c,12 + 2 * h:13 + 2 * h, :] = jnp.broadcast_to(m_loc, (1, LANES))
            vec_ref[0, c,13 + 2 * h:14 + 2 * h, :] = jnp.broadcast_to(b_end, (1, LANES))


def _mlstm_scan_kernel(*refs, n_batch):
    ins, (hf_ref, hb_ref, cn_ref, m_ref) = refs[:8], refs[8:]

    @pl.when(pl.program_id(0) == 0)
    def _():
        cn_ref[...] = jnp.zeros_like(cn_ref)
        m_ref[...] = jnp.zeros_like(m_ref)

    for d, h_ref in enumerate((hf_ref, hb_ref)):
        qk_ref, num_ref, vec_ref, cl_ref = ins[4 * d:4 * d + 4]
        for b in range(n_batch):
            for h in range(M_HEADS):
                idx = (d * n_batch + b) * M_HEADS + h
                q = qk_ref[b, :, h * LANES:h * LANES + M_DK]
                row = lambda r: vec_ref[b, 0, r:r + 1, :]
                den_i, m_i, b_r = row(3 * h), row(3 * h + 1), row(3 * h + 2)
                m_loc, b_end = row(12 + 2 * h), row(13 + 2 * h)
                m_prev = m_ref[idx]
                cn = cn_ref[idx]

                inter = b_r + m_prev
                m_t = jnp.maximum(inter, m_i)
                a = jnp.exp(inter - m_t)
                e = jnp.exp(m_i - m_t)
                cq = lax.dot_general(cn.astype(BF16), q, (((1,), (1,)), ((), ())),
                                     preferred_element_type=F32)
                den = e * den_i + a * cq[M_DV:M_DV + 1, :]
                scale = 1.0 / jnp.maximum(jnp.abs(den), jnp.exp(-m_t))
                ht = (e * num_ref[b, 0, h] + a * cq[0:M_DV, :]) * scale
                h_ref[b, :, h * M_DV:(h + 1) * M_DV] = ht.T.astype(BF16)

                m_new = jnp.maximum(b_end + m_prev, m_loc)
                a_s = jnp.exp(b_end + m_prev - m_new)
                s_s = jnp.exp(m_loc - m_new)
                cn_ref[idx] = a_s[:, :M_DK] * cn + s_s[:, :M_DK] * cl_ref[b, 0, h]
                m_ref[idx] = m_new


def _mlstm(p, g, gate_b, ctx_chunks):
    B, L, _ = p.shape
    nc = L // M_CHUNK
    width = M_HEADS * M_DV
    bias = jnp.zeros((1, G_WIDTH), F32).at[0, :16].set(gate_b)
    assert nc % INTRA_CHUNKS == 0
    tok = lambda w, cb=0: pl.BlockSpec((1, INTRA_CHUNKS * M_CHUNK, w), lambda b, c: (b, c, cb))
    num_shape, vec_shape, cl_shape = (M_HEADS, M_DV, M_CHUNK), (VEC_ROWS, LANES), (M_HEADS, C_ROWS, M_DK)
    per_chunk = lambda s: pl.BlockSpec((1, INTRA_CHUNKS) + s, lambda b, c: (b, c) + (0,) * len(s))
    f32 = lambda *s: jax.ShapeDtypeStruct(s, F32)
    numf, numb, vecf, vecb, clf, clb = pl.pallas_call(
        _mlstm_intra_kernel,
        out_shape=(f32(B, nc, *num_shape),) * 2 + (f32(B, nc, *vec_shape),) * 2 + (f32(B, nc, *cl_shape),) * 2,
        grid=(B, nc // INTRA_CHUNKS),
        in_specs=[tok(width, P_QK // width), tok(width, P_MV // width), tok(G_WIDTH),
                  pl.BlockSpec((1, G_WIDTH), lambda b, c: (0, 0))],
        out_specs=(per_chunk(num_shape),) * 2 + (per_chunk(vec_shape),) * 2 + (per_chunk(cl_shape),) * 2,
        compiler_params=_params(("parallel", "parallel")),
        name="mlstm_intra",
    )(p, p, g, bias)

    fwd = lambda j: j
    bwd = lambda j: jnp.where(j < ctx_chunks, ctx_chunks - 1 - j, nc - 1 + ctx_chunks - j)
    stok = lambda cm, w, cb=0: pl.BlockSpec((B, M_CHUNK, w), lambda j: (0, cm(j), cb))
    schunk = lambda cm, s: pl.BlockSpec((B, 1) + s, lambda j: (0, cm(j)) + (0,) * len(s))
    side = lambda cm: [stok(cm, width, P_QK // width), schunk(cm, num_shape), schunk(cm, vec_shape),
                       schunk(cm, cl_shape)]
    chains = 2 * B * M_HEADS
    return pl.pallas_call(
        functools.partial(_mlstm_scan_kernel, n_batch=B),
        out_shape=(jax.ShapeDtypeStruct((B, L, width), BF16),) * 2,
        grid=(nc,),
        in_specs=side(fwd) + side(bwd),
        out_specs=(stok(fwd, width), stok(bwd, width)),
        scratch_shapes=[pltpu.VMEM((chains, C_ROWS, M_DK), F32),
                        pltpu.VMEM((chains, 1, LANES), F32)],
        compiler_params=_params(("arbitrary",)),
        name="mlstm_scan",
    )(p, numf, vecf, clf, p, numb, vecb, clb)


def _head_norm_rope(x, w, cos, sin, bd):
    sq = x * x
    hi = sq.astype(BF16)
    lo = (sq - hi.astype(F32)).astype(BF16)
    ms = jnp.dot(hi, bd, preferred_element_type=F32) + jnp.dot(lo, bd, preferred_element_type=F32)
    y = x * lax.rsqrt(ms + EPS) * w
    lane = lax.broadcasted_iota(jnp.int32, y.shape, 1)
    partner = jnp.where(lane % A_DH < A_DH // 2,
                        pltpu.roll(y, LANES - A_DH // 2, axis=1), pltpu.roll(y, A_DH // 2, axis=1))
    return y * cos + partner * sin


def _attn_prep_kernel(q_ref, kv_ref, cos_ref, sin_ref, qw_ref, kw_ref, qt_ref, k_ref, vt_ref, *, q_scale):
    r = lax.broadcasted_iota(jnp.int32, (LANES, LANES), 0) // A_DH
    c = lax.broadcasted_iota(jnp.int32, (LANES, LANES), 1) // A_DH
    bd = jnp.where(r == c, 1.0 / A_DH, 0.0).astype(BF16)
    cos, sin = cos_ref[...], sin_ref[...]
    for pair in range(A_HEADS // 2):
        x = q_ref[0, :, pair * LANES:(pair + 1) * LANES].astype(F32)
        y = _head_norm_rope(x, qw_ref[...], cos, sin, bd) * q_scale
        qt_ref[0, pair * LANES:(pair + 1) * LANES, :] = y.T.astype(BF16)
    kv = kv_ref[0].astype(F32)
    k = _head_norm_rope(kv[:, :LANES], kw_ref[...], cos, sin, bd).astype(BF16)
    for kvh in range(A_KV_HEADS):
        k_ref[0, kvh, 0] = k[:, kvh * A_DH:(kvh + 1) * A_DH]
    vt = kv[:, LANES:].T.astype(BF16)
    ones = jnp.ones((VT_ROWS - A_DH, vt.shape[1]), BF16)
    for kvh in range(A_KV_HEADS):
        vt_ref[0, kvh, 0] = jnp.concatenate([vt[kvh * A_DH:(kvh + 1) * A_DH, :], ones], axis=0)


def _attn_prep(p, cos, sin, qw, kw, ts):
    B, L, _ = p.shape
    per = ts // ROW_TILE
    nblk = L // ts
    q_scale = A_DH ** -0.5 * math.log2(math.e)
    return pl.pallas_call(
        functools.partial(_attn_prep_kernel, q_scale=q_scale),
        out_shape=(jax.ShapeDtypeStruct((B, A_HEADS * A_DH, L), BF16),
                   jax.ShapeDtypeStruct((B, A_KV_HEADS, nblk, ts, A_DH), BF16),
                   jax.ShapeDtypeStruct((B, A_KV_HEADS, nblk, VT_ROWS, ts), BF16)),
        grid=(B, L // ROW_TILE),
        in_specs=[pl.BlockSpec((1, ROW_TILE, A_HEADS * A_DH), lambda b, t: (b, t, P_AQ // (A_HEADS * A_DH))),
                  pl.BlockSpec((1, ROW_TILE, 2 * LANES), lambda b, t: (b, t, P_AKV // (2 * LANES))),
                  pl.BlockSpec((ROW_TILE, LANES), lambda b, t: (t, 0)),
                  pl.BlockSpec((ROW_TILE, LANES), lambda b, t: (t, 0)),
                  pl.BlockSpec((1, LANES), lambda b, t: (0, 0)),
                  pl.BlockSpec((1, LANES), lambda b, t: (0, 0))],
        out_specs=(pl.BlockSpec((1, A_HEADS * A_DH, ROW_TILE), lambda b, t: (b, 0, t)),
                   pl.BlockSpec((1, A_KV_HEADS, 1, ROW_TILE, A_DH), lambda b, t: (b, 0, t // per, t % per, 0)),
                   pl.BlockSpec((1, A_KV_HEADS, 1, VT_ROWS, ROW_TILE), lambda b, t: (b, 0, t // per, 0, t % per))),
        compiler_params=_params(("parallel", "arbitrary")),
        name="attn_prep",
    )(p, p, cos, sin, qw, kw)


ATT_SUB = 256
ATT_PIECE = 128
VT_ROWS = A_DH + 16


def _attn_kernel(qt_ref, k_ref, vt_ref, o_ref, sa_ref, sb_ref, ma_ref, mb_ref, acc_ref,
                 *, nblk, nsub, tq, ctx_tiles, ctx_sub, q_tile0):
    q_of = lambda g: qt_ref[0, g * A_DH:(g + 1) * A_DH, :]
    n = A_GROUP * tq
    lanes = lambda g: slice(g * tq, (g + 1) * tq)
    keys = lambda r: pl.ds(r * ATT_SUB, ATT_SUB)

    def step(nxt, cur, ms, subs):
        out = []
        for g in range(A_GROUP):
            if cur is not None:
                ci, cs_ref, cm_ref = cur
                m_new = jnp.maximum(ms[g], cm_ref[:, lanes(g)])
                alpha = jnp.exp2(ms[g] - m_new)
            best, pv = None, None
            for r in range(subs):
                parts = []
                for piece in range(ATT_SUB // ATT_PIECE):
                    rows = pl.ds(r * ATT_SUB + piece * ATT_PIECE, ATT_PIECE)
                    if nxt is not None:
                        ni, ns_ref, _ = nxt
                        s = jnp.dot(k_ref[0, 0, ni, rows, :], q_of(g), preferred_element_type=F32)
                        ns_ref[g, rows, :] = s
                        top = jnp.max(s, axis=0, keepdims=True)
                        best = top if best is None else jnp.maximum(best, top)
                    if cur is not None:
                        parts.append(jnp.exp2(cs_ref[g, rows, :] - m_new).astype(BF16))
                if cur is not None:
                    p = jnp.concatenate(parts, axis=0)
                    d = jnp.dot(vt_ref[0, 0, ci, :, keys(r)], p, preferred_element_type=F32)
                    pv = d if pv is None else pv + d
            if nxt is not None:
                nxt[2][:, lanes(g)] = best
            if cur is not None:
                acc_ref[:, lanes(g)] = alpha * acc_ref[:, lanes(g)] + pv
                out.append(m_new)
            else:
                out.append(ms[g])
        return tuple(out)

    def finish():
        o = acc_ref[0:A_DH, :] / acc_ref[A_DH:A_DH + 1, :]
        o = jnp.concatenate([o[:, lanes(g)] for g in range(A_GROUP)], axis=0)
        o_ref[0] = o.T.astype(BF16)

    acc_ref[...] = jnp.zeros_like(acc_ref)
    init = (jnp.full((1, tq), -jnp.inf, F32),) * A_GROUP
    is_ctx = pl.program_id(2) + q_tile0 < ctx_tiles
    buf_a, buf_b = (sa_ref, ma_ref), (sb_ref, mb_ref)

    @pl.when(is_ctx)
    def _():
        step((0, *buf_a), None, init, ctx_sub)
        step(None, (0, *buf_a), init, ctx_sub)
        finish()

    @pl.when(jnp.logical_not(is_ctx))
    def _():
        step((0, *buf_a), None, init, nsub)

        def pair(j, ms):
            i = 2 * j
            ms = step((i + 1, *buf_b), (i, *buf_a), ms, nsub)
            return step((i + 2, *buf_a), (i + 1, *buf_b), ms, nsub)

        ms = lax.fori_loop(0, (nblk - 1) // 2, pair, init)
        step(None, (nblk - 1, *buf_a), ms, nsub)
        finish()


def _attention(qt, k, vt, *, q_tile0, n_ctx, tq):
    B, _, L = qt.shape
    nblk, ts = k.shape[2], k.shape[3]
    assert n_ctx <= ts and n_ctx % tq == 0 and n_ctx % ATT_SUB == 0 and ts % ATT_SUB == 0 and nblk % 2 == 1
    width = A_GROUP * A_DH
    n = A_GROUP * tq
    s_buf, m_buf = pltpu.VMEM((A_GROUP, ts, tq), F32), pltpu.VMEM((1, n), F32)
    return pl.pallas_call(
        functools.partial(_attn_kernel, nblk=nblk, nsub=ts // ATT_SUB, tq=tq, ctx_tiles=n_ctx // tq,
                          ctx_sub=n_ctx // ATT_SUB, q_tile0=q_tile0),
        out_shape=jax.ShapeDtypeStruct((B, L - q_tile0 * tq, A_HEADS * A_DH), BF16),
        grid=(B, A_KV_HEADS, L // tq - q_tile0),
        in_specs=[pl.BlockSpec((1, width, tq), lambda b, kv, t: (b, kv, t + q_tile0)),
                  pl.BlockSpec((1, 1, nblk, ts, A_DH), lambda b, kv, t: (b, kv, 0, 0, 0)),
                  pl.BlockSpec((1, 1, nblk, VT_ROWS, ts), lambda b, kv, t: (b, kv, 0, 0, 0))],
        out_specs=pl.BlockSpec((1, tq, width), lambda b, kv, t: (b, t, kv)),
        scratch_shapes=[s_buf, s_buf, m_buf, m_buf, pltpu.VMEM((VT_ROWS, n), F32)],
        compiler_params=_params(("parallel", "parallel", "arbitrary")),
        name="attention",
    )(qt, k, vt)


def _mixer_out_kernel(*refs, with_router, n_stream, row_off):
    hf_ref, hb_ref, mo_ref, a_ref = refs[:4]
    mod_ref, mnw_ref, n2w_ref, wout_ref = refs[4 + n_stream:8 + n_stream]
    rest = refs[8 + n_stream:]
    hs = hf_ref[0].astype(F32) + hb_ref[0].astype(F32)
    hn = jnp.concatenate([_rms_rows(hs[:, h * M_DV:(h + 1) * M_DV]) for h in range(M_HEADS)], axis=1)
    m = hn * mnw_ref[...] * _sigmoid(mo_ref[0].astype(F32))
    y_in = jnp.concatenate([m.astype(BF16), a_ref[0]], axis=1)
    mod = mod_ref[0, 0]
    x1 = _stream_tile(refs[4:4 + n_stream], row_off) + mod[2:3] * jnp.dot(y_in, wout_ref[...],
                                                                          preferred_element_type=F32)
    h2 = _rms_rows(x1) * n2w_ref[...] * (1.0 + mod[4:5]) + mod[3:4]
    if not with_router:
        wg_ref, wu_ref, wd_ref, o_ref = rest
        hb16 = h2.astype(BF16)
        g = jnp.dot(hb16, wg_ref[...], preferred_element_type=F32)
        u = jnp.dot(hb16, wu_ref[...], preferred_element_type=F32)
        act = (g * _sigmoid(g) * u).astype(BF16)
        o_ref[0] = x1 + mod[5:6] * jnp.dot(act, wd_ref[...], preferred_element_type=F32)
        return
    router_ref, x1_ref, h2_ref, ids_ref, gates_ref = rest
    x1_ref[0] = x1
    h2_ref[0] = h2
    h_hi = h2.astype(BF16)
    h_lo = (h2 - h_hi.astype(F32)).astype(BF16)
    logits = (jnp.dot(h_hi, router_ref[0], preferred_element_type=F32)
              + jnp.dot(h_lo, router_ref[0], preferred_element_type=F32)
              + jnp.dot(h_hi, router_ref[1], preferred_element_type=F32))
    lane = lax.broadcasted_iota(jnp.int32, logits.shape, 1)
    logits = jnp.where(lane < N_EXPERTS, logits, -jnp.inf)
    m1 = jnp.max(logits, axis=-1, keepdims=True)
    i1 = jnp.min(jnp.where(logits == m1, lane, LANES), axis=-1, keepdims=True)
    rest = jnp.where(lane == i1, -jnp.inf, logits)
    m2 = jnp.max(rest, axis=-1, keepdims=True)
    i2 = jnp.min(jnp.where(rest == m2, lane, LANES), axis=-1, keepdims=True)
    e2 = jnp.exp(m2 - m1)
    g1 = 1.0 / (1.0 + e2)
    ids_ref[0] = jnp.where(lane == 0, i1, jnp.where(lane == 1, i2, -1))
    gates_ref[0] = jnp.where(lane == 0, g1, jnp.where(lane == 1, e2 * g1, 0.0))


def _mixer_out(hf, hb, p, a, stream, mods, mnw, n2w, wout, *, ffn=None, router=None, row_off=0):
    B, L = p.shape[:2]
    D = D_MODEL
    nt = L // ROW_TILE - row_off
    rin = lambda w, cb=0: pl.BlockSpec((1, ROW_TILE, w), lambda b, t: (b, t + row_off, cb))
    rout = lambda w: pl.BlockSpec((1, ROW_TILE, w), lambda b, t: (b, t, 0))
    const = lambda arr: pl.BlockSpec(arr.shape, lambda b, t: (0,) * arr.ndim, pipeline_mode=pl.Buffered(1))
    mw = M_HEADS * M_DV
    a_off = row_off - (L - a.shape[1]) // ROW_TILE
    a_spec = pl.BlockSpec((1, ROW_TILE, A_HEADS * A_DH), lambda b, t: (b, t + a_off, 0))
    s_specs, s_args = _stream_specs(stream, row_off)
    in_specs = [rin(mw), rin(mw), rin(mw, P_MO // mw), a_spec, *s_specs, _mod_spec(row_off),
                pl.BlockSpec((1, mw), lambda b, t: (0, 0)), pl.BlockSpec((1, D), lambda b, t: (0, 0)),
                const(wout)]
    args = [hf, hb, p, a, *s_args, mods, mnw.reshape(1, mw), n2w.reshape(1, D), wout]
    rows = nt * ROW_TILE
    if router is None:
        in_specs += [const(w) for w in ffn]
        args += list(ffn)
        out_shape, out_specs = jax.ShapeDtypeStruct((B, rows, D), F32), rout(D)
    else:
        in_specs.append(const(router))
        args.append(router)
        out_shape = (jax.ShapeDtypeStruct((B, rows, D), F32), jax.ShapeDtypeStruct((B, rows, D), F32),
                     jax.ShapeDtypeStruct((B, rows, LANES), jnp.int32), jax.ShapeDtypeStruct((B, rows, LANES), F32))
        out_specs = (rout(D), rout(D), rout(LANES), rout(LANES))
    return pl.pallas_call(
        functools.partial(_mixer_out_kernel, with_router=router is not None, n_stream=len(s_args), row_off=row_off),
        out_shape=out_shape,
        grid=(B, nt),
        in_specs=in_specs,
        out_specs=out_specs,
        compiler_params=_params(("parallel", "arbitrary")),
        name="mixer_out",
    )(*args)


MOE_TM = 512
MOE_FF = 1792
MOE_TD = 256
RANK_TILE = 1024
POS_TILE = 2048
ISSUE_UNROLL = 8


def _moe_rank_kernel(ids_ref, rank_ref, cnt_ref, carry_ref, before_ref):
    @pl.when(pl.program_id(0) == 0)
    def _():
        carry_ref[...] = jnp.zeros_like(carry_ref)
        r = lax.broadcasted_iota(jnp.int32, (RANK_TILE, RANK_TILE), 0)
        c = lax.broadcasted_iota(jnp.int32, (RANK_TILE, RANK_TILE), 1)
        before_ref[...] = jnp.where(c < r, 1.0, 0.0).astype(BF16)

    ids = ids_ref[...]
    lane = lax.broadcasted_iota(jnp.int32, ids.shape, 1)
    onehot = jnp.where(jnp.logical_or(lane == ids[:, 0:1], lane == ids[:, 1:2]), 1.0, 0.0)
    rank_ref[...] = jnp.dot(before_ref[...], onehot.astype(BF16), preferred_element_type=F32) + carry_ref[...]
    carry_ref[...] += jnp.sum(onehot, axis=0, keepdims=True)
    cnt_ref[...] = carry_ref[...]


def _moe_rank(ids):
    n = ids.shape[0]
    return pl.pallas_call(
        _moe_rank_kernel,
        out_shape=(jax.ShapeDtypeStruct((n, LANES), F32), jax.ShapeDtypeStruct((1, LANES), F32)),
        grid=(n // RANK_TILE,),
        in_specs=[pl.BlockSpec((RANK_TILE, LANES), lambda t: (t, 0))],
        out_specs=(pl.BlockSpec((RANK_TILE, LANES), lambda t: (t, 0)), pl.BlockSpec((1, LANES), lambda t: (0, 0))),
        scratch_shapes=[pltpu.VMEM((1, LANES), F32), pltpu.VMEM((RANK_TILE, RANK_TILE), BF16)],
        compiler_params=_params(("arbitrary",)),
        name="moe_rank",
    )(ids)


def _moe_pos_kernel(ids_ref, rank_ref, start_ref, pos_ref):
    ids = ids_ref[...]
    lane = lax.broadcasted_iota(jnp.int32, ids.shape, 1)
    tgt = start_ref[...] + rank_ref[...]
    p0 = jnp.sum(jnp.where(lane == ids[:, 0:1], tgt, 0.0), axis=-1, keepdims=True)
    p1 = jnp.sum(jnp.where(lane == ids[:, 1:2], tgt, 0.0), axis=-1, keepdims=True)
    pos_ref[...] = jnp.where(lane == 0, p0, jnp.where(lane == 1, p1, 0.0)).astype(jnp.int32)


def _moe_pos(ids, rank, start_row):
    n = ids.shape[0]
    blk = pl.BlockSpec((POS_TILE, LANES), lambda t: (t, 0))
    return pl.pallas_call(
        _moe_pos_kernel,
        out_shape=jax.ShapeDtypeStruct((n, LANES), jnp.int32),
        grid=(n // POS_TILE,),
        in_specs=[blk, blk, pl.BlockSpec((1, LANES), lambda t: (0, 0))],
        out_specs=blk,
        compiler_params=_params(("parallel",)),
        name="moe_pos",
    )(ids, rank, start_row)


def _row_copy(src, src_row, dst, dst_row, sem):
    return pltpu.make_async_copy(src.at[pl.ds(src_row, 1), :], dst.at[pl.ds(dst_row, 1), :], sem)


def _moe_dispatch_kernel(pad_ref, pos_ref, h_ref, xs_ref, zero_ref, sem):
    @pl.when(pl.program_id(0) == 0)
    def _():
        zero_ref[...] = jnp.zeros_like(zero_ref)
        fills = [pltpu.make_async_copy(
            zero_ref, xs_ref.at[pl.ds(pl.multiple_of(pad_ref[e], SUBLANES), MOE_TM + SUBLANES), :], sem)
            for e in range(N_EXPERTS)]
        for cp in fills:
            cp.start()
        for cp in fills:
            cp.wait()

        def fill_tile(j, carry):
            cp = pltpu.make_async_copy(zero_ref.at[pl.ds(0, MOE_TM), :],
                                       xs_ref.at[pl.ds(pl.multiple_of(j * MOE_TM, MOE_TM), MOE_TM), :], sem)
            cp.start()
            cp.wait()
            return carry

        lax.fori_loop(pad_ref[N_EXPERTS], xs_ref.shape[0] // MOE_TM, fill_tile, 0)

    def issue(r, carry):
        for k in range(2):
            _row_copy(h_ref, r, xs_ref, pos_ref[0, 0, 2 * r + k], sem).start(priority=k)
        return carry

    lax.fori_loop(0, MOE_TD, issue, 0, unroll=ISSUE_UNROLL)
    for k in range(2):
        pltpu.make_async_copy(h_ref, xs_ref.at[pl.ds(0, MOE_TD), :], sem).wait()


def _moe_dispatch(h, pos, fill_meta, ns):
    n, d = h.shape
    return pl.pallas_call(
        _moe_dispatch_kernel,
        out_shape=jax.ShapeDtypeStruct((ns, d), F32),
        grid_spec=pltpu.PrefetchScalarGridSpec(
            num_scalar_prefetch=1,
            grid=(n // MOE_TD,),
            in_specs=[pl.BlockSpec((1, 1, 2 * MOE_TD), lambda t, pad: (t, 0, 0), memory_space=pltpu.SMEM),
                      pl.BlockSpec((MOE_TD, d), lambda t, pad: (t, 0))],
            out_specs=pl.BlockSpec(memory_space=pl.ANY),
            scratch_shapes=[pltpu.VMEM((MOE_TM + SUBLANES, d), F32), pltpu.SemaphoreType.DMA(())]),
        compiler_params=_params(("arbitrary",)),
        name="moe_dispatch",
    )(fill_meta, pos, h)


def _moe_group_kernel(te_ref, nv_ref, xs_ref, wg_ref, wu_ref, wd_ref, ys_ref, acc_ref):
    i, f = pl.program_id(0), pl.program_id(1)
    last = pl.num_programs(1) - 1
    valid = i < nv_ref[0]

    @pl.when(valid)
    def _():
        @pl.when(f == 0)
        def _():
            acc_ref[...] = jnp.zeros_like(acc_ref)

        x = xs_ref[...].astype(BF16)
        g = jnp.dot(x, wg_ref[0], preferred_element_type=F32)
        u = jnp.dot(x, wu_ref[0], preferred_element_type=F32)
        act = (g * _sigmoid(g) * u).astype(BF16)
        acc_ref[...] += jnp.dot(act, wd_ref[0], preferred_element_type=F32)

        @pl.when(f == last)
        def _():
            ys_ref[...] = acc_ref[...]

    @pl.when(jnp.logical_and(jnp.logical_not(valid), f == last))
    def _():
        ys_ref[...] = jnp.zeros_like(ys_ref)


def _moe_group(xs, tile_expert, n_valid, wg, wu, wd):
    ns, d = xs.shape
    n_tiles = ns // MOE_TM - 1
    ff = wg.shape[2]
    live = lambda i, nv: i < nv[0]
    return pl.pallas_call(
        _moe_group_kernel,
        out_shape=jax.ShapeDtypeStruct((n_tiles * MOE_TM, d), F32),
        grid_spec=pltpu.PrefetchScalarGridSpec(
            num_scalar_prefetch=2,
            grid=(n_tiles, ff // MOE_FF),
            in_specs=[pl.BlockSpec((MOE_TM, d), lambda i, f, te, nv: (jnp.where(live(i, nv), i, 0), 0)),
                      pl.BlockSpec((1, d, MOE_FF), lambda i, f, te, nv: (te[i], 0, f)),
                      pl.BlockSpec((1, d, MOE_FF), lambda i, f, te, nv: (te[i], 0, f)),
                      pl.BlockSpec((1, MOE_FF, d), lambda i, f, te, nv: (te[i], f, 0))],
            out_specs=pl.BlockSpec((MOE_TM, d), lambda i, f, te, nv: (i, 0)),
            scratch_shapes=[pltpu.VMEM((MOE_TM, d), F32)]),
        compiler_params=_params(("arbitrary", "arbitrary")),
        name="moe_group",
    )(tile_expert, n_valid, xs, wg, wu, wd)


def _moe_combine_kernel(pos_ref, x_ref, gates_ref, mod_ref, fw_ref, ys_ref, o_ref, ybuf, sem):
    def issue(r, carry):
        for k in range(2):
            _row_copy(ys_ref, pos_ref[0, 0, 2 * r + k], ybuf.at[k], r, sem).start(priority=k)
        return carry

    lax.fori_loop(0, MOE_TD, issue, 0, unroll=ISSUE_UNROLL)
    for k in range(2):
        pltpu.make_async_copy(ys_ref.at[pl.ds(0, MOE_TD), :], ybuf.at[k], sem).wait()
    gates = gates_ref[...]
    y = gates[:, 0:1] * ybuf[0] + gates[:, 1:2] * ybuf[1]
    x2 = x_ref[...] + mod_ref[0, 0][5:6] * y
    o_ref[...] = _rms_rows(x2) * fw_ref[...]


def _moe_combine(pos, x1, gates, mods, fw, ys, tokens_per_sample):
    n, d = x1.shape
    per = tokens_per_sample // MOE_TD
    return pl.pallas_call(
        _moe_combine_kernel,
        out_shape=jax.ShapeDtypeStruct((n, d), F32),
        grid=(n // MOE_TD,),
        in_specs=[pl.BlockSpec((1, 1, 2 * MOE_TD), lambda t: (t, 0, 0), memory_space=pltpu.SMEM),
                  pl.BlockSpec((MOE_TD, d), lambda t: (t, 0)),
                  pl.BlockSpec((MOE_TD, LANES), lambda t: (t, 0)),
                  pl.BlockSpec((1, 1, 6, d), lambda t: (t // per, 1, 0, 0)),
                  pl.BlockSpec((1, d), lambda t: (0, 0)),
                  pl.BlockSpec(memory_space=pl.ANY)],
        out_specs=pl.BlockSpec((MOE_TD, d), lambda t: (t, 0)),
        scratch_shapes=[pltpu.VMEM((2, MOE_TD, d), F32), pltpu.SemaphoreType.DMA(())],
        compiler_params=_params(("arbitrary",)),
        name="moe_combine",
    )(pos, x1, gates, mods, fw.reshape(1, d), ys)


def _moe(h2, ids, gates, x1, mods, wg, wu, wd, fw):
    B, T, D = x1.shape
    n = B * T
    ids, gates = ids.reshape(n, LANES), gates.reshape(n, LANES)
    rank, cnt = _moe_rank(ids)
    cnt = cnt[0, :N_EXPERTS].astype(jnp.int32)
    padded = (cnt + MOE_TM - 1) // MOE_TM * MOE_TM
    end = jnp.cumsum(padded)
    start = end - padded
    n_tiles = 2 * n // MOE_TM + N_EXPERTS
    tile_expert = jnp.minimum(jnp.sum(jnp.arange(n_tiles)[:, None] >= (end // MOE_TM)[None, :], axis=1),
                              N_EXPERTS - 1).astype(jnp.int32)
    n_valid = (end[-1:] // MOE_TM).astype(jnp.int32)
    start_row = jnp.zeros((1, LANES), F32).at[0, :N_EXPERTS].set(start.astype(F32))
    pos = _moe_pos(ids, rank, start_row)
    pos = pos[:, :2].reshape(n // MOE_TD, 1, 2 * MOE_TD)
    fill_meta = jnp.concatenate([(start + cnt) // SUBLANES * SUBLANES, n_valid]).astype(jnp.int32)
    xs = _moe_dispatch(h2.reshape(n, D), pos, fill_meta, (n_tiles + 1) * MOE_TM)
    ys = _moe_group(xs, tile_expert, n_valid, wg, wu, wd)
    return _moe_combine(pos, x1.reshape(n, D), gates, mods, fw, ys, T).reshape(B, T, D)


_ROT_PERM = np.concatenate([np.arange(0, A_DH, 2), np.arange(1, A_DH, 2)])


def _prep_w_in(w):
    o = np.cumsum([0, M_HEADS * M_DK, M_HEADS * M_DK, M_HEADS * M_DV, M_HEADS * M_DV, 4 * M_HEADS,
                   A_HEADS * A_DH, A_KV_HEADS * A_DH, A_KV_HEADS * A_DH])
    mq, mk, mv, mo, mg, aq, ak, av = [w[:, o[i]:o[i + 1]] for i in range(8)]
    qk = jnp.concatenate([jnp.concatenate([mq[:, h * M_DK:(h + 1) * M_DK] * (M_DK ** -0.5),
                                           mk[:, h * M_DK:(h + 1) * M_DK]], axis=1) for h in range(M_HEADS)], axis=1)
    perm_q = np.concatenate([h * A_DH + _ROT_PERM for h in range(A_HEADS)])
    perm_k = np.concatenate([h * A_DH + _ROT_PERM for h in range(A_KV_HEADS)])
    pad = jnp.zeros((w.shape[0], G_WIDTH - 4 * M_HEADS), w.dtype)
    return jnp.concatenate([qk, mv, mo, aq[:, perm_q], ak[:, perm_k], av, mg, pad], axis=1).astype(BF16)


def _rope_tables(n_tok, n_ctx):
    rows = n_tok // GRID_W
    row = jnp.broadcast_to(jnp.arange(rows, dtype=F32)[:, None], (rows, GRID_W)).reshape(n_tok)
    col = jnp.broadcast_to(jnp.arange(GRID_W, dtype=F32)[None, :], (rows, GRID_W)).reshape(n_tok)
    n_freq = A_DH // 4
    inv_freq = ROPE_THETA ** (-jnp.arange(n_freq, dtype=F32) / n_freq)
    ang = jnp.concatenate([row[:, None] * inv_freq, col[:, None] * inv_freq], axis=-1)
    cos, sin = jnp.cos(ang), jnp.sin(ang)
    cos = jnp.concatenate([jnp.ones((n_ctx, A_DH // 2), F32), cos], axis=0)
    sin = jnp.concatenate([jnp.zeros((n_ctx, A_DH // 2), F32), sin], axis=0)
    return jnp.tile(cos, (1, 4)), jnp.tile(jnp.concatenate([-sin, sin], axis=1), (1, 2))


def kernel(x, c, ctx, c_ctx, ada_w, ada_b, norm1_w, norm2_w, w_in, mlstm_gate_b, mlstm_norm_w, q_norm_w, k_norm_w,
           w_out, ffn_w_gate, ffn_w_up, ffn_w_down, moe_router, moe_w_gate, moe_w_up, moe_w_down, final_norm_w):
    B, T, D = x.shape
    n_ctx = ctx.shape[1]
    L = n_ctx + T
    depth = w_in.shape[0]
    assert D == D_MODEL and n_ctx == ROW_TILE and T % RANK_TILE == 0 and depth == 2
    ts = 3 * ROW_TILE
    assert L % ts == 0
    ctx_tiles = n_ctx // ROW_TILE
    tq = ROW_TILE

    xa = (ctx, x)
    cvec =jnp.concatenate([c, c_ctx[None], jnp.zeros((8 - B - 1, D), F32)], axis=0)
    cos, sin = _rope_tables(T, n_ctx)
    out = None
    for i in range(depth):
        last = i == depth - 1
        modraw = _ada(cvec, ada_w[i], ada_b[i])
        mods = jnp.stack([jnp.broadcast_to(modraw[B].reshape(1, 6, D), (B, 6, D)),
                          modraw[:B].reshape(B, 6, D)], axis=1)
        p, g = _in_proj(xa, mods, norm1_w[i], _prep_w_in(w_in[i]))
        hf, hb = _mlstm(p, g, mlstm_gate_b[i], n_ctx // M_CHUNK)
        qw = jnp.tile(q_norm_w[i][_ROT_PERM], 2).reshape(1, LANES)
        kw = jnp.tile(k_norm_w[i][_ROT_PERM], 2).reshape(1, LANES)
        qt, k, vt = _attn_prep(p, cos, sin, qw, kw, ts)
        a = _attention(qt, k, vt, q_tile0=ctx_tiles if last else 0, n_ctx=n_ctx, tq=tq)
        wout = w_out[i].astype(BF16)
        if not last:
            j = i // 2
            ffn = (ffn_w_gate[j].astype(BF16), ffn_w_up[j].astype(BF16), ffn_w_down[j].astype(BF16))
            xa = _mixer_out(hf, hb, p, a, xa, mods, mlstm_norm_w[i], norm2_w[i], wout, ffn=ffn)
        else:
            j = i // 2
            router = jnp.zeros((D, LANES), F32).at[:, :N_EXPERTS].set(moe_router[j])
            router_hi = router.astype(BF16)
            router = jnp.stack([router_hi, (router - router_hi.astype(F32)).astype(BF16)])
            x1, h2, ids, gates = _mixer_out(hf, hb, p, a, xa, mods, mlstm_norm_w[i], norm2_w[i], wout,
                                            router=router, row_off=ctx_tiles)
            out = _moe(h2, ids, gates, x1, mods, moe_w_gate[j].astype(BF16), moe_w_up[j].astype(BF16),
                       moe_w_down[j].astype(BF16), final_norm_w)
    return out
```

```python
import functools
import math

import numpy as np
import jax
import jax.numpy as jnp
from jax import lax
from jax.experimental import pallas as pl
from jax.experimental.pallas import tpu as pltpu

F32 = jnp.float32
BF16 = jnp.bfloat16
HIGHEST = lax.Precision.HIGHEST

D_MODEL = 1024
GRID_W = 64
M_HEADS = 4
M_DV = 128
M_DK = 64
M_CHUNK = 128
A_HEADS = 8
A_KV_HEADS = 2
A_GROUP = A_HEADS // A_KV_HEADS
A_DH = 64
ROPE_THETA = 10000.0
N_EXPERTS = 8
EPS = 1e-6

LANES = 128
SUBLANES = 8
ROW_TILE = 256
VMEM_LIMIT = 56 * 1024 * 1024

P_QK = 0
P_MV = 512
P_MO = 1024
P_AQ = 1536
P_AKV = 2048
P_WIDTH = 2304
G_WIDTH = LANES


def _params(sem, vmem=VMEM_LIMIT, flags=None):
    return pltpu.CompilerParams(dimension_semantics=sem, vmem_limit_bytes=vmem, flags=flags)


def _sigmoid(x):
    return 1.0 / (1.0 + jnp.exp(-x))


def _rms_rows(x):
    return x * lax.rsqrt(jnp.mean(x * x, axis=-1, keepdims=True) + EPS)


def _ada_kernel(c_ref, w_ref, b_ref, o_ref):
    c = c_ref[...]
    s = c * _sigmoid(c)
    o_ref[...] = jnp.dot(s, w_ref[...], precision=HIGHEST, preferred_element_type=F32) + b_ref[...]


def _ada(cvec, w, b):
    n = w.shape[1]
    bn = 1536
    return pl.pallas_call(
        _ada_kernel,
        out_shape=jax.ShapeDtypeStruct((cvec.shape[0], n), F32),
        grid=(n // bn,),
        in_specs=[pl.BlockSpec(cvec.shape, lambda j: (0, 0)),
                  pl.BlockSpec((w.shape[0], bn), lambda j: (0, j)),
                  pl.BlockSpec((1, bn), lambda j: (0, j))],
        out_specs=pl.BlockSpec((cvec.shape[0], bn), lambda j: (0, j)),
        compiler_params=_params(("arbitrary",)),
        name="ada_mod",
    )(cvec, w, b.reshape(1, n))


def _mod_spec(off=0):
    return pl.BlockSpec((1, 1, 6, D_MODEL), lambda b, t: (b, jnp.minimum(t + off, 1), 0, 0))


def _stream_specs(stream, off=0):
    tile = (1, ROW_TILE, D_MODEL)
    if not isinstance(stream, tuple):
        return [pl.BlockSpec(tile, lambda b, t: (b, t + off, 0))], [stream]
    return ([pl.BlockSpec(tile, lambda b, t: (b, 0, 0)),
             pl.BlockSpec(tile, lambda b, t: (b, jnp.maximum(t + off - 1, 0), 0))], list(stream))


def _stream_tile(refs, off=0):
    if len(refs) == 1:
        return refs[0][0]
    return jnp.where(pl.program_id(1) + off == 0, refs[0][0], refs[1][0])


def _in_proj_kernel(*refs, n_stream):
    mod_ref, nw_ref, w_ref, cos_ref, sin_ref, qw_ref, kw_ref, p_ref, g_ref, qt_ref, k_ref, vt_ref = refs[n_stream:]
    mod = mod_ref[0, 0]
    h = _rms_rows(_stream_tile(refs[:n_stream])) * nw_ref[...] * (1.0 + mod[1:2]) + mod[0:1]
    r = jnp.dot(h.astype(BF16), w_ref[...], preferred_element_type=F32)
    p_ref[0] = r[:, :P_AQ].astype(BF16)
    g_ref[0] = r[:, P_WIDTH:]
    _attn_prep_tile(r[:, P_AQ:P_AKV], r[:, P_AKV:P_WIDTH], cos_ref[...], sin_ref[...], qw_ref[...], kw_ref[...],
                    qt_ref, k_ref, vt_ref)


def _in_proj(stream, mods, nw, wp, cos, sin, qw, kw):
    s_specs, s_args = _stream_specs(stream)
    B = s_args[0].shape[0]
    L = sum(s.shape[1] for s in s_args)
    D = D_MODEL
    nt = L // ROW_TILE
    row = lambda w: pl.BlockSpec((1, ROW_TILE, w), lambda b, t: (b, t, 0))
    vec = pl.BlockSpec((1, LANES), lambda b, t: (0, 0))
    table = pl.BlockSpec((ROW_TILE, LANES), lambda b, t: (t, 0))
    return pl.pallas_call(
        functools.partial(_in_proj_kernel, n_stream=len(s_args)),
        out_shape=(jax.ShapeDtypeStruct((B, L, P_AQ), BF16),
                   jax.ShapeDtypeStruct((B, L, G_WIDTH), F32),
                   jax.ShapeDtypeStruct((B, A_HEADS * A_DH, L), BF16),
                   jax.ShapeDtypeStruct((B, A_KV_HEADS, nt, ROW_TILE, A_DH), BF16),
                   jax.ShapeDtypeStruct((B, A_KV_HEADS, nt, VT_ROWS, ROW_TILE), BF16)),
        grid=(B, nt),
        in_specs=[*s_specs, _mod_spec(),
                  pl.BlockSpec((1, D), lambda b, t: (0, 0)),
                  pl.BlockSpec(wp.shape, lambda b, t: (0, 0), pipeline_mode=pl.Buffered(1)),
                  table, table, vec, vec],
        out_specs=(row(P_AQ), row(G_WIDTH),
                   pl.BlockSpec((1, A_HEADS * A_DH, ROW_TILE), lambda b, t: (b, 0, t)),
                   pl.BlockSpec((1, A_KV_HEADS, 1, ROW_TILE, A_DH), lambda b, t: (b, 0, t, 0, 0)),
                   pl.BlockSpec((1, A_KV_HEADS, 1, VT_ROWS, ROW_TILE), lambda b, t: (b, 0, t, 0, 0))),
        compiler_params=_params(("parallel", "arbitrary")),
        name="in_proj",
    )(*s_args, mods, nw.reshape(1, D), wp, cos, sin, qw, kw)


C_ROWS = M_DV + 16
VEC_ROWS = 24
INTRA_CHUNKS = 2

def _scan_lanes(x, reverse):
    lane = lax.broadcasted_iota(jnp.int32, x.shape, 1)
    k = 1
    while k < M_CHUNK:
        if reverse:
            x = x + jnp.where(lane < M_CHUNK - k, pltpu.roll(x, M_CHUNK - k, axis=1), 0.0)
        else:
            x = x + jnp.where(lane >= k, pltpu.roll(x, k, axis=1), 0.0)
        k *= 2
    return x


def _mlstm_intra_kernel(*refs):
    for c in range(INTRA_CHUNKS):
        _mlstm_intra_chunk(c, *refs)


def _mlstm_intra_chunk(c, qk_ref, v_ref, g_ref, bias_ref, numf_ref, numb_ref, vecf_ref, vecb_ref,
                       clf_ref, clb_ref):
    toks = pl.ds(c * M_CHUNK, M_CHUNK)
    row = lax.broadcasted_iota(jnp.int32, (M_CHUNK, M_CHUNK), 0)
    col = lax.broadcasted_iota(jnp.int32, (M_CHUNK, M_CHUNK), 1)
    g = g_ref[0, toks, :] + bias_ref[...]
    g_row = g.T[0:16, :]
    lf_row = jnp.minimum(g_row, 0.0) - jnp.log1p(jnp.exp(-jnp.abs(g_row)))
    scans = (_scan_lanes(lf_row, False), _scan_lanes(lf_row, True))
    gate_row = lax.broadcasted_iota(jnp.int32, (16, M_CHUNK), 0)
    gaps = g_row - pltpu.roll(jnp.where(gate_row < 8, scans[0], scans[1]), 12, axis=0)
    gap_cols = jnp.concatenate([gaps, jnp.zeros((M_CHUNK - 16, M_CHUNK), F32)], axis=0).T
    outs = ((numf_ref, vecf_ref, clf_ref, row <= col, M_CHUNK - 1),
            (numb_ref, vecb_ref, clb_ref, row >= col, 0))
    for vec_ref in (vecf_ref, vecb_ref):
        vec_ref[0, c,12 + 2 * M_HEADS:VEC_ROWS, :] = jnp.zeros((VEC_ROWS - 12 - 2 * M_HEADS, LANES), F32)
    tail_row = lax.broadcasted_iota(jnp.int32, (C_ROWS - M_DV, M_CHUNK), 0)
    for h in range(M_HEADS):
        qk = qk_ref[0, toks, h * LANES:(h + 1) * LANES]
        q, k = qk[:, :M_DK], qk[:, M_DK:]
        vt = v_ref[0, toks, h * M_DV:(h + 1) * M_DV].astype(F32).T
        vt_bf = vt.astype(BF16)
        s_raw = lax.dot_general(k, q, (((1,), (1,)), ((), ())), preferred_element_type=F32)
        for d, (num_ref, vec_ref, cl_ref, allowed, last) in enumerate(outs):
            b_r = scans[d][8 * d + 4 + h:8 * d + 5 + h, :]
            i_r = g_row[8 * d + h:8 * d + h + 1, :]
            j = 8 * d + h
            b_end = b_r[:, last:last + 1]
            d_log = jnp.where(allowed, b_r + gap_cols[:, j:j + 1], -jnp.inf)
            m_intra = jnp.max(d_log, axis=0, keepdims=True)
            s = s_raw * jnp.exp(d_log - m_intra)
            num_ref[0, c, h] = jnp.dot(vt_bf, s.astype(BF16), preferred_element_type=F32)
            vec_ref[0, c,3 * h:3 * h + 1, :] = jnp.sum(s, axis=0, keepdims=True)
            vec_ref[0, c,3 * h + 1:3 * h + 2, :] = m_intra
            vec_ref[0, c,3 * h + 2:3 * h + 3, :] = b_r
            w_log = b_end - b_r + i_r
            m_loc = jnp.max(w_log, axis=-1, keepdims=True)
            w_row = jnp.exp(w_log - m_loc)
            vw = jnp.concatenate([vt * w_row, jnp.where(tail_row == 0, w_row, 0.0)], axis=0).astype(BF16)
            cl_ref[0, c, h] = jnp.dot(vw, k, preferred_element_type=F32)
            vec_ref[0, c,12 + 2 * h:13 + 2 * h, :] = jnp.broadcast_to(m_loc, (1, LANES))
            vec_ref[0, c,13 + 2 * h:14 + 2 * h, :] = jnp.broadcast_to(b_end, (1, LANES))


def _mlstm_scan_kernel(*refs, n_batch):
    ins, (hf_ref, hb_ref, cn_ref, m_ref) = refs[:8], refs[8:]

    @pl.when(pl.program_id(0) == 0)
    def _():
        cn_ref[...] = jnp.zeros_like(cn_ref)
        m_ref[...] = jnp.zeros_like(m_ref)

    for d, h_ref in enumerate((hf_ref, hb_ref)):
        qk_ref, num_ref, vec_ref, cl_ref = ins[4 * d:4 * d + 4]
        for b in range(n_batch):
            for h in range(M_HEADS):
                idx = (d * n_batch + b) * M_HEADS + h
                q = qk_ref[b, :, h * LANES:h * LANES + M_DK]
                row = lambda r: vec_ref[b, 0, r:r + 1, :]
                den_i, m_i, b_r = row(3 * h), row(3 * h + 1), row(3 * h + 2)
                m_loc, b_end = row(12 + 2 * h), row(13 + 2 * h)
                m_prev = m_ref[idx]
                cn = cn_ref[idx]

                inter = b_r + m_prev
                m_t = jnp.maximum(inter, m_i)
                a = jnp.exp(inter - m_t)
                e = jnp.exp(m_i - m_t)
                cq = lax.dot_general(cn.astype(BF16), q, (((1,), (1,)), ((), ())),
                                     preferred_element_type=F32)
                den = e * den_i + a * cq[M_DV:M_DV + 1, :]
                scale = 1.0 / jnp.maximum(jnp.abs(den), jnp.exp(-m_t))
                ht = (e * num_ref[b, 0, h] + a * cq[0:M_DV, :]) * scale
                h_ref[b, :, h * M_DV:(h + 1) * M_DV] = ht.T.astype(BF16)

                m_new = jnp.maximum(b_end + m_prev, m_loc)
                a_s = jnp.exp(b_end + m_prev - m_new)
                s_s = jnp.exp(m_loc - m_new)
                cn_ref[idx] = a_s[:, :M_DK] * cn + s_s[:, :M_DK] * cl_ref[b, 0, h]
                m_ref[idx] = m_new


def _mlstm(p, g, gate_b, ctx_chunks):
    B, L, _ = p.shape
    nc = L // M_CHUNK
    width = M_HEADS * M_DV
    bias = jnp.zeros((1, G_WIDTH), F32).at[0, :16].set(gate_b)
    assert nc % INTRA_CHUNKS == 0
    tok = lambda w, cb=0: pl.BlockSpec((1, INTRA_CHUNKS * M_CHUNK, w), lambda b, c: (b, c, cb))
    num_shape, vec_shape, cl_shape = (M_HEADS, M_DV, M_CHUNK), (VEC_ROWS, LANES), (M_HEADS, C_ROWS, M_DK)
    per_chunk = lambda s: pl.BlockSpec((1, INTRA_CHUNKS) + s, lambda b, c: (b, c) + (0,) * len(s))
    f32 = lambda *s: jax.ShapeDtypeStruct(s, F32)
    numf, numb, vecf, vecb, clf, clb = pl.pallas_call(
        _mlstm_intra_kernel,
        out_shape=(f32(B, nc, *num_shape),) * 2 + (f32(B, nc, *vec_shape),) * 2 + (f32(B, nc, *cl_shape),) * 2,
        grid=(B, nc // INTRA_CHUNKS),
        in_specs=[tok(width, P_QK // width), tok(width, P_MV // width), tok(G_WIDTH),
                  pl.BlockSpec((1, G_WIDTH), lambda b, c: (0, 0))],
        out_specs=(per_chunk(num_shape),) * 2 + (per_chunk(vec_shape),) * 2 + (per_chunk(cl_shape),) * 2,
        compiler_params=_params(("parallel", "parallel")),
        name="mlstm_intra",
    )(p, p, g, bias)

    fwd = lambda j: j
    bwd = lambda j: jnp.where(j < ctx_chunks, ctx_chunks - 1 - j, nc - 1 + ctx_chunks - j)
    stok = lambda cm, w, cb=0: pl.BlockSpec((B, M_CHUNK, w), lambda j: (0, cm(j), cb))
    schunk = lambda cm, s: pl.BlockSpec((B, 1) + s, lambda j: (0, cm(j)) + (0,) * len(s))
    side = lambda cm: [stok(cm, width, P_QK // width), schunk(cm, num_shape), schunk(cm, vec_shape),
                       schunk(cm, cl_shape)]
    chains = 2 * B * M_HEADS
    return pl.pallas_call(
        functools.partial(_mlstm_scan_kernel, n_batch=B),
        out_shape=(jax.ShapeDtypeStruct((B, L, width), BF16),) * 2,
        grid=(nc,),
        in_specs=side(fwd) + side(bwd),
        out_specs=(stok(fwd, width), stok(bwd, width)),
        scratch_shapes=[pltpu.VMEM((chains, C_ROWS, M_DK), F32),
                        pltpu.VMEM((chains, 1, LANES), F32)],
        compiler_params=_params(("arbitrary",)),
        name="mlstm_scan",
    )(p, numf, vecf, clf, p, numb, vecb, clb)


def _head_norm_rope(x, w, cos, sin, bd):
    sq = x * x
    hi = sq.astype(BF16)
    lo = (sq - hi.astype(F32)).astype(BF16)
    ms = jnp.dot(hi, bd, preferred_element_type=F32) + jnp.dot(lo, bd, preferred_element_type=F32)
    y = x * lax.rsqrt(ms + EPS) * w
    lane = lax.broadcasted_iota(jnp.int32, y.shape, 1)
    partner = jnp.where(lane % A_DH < A_DH // 2,
                        pltpu.roll(y, LANES - A_DH // 2, axis=1), pltpu.roll(y, A_DH // 2, axis=1))
    return y * cos + partner * sin


Q_SCALE = A_DH ** -0.5 * math.log2(math.e)


def _attn_prep_tile(q, kv, cos, sin, qw, kw, qt_ref, k_ref, vt_ref):
    r = lax.broadcasted_iota(jnp.int32, (LANES, LANES), 0) // A_DH
    c = lax.broadcasted_iota(jnp.int32, (LANES, LANES), 1) // A_DH
    bd = jnp.where(r == c, 1.0 / A_DH, 0.0).astype(BF16)
    for pair in range(A_HEADS // 2):
        y = _head_norm_rope(q[:, pair * LANES:(pair + 1) * LANES], qw, cos, sin, bd) * Q_SCALE
        qt_ref[0, pair * LANES:(pair + 1) * LANES, :] = y.T.astype(BF16)
    k = _head_norm_rope(kv[:, :LANES], kw, cos, sin, bd).astype(BF16)
    for kvh in range(A_KV_HEADS):
        k_ref[0, kvh, 0] = k[:, kvh * A_DH:(kvh + 1) * A_DH]
    vt = kv[:, LANES:].T.astype(BF16)
    ones = jnp.ones((VT_ROWS - A_DH, vt.shape[1]), BF16)
    for kvh in range(A_KV_HEADS):
        vt_ref[0, kvh, 0] = jnp.concatenate([vt[kvh * A_DH:(kvh + 1) * A_DH, :], ones], axis=0)


ATT_SUB = 256
ATT_PIECE = 128
VT_ROWS = A_DH + 16


def _attn_kernel(qt_ref, k_ref, vt_ref, o_ref, sa_ref, sb_ref, ma_ref, mb_ref, acc_ref,
                 *, blocks, tq, ctx_tiles, q_tile0):
    q_of = lambda g: qt_ref[0, g * A_DH:(g + 1) * A_DH, :]
    lanes = lambda g: slice(g * tq, (g + 1) * tq)
    head, mid, n_mid, tail = blocks

    def step(nxt, cur, ms):
        out = []
        for g in range(A_GROUP):
            if cur is not None:
                c0, c_subs, cs_ref, cm_ref = cur
                m_new = jnp.maximum(ms[g], cm_ref[:, lanes(g)])
                alpha = jnp.exp2(ms[g] - m_new)
            best, pv = None, None
            for r in range(max(nxt[1] if nxt else 0, cur[1] if cur else 0)):
                parts = []
                for piece in range(ATT_SUB // ATT_PIECE):
                    rows = pl.ds(piece * ATT_PIECE, ATT_PIECE)
                    buf_rows = pl.ds(r * ATT_SUB + piece * ATT_PIECE, ATT_PIECE)
                    if nxt is not None and r < nxt[1]:
                        s = jnp.dot(k_ref[0, 0, nxt[0] + r, rows, :], q_of(g), preferred_element_type=F32)
                        nxt[2][g, buf_rows, :] = s
                        top = jnp.max(s, axis=0, keepdims=True)
                        best = top if best is None else jnp.maximum(best, top)
                    if cur is not None and r < c_subs:
                        parts.append(jnp.exp2(cs_ref[g, buf_rows, :] - m_new).astype(BF16))
                if cur is not None and r < c_subs:
                    d = jnp.dot(vt_ref[0, 0, c0 + r], jnp.concatenate(parts, axis=0), preferred_element_type=F32)
                    pv = d if pv is None else pv + d
            if nxt is not None:
                nxt[3][:, lanes(g)] = best
            if cur is not None:
                acc_ref[:, lanes(g)] = alpha * acc_ref[:, lanes(g)] + pv
                out.append(m_new)
            else:
                out.append(ms[g])
        return tuple(out)

    def finish():
        o = acc_ref[0:A_DH, :] / acc_ref[A_DH:A_DH + 1, :]
        o = jnp.concatenate([o[:, lanes(g)] for g in range(A_GROUP)], axis=0)
        o_ref[0] = o.T.astype(BF16)

    acc_ref[...] = jnp.zeros_like(acc_ref)
    init = (jnp.full((1, tq), -jnp.inf, F32),) * A_GROUP
    is_ctx = pl.program_id(2) + q_tile0 < ctx_tiles
    buf_a, buf_b = (sa_ref, ma_ref), (sb_ref, mb_ref)

    mid_block = lambda i, buf: (head + mid * i, mid, *buf)

    @pl.when(is_ctx)
    def _():
        step((0, head, *buf_a), None, init)
        step(None, (0, head, *buf_a), init)
        finish()

    @pl.when(jnp.logical_not(is_ctx))
    def _():
        step((0, head, *buf_a), None, init)
        ms = step(mid_block(0, buf_b), (0, head, *buf_a), init)

        def pair(j, ms):
            ms = step(mid_block(2 * j + 1, buf_a), mid_block(2 * j, buf_b), ms)
            return step(mid_block(2 * j + 2, buf_b), mid_block(2 * j + 1, buf_a), ms)

        ms = lax.fori_loop(0, n_mid // 2 - 1, pair, ms)
        last = n_mid - 1
        ms = step(mid_block(last, buf_a), mid_block(last - 1, buf_b), ms)
        tail_block = (head + mid * n_mid, tail, *buf_b)
        ms = step(tail_block, mid_block(last, buf_a), ms)
        step(None, tail_block, ms)
        finish()


def _attention(qt, k, vt, *, q_tile0, n_ctx, tq):
    B, _, L = qt.shape
    n_sub = k.shape[2]
    head, mid = n_ctx // ATT_SUB, 3
    n_mid = (n_sub - head - 1) // mid // 2 * 2
    tail = n_sub - head - mid * n_mid
    assert k.shape[3] == ATT_SUB and n_ctx == tq == ATT_SUB and n_mid >= 2 and 1 <= tail <= mid
    width = A_GROUP * A_DH
    n = A_GROUP * tq
    s_buf, m_buf = pltpu.VMEM((A_GROUP, mid * ATT_SUB, tq), F32), pltpu.VMEM((1, n), F32)
    return pl.pallas_call(
        functools.partial(_attn_kernel, blocks=(head, mid, n_mid, tail), tq=tq, ctx_tiles=n_ctx // tq,
                          q_tile0=q_tile0),
        out_shape=jax.ShapeDtypeStruct((B, L - q_tile0 * tq, A_HEADS * A_DH), BF16),
        grid=(B, A_KV_HEADS, L // tq - q_tile0),
        in_specs=[pl.BlockSpec((1, width, tq), lambda b, kv, t: (b, kv, t + q_tile0)),
                  pl.BlockSpec((1, 1, n_sub, ATT_SUB, A_DH), lambda b, kv, t: (b, kv, 0, 0, 0)),
                  pl.BlockSpec((1, 1, n_sub, VT_ROWS, ATT_SUB), lambda b, kv, t: (b, kv, 0, 0, 0))],
        out_specs=pl.BlockSpec((1, tq, width), lambda b, kv, t: (b, t, kv)),
        scratch_shapes=[s_buf, s_buf, m_buf, m_buf, pltpu.VMEM((VT_ROWS, n), F32)],
        compiler_params=_params(("parallel", "parallel", "arbitrary")),
        name="attention",
    )(qt, k, vt)


def _mixer_out_kernel(*refs, with_router, n_stream, row_off):
    hf_ref, hb_ref, mo_ref, a_ref = refs[:4]
    mod_ref, mnw_ref, n2w_ref, wout_ref = refs[4 + n_stream:8 + n_stream]
    rest = refs[8 + n_stream:]
    hs = hf_ref[0].astype(F32) + hb_ref[0].astype(F32)
    hn = jnp.concatenate([_rms_rows(hs[:, h * M_DV:(h + 1) * M_DV]) for h in range(M_HEADS)], axis=1)
    m = hn * mnw_ref[...] * _sigmoid(mo_ref[0].astype(F32))
    y_in = jnp.concatenate([m.astype(BF16), a_ref[0]], axis=1)
    mod = mod_ref[0, 0]
    x1 = _stream_tile(refs[4:4 + n_stream], row_off) + mod[2:3] * jnp.dot(y_in, wout_ref[...],
                                                                          preferred_element_type=F32)
    h2 = _rms_rows(x1) * n2w_ref[...] * (1.0 + mod[4:5]) + mod[3:4]
    if not with_router:
        wg_ref, wu_ref, wd_ref, o_ref = rest
        hb16 = h2.astype(BF16)
        g = jnp.dot(hb16, wg_ref[...], preferred_element_type=F32)
        u = jnp.dot(hb16, wu_ref[...], preferred_element_type=F32)
        act = (g * _sigmoid(g) * u).astype(BF16)
        o_ref[0] = x1 + mod[5:6] * jnp.dot(act, wd_ref[...], preferred_element_type=F32)
        return
    router_ref, x1_ref, h2_ref, ids_ref, gates_ref = rest
    x1_ref[0] = x1
    h2_ref[0] = h2
    h_hi = h2.astype(BF16)
    h_lo = (h2 - h_hi.astype(F32)).astype(BF16)
    logits = (jnp.dot(h_hi, router_ref[0], preferred_element_type=F32)
              + jnp.dot(h_lo, router_ref[0], preferred_element_type=F32)
              + jnp.dot(h_hi, router_ref[1], preferred_element_type=F32))
    lane = lax.broadcasted_iota(jnp.int32, logits.shape, 1)
    logits = jnp.where(lane < N_EXPERTS, logits, -jnp.inf)
    m1 = jnp.max(logits, axis=-1, keepdims=True)
    i1 = jnp.min(jnp.where(logits == m1, lane, LANES), axis=-1, keepdims=True)
    rest = jnp.where(lane == i1, -jnp.inf, logits)
    m2 = jnp.max(rest, axis=-1, keepdims=True)
    i2 = jnp.min(jnp.where(rest == m2, lane, LANES), axis=-1, keepdims=True)
    e2 = jnp.exp(m2 - m1)
    g1 = 1.0 / (1.0 + e2)
    ids_ref[0] = jnp.where(lane == 0, i1, jnp.where(lane == 1, i2, -1))
    gates_ref[0] = jnp.where(lane == 0, g1, jnp.where(lane == 1, e2 * g1, 0.0))


def _mixer_out(hf, hb, p, a, stream, mods, mnw, n2w, wout, *, ffn=None, router=None, row_off=0):
    B, L = p.shape[:2]
    D = D_MODEL
    nt = L // ROW_TILE - row_off
    rin = lambda w, cb=0: pl.BlockSpec((1, ROW_TILE, w), lambda b, t: (b, t + row_off, cb))
    rout = lambda w: pl.BlockSpec((1, ROW_TILE, w), lambda b, t: (b, t, 0))
    const = lambda arr: pl.BlockSpec(arr.shape, lambda b, t: (0,) * arr.ndim, pipeline_mode=pl.Buffered(1))
    mw = M_HEADS * M_DV
    a_off = row_off - (L - a.shape[1]) // ROW_TILE
    a_spec = pl.BlockSpec((1, ROW_TILE, A_HEADS * A_DH), lambda b, t: (b, t + a_off, 0))
    s_specs, s_args = _stream_specs(stream, row_off)
    in_specs = [rin(mw), rin(mw), rin(mw, P_MO // mw), a_spec, *s_specs, _mod_spec(row_off),
                pl.BlockSpec((1, mw), lambda b, t: (0, 0)), pl.BlockSpec((1, D), lambda b, t: (0, 0)),
                const(wout)]
    args = [hf, hb, p, a, *s_args, mods, mnw.reshape(1, mw), n2w.reshape(1, D), wout]
    rows = nt * ROW_TILE
    if router is None:
        in_specs += [const(w) for w in ffn]
        args += list(ffn)
        out_shape, out_specs = jax.ShapeDtypeStruct((B, rows, D), F32), rout(D)
    else:
        in_specs.append(const(router))
        args.append(router)
        out_shape = (jax.ShapeDtypeStruct((B, rows, D), F32), jax.ShapeDtypeStruct((B, rows, D), F32),
                     jax.ShapeDtypeStruct((B, rows, LANES), jnp.int32), jax.ShapeDtypeStruct((B, rows, LANES), F32))
        out_specs = (rout(D), rout(D), rout(LANES), rout(LANES))
    return pl.pallas_call(
        functools.partial(_mixer_out_kernel, with_router=router is not None, n_stream=len(s_args), row_off=row_off),
        out_shape=out_shape,
        grid=(B, nt),
        in_specs=in_specs,
        out_specs=out_specs,
        compiler_params=_params(("parallel", "arbitrary")),
        name="mixer_out",
    )(*args)


MOE_TM = 512
MOE_FF = 1792
MOE_TD = 256
RANK_TILE = 1024
POS_TILE = 2048
ISSUE_UNROLL = 8


def _moe_rank_kernel(ids_ref, rank_ref, cnt_ref, carry_ref, before_ref):
    @pl.when(pl.program_id(0) == 0)
    def _():
        carry_ref[...] = jnp.zeros_like(carry_ref)
        r = lax.broadcasted_iota(jnp.int32, (RANK_TILE, RANK_TILE), 0)
        c = lax.broadcasted_iota(jnp.int32, (RANK_TILE, RANK_TILE), 1)
        before_ref[...] = jnp.where(c < r, 1.0, 0.0).astype(BF16)

    ids = ids_ref[...]
    lane = lax.broadcasted_iota(jnp.int32, ids.shape, 1)
    onehot = jnp.where(jnp.logical_or(lane == ids[:, 0:1], lane == ids[:, 1:2]), 1.0, 0.0)
    rank_ref[...] = jnp.dot(before_ref[...], onehot.astype(BF16), preferred_element_type=F32) + carry_ref[...]
    carry_ref[...] += jnp.sum(onehot, axis=0, keepdims=True)
    cnt_ref[...] = carry_ref[...]


def _moe_rank(ids):
    n = ids.shape[0]
    return pl.pallas_call(
        _moe_rank_kernel,
        out_shape=(jax.ShapeDtypeStruct((n, LANES), F32), jax.ShapeDtypeStruct((1, LANES), F32)),
        grid=(n // RANK_TILE,),
        in_specs=[pl.BlockSpec((RANK_TILE, LANES), lambda t: (t, 0))],
        out_specs=(pl.BlockSpec((RANK_TILE, LANES), lambda t: (t, 0)), pl.BlockSpec((1, LANES), lambda t: (0, 0))),
        scratch_shapes=[pltpu.VMEM((1, LANES), F32), pltpu.VMEM((RANK_TILE, RANK_TILE), BF16)],
        compiler_params=_params(("arbitrary",)),
        name="moe_rank",
    )(ids)


def _moe_pos_kernel(ids_ref, rank_ref, start_ref, pos_ref):
    ids = ids_ref[...]
    lane = lax.broadcasted_iota(jnp.int32, ids.shape, 1)
    tgt = start_ref[...] + rank_ref[...]
    p0 = jnp.sum(jnp.where(lane == ids[:, 0:1], tgt, 0.0), axis=-1, keepdims=True)
    p1 = jnp.sum(jnp.where(lane == ids[:, 1:2], tgt, 0.0), axis=-1, keepdims=True)
    pos_ref[...] = jnp.where(lane == 0, p0, jnp.where(lane == 1, p1, 0.0)).astype(jnp.int32)


def _moe_pos(ids, rank, start_row):
    n = ids.shape[0]
    blk = pl.BlockSpec((POS_TILE, LANES), lambda t: (t, 0))
    return pl.pallas_call(
        _moe_pos_kernel,
        out_shape=jax.ShapeDtypeStruct((n, LANES), jnp.int32),
        grid=(n // POS_TILE,),
        in_specs=[blk, blk, pl.BlockSpec((1, LANES), lambda t: (0, 0))],
        out_specs=blk,
        compiler_params=_params(("parallel",)),
        name="moe_pos",
    )(ids, rank, start_row)


def _row_copy(src, src_row, dst, dst_row, sem):
    return pltpu.make_async_copy(src.at[pl.ds(src_row, 1), :], dst.at[pl.ds(dst_row, 1), :], sem)


def _moe_dispatch_kernel(pad_ref, pos_ref, h_ref, xs_ref, zero_ref, sem):
    @pl.when(pl.program_id(0) == 0)
    def _():
        zero_ref[...] = jnp.zeros_like(zero_ref)
        fills = [pltpu.make_async_copy(
            zero_ref, xs_ref.at[pl.ds(pl.multiple_of(pad_ref[e], SUBLANES), MOE_TM + SUBLANES), :], sem)
            for e in range(N_EXPERTS)]
        for cp in fills:
            cp.start()
        for cp in fills:
            cp.wait()

        def fill_tile(j, carry):
            cp = pltpu.make_async_copy(zero_ref.at[pl.ds(0, MOE_TM), :],
                                       xs_ref.at[pl.ds(pl.multiple_of(j * MOE_TM, MOE_TM), MOE_TM), :], sem)
            cp.start()
            cp.wait()
            return carry

        lax.fori_loop(pad_ref[N_EXPERTS], xs_ref.shape[0] // MOE_TM, fill_tile, 0)

    def issue(r, carry):
        for k in range(2):
            _row_copy(h_ref, r, xs_ref, pos_ref[0, 0, 2 * r + k], sem).start(priority=k)
        return carry

    lax.fori_loop(0, MOE_TD, issue, 0, unroll=ISSUE_UNROLL)
    for k in range(2):
        pltpu.make_async_copy(h_ref, xs_ref.at[pl.ds(0, MOE_TD), :], sem).wait()


def _moe_dispatch(h, pos, fill_meta, ns):
    n, d = h.shape
    return pl.pallas_call(
        _moe_dispatch_kernel,
        out_shape=jax.ShapeDtypeStruct((ns, d), F32),
        grid_spec=pltpu.PrefetchScalarGridSpec(
            num_scalar_prefetch=1,
            grid=(n // MOE_TD,),
            in_specs=[pl.BlockSpec((1, 1, 2 * MOE_TD), lambda t, pad: (t, 0, 0), memory_space=pltpu.SMEM),
                      pl.BlockSpec((MOE_TD, d), lambda t, pad: (t, 0))],
            out_specs=pl.BlockSpec(memory_space=pl.ANY),
            scratch_shapes=[pltpu.VMEM((MOE_TM + SUBLANES, d), F32), pltpu.SemaphoreType.DMA(())]),
        compiler_params=_params(("arbitrary",)),
        name="moe_dispatch",
    )(fill_meta, pos, h)


def _moe_group_kernel(te_ref, nv_ref, xs_ref, wg_ref, wu_ref, wd_ref, ys_ref, acc_ref):
    i, f = pl.program_id(0), pl.program_id(1)
    last = pl.num_programs(1) - 1
    valid = i < nv_ref[0]

    @pl.when(valid)
    def _():
        @pl.when(f == 0)
        def _():
            acc_ref[...] = jnp.zeros_like(acc_ref)

        x = xs_ref[...].astype(BF16)
        g = jnp.dot(x, wg_ref[0], preferred_element_type=F32)
        u = jnp.dot(x, wu_ref[0], preferred_element_type=F32)
        act = (g * _sigmoid(g) * u).astype(BF16)
        acc_ref[...] += jnp.dot(act, wd_ref[0], preferred_element_type=F32)

        @pl.when(f == last)
        def _():
            ys_ref[...] = acc_ref[...]

    @pl.when(jnp.logical_and(jnp.logical_not(valid), f == last))
    def _():
        ys_ref[...] = jnp.zeros_like(ys_ref)


def _moe_group(xs, tile_expert, n_valid, wg, wu, wd):
    ns, d = xs.shape
    n_tiles = ns // MOE_TM - 1
    ff = wg.shape[2]
    live = lambda i, nv: i < nv[0]
    return pl.pallas_call(
        _moe_group_kernel,
        out_shape=jax.ShapeDtypeStruct((n_tiles * MOE_TM, d), F32),
        grid_spec=pltpu.PrefetchScalarGridSpec(
            num_scalar_prefetch=2,
            grid=(n_tiles, ff // MOE_FF),
            in_specs=[pl.BlockSpec((MOE_TM, d), lambda i, f, te, nv: (jnp.where(live(i, nv), i, 0), 0)),
                      pl.BlockSpec((1, d, MOE_FF), lambda i, f, te, nv: (te[i], 0, f)),
                      pl.BlockSpec((1, d, MOE_FF), lambda i, f, te, nv: (te[i], 0, f)),
                      pl.BlockSpec((1, MOE_FF, d), lambda i, f, te, nv: (te[i], f, 0))],
            out_specs=pl.BlockSpec((MOE_TM, d), lambda i, f, te, nv: (i, 0)),
            scratch_shapes=[pltpu.VMEM((MOE_TM, d), F32)]),
        compiler_params=_params(("arbitrary", "arbitrary")),
        name="moe_group",
    )(tile_expert, n_valid, xs, wg, wu, wd)


def _moe_combine_kernel(pos_ref, x_ref, gates_ref, mod_ref, fw_ref, ys_ref, o_ref, ybuf, sem):
    def issue(r, carry):
        for k in range(2):
            _row_copy(ys_ref, pos_ref[0, 0, 2 * r + k], ybuf.at[k], r, sem).start(priority=k)
        return carry

    lax.fori_loop(0, MOE_TD, issue, 0, unroll=ISSUE_UNROLL)
    for k in range(2):
        pltpu.make_async_copy(ys_ref.at[pl.ds(0, MOE_TD), :], ybuf.at[k], sem).wait()
    gates = gates_ref[...]
    y = gates[:, 0:1] * ybuf[0] + gates[:, 1:2] * ybuf[1]
    x2 = x_ref[...] + mod_ref[0, 0][5:6] * y
    o_ref[...] = _rms_rows(x2) * fw_ref[...]


def _moe_combine(pos, x1, gates, mods, fw, ys, tokens_per_sample):
    n, d = x1.shape
    per = tokens_per_sample // MOE_TD
    return pl.pallas_call(
        _moe_combine_kernel,
        out_shape=jax.ShapeDtypeStruct((n, d), F32),
        grid=(n // MOE_TD,),
        in_specs=[pl.BlockSpec((1, 1, 2 * MOE_TD), lambda t: (t, 0, 0), memory_space=pltpu.SMEM),
                  pl.BlockSpec((MOE_TD, d), lambda t: (t, 0)),
                  pl.BlockSpec((MOE_TD, LANES), lambda t: (t, 0)),
                  pl.BlockSpec((1, 1, 6, d), lambda t: (t // per, 1, 0, 0)),
                  pl.BlockSpec((1, d), lambda t: (0, 0)),
                  pl.BlockSpec(memory_space=pl.ANY)],
        out_specs=pl.BlockSpec((MOE_TD, d), lambda t: (t, 0)),
        scratch_shapes=[pltpu.VMEM((2, MOE_TD, d), F32), pltpu.SemaphoreType.DMA(())],
        compiler_params=_params(("arbitrary",)),
        name="moe_combine",
    )(pos, x1, gates, mods, fw.reshape(1, d), ys)


def _moe(h2, ids, gates, x1, mods, wg, wu, wd, fw):
    B, T, D = x1.shape
    n = B * T
    ids, gates = ids.reshape(n, LANES), gates.reshape(n, LANES)
    rank, cnt = _moe_rank(ids)
    cnt = cnt[0, :N_EXPERTS].astype(jnp.int32)
    padded = (cnt + MOE_TM - 1) // MOE_TM * MOE_TM
    end = jnp.cumsum(padded)
    start = end - padded
    n_tiles = 2 * n // MOE_TM + N_EXPERTS
    tile_expert = jnp.minimum(jnp.sum(jnp.arange(n_tiles)[:, None] >= (end // MOE_TM)[None, :], axis=1),
                              N_EXPERTS - 1).astype(jnp.int32)
    n_valid = (end[-1:] // MOE_TM).astype(jnp.int32)
    start_row = jnp.zeros((1, LANES), F32).at[0, :N_EXPERTS].set(start.astype(F32))
    pos = _moe_pos(ids, rank, start_row)
    pos = pos[:, :2].reshape(n // MOE_TD, 1, 2 * MOE_TD)
    fill_meta = jnp.concatenate([(start + cnt) // SUBLANES * SUBLANES, n_valid]).astype(jnp.int32)
    xs = _moe_dispatch(h2.reshape(n, D), pos, fill_meta, (n_tiles + 1) * MOE_TM)
    ys = _moe_group(xs, tile_expert, n_valid, wg, wu, wd)
    return _moe_combine(pos, x1.reshape(n, D), gates, mods, fw, ys, T).reshape(B, T, D)


_ROT_PERM = np.concatenate([np.arange(0, A_DH, 2), np.arange(1, A_DH, 2)])


def _prep_w_in(w):
    o = np.cumsum([0, M_HEADS * M_DK, M_HEADS * M_DK, M_HEADS * M_DV, M_HEADS * M_DV, 4 * M_HEADS,
                   A_HEADS * A_DH, A_KV_HEADS * A_DH, A_KV_HEADS * A_DH])
    mq, mk, mv, mo, mg, aq, ak, av = [w[:, o[i]:o[i + 1]] for i in range(8)]
    qk = jnp.concatenate([jnp.concatenate([mq[:, h * M_DK:(h + 1) * M_DK] * (M_DK ** -0.5),
                                           mk[:, h * M_DK:(h + 1) * M_DK]], axis=1) for h in range(M_HEADS)], axis=1)
    perm_q = np.concatenate([h * A_DH + _ROT_PERM for h in range(A_HEADS)])
    perm_k = np.concatenate([h * A_DH + _ROT_PERM for h in range(A_KV_HEADS)])
    pad = jnp.zeros((w.shape[0], G_WIDTH - 4 * M_HEADS), w.dtype)
    return jnp.concatenate([qk, mv, mo, aq[:, perm_q], ak[:, perm_k], av, mg, pad], axis=1).astype(BF16)


def _rope_tables(n_tok, n_ctx):
    rows = n_tok // GRID_W
    row = jnp.broadcast_to(jnp.arange(rows, dtype=F32)[:, None], (rows, GRID_W)).reshape(n_tok)
    col = jnp.broadcast_to(jnp.arange(GRID_W, dtype=F32)[None, :], (rows, GRID_W)).reshape(n_tok)
    n_freq = A_DH // 4
    inv_freq = ROPE_THETA ** (-jnp.arange(n_freq, dtype=F32) / n_freq)
    ang = jnp.concatenate([row[:, None] * inv_freq, col[:, None] * inv_freq], axis=-1)
    cos, sin = jnp.cos(ang), jnp.sin(ang)
    cos = jnp.concatenate([jnp.ones((n_ctx, A_DH // 2), F32), cos], axis=0)
    sin = jnp.concatenate([jnp.zeros((n_ctx, A_DH // 2), F32), sin], axis=0)
    return jnp.tile(cos, (1, 4)), jnp.tile(jnp.concatenate([-sin, sin], axis=1), (1, 2))


def kernel(x, c, ctx, c_ctx, ada_w, ada_b, norm1_w, norm2_w, w_in, mlstm_gate_b, mlstm_norm_w, q_norm_w, k_norm_w,
           w_out, ffn_w_gate, ffn_w_up, ffn_w_down, moe_router, moe_w_gate, moe_w_up, moe_w_down, final_norm_w):
    B, T, D = x.shape
    n_ctx = ctx.shape[1]
    L = n_ctx + T
    depth = w_in.shape[0]
    assert D == D_MODEL and n_ctx == ROW_TILE and T % RANK_TILE == 0 and depth == 2
    ctx_tiles = n_ctx // ROW_TILE
    tq = ROW_TILE

    xa = (ctx, x)
    cvec =jnp.concatenate([c, c_ctx[None], jnp.zeros((8 - B - 1, D), F32)], axis=0)
    cos, sin = _rope_tables(T, n_ctx)
    out = None
    for i in range(depth):
        last = i == depth - 1
        modraw = _ada(cvec, ada_w[i], ada_b[i])
        mods = jnp.stack([jnp.broadcast_to(modraw[B].reshape(1, 6, D), (B, 6, D)),
                          modraw[:B].reshape(B, 6, D)], axis=1)
        qw = jnp.tile(q_norm_w[i][_ROT_PERM], 2).reshape(1, LANES)
        kw = jnp.tile(k_norm_w[i][_ROT_PERM], 2).reshape(1, LANES)
        p, g, qt, k, vt = _in_proj(xa, mods, norm1_w[i], _prep_w_in(w_in[i]), cos, sin, qw, kw)
        hf, hb = _mlstm(p, g, mlstm_gate_b[i], n_ctx // M_CHUNK)
        a = _attention(qt, k, vt, q_tile0=ctx_tiles if last else 0, n_ctx=n_ctx, tq=tq)
        wout = w_out[i].astype(BF16)
        if not last:
            j = i // 2
            ffn = (ffn_w_gate[j].astype(BF16), ffn_w_up[j].astype(BF16), ffn_w_down[j].astype(BF16))
            xa = _mixer_out(hf, hb, p, a, xa, mods, mlstm_norm_w[i], norm2_w[i], wout, ffn=ffn)
        else:
            j = i // 2
            router = jnp.zeros((D, LANES), F32).at[:, :N_EXPERTS].set(moe_router[j])
            router_hi = router.astype(BF16)
            router = jnp.stack([router_hi, (router - router_hi.astype(F32)).astype(BF16)])
            x1, h2, ids, gates = _mixer_out(hf, hb, p, a, xa, mods, mlstm_norm_w[i], norm2_w[i], wout,
                                            router=router, row_off=ctx_tiles)
            out = _moe(h2, ids, gates, x1, mods, moe_w_gate[j].astype(BF16), moe_w_up[j].astype(BF16),
                       moe_w_down[j].astype(BF16), final_norm_w)
    return out
```

```python
import functools
import math

import numpy as np
import jax
import jax.numpy as jnp
from jax import lax
from jax.experimental import pallas as pl
from jax.experimental.pallas import tpu as pltpu

F32 = jnp.float32
BF16 = jnp.bfloat16
HIGHEST = lax.Precision.HIGHEST

D_MODEL = 1024
GRID_W = 64
M_HEADS = 4
M_DV = 128
M_DK = 64
M_CHUNK = 128
A_HEADS = 8
A_KV_HEADS = 2
A_GROUP = A_HEADS // A_KV_HEADS
A_DH = 64
ROPE_THETA = 10000.0
N_EXPERTS = 8
EPS = 1e-6

LANES = 128
SUBLANES = 8
ROW_TILE = 256
VMEM_LIMIT = 56 * 1024 * 1024

P_QK = 0
P_MV = 512
P_MO = 1024
P_AQ = 1536
P_AKV = 2048
P_WIDTH = 2304
G_WIDTH = LANES


def _params(sem, vmem=VMEM_LIMIT, flags=None):
    return pltpu.CompilerParams(dimension_semantics=sem, vmem_limit_bytes=vmem, flags=flags)


def _sigmoid(x):
    return 1.0 / (1.0 + jnp.exp(-x))


def _rms_rows(x):
    return x * lax.rsqrt(jnp.mean(x * x, axis=-1, keepdims=True) + EPS)


def _ada_kernel(c_ref, w_ref, b_ref, o_ref):
    c = c_ref[...]
    s = c * _sigmoid(c)
    o_ref[...] = jnp.dot(s, w_ref[...], precision=HIGHEST, preferred_element_type=F32) + b_ref[...]


def _ada(cvec, w, b):
    n = w.shape[1]
    bn = 1536
    return pl.pallas_call(
        _ada_kernel,
        out_shape=jax.ShapeDtypeStruct((cvec.shape[0], n), F32),
        grid=(n // bn,),
        in_specs=[pl.BlockSpec(cvec.shape, lambda j: (0, 0)),
                  pl.BlockSpec((w.shape[0], bn), lambda j: (0, j)),
                  pl.BlockSpec((1, bn), lambda j: (0, j))],
        out_specs=pl.BlockSpec((cvec.shape[0], bn), lambda j: (0, j)),
        compiler_params=_params(("arbitrary",)),
        name="ada_mod",
    )(cvec, w, b.reshape(1, n))


def _mod_spec(off=0):
    return pl.BlockSpec((1, 1, 6, D_MODEL), lambda b, t: (b, jnp.minimum(t + off, 1), 0, 0))


def _stream_specs(stream, off=0):
    tile = (1, ROW_TILE, D_MODEL)
    if not isinstance(stream, tuple):
        return [pl.BlockSpec(tile, lambda b, t: (b, t + off, 0))], [stream]
    return ([pl.BlockSpec(tile, lambda b, t: (b, 0, 0)),
             pl.BlockSpec(tile, lambda b, t: (b, jnp.maximum(t + off - 1, 0), 0))], list(stream))


def _stream_tile(refs, off=0):
    if len(refs) == 1:
        return refs[0][0]
    return jnp.where(pl.program_id(1) + off == 0, refs[0][0], refs[1][0])


def _in_proj_kernel(*refs, n_stream):
    mod_ref, nw_ref, w_ref, cos_ref, sin_ref, qw_ref, kw_ref, p_ref, g_ref, qt_ref, k_ref, vt_ref = refs[n_stream:]
    mod = mod_ref[0, 0]
    h = _rms_rows(_stream_tile(refs[:n_stream])) * nw_ref[...] * (1.0 + mod[1:2]) + mod[0:1]
    r = jnp.dot(h.astype(BF16), w_ref[...], preferred_element_type=F32)
    p_ref[0] = r[:, :P_AQ].astype(BF16)
    g_ref[0] = r[:, P_WIDTH:]
    _attn_prep_tile(r[:, P_AQ:P_AKV], r[:, P_AKV:P_WIDTH], cos_ref[...], sin_ref[...], qw_ref[...], kw_ref[...],
                    qt_ref, k_ref, vt_ref)


def _in_proj(stream, mods, nw, wp, cos, sin, qw, kw):
    s_specs, s_args = _stream_specs(stream)
    B = s_args[0].shape[0]
    L = sum(s.shape[1] for s in s_args)
    D = D_MODEL
    nt = L // ROW_TILE
    row = lambda w: pl.BlockSpec((1, ROW_TILE, w), lambda b, t: (b, t, 0))
    vec = pl.BlockSpec((1, LANES), lambda b, t: (0, 0))
    table = pl.BlockSpec((ROW_TILE, LANES), lambda b, t: (t, 0))
    return pl.pallas_call(
        functools.partial(_in_proj_kernel, n_stream=len(s_args)),
        out_shape=(jax.ShapeDtypeStruct((B, L, P_AQ), BF16),
                   jax.ShapeDtypeStruct((B, L, G_WIDTH), F32),
                   jax.ShapeDtypeStruct((B, A_HEADS * A_DH, L), BF16),
                   jax.ShapeDtypeStruct((B, A_KV_HEADS, nt, ROW_TILE, A_DH), BF16),
                   jax.ShapeDtypeStruct((B, A_KV_HEADS, nt, VT_ROWS, ROW_TILE), BF16)),
        grid=(B, nt),
        in_specs=[*s_specs, _mod_spec(),
                  pl.BlockSpec((1, D), lambda b, t: (0, 0)),
                  pl.BlockSpec(wp.shape, lambda b, t: (0, 0), pipeline_mode=pl.Buffered(1)),
                  table, table, vec, vec],
        out_specs=(row(P_AQ), row(G_WIDTH),
                   pl.BlockSpec((1, A_HEADS * A_DH, ROW_TILE), lambda b, t: (b, 0, t)),
                   pl.BlockSpec((1, A_KV_HEADS, 1, ROW_TILE, A_DH), lambda b, t: (b, 0, t, 0, 0)),
                   pl.BlockSpec((1, A_KV_HEADS, 1, VT_ROWS, ROW_TILE), lambda b, t: (b, 0, t, 0, 0))),
        compiler_params=_params(("parallel", "arbitrary")),
        name="in_proj",
    )(*s_args, mods, nw.reshape(1, D), wp, cos, sin, qw, kw)


C_ROWS = M_DV + 16
VEC_ROWS = 24
INTRA_CHUNKS = 6

def _scan_lanes(x, reverse):
    lane = lax.broadcasted_iota(jnp.int32, x.shape, 1)
    k = 1
    while k < M_CHUNK:
        if reverse:
            x = x + jnp.where(lane < M_CHUNK - k, pltpu.roll(x, M_CHUNK - k, axis=1), 0.0)
        else:
            x = x + jnp.where(lane >= k, pltpu.roll(x, k, axis=1), 0.0)
        k *= 2
    return x


def _mlstm_intra_kernel(qk_ref, v_ref, g_ref, bias_ref, numf_ref, numb_ref, vecf_ref, vecb_ref, clf_ref, clb_ref):
    row = lax.broadcasted_iota(jnp.int32, (M_CHUNK, M_CHUNK), 0)
    col = lax.broadcasted_iota(jnp.int32, (M_CHUNK, M_CHUNK), 1)
    gate_row = lax.broadcasted_iota(jnp.int32, (16, M_CHUNK), 0)
    tail_row = lax.broadcasted_iota(jnp.int32, (C_ROWS - M_DV, M_CHUNK), 0)
    outs = ((numf_ref, vecf_ref, clf_ref, row <= col, M_CHUNK - 1),
            (numb_ref, vecb_ref, clb_ref, row >= col, 0))
    chunks = range(INTRA_CHUNKS)
    toks = [pl.ds(c * M_CHUNK, M_CHUNK) for c in chunks]

    g_row, scans, gap_cols = [], [], []
    for c in chunks:
        g = g_ref[0, toks[c], :] + bias_ref[...]
        gr = g.T[0:16, :]
        lf_row = jnp.minimum(gr, 0.0) - jnp.log1p(jnp.exp(-jnp.abs(gr)))
        sc = (_scan_lanes(lf_row, False), _scan_lanes(lf_row, True))
        gaps = gr - pltpu.roll(jnp.where(gate_row < 8, sc[0], sc[1]), 12, axis=0)
        g_row.append(gr)
        scans.append(sc)
        gap_cols.append(jnp.concatenate([gaps, jnp.zeros((M_CHUNK - 16, M_CHUNK), F32)], axis=0).T)
        for vec_ref in (vecf_ref, vecb_ref):
            vec_ref[0, c, 12 + 2 * M_HEADS:VEC_ROWS, :] = jnp.zeros((VEC_ROWS - 12 - 2 * M_HEADS, LANES), F32)

    heads = [(c, h) for c in chunks for h in range(M_HEADS)]
    ks, vts, s_raws = {}, {}, {}
    for c, h in heads:
        qk = qk_ref[0, toks[c], h * LANES:(h + 1) * LANES]
        q, ks[c, h] = qk[:, :M_DK], qk[:, M_DK:]
        vts[c, h] = v_ref[0, toks[c], h * M_DV:(h + 1) * M_DV].astype(F32).T
        s_raws[c, h] = lax.dot_general(ks[c, h], q, (((1,), (1,)), ((), ())), preferred_element_type=F32)

    units = [(c, h, d) for c, h in heads for d in range(2)]
    s_w, vws = {}, {}
    for c, h, d in units:
        _, vec_ref, _, allowed, last = outs[d]
        b_r = scans[c][d][8 * d + 4 + h:8 * d + 5 + h, :]
        i_r = g_row[c][8 * d + h:8 * d + h + 1, :]
        j = 8 * d + h
        b_end = b_r[:, last:last + 1]
        d_log = jnp.where(allowed, b_r + gap_cols[c][:, j:j + 1], -jnp.inf)
        m_intra = jnp.max(d_log, axis=0, keepdims=True)
        s = s_raws[c, h] * jnp.exp(d_log - m_intra)
        s_w[c, h, d] = s.astype(BF16)
        vec_ref[0, c, 3 * h:3 * h + 1, :] = jnp.sum(s, axis=0, keepdims=True)
        vec_ref[0, c, 3 * h + 1:3 * h + 2, :] = m_intra
        vec_ref[0, c, 3 * h + 2:3 * h + 3, :] = b_r
        w_log = b_end - b_r + i_r
        m_loc = jnp.max(w_log, axis=-1, keepdims=True)
        w_row = jnp.exp(w_log - m_loc)
        vws[c, h, d] = jnp.concatenate([vts[c, h] * w_row, jnp.where(tail_row == 0, w_row, 0.0)],
                                       axis=0).astype(BF16)
        vec_ref[0, c, 12 + 2 * h:13 + 2 * h, :] = jnp.broadcast_to(m_loc, (1, LANES))
        vec_ref[0, c, 13 + 2 * h:14 + 2 * h, :] = jnp.broadcast_to(b_end, (1, LANES))

    for c, h, d in units:
        num_ref, _, cl_ref, _, _ = outs[d]
        num_ref[0, c, h] = jnp.dot(vts[c, h].astype(BF16), s_w[c, h, d], preferred_element_type=F32)
        cl_ref[0, c, h] = jnp.dot(vws[c, h, d], ks[c, h], preferred_element_type=F32)


def _mlstm_scan_kernel(*refs, n_batch):
    ins, (hf_ref, hb_ref, cn_ref, m_ref) = refs[:8], refs[8:]

    @pl.when(pl.program_id(0) == 0)
    def _():
        cn_ref[...] = jnp.zeros_like(cn_ref)
        m_ref[...] = jnp.zeros_like(m_ref)

    for d, h_ref in enumerate((hf_ref, hb_ref)):
        qk_ref, num_ref, vec_ref, cl_ref = ins[4 * d:4 * d + 4]
        for b in range(n_batch):
            for h in range(M_HEADS):
                idx = (d * n_batch + b) * M_HEADS + h
                q = qk_ref[b, :, h * LANES:h * LANES + M_DK]
                row = lambda r: vec_ref[b, 0, r:r + 1, :]
                den_i, m_i, b_r = row(3 * h), row(3 * h + 1), row(3 * h + 2)
                m_loc, b_end = row(12 + 2 * h), row(13 + 2 * h)
                m_prev = m_ref[idx]
                cn = cn_ref[idx]

                inter = b_r + m_prev
                m_t = jnp.maximum(inter, m_i)
                a = jnp.exp(inter - m_t)
                e = jnp.exp(m_i - m_t)
                cq = lax.dot_general(cn.astype(BF16), q, (((1,), (1,)), ((), ())),
                                     preferred_element_type=F32)
                den = e * den_i + a * cq[M_DV:M_DV + 1, :]
                scale = 1.0 / jnp.maximum(jnp.abs(den), jnp.exp(-m_t))
                ht = (e * num_ref[b, 0, h] + a * cq[0:M_DV, :]) * scale
                h_ref[b, :, h * M_DV:(h + 1) * M_DV] = ht.T.astype(BF16)

                m_new = jnp.maximum(b_end + m_prev, m_loc)
                a_s = jnp.exp(b_end + m_prev - m_new)
                s_s = jnp.exp(m_loc - m_new)
                cn_ref[idx] = a_s[:, :M_DK] * cn + s_s[:, :M_DK] * cl_ref[b, 0, h]
                m_ref[idx] = m_new


def _mlstm(p, g, gate_b, ctx_chunks):
    B, L, _ = p.shape
    nc = L // M_CHUNK
    width = M_HEADS * M_DV
    bias = jnp.zeros((1, G_WIDTH), F32).at[0, :16].set(gate_b)
    assert nc % INTRA_CHUNKS == 0
    tok = lambda w, cb=0: pl.BlockSpec((1, INTRA_CHUNKS * M_CHUNK, w), lambda b, c: (b, c, cb))
    num_shape, vec_shape, cl_shape = (M_HEADS, M_DV, M_CHUNK), (VEC_ROWS, LANES), (M_HEADS, C_ROWS, M_DK)
    per_chunk = lambda s: pl.BlockSpec((1, INTRA_CHUNKS) + s, lambda b, c: (b, c) + (0,) * len(s))
    f32 = lambda *s: jax.ShapeDtypeStruct(s, F32)
    numf, numb, vecf, vecb, clf, clb = pl.pallas_call(
        _mlstm_intra_kernel,
        out_shape=(f32(B, nc, *num_shape),) * 2 + (f32(B, nc, *vec_shape),) * 2 + (f32(B, nc, *cl_shape),) * 2,
        grid=(B, nc // INTRA_CHUNKS),
        in_specs=[tok(width, P_QK // width), tok(width, P_MV // width), tok(G_WIDTH),
                  pl.BlockSpec((1, G_WIDTH), lambda b, c: (0, 0))],
        out_specs=(per_chunk(num_shape),) * 2 + (per_chunk(vec_shape),) * 2 + (per_chunk(cl_shape),) * 2,
        compiler_params=_params(("parallel", "parallel")),
        name="mlstm_intra",
    )(p, p, g, bias)

    fwd = lambda j: j
    bwd = lambda j: jnp.where(j < ctx_chunks, ctx_chunks - 1 - j, nc - 1 + ctx_chunks - j)
    stok = lambda cm, w, cb=0: pl.BlockSpec((B, M_CHUNK, w), lambda j: (0, cm(j), cb))
    schunk = lambda cm, s: pl.BlockSpec((B, 1) + s, lambda j: (0, cm(j)) + (0,) * len(s))
    side = lambda cm: [stok(cm, width, P_QK // width), schunk(cm, num_shape), schunk(cm, vec_shape),
                       schunk(cm, cl_shape)]
    chains = 2 * B * M_HEADS
    return pl.pallas_call(
        functools.partial(_mlstm_scan_kernel, n_batch=B),
        out_shape=(jax.ShapeDtypeStruct((B, L, width), BF16),) * 2,
        grid=(nc,),
        in_specs=side(fwd) + side(bwd),
        out_specs=(stok(fwd, width), stok(bwd, width)),
        scratch_shapes=[pltpu.VMEM((chains, C_ROWS, M_DK), F32),
                        pltpu.VMEM((chains, 1, LANES), F32)],
        compiler_params=_params(("arbitrary",)),
        name="mlstm_scan",
    )(p, numf, vecf, clf, p, numb, vecb, clb)


def _head_norm_rope(xs, ws, cos, sin, bd):
    sqs = [x * x for x in xs]
    his = [sq.astype(BF16) for sq in sqs]
    los = [(sq - hi.astype(F32)).astype(BF16) for sq, hi in zip(sqs, his)]
    mss = [jnp.dot(hi, bd, preferred_element_type=F32) + jnp.dot(lo, bd, preferred_element_type=F32)
           for hi, lo in zip(his, los)]
    ys = [x * lax.rsqrt(ms + EPS) * w for x, ms, w in zip(xs, mss, ws)]
    lane = lax.broadcasted_iota(jnp.int32, ys[0].shape, 1)
    first_half = lane % A_DH < A_DH // 2
    partners = [jnp.where(first_half, pltpu.roll(y, LANES - A_DH // 2, axis=1), pltpu.roll(y, A_DH // 2, axis=1))
                for y in ys]
    return [y * cos + partner * sin for y, partner in zip(ys, partners)]


Q_SCALE = A_DH ** -0.5 * math.log2(math.e)


def _attn_prep_tile(q, kv, cos, sin, qw, kw, qt_ref, k_ref, vt_ref):
    r = lax.broadcasted_iota(jnp.int32, (LANES, LANES), 0) // A_DH
    c = lax.broadcasted_iota(jnp.int32, (LANES, LANES), 1) // A_DH
    bd = jnp.where(r == c, 1.0 / A_DH, 0.0).astype(BF16)
    n_pairs = A_HEADS // 2
    tiles = [q[:, pair * LANES:(pair + 1) * LANES] for pair in range(n_pairs)] + [kv[:, :LANES]]
    rotated = _head_norm_rope(tiles, [qw] * n_pairs + [kw], cos, sin, bd)
    for pair in range(n_pairs):
        qt_ref[0, pair * LANES:(pair + 1) * LANES, :] = (rotated[pair] * Q_SCALE).T.astype(BF16)
    k = rotated[n_pairs].astype(BF16)
    for kvh in range(A_KV_HEADS):
        k_ref[0, kvh, 0] = k[:, kvh * A_DH:(kvh + 1) * A_DH]
    vt = kv[:, LANES:].T.astype(BF16)
    ones = jnp.ones((VT_ROWS - A_DH, vt.shape[1]), BF16)
    for kvh in range(A_KV_HEADS):
        vt_ref[0, kvh, 0] = jnp.concatenate([vt[kvh * A_DH:(kvh + 1) * A_DH, :], ones], axis=0)


ATT_SUB = 256
ATT_PIECE = 128
VT_ROWS = A_DH + 16


def _attn_kernel(qt_ref, k_ref, vt_ref, o_ref, sa_ref, sb_ref, ma_ref, mb_ref, acc_ref,
                 *, blocks, tq, ctx_tiles, q_tile0):
    q_of = lambda g: qt_ref[0, g * A_DH:(g + 1) * A_DH, :]
    lanes = lambda g: slice(g * tq, (g + 1) * tq)
    head, mid, n_mid, tail = blocks

    def step(nxt, cur, ms):
        out = []
        for g in range(A_GROUP):
            if cur is not None:
                c0, c_subs, cs_ref, cm_ref = cur
                m_new = jnp.maximum(ms[g], cm_ref[:, lanes(g)])
                alpha = jnp.exp2(ms[g] - m_new)
            best, pv = None, None
            for r in range(max(nxt[1] if nxt else 0, cur[1] if cur else 0)):
                parts = []
                for piece in range(ATT_SUB // ATT_PIECE):
                    rows = pl.ds(piece * ATT_PIECE, ATT_PIECE)
                    buf_rows = pl.ds(r * ATT_SUB + piece * ATT_PIECE, ATT_PIECE)
                    if nxt is not None and r < nxt[1]:
                        s = jnp.dot(k_ref[0, 0, nxt[0] + r, rows, :], q_of(g), preferred_element_type=F32)
                        nxt[2][g, buf_rows, :] = s
                        top = jnp.max(s, axis=0, keepdims=True)
                        best = top if best is None else jnp.maximum(best, top)
                    if cur is not None and r < c_subs:
                        parts.append(jnp.exp2(cs_ref[g, buf_rows, :] - m_new).astype(BF16))
                if cur is not None and r < c_subs:
                    d = jnp.dot(vt_ref[0, 0, c0 + r], jnp.concatenate(parts, axis=0), preferred_element_type=F32)
                    pv = d if pv is None else pv + d
            if nxt is not None:
                nxt[3][:, lanes(g)] = best
            if cur is not None:
                acc_ref[:, lanes(g)] = alpha * acc_ref[:, lanes(g)] + pv
                out.append(m_new)
            else:
                out.append(ms[g])
        return tuple(out)

    def finish():
        o = acc_ref[0:A_DH, :] / acc_ref[A_DH:A_DH + 1, :]
        o = jnp.concatenate([o[:, lanes(g)] for g in range(A_GROUP)], axis=0)
        o_ref[0] = o.T.astype(BF16)

    acc_ref[...] = jnp.zeros_like(acc_ref)
    init = (jnp.full((1, tq), -jnp.inf, F32),) * A_GROUP
    is_ctx = pl.program_id(2) + q_tile0 < ctx_tiles
    buf_a, buf_b = (sa_ref, ma_ref), (sb_ref, mb_ref)

    mid_block = lambda i, buf: (head + mid * i, mid, *buf)

    @pl.when(is_ctx)
    def _():
        step((0, head, *buf_a), None, init)
        step(None, (0, head, *buf_a), init)
        finish()

    @pl.when(jnp.logical_not(is_ctx))
    def _():
        step((0, head, *buf_a), None, init)
        ms = step(mid_block(0, buf_b), (0, head, *buf_a), init)

        def pair(j, ms):
            ms = step(mid_block(2 * j + 1, buf_a), mid_block(2 * j, buf_b), ms)
            return step(mid_block(2 * j + 2, buf_b), mid_block(2 * j + 1, buf_a), ms)

        ms = lax.fori_loop(0, n_mid // 2 - 1, pair, ms)
        last = n_mid - 1
        ms = step(mid_block(last, buf_a), mid_block(last - 1, buf_b), ms)
        tail_block = (head + mid * n_mid, tail, *buf_b)
        ms = step(tail_block, mid_block(last, buf_a), ms)
        step(None, tail_block, ms)
        finish()


def _attention(qt, k, vt, *, q_tile0, n_ctx, tq):
    B, _, L = qt.shape
    n_sub = k.shape[2]
    head, mid = n_ctx // ATT_SUB, 3
    n_mid = (n_sub - head - 1) // mid // 2 * 2
    tail = n_sub - head - mid * n_mid
    assert k.shape[3] == ATT_SUB and n_ctx == tq == ATT_SUB and n_mid >= 2 and 1 <= tail <= mid
    width = A_GROUP * A_DH
    n = A_GROUP * tq
    s_buf, m_buf = pltpu.VMEM((A_GROUP, mid * ATT_SUB, tq), F32), pltpu.VMEM((1, n), F32)
    return pl.pallas_call(
        functools.partial(_attn_kernel, blocks=(head, mid, n_mid, tail), tq=tq, ctx_tiles=n_ctx // tq,
                          q_tile0=q_tile0),
        out_shape=jax.ShapeDtypeStruct((B, L - q_tile0 * tq, A_HEADS * A_DH), BF16),
        grid=(B, A_KV_HEADS, L // tq - q_tile0),
        in_specs=[pl.BlockSpec((1, width, tq), lambda b, kv, t: (b, kv, t + q_tile0)),
                  pl.BlockSpec((1, 1, n_sub, ATT_SUB, A_DH), lambda b, kv, t: (b, kv, 0, 0, 0)),
                  pl.BlockSpec((1, 1, n_sub, VT_ROWS, ATT_SUB), lambda b, kv, t: (b, kv, 0, 0, 0))],
        out_specs=pl.BlockSpec((1, tq, width), lambda b, kv, t: (b, t, kv)),
        scratch_shapes=[s_buf, s_buf, m_buf, m_buf, pltpu.VMEM((VT_ROWS, n), F32)],
        compiler_params=_params(("parallel", "parallel", "arbitrary")),
        name="attention",
    )(qt, k, vt)


def _mixer_out_kernel(*refs, with_router, n_stream, row_off):
    hf_ref, hb_ref, mo_ref, a_ref = refs[:4]
    mod_ref, mnw_ref, n2w_ref, wout_ref = refs[4 + n_stream:8 + n_stream]
    rest = refs[8 + n_stream:]
    hs = hf_ref[0].astype(F32) + hb_ref[0].astype(F32)
    hn = jnp.concatenate([_rms_rows(hs[:, h * M_DV:(h + 1) * M_DV]) for h in range(M_HEADS)], axis=1)
    m = hn * mnw_ref[...] * _sigmoid(mo_ref[0].astype(F32))
    y_in = jnp.concatenate([m.astype(BF16), a_ref[0]], axis=1)
    mod = mod_ref[0, 0]
    x1 = _stream_tile(refs[4:4 + n_stream], row_off) + mod[2:3] * jnp.dot(y_in, wout_ref[...],
                                                                          preferred_element_type=F32)
    h2 = _rms_rows(x1) * n2w_ref[...] * (1.0 + mod[4:5]) + mod[3:4]
    if not with_router:
        wg_ref, wu_ref, wd_ref, o_ref = rest
        hb16 = h2.astype(BF16)
        g = jnp.dot(hb16, wg_ref[...], preferred_element_type=F32)
        u = jnp.dot(hb16, wu_ref[...], preferred_element_type=F32)
        act = (g * _sigmoid(g) * u).astype(BF16)
        o_ref[0] = x1 + mod[5:6] * jnp.dot(act, wd_ref[...], preferred_element_type=F32)
        return
    router_ref, x1_ref, h2_ref, ids_ref, gates_ref = rest
    x1_ref[0] = x1
    h2_ref[0] = h2
    h_hi = h2.astype(BF16)
    h_lo = (h2 - h_hi.astype(F32)).astype(BF16)
    logits = (jnp.dot(h_hi, router_ref[0], preferred_element_type=F32)
              + jnp.dot(h_lo, router_ref[0], preferred_element_type=F32)
              + jnp.dot(h_hi, router_ref[1], preferred_element_type=F32))
    lane = lax.broadcasted_iota(jnp.int32, logits.shape, 1)
    logits = jnp.where(lane < N_EXPERTS, logits, -jnp.inf)
    m1 = jnp.max(logits, axis=-1, keepdims=True)
    i1 = jnp.min(jnp.where(logits == m1, lane, LANES), axis=-1, keepdims=True)
    rest = jnp.where(lane == i1, -jnp.inf, logits)
    m2 = jnp.max(rest, axis=-1, keepdims=True)
    i2 = jnp.min(jnp.where(rest == m2, lane, LANES), axis=-1, keepdims=True)
    e2 = jnp.exp(m2 - m1)
    g1 = 1.0 / (1.0 + e2)
    ids_ref[0] = jnp.where(lane == 0, i1, jnp.where(lane == 1, i2, -1))
    gates_ref[0] = jnp.where(lane == 0, g1, jnp.where(lane == 1, e2 * g1, 0.0))


def _mixer_out(hf, hb, p, a, stream, mods, mnw, n2w, wout, *, ffn=None, router=None, row_off=0):
    B, L = p.shape[:2]
    D = D_MODEL
    nt = L // ROW_TILE - row_off
    rin = lambda w, cb=0: pl.BlockSpec((1, ROW_TILE, w), lambda b, t: (b, t + row_off, cb))
    rout = lambda w: pl.BlockSpec((1, ROW_TILE, w), lambda b, t: (b, t, 0))
    const = lambda arr: pl.BlockSpec(arr.shape, lambda b, t: (0,) * arr.ndim, pipeline_mode=pl.Buffered(1))
    mw = M_HEADS * M_DV
    a_off = row_off - (L - a.shape[1]) // ROW_TILE
    a_spec = pl.BlockSpec((1, ROW_TILE, A_HEADS * A_DH), lambda b, t: (b, t + a_off, 0))
    s_specs, s_args = _stream_specs(stream, row_off)
    in_specs = [rin(mw), rin(mw), rin(mw, P_MO // mw), a_spec, *s_specs, _mod_spec(row_off),
                pl.BlockSpec((1, mw), lambda b, t: (0, 0)), pl.BlockSpec((1, D), lambda b, t: (0, 0)),
                const(wout)]
    args = [hf, hb, p, a, *s_args, mods, mnw.reshape(1, mw), n2w.reshape(1, D), wout]
    rows = nt * ROW_TILE
    if router is None:
        in_specs += [const(w) for w in ffn]
        args += list(ffn)
        out_shape, out_specs = jax.ShapeDtypeStruct((B, rows, D), F32), rout(D)
    else:
        in_specs.append(const(router))
        args.append(router)
        out_shape = (jax.ShapeDtypeStruct((B, rows, D), F32), jax.ShapeDtypeStruct((B, rows, D), F32),
                     jax.ShapeDtypeStruct((B, rows, LANES), jnp.int32), jax.ShapeDtypeStruct((B, rows, LANES), F32))
        out_specs = (rout(D), rout(D), rout(LANES), rout(LANES))
    return pl.pallas_call(
        functools.partial(_mixer_out_kernel, with_router=router is not None, n_stream=len(s_args), row_off=row_off),
        out_shape=out_shape,
        grid=(B, nt),
        in_specs=in_specs,
        out_specs=out_specs,
        compiler_params=_params(("parallel", "arbitrary")),
        name="mixer_out",
    )(*args)


MOE_TM = 512
MOE_FF = 1792
MOE_TD = 256
RANK_TILE = 1024
POS_TILE = 2048
ISSUE_UNROLL = 8


def _moe_rank_kernel(ids_ref, rank_ref, cnt_ref, carry_ref, before_ref):
    @pl.when(pl.program_id(0) == 0)
    def _():
        carry_ref[...] = jnp.zeros_like(carry_ref)
        r = lax.broadcasted_iota(jnp.int32, (RANK_TILE, RANK_TILE), 0)
        c = lax.broadcasted_iota(jnp.int32, (RANK_TILE, RANK_TILE), 1)
        before_ref[...] = jnp.where(c < r, 1.0, 0.0).astype(BF16)

    ids = ids_ref[...]
    lane = lax.broadcasted_iota(jnp.int32, ids.shape, 1)
    onehot = jnp.where(jnp.logical_or(lane == ids[:, 0:1], lane == ids[:, 1:2]), 1.0, 0.0)
    rank_ref[...] = jnp.dot(before_ref[...], onehot.astype(BF16), preferred_element_type=F32) + carry_ref[...]
    carry_ref[...] += jnp.sum(onehot, axis=0, keepdims=True)
    cnt_ref[...] = carry_ref[...]


def _moe_rank(ids):
    n = ids.shape[0]
    return pl.pallas_call(
        _moe_rank_kernel,
        out_shape=(jax.ShapeDtypeStruct((n, LANES), F32), jax.ShapeDtypeStruct((1, LANES), F32)),
        grid=(n // RANK_TILE,),
        in_specs=[pl.BlockSpec((RANK_TILE, LANES), lambda t: (t, 0))],
        out_specs=(pl.BlockSpec((RANK_TILE, LANES), lambda t: (t, 0)), pl.BlockSpec((1, LANES), lambda t: (0, 0))),
        scratch_shapes=[pltpu.VMEM((1, LANES), F32), pltpu.VMEM((RANK_TILE, RANK_TILE), BF16)],
        compiler_params=_params(("arbitrary",)),
        name="moe_rank",
    )(ids)


def _moe_pos_kernel(ids_ref, rank_ref, start_ref, pos_ref):
    ids = ids_ref[...]
    lane = lax.broadcasted_iota(jnp.int32, ids.shape, 1)
    tgt = start_ref[...] + rank_ref[...]
    p0 = jnp.sum(jnp.where(lane == ids[:, 0:1], tgt, 0.0), axis=-1, keepdims=True)
    p1 = jnp.sum(jnp.where(lane == ids[:, 1:2], tgt, 0.0), axis=-1, keepdims=True)
    pos_ref[...] = jnp.where(lane == 0, p0, jnp.where(lane == 1, p1, 0.0)).astype(jnp.int32)


def _moe_pos(ids, rank, start_row):
    n = ids.shape[0]
    blk = pl.BlockSpec((POS_TILE, LANES), lambda t: (t, 0))
    return pl.pallas_call(
        _moe_pos_kernel,
        out_shape=jax.ShapeDtypeStruct((n, LANES), jnp.int32),
        grid=(n // POS_TILE,),
        in_specs=[blk, blk, pl.BlockSpec((1, LANES), lambda t: (0, 0))],
        out_specs=blk,
        compiler_params=_params(("parallel",)),
        name="moe_pos",
    )(ids, rank, start_row)


def _row_copy(src, src_row, dst, dst_row, sem):
    return pltpu.make_async_copy(src.at[pl.ds(src_row, 1), :], dst.at[pl.ds(dst_row, 1), :], sem)


def _moe_dispatch_kernel(pad_ref, pos_ref, h_ref, xs_ref, zero_ref, sem):
    @pl.when(pl.program_id(0) == 0)
    def _():
        zero_ref[...] = jnp.zeros_like(zero_ref)
        fills = [pltpu.make_async_copy(
            zero_ref, xs_ref.at[pl.ds(pl.multiple_of(pad_ref[e], SUBLANES), MOE_TM + SUBLANES), :], sem)
            for e in range(N_EXPERTS)]
        for cp in fills:
            cp.start()
        for cp in fills:
            cp.wait()

        def fill_tile(j, carry):
            cp = pltpu.make_async_copy(zero_ref.at[pl.ds(0, MOE_TM), :],
                                       xs_ref.at[pl.ds(pl.multiple_of(j * MOE_TM, MOE_TM), MOE_TM), :], sem)
            cp.start()
            cp.wait()
            return carry

        lax.fori_loop(pad_ref[N_EXPERTS], xs_ref.shape[0] // MOE_TM, fill_tile, 0)

    def issue(r, carry):
        for k in range(2):
            _row_copy(h_ref, r, xs_ref, pos_ref[0, 0, 2 * r + k], sem).start(priority=k)
        return carry

    lax.fori_loop(0, MOE_TD, issue, 0, unroll=ISSUE_UNROLL)
    for k in range(2):
        pltpu.make_async_copy(h_ref, xs_ref.at[pl.ds(0, MOE_TD), :], sem).wait()


def _moe_dispatch(h, pos, fill_meta, ns):
    n, d = h.shape
    return pl.pallas_call(
        _moe_dispatch_kernel,
        out_shape=jax.ShapeDtypeStruct((ns, d), F32),
        grid_spec=pltpu.PrefetchScalarGridSpec(
            num_scalar_prefetch=1,
            grid=(n // MOE_TD,),
            in_specs=[pl.BlockSpec((1, 1, 2 * MOE_TD), lambda t, pad: (t, 0, 0), memory_space=pltpu.SMEM),
                      pl.BlockSpec((MOE_TD, d), lambda t, pad: (t, 0))],
            out_specs=pl.BlockSpec(memory_space=pl.ANY),
            scratch_shapes=[pltpu.VMEM((MOE_TM + SUBLANES, d), F32), pltpu.SemaphoreType.DMA(())]),
        compiler_params=_params(("arbitrary",)),
        name="moe_dispatch",
    )(fill_meta, pos, h)


def _moe_group_kernel(te_ref, nv_ref, xs_ref, wg_ref, wu_ref, wd_ref, ys_ref, acc_ref):
    i, f = pl.program_id(0), pl.program_id(1)
    last = pl.num_programs(1) - 1
    valid = i < nv_ref[0]

    @pl.when(valid)
    def _():
        @pl.when(f == 0)
        def _():
            acc_ref[...] = jnp.zeros_like(acc_ref)

        x = xs_ref[...].astype(BF16)
        g = jnp.dot(x, wg_ref[0], preferred_element_type=F32)
        u = jnp.dot(x, wu_ref[0], preferred_element_type=F32)
        act = (g * _sigmoid(g) * u).astype(BF16)
        acc_ref[...] += jnp.dot(act, wd_ref[0], preferred_element_type=F32)

        @pl.when(f == last)
        def _():
            ys_ref[...] = acc_ref[...]

    @pl.when(jnp.logical_and(jnp.logical_not(valid), f == last))
    def _():
        ys_ref[...] = jnp.zeros_like(ys_ref)


def _moe_group(xs, tile_expert, n_valid, wg, wu, wd):
    ns, d = xs.shape
    n_tiles = ns // MOE_TM - 1
    ff = wg.shape[2]
    live = lambda i, nv: i < nv[0]
    nf = ff // MOE_FF
    step = lambda i, f, nv: jnp.where(live(i, nv), f, nf - 1)
    return pl.pallas_call(
        _moe_group_kernel,
        out_shape=jax.ShapeDtypeStruct((n_tiles * MOE_TM, d), F32),
        grid_spec=pltpu.PrefetchScalarGridSpec(
            num_scalar_prefetch=2,
            grid=(n_tiles, nf),
            in_specs=[pl.BlockSpec((MOE_TM, d), lambda i, f, te, nv: (jnp.where(live(i, nv), i, 0), 0)),
                      pl.BlockSpec((1, d, MOE_FF), lambda i, f, te, nv: (te[i], 0, step(i, f, nv))),
                      pl.BlockSpec((1, d, MOE_FF), lambda i, f, te, nv: (te[i], 0, step(i, f, nv))),
                      pl.BlockSpec((1, MOE_FF, d), lambda i, f, te, nv: (te[i], step(i, f, nv), 0))],
            out_specs=pl.BlockSpec((MOE_TM, d), lambda i, f, te, nv: (i, 0)),
            scratch_shapes=[pltpu.VMEM((MOE_TM, d), F32)]),
        compiler_params=_params(("arbitrary", "arbitrary")),
        name="moe_group",
    )(tile_expert, n_valid, xs, wg, wu, wd)


def _moe_combine_kernel(pos_ref, x_ref, gates_ref, mod_ref, fw_ref, ys_ref, o_ref, ybuf, sem):
    def issue(r, carry):
        for k in range(2):
            _row_copy(ys_ref, pos_ref[0, 0, 2 * r + k], ybuf.at[k], r, sem).start(priority=k)
        return carry

    lax.fori_loop(0, MOE_TD, issue, 0, unroll=ISSUE_UNROLL)
    for k in range(2):
        pltpu.make_async_copy(ys_ref.at[pl.ds(0, MOE_TD), :], ybuf.at[k], sem).wait()
    gates = gates_ref[...]
    y = gates[:, 0:1] * ybuf[0] + gates[:, 1:2] * ybuf[1]
    x2 = x_ref[...] + mod_ref[0, 0][5:6] * y
    o_ref[...] = _rms_rows(x2) * fw_ref[...]


def _moe_combine(pos, x1, gates, mods, fw, ys, tokens_per_sample):
    n, d = x1.shape
    per = tokens_per_sample // MOE_TD
    return pl.pallas_call(
        _moe_combine_kernel,
        out_shape=jax.ShapeDtypeStruct((n, d), F32),
        grid=(n // MOE_TD,),
        in_specs=[pl.BlockSpec((1, 1, 2 * MOE_TD), lambda t: (t, 0, 0), memory_space=pltpu.SMEM),
                  pl.BlockSpec((MOE_TD, d), lambda t: (t, 0)),
                  pl.BlockSpec((MOE_TD, LANES), lambda t: (t, 0)),
                  pl.BlockSpec((1, 1, 6, d), lambda t: (t // per, 1, 0, 0)),
                  pl.BlockSpec((1, d), lambda t: (0, 0)),
                  pl.BlockSpec(memory_space=pl.ANY)],
        out_specs=pl.BlockSpec((MOE_TD, d), lambda t: (t, 0)),
        scratch_shapes=[pltpu.VMEM((2, MOE_TD, d), F32), pltpu.SemaphoreType.DMA(())],
        compiler_params=_params(("arbitrary",)),
        name="moe_combine",
    )(pos, x1, gates, mods, fw.reshape(1, d), ys)


def _moe(h2, ids, gates, x1, mods, wg, wu, wd, fw):
    B, T, D = x1.shape
    n = B * T
    ids, gates = ids.reshape(n, LANES), gates.reshape(n, LANES)
    rank, cnt = _moe_rank(ids)
    cnt = cnt[0, :N_EXPERTS].astype(jnp.int32)
    padded = (cnt + MOE_TM - 1) // MOE_TM * MOE_TM
    end = jnp.cumsum(padded)
    start = end - padded
    n_tiles = 2 * n // MOE_TM + N_EXPERTS
    tile_expert = jnp.minimum(jnp.sum(jnp.arange(n_tiles)[:, None] >= (end // MOE_TM)[None, :], axis=1),
                              N_EXPERTS - 1).astype(jnp.int32)
    n_valid = (end[-1:] // MOE_TM).astype(jnp.int32)
    start_row = jnp.zeros((1, LANES), F32).at[0, :N_EXPERTS].set(start.astype(F32))
    pos = _moe_pos(ids, rank, start_row)
    pos = pos[:, :2].reshape(n // MOE_TD, 1, 2 * MOE_TD)
    fill_meta = jnp.concatenate([(start + cnt) // SUBLANES * SUBLANES, n_valid]).astype(jnp.int32)
    xs = _moe_dispatch(h2.reshape(n, D), pos, fill_meta, (n_tiles + 1) * MOE_TM)
    ys = _moe_group(xs, tile_expert, n_valid, wg, wu, wd)
    return _moe_combine(pos, x1.reshape(n, D), gates, mods, fw, ys, T).reshape(B, T, D)


_ROT_PERM = np.concatenate([np.arange(0, A_DH, 2), np.arange(1, A_DH, 2)])


def _prep_w_in(w):
    o = np.cumsum([0, M_HEADS * M_DK, M_HEADS * M_DK, M_HEADS * M_DV, M_HEADS * M_DV, 4 * M_HEADS,
                   A_HEADS * A_DH, A_KV_HEADS * A_DH, A_KV_HEADS * A_DH])
    mq, mk, mv, mo, mg, aq, ak, av = [w[:, o[i]:o[i + 1]] for i in range(8)]
    qk = jnp.concatenate([jnp.concatenate([mq[:, h * M_DK:(h + 1) * M_DK] * (M_DK ** -0.5),
                                           mk[:, h * M_DK:(h + 1) * M_DK]], axis=1) for h in range(M_HEADS)], axis=1)
    perm_q = np.concatenate([h * A_DH + _ROT_PERM for h in range(A_HEADS)])
    perm_k = np.concatenate([h * A_DH + _ROT_PERM for h in range(A_KV_HEADS)])
    pad = jnp.zeros((w.shape[0], G_WIDTH - 4 * M_HEADS), w.dtype)
    return jnp.concatenate([qk, mv, mo, aq[:, perm_q], ak[:, perm_k], av, mg, pad], axis=1).astype(BF16)


def _rope_tables(n_tok, n_ctx):
    rows = n_tok // GRID_W
    row = jnp.broadcast_to(jnp.arange(rows, dtype=F32)[:, None], (rows, GRID_W)).reshape(n_tok)
    col = jnp.broadcast_to(jnp.arange(GRID_W, dtype=F32)[None, :], (rows, GRID_W)).reshape(n_tok)
    n_freq = A_DH // 4
    inv_freq = ROPE_THETA ** (-jnp.arange(n_freq, dtype=F32) / n_freq)
    ang = jnp.concatenate([row[:, None] * inv_freq, col[:, None] * inv_freq], axis=-1)
    cos, sin = jnp.cos(ang), jnp.sin(ang)
    cos = jnp.concatenate([jnp.ones((n_ctx, A_DH // 2), F32), cos], axis=0)
    sin = jnp.concatenate([jnp.zeros((n_ctx, A_DH // 2), F32), sin], axis=0)
    return jnp.tile(cos, (1, 4)), jnp.tile(jnp.concatenate([-sin, sin], axis=1), (1, 2))


def kernel(x, c, ctx, c_ctx, ada_w, ada_b, norm1_w, norm2_w, w_in, mlstm_gate_b, mlstm_norm_w, q_norm_w, k_norm_w,
           w_out, ffn_w_gate, ffn_w_up, ffn_w_down, moe_router, moe_w_gate, moe_w_up, moe_w_down, final_norm_w):
    B, T, D = x.shape
    n_ctx = ctx.shape[1]
    L = n_ctx + T
    depth = w_in.shape[0]
    assert D == D_MODEL and n_ctx == ROW_TILE and T % RANK_TILE == 0 and depth == 2
    ctx_tiles = n_ctx // ROW_TILE
    tq = ROW_TILE

    xa = (ctx, x)
    cvec =jnp.concatenate([c, c_ctx[None], jnp.zeros((8 - B - 1, D), F32)], axis=0)
    cos, sin = _rope_tables(T, n_ctx)
    out = None
    for i in range(depth):
        last = i == depth - 1
        modraw = _ada(cvec, ada_w[i], ada_b[i])
        mods = jnp.stack([jnp.broadcast_to(modraw[B].reshape(1, 6, D), (B, 6, D)),
                          modraw[:B].reshape(B, 6, D)], axis=1)
        qw = jnp.tile(q_norm_w[i][_ROT_PERM], 2).reshape(1, LANES)
        kw = jnp.tile(k_norm_w[i][_ROT_PERM], 2).reshape(1, LANES)
        p, g, qt, k, vt = _in_proj(xa, mods, norm1_w[i], _prep_w_in(w_in[i]), cos, sin, qw, kw)
        hf, hb = _mlstm(p, g, mlstm_gate_b[i], n_ctx // M_CHUNK)
        a = _attention(qt, k, vt, q_tile0=ctx_tiles if last else 0, n_ctx=n_ctx, tq=tq)
        wout = w_out[i].astype(BF16)
        if not last:
            j = i // 2
            ffn = (ffn_w_gate[j].astype(BF16), ffn_w_up[j].astype(BF16), ffn_w_down[j].astype(BF16))
            xa = _mixer_out(hf, hb, p, a, xa, mods, mlstm_norm_w[i], norm2_w[i], wout, ffn=ffn)
        else:
            j = i // 2
            router = jnp.zeros((D, LANES), F32).at[:, :N_EXPERTS].set(moe_router[j])
            router_hi = router.astype(BF16)
            router = jnp.stack([router_hi, (router - router_hi.astype(F32)).astype(BF16)])
            x1, h2, ids, gates = _mixer_out(hf, hb, p, a, xa, mods, mlstm_norm_w[i], norm2_w[i], wout,
                                            router=router, row_off=ctx_tiles)
            out = _moe(h2, ids, gates, x1, mods, moe_w_gate[j].astype(BF16), moe_w_up[j].astype(BF16),
                       moe_w_down[j].astype(BF16), final_norm_w)
    return out
```

```python
import functools
import math

import numpy as np
import jax
import jax.numpy as jnp
from jax import lax
from jax.experimental import pallas as pl
from jax.experimental.pallas import tpu as pltpu

F32 = jnp.float32
BF16 = jnp.bfloat16
HIGHEST = lax.Precision.HIGHEST

D_MODEL = 1024
GRID_W = 64
M_HEADS = 4
M_DV = 128
M_DK = 64
M_CHUNK = 128
A_HEADS = 8
A_KV_HEADS = 2
A_GROUP = A_HEADS // A_KV_HEADS
A_DH = 64
ROPE_THETA = 10000.0
N_EXPERTS = 8
EPS = 1e-6

LANES = 128
SUBLANES = 8
ROW_TILE = 256
VMEM_LIMIT = 56 * 1024 * 1024

P_QK = 0
P_MV = 512
P_MO = 1024
P_AQ = 1536
P_AKV = 2048
P_WIDTH = 2304
G_WIDTH = LANES


def _params(sem, vmem=VMEM_LIMIT, flags=None):
    return pltpu.CompilerParams(dimension_semantics=sem, vmem_limit_bytes=vmem, flags=flags)


def _sigmoid(x):
    return 1.0 / (1.0 + jnp.exp(-x))


def _rms_rows(x):
    return x * lax.rsqrt(jnp.mean(x * x, axis=-1, keepdims=True) + EPS)


def _ada_kernel(c_ref, w_ref, b_ref, o_ref):
    c = c_ref[...]
    s = c * _sigmoid(c)
    o_ref[...] = jnp.dot(s, w_ref[...], precision=HIGHEST, preferred_element_type=F32) + b_ref[...]


def _ada(cvec, w, b):
    n = w.shape[1]
    bn = 1536
    return pl.pallas_call(
        _ada_kernel,
        out_shape=jax.ShapeDtypeStruct((cvec.shape[0], n), F32),
        grid=(n // bn,),
        in_specs=[pl.BlockSpec(cvec.shape, lambda j: (0, 0)),
                  pl.BlockSpec((w.shape[0], bn), lambda j: (0, j)),
                  pl.BlockSpec((1, bn), lambda j: (0, j))],
        out_specs=pl.BlockSpec((cvec.shape[0], bn), lambda j: (0, j)),
        compiler_params=_params(("arbitrary",)),
        name="ada_mod",
    )(cvec, w, b.reshape(1, n))


def _mod_spec(off=0):
    return pl.BlockSpec((1, 1, 6, D_MODEL), lambda b, t: (b, jnp.minimum(t + off, 1), 0, 0))


def _stream_specs(stream, off=0):
    tile = (1, ROW_TILE, D_MODEL)
    if not isinstance(stream, tuple):
        return [pl.BlockSpec(tile, lambda b, t: (b, t + off, 0))], [stream]
    return ([pl.BlockSpec(tile, lambda b, t: (b, 0, 0)),
             pl.BlockSpec(tile, lambda b, t: (b, jnp.maximum(t + off - 1, 0), 0))], list(stream))


def _stream_tile(refs, off=0):
    if len(refs) == 1:
        return refs[0][0]
    return jnp.where(pl.program_id(1) + off == 0, refs[0][0], refs[1][0])


def _in_proj_kernel(*refs, n_stream):
    mod_ref, nw_ref, w_ref, cos_ref, sin_ref, qw_ref, kw_ref, p_ref, g_ref, qt_ref, k_ref, vt_ref = refs[n_stream:]
    mod = mod_ref[0, 0]
    h = _rms_rows(_stream_tile(refs[:n_stream])) * nw_ref[...] * (1.0 + mod[1:2]) + mod[0:1]
    hb = h.astype(BF16)
    r_att = jnp.dot(hb, w_ref[:, P_AQ:P_WIDTH], preferred_element_type=F32)
    _attn_prep_tile(r_att[:, :P_AKV - P_AQ], r_att[:, P_AKV - P_AQ:], cos_ref[...], sin_ref[...], qw_ref[...],
                    kw_ref[...], qt_ref, k_ref, vt_ref)
    p_ref[0] = jnp.dot(hb, w_ref[:, :P_AQ], preferred_element_type=F32).astype(BF16)
    g_ref[0] = jnp.dot(hb, w_ref[:, P_WIDTH:], preferred_element_type=F32)


def _in_proj(stream, mods, nw, wp, cos, sin, qw, kw):
    s_specs, s_args = _stream_specs(stream)
    B = s_args[0].shape[0]
    L = sum(s.shape[1] for s in s_args)
    D = D_MODEL
    nt = L // ROW_TILE
    row = lambda w: pl.BlockSpec((1, ROW_TILE, w), lambda b, t: (b, t, 0))
    vec = pl.BlockSpec((1, LANES), lambda b, t: (0, 0))
    table = pl.BlockSpec((ROW_TILE, LANES), lambda b, t: (t, 0))
    return pl.pallas_call(
        functools.partial(_in_proj_kernel, n_stream=len(s_args)),
        out_shape=(jax.ShapeDtypeStruct((B, L, P_AQ), BF16),
                   jax.ShapeDtypeStruct((B, L, G_WIDTH), F32),
                   jax.ShapeDtypeStruct((B, A_HEADS * A_DH, L), BF16),
                   jax.ShapeDtypeStruct((B, A_KV_HEADS, nt, ROW_TILE, A_DH), BF16),
                   jax.ShapeDtypeStruct((B, A_KV_HEADS, nt, VT_ROWS, ROW_TILE), BF16)),
        grid=(B, nt),
        in_specs=[*s_specs, _mod_spec(),
                  pl.BlockSpec((1, D), lambda b, t: (0, 0)),
                  pl.BlockSpec(wp.shape, lambda b, t: (0, 0), pipeline_mode=pl.Buffered(1)),
                  table, table, vec, vec],
        out_specs=(row(P_AQ), row(G_WIDTH),
                   pl.BlockSpec((1, A_HEADS * A_DH, ROW_TILE), lambda b, t: (b, 0, t)),
                   pl.BlockSpec((1, A_KV_HEADS, 1, ROW_TILE, A_DH), lambda b, t: (b, 0, t, 0, 0)),
                   pl.BlockSpec((1, A_KV_HEADS, 1, VT_ROWS, ROW_TILE), lambda b, t: (b, 0, t, 0, 0))),
        compiler_params=_params(("parallel", "arbitrary")),
        name="in_proj",
    )(*s_args, mods, nw.reshape(1, D), wp, cos, sin, qw, kw)


C_ROWS = M_DV + 16
VEC_ROWS = 24
INTRA_CHUNKS = 6

def _scan_lanes(x, reverse):
    lane = lax.broadcasted_iota(jnp.int32, x.shape, 1)
    k = 1
    while k < M_CHUNK:
        if reverse:
            x = x + jnp.where(lane < M_CHUNK - k, pltpu.roll(x, M_CHUNK - k, axis=1), 0.0)
        else:
            x = x + jnp.where(lane >= k, pltpu.roll(x, k, axis=1), 0.0)
        k *= 2
    return x


def _mlstm_intra_kernel(qk_ref, v_ref, g_ref, bias_ref, numf_ref, numb_ref, vecf_ref, vecb_ref, clf_ref, clb_ref):
    row = lax.broadcasted_iota(jnp.int32, (M_CHUNK, M_CHUNK), 0)
    col = lax.broadcasted_iota(jnp.int32, (M_CHUNK, M_CHUNK), 1)
    gate_row = lax.broadcasted_iota(jnp.int32, (16, M_CHUNK), 0)
    tail_row = lax.broadcasted_iota(jnp.int32, (C_ROWS - M_DV, M_CHUNK), 0)
    outs = ((numf_ref, vecf_ref, clf_ref, row <= col, M_CHUNK - 1),
            (numb_ref, vecb_ref, clb_ref, row >= col, 0))
    chunks = range(INTRA_CHUNKS)
    toks = [pl.ds(c * M_CHUNK, M_CHUNK) for c in chunks]

    g_row, scans, gap_cols = [], [], []
    for c in chunks:
        g = g_ref[0, toks[c], :] + bias_ref[...]
        gr = g.T[0:16, :]
        lf_row = jnp.minimum(gr, 0.0) - jnp.log1p(jnp.exp(-jnp.abs(gr)))
        sc = (_scan_lanes(lf_row, False), _scan_lanes(lf_row, True))
        gaps = gr - pltpu.roll(jnp.where(gate_row < 8, sc[0], sc[1]), 12, axis=0)
        g_row.append(gr)
        scans.append(sc)
        gap_cols.append(jnp.concatenate([gaps, jnp.zeros((M_CHUNK - 16, M_CHUNK), F32)], axis=0).T)
        for vec_ref in (vecf_ref, vecb_ref):
            vec_ref[0, c, 12 + 2 * M_HEADS:VEC_ROWS, :] = jnp.zeros((VEC_ROWS - 12 - 2 * M_HEADS, LANES), F32)

    heads = [(c, h) for c in chunks for h in range(M_HEADS)]
    ks, vts, s_raws = {}, {}, {}
    for c, h in heads:
        qk = qk_ref[0, toks[c], h * LANES:(h + 1) * LANES]
        q, ks[c, h] = qk[:, :M_DK], qk[:, M_DK:]
        vts[c, h] = v_ref[0, toks[c], h * M_DV:(h + 1) * M_DV].astype(F32).T
        s_raws[c, h] = lax.dot_general(ks[c, h], q, (((1,), (1,)), ((), ())), preferred_element_type=F32)

    units = [(c, h, d) for c, h in heads for d in range(2)]
    s_w, vws = {}, {}
    for c, h, d in units:
        _, vec_ref, _, allowed, last = outs[d]
        b_r = scans[c][d][8 * d + 4 + h:8 * d + 5 + h, :]
        i_r = g_row[c][8 * d + h:8 * d + h + 1, :]
        j = 8 * d + h
        b_end = b_r[:, last:last + 1]
        d_log = jnp.where(allowed, b_r + gap_cols[c][:, j:j + 1], -jnp.inf)
        m_intra = jnp.max(d_log, axis=0, keepdims=True)
        s = s_raws[c, h] * jnp.exp(d_log - m_intra)
        s_w[c, h, d] = s.astype(BF16)
        vec_ref[0, c, 3 * h:3 * h + 1, :] = jnp.sum(s, axis=0, keepdims=True)
        vec_ref[0, c, 3 * h + 1:3 * h + 2, :] = m_intra
        vec_ref[0, c, 3 * h + 2:3 * h + 3, :] = b_r
        w_log = b_end - b_r + i_r
        m_loc = jnp.max(w_log, axis=-1, keepdims=True)
        w_row = jnp.exp(w_log - m_loc)
        vws[c, h, d] = jnp.concatenate([vts[c, h] * w_row, jnp.where(tail_row == 0, w_row, 0.0)],
                                       axis=0).astype(BF16)
        vec_ref[0, c, 12 + 2 * h:13 + 2 * h, :] = jnp.broadcast_to(m_loc, (1, LANES))
        vec_ref[0, c, 13 + 2 * h:14 + 2 * h, :] = jnp.broadcast_to(b_end, (1, LANES))

    for c, h, d in units:
        num_ref, _, cl_ref, _, _ = outs[d]
        num_ref[0, c, h] = jnp.dot(vts[c, h].astype(BF16), s_w[c, h, d],
                                   preferred_element_type=F32).astype(BF16)
        cl_ref[0, c, h] = jnp.dot(vws[c, h, d], ks[c, h], preferred_element_type=F32)


def _mlstm_scan_kernel(*refs, n_batch):
    ins, (hf_ref, hb_ref, cn_ref, m_ref) = refs[:8], refs[8:]

    @pl.when(pl.program_id(0) == 0)
    def _():
        cn_ref[...] = jnp.zeros_like(cn_ref)
        m_ref[...] = jnp.zeros_like(m_ref)

    for d, h_ref in enumerate((hf_ref, hb_ref)):
        qk_ref, num_ref, vec_ref, cl_ref = ins[4 * d:4 * d + 4]
        for b in range(n_batch):
            for h in range(M_HEADS):
                idx = (d * n_batch + b) * M_HEADS + h
                q = qk_ref[b, :, h * LANES:h * LANES + M_DK]
                row = lambda r: vec_ref[b, 0, r:r + 1, :]
                den_i, m_i, b_r = row(3 * h), row(3 * h + 1), row(3 * h + 2)
                m_loc, b_end = row(12 + 2 * h), row(13 + 2 * h)
                m_prev = m_ref[idx]
                cn = cn_ref[idx]

                inter = b_r + m_prev
                m_t = jnp.maximum(inter, m_i)
                a = jnp.exp(inter - m_t)
                e = jnp.exp(m_i - m_t)
                cq = lax.dot_general(cn.astype(BF16), q, (((1,), (1,)), ((), ())),
                                     preferred_element_type=F32)
                den = e * den_i + a * cq[M_DV:M_DV + 1, :]
                scale = 1.0 / jnp.maximum(jnp.abs(den), jnp.exp(-m_t))
                ht = (e * num_ref[b, 0, h].astype(F32) + a * cq[0:M_DV, :]) * scale
                h_ref[b, :, h * M_DV:(h + 1) * M_DV] = ht.T.astype(BF16)

                m_new = jnp.maximum(b_end + m_prev, m_loc)
                a_s = jnp.exp(b_end + m_prev - m_new)
                s_s = jnp.exp(m_loc - m_new)
                cn_ref[idx] = a_s[:, :M_DK] * cn + s_s[:, :M_DK] * cl_ref[b, 0, h]
                m_ref[idx] = m_new


def _mlstm(p, g, gate_b, ctx_chunks):
    B, L, _ = p.shape
    nc = L // M_CHUNK
    width = M_HEADS * M_DV
    bias = jnp.zeros((1, G_WIDTH), F32).at[0, :16].set(gate_b)
    assert nc % INTRA_CHUNKS == 0
    tok = lambda w, cb=0: pl.BlockSpec((1, INTRA_CHUNKS * M_CHUNK, w), lambda b, c: (b, c, cb))
    num_shape, vec_shape, cl_shape = (M_HEADS, M_DV, M_CHUNK), (VEC_ROWS, LANES), (M_HEADS, C_ROWS, M_DK)
    per_chunk = lambda s: pl.BlockSpec((1, INTRA_CHUNKS) + s, lambda b, c: (b, c) + (0,) * len(s))
    f32 = lambda *s: jax.ShapeDtypeStruct(s, F32)
    numf, numb, vecf, vecb, clf, clb = pl.pallas_call(
        _mlstm_intra_kernel,
        out_shape=(jax.ShapeDtypeStruct((B, nc, *num_shape), BF16),) * 2 + (f32(B, nc, *vec_shape),) * 2
                  + (f32(B, nc, *cl_shape),) * 2,
        grid=(B, nc // INTRA_CHUNKS),
        in_specs=[tok(width, P_QK // width), tok(width, P_MV // width), tok(G_WIDTH),
                  pl.BlockSpec((1, G_WIDTH), lambda b, c: (0, 0))],
        out_specs=(per_chunk(num_shape),) * 2 + (per_chunk(vec_shape),) * 2 + (per_chunk(cl_shape),) * 2,
        compiler_params=_params(("parallel", "parallel")),
        name="mlstm_intra",
    )(p, p, g, bias)

    fwd = lambda j: j
    bwd = lambda j: jnp.where(j < ctx_chunks, ctx_chunks - 1 - j, nc - 1 + ctx_chunks - j)
    stok = lambda cm, w, cb=0: pl.BlockSpec((B, M_CHUNK, w), lambda j: (0, cm(j), cb))
    schunk = lambda cm, s: pl.BlockSpec((B, 1) + s, lambda j: (0, cm(j)) + (0,) * len(s))
    side = lambda cm: [stok(cm, width, P_QK // width), schunk(cm, num_shape), schunk(cm, vec_shape),
                       schunk(cm, cl_shape)]
    chains = 2 * B * M_HEADS
    return pl.pallas_call(
        functools.partial(_mlstm_scan_kernel, n_batch=B),
        out_shape=(jax.ShapeDtypeStruct((B, L, width), BF16),) * 2,
        grid=(nc,),
        in_specs=side(fwd) + side(bwd),
        out_specs=(stok(fwd, width), stok(bwd, width)),
        scratch_shapes=[pltpu.VMEM((chains, C_ROWS, M_DK), F32),
                        pltpu.VMEM((chains, 1, LANES), F32)],
        compiler_params=_params(("arbitrary",)),
        name="mlstm_scan",
    )(p, numf, vecf, clf, p, numb, vecb, clb)


def _head_norm_rope(xs, ws, cos, sin, bd):
    sqs = [x * x for x in xs]
    his = [sq.astype(BF16) for sq in sqs]
    los = [(sq - hi.astype(F32)).astype(BF16) for sq, hi in zip(sqs, his)]
    mss = [jnp.dot(hi, bd, preferred_element_type=F32) + jnp.dot(lo, bd, preferred_element_type=F32)
           for hi, lo in zip(his, los)]
    ys = [x * lax.rsqrt(ms + EPS) * w for x, ms, w in zip(xs, mss, ws)]
    lane = lax.broadcasted_iota(jnp.int32, ys[0].shape, 1)
    first_half = lane % A_DH < A_DH // 2
    partners = [jnp.where(first_half, pltpu.roll(y, LANES - A_DH // 2, axis=1), pltpu.roll(y, A_DH // 2, axis=1))
                for y in ys]
    return [y * cos + partner * sin for y, partner in zip(ys, partners)]


Q_SCALE = A_DH ** -0.5 * math.log2(math.e)


def _attn_prep_tile(q, kv, cos, sin, qw, kw, qt_ref, k_ref, vt_ref):
    r = lax.broadcasted_iota(jnp.int32, (LANES, LANES), 0) // A_DH
    c = lax.broadcasted_iota(jnp.int32, (LANES, LANES), 1) // A_DH
    bd = jnp.where(r == c, 1.0 / A_DH, 0.0).astype(BF16)
    n_pairs = A_HEADS // 2
    tiles = [q[:, pair * LANES:(pair + 1) * LANES] for pair in range(n_pairs)] + [kv[:, :LANES]]
    rotated = _head_norm_rope(tiles, [qw] * n_pairs + [kw], cos, sin, bd)
    for pair in range(n_pairs):
        qt_ref[0, pair * LANES:(pair + 1) * LANES, :] = (rotated[pair] * Q_SCALE).T.astype(BF16)
    k = rotated[n_pairs].astype(BF16)
    for kvh in range(A_KV_HEADS):
        k_ref[0, kvh, 0] = k[:, kvh * A_DH:(kvh + 1) * A_DH]
    vt = kv[:, LANES:].T.astype(BF16)
    ones = jnp.ones((VT_ROWS - A_DH, vt.shape[1]), BF16)
    for kvh in range(A_KV_HEADS):
        vt_ref[0, kvh, 0] = jnp.concatenate([vt[kvh * A_DH:(kvh + 1) * A_DH, :], ones], axis=0)


ATT_SUB = 256
ATT_PIECE = 128
VT_ROWS = A_DH + 16


def _attn_kernel(qt_ref, k_ref, vt_ref, o_ref, sa_ref, sb_ref, ma_ref, mb_ref, acc_ref,
                 *, blocks, tq, ctx_tiles, q_tile0):
    q_of = lambda g: qt_ref[0, g * A_DH:(g + 1) * A_DH, :]
    lanes = lambda g: slice(g * tq, (g + 1) * tq)
    head, mid, n_mid, tail = blocks

    def step(nxt, cur, ms):
        out = []
        for g in range(A_GROUP):
            if cur is not None:
                c0, c_subs, cs_ref, cm_ref = cur
                m_new = jnp.maximum(ms[g], cm_ref[:, lanes(g)])
                alpha = jnp.exp2(ms[g] - m_new)
            best, pv = None, None
            for r in range(max(nxt[1] if nxt else 0, cur[1] if cur else 0)):
                parts = []
                for piece in range(ATT_SUB // ATT_PIECE):
                    rows = pl.ds(piece * ATT_PIECE, ATT_PIECE)
                    buf_rows = pl.ds(r * ATT_SUB + piece * ATT_PIECE, ATT_PIECE)
                    if nxt is not None and r < nxt[1]:
                        s = jnp.dot(k_ref[0, 0, nxt[0] + r, rows, :], q_of(g), preferred_element_type=F32)
                        nxt[2][g, buf_rows, :] = s
                        top = jnp.max(s, axis=0, keepdims=True)
                        best = top if best is None else jnp.maximum(best, top)
                    if cur is not None and r < c_subs:
                        parts.append(jnp.exp2(cs_ref[g, buf_rows, :] - m_new).astype(BF16))
                if cur is not None and r < c_subs:
                    d = jnp.dot(vt_ref[0, 0, c0 + r], jnp.concatenate(parts, axis=0), preferred_element_type=F32)
                    pv = d if pv is None else pv + d
            if nxt is not None:
                nxt[3][:, lanes(g)] = best
            if cur is not None:
                acc_ref[:, lanes(g)] = alpha * acc_ref[:, lanes(g)] + pv
                out.append(m_new)
            else:
                out.append(ms[g])
        return tuple(out)

    def finish():
        o = acc_ref[0:A_DH, :] / acc_ref[A_DH:A_DH + 1, :]
        o = jnp.concatenate([o[:, lanes(g)] for g in range(A_GROUP)], axis=0)
        o_ref[0] = o.T.astype(BF16)

    acc_ref[...] = jnp.zeros_like(acc_ref)
    init = (jnp.full((1, tq), -jnp.inf, F32),) * A_GROUP
    is_ctx = pl.program_id(2) + q_tile0 < ctx_tiles
    buf_a, buf_b = (sa_ref, ma_ref), (sb_ref, mb_ref)

    mid_block = lambda i, buf: (head + mid * i, mid, *buf)

    @pl.when(is_ctx)
    def _():
        step((0, head, *buf_a), None, init)
        step(None, (0, head, *buf_a), init)
        finish()

    @pl.when(jnp.logical_not(is_ctx))
    def _():
        step((0, head, *buf_a), None, init)
        ms = step(mid_block(0, buf_b), (0, head, *buf_a), init)

        def pair(j, ms):
            ms = step(mid_block(2 * j + 1, buf_a), mid_block(2 * j, buf_b), ms)
            return step(mid_block(2 * j + 2, buf_b), mid_block(2 * j + 1, buf_a), ms)

        ms = lax.fori_loop(0, n_mid // 2 - 1, pair, ms)
        last = n_mid - 1
        ms = step(mid_block(last, buf_a), mid_block(last - 1, buf_b), ms)
        tail_block = (head + mid * n_mid, tail, *buf_b)
        ms = step(tail_block, mid_block(last, buf_a), ms)
        step(None, tail_block, ms)
        finish()


def _attention(qt, k, vt, *, q_tile0, n_ctx, tq):
    B, _, L = qt.shape
    n_sub = k.shape[2]
    head, mid = n_ctx // ATT_SUB, 3
    n_mid = (n_sub - head - 1) // mid // 2 * 2
    tail = n_sub - head - mid * n_mid
    assert k.shape[3] == ATT_SUB and n_ctx == tq == ATT_SUB and n_mid >= 2 and 1 <= tail <= mid
    width = A_GROUP * A_DH
    n = A_GROUP * tq
    s_buf, m_buf = pltpu.VMEM((A_GROUP, mid * ATT_SUB, tq), F32), pltpu.VMEM((1, n), F32)
    return pl.pallas_call(
        functools.partial(_attn_kernel, blocks=(head, mid, n_mid, tail), tq=tq, ctx_tiles=n_ctx // tq,
                          q_tile0=q_tile0),
        out_shape=jax.ShapeDtypeStruct((B, L - q_tile0 * tq, A_HEADS * A_DH), BF16),
        grid=(B, A_KV_HEADS, L // tq - q_tile0),
        in_specs=[pl.BlockSpec((1, width, tq), lambda b, kv, t: (b, kv, t + q_tile0)),
                  pl.BlockSpec((1, 1, n_sub, ATT_SUB, A_DH), lambda b, kv, t: (b, kv, 0, 0, 0)),
                  pl.BlockSpec((1, 1, n_sub, VT_ROWS, ATT_SUB), lambda b, kv, t: (b, kv, 0, 0, 0))],
        out_specs=pl.BlockSpec((1, tq, width), lambda b, kv, t: (b, t, kv)),
        scratch_shapes=[s_buf, s_buf, m_buf, m_buf, pltpu.VMEM((VT_ROWS, n), F32)],
        compiler_params=_params(("parallel", "parallel", "arbitrary")),
        name="attention",
    )(qt, k, vt)


def _mixer_out_kernel(*refs, with_router, n_stream, row_off):
    hf_ref, hb_ref, mo_ref, a_ref = refs[:4]
    mod_ref, mnw_ref, n2w_ref, wout_ref = refs[4 + n_stream:8 + n_stream]
    rest = refs[8 + n_stream:]
    hs = hf_ref[0].astype(F32) + hb_ref[0].astype(F32)
    hn = jnp.concatenate([_rms_rows(hs[:, h * M_DV:(h + 1) * M_DV]) for h in range(M_HEADS)], axis=1)
    m = hn * mnw_ref[...] * _sigmoid(mo_ref[0].astype(F32))
    y_in = jnp.concatenate([m.astype(BF16), a_ref[0]], axis=1)
    mod = mod_ref[0, 0]
    x1 = _stream_tile(refs[4:4 + n_stream], row_off) + mod[2:3] * jnp.dot(y_in, wout_ref[...],
                                                                          preferred_element_type=F32)
    h2 = _rms_rows(x1) * n2w_ref[...] * (1.0 + mod[4:5]) + mod[3:4]
    if not with_router:
        wg_ref, wu_ref, wd_ref, o_ref = rest
        hb16 = h2.astype(BF16)
        g = jnp.dot(hb16, wg_ref[...], preferred_element_type=F32)
        u = jnp.dot(hb16, wu_ref[...], preferred_element_type=F32)
        act = (g * _sigmoid(g) * u).astype(BF16)
        o_ref[0] = x1 + mod[5:6] * jnp.dot(act, wd_ref[...], preferred_element_type=F32)
        return
    router_ref, x1_ref, h2_ref, ids_ref, gates_ref = rest
    x1_ref[0] = x1
    h2_ref[0] = h2
    h_hi = h2.astype(BF16)
    h_lo = (h2 - h_hi.astype(F32)).astype(BF16)
    logits = (jnp.dot(h_hi, router_ref[0], preferred_element_type=F32)
              + jnp.dot(h_lo, router_ref[0], preferred_element_type=F32)
              + jnp.dot(h_hi, router_ref[1], preferred_element_type=F32))
    lane = lax.broadcasted_iota(jnp.int32, logits.shape, 1)
    logits = jnp.where(lane < N_EXPERTS, logits, -jnp.inf)
    m1 = jnp.max(logits, axis=-1, keepdims=True)
    i1 = jnp.min(jnp.where(logits == m1, lane, LANES), axis=-1, keepdims=True)
    rest = jnp.where(lane == i1, -jnp.inf, logits)
    m2 = jnp.max(rest, axis=-1, keepdims=True)
    i2 = jnp.min(jnp.where(rest == m2, lane, LANES), axis=-1, keepdims=True)
    e2 = jnp.exp(m2 - m1)
    g1 = 1.0 / (1.0 + e2)
    ids_ref[0] = jnp.where(lane == 0, i1, jnp.where(lane == 1, i2, -1))
    gates_ref[0] = jnp.where(lane == 0, g1, jnp.where(lane == 1, e2 * g1, 0.0))


def _mixer_out(hf, hb, p, a, stream, mods, mnw, n2w, wout, *, ffn=None, router=None, row_off=0):
    B, L = p.shape[:2]
    D = D_MODEL
    nt = L // ROW_TILE - row_off
    rin = lambda w, cb=0: pl.BlockSpec((1, ROW_TILE, w), lambda b, t: (b, t + row_off, cb))
    rout = lambda w: pl.BlockSpec((1, ROW_TILE, w), lambda b, t: (b, t, 0))
    const = lambda arr: pl.BlockSpec(arr.shape, lambda b, t: (0,) * arr.ndim, pipeline_mode=pl.Buffered(1))
    mw = M_HEADS * M_DV
    a_off = row_off - (L - a.shape[1]) // ROW_TILE
    a_spec = pl.BlockSpec((1, ROW_TILE, A_HEADS * A_DH), lambda b, t: (b, t + a_off, 0))
    s_specs, s_args = _stream_specs(stream, row_off)
    in_specs = [rin(mw), rin(mw), rin(mw, P_MO // mw), a_spec, *s_specs, _mod_spec(row_off),
                pl.BlockSpec((1, mw), lambda b, t: (0, 0)), pl.BlockSpec((1, D), lambda b, t: (0, 0)),
                const(wout)]
    args = [hf, hb, p, a, *s_args, mods, mnw.reshape(1, mw), n2w.reshape(1, D), wout]
    rows = nt * ROW_TILE
    if router is None:
        in_specs += [const(w) for w in ffn]
        args += list(ffn)
        out_shape, out_specs = jax.ShapeDtypeStruct((B, rows, D), F32), rout(D)
    else:
        in_specs.append(const(router))
        args.append(router)
        out_shape = (jax.ShapeDtypeStruct((B, rows, D), F32), jax.ShapeDtypeStruct((B, rows, D), F32),
                     jax.ShapeDtypeStruct((B, rows, LANES), jnp.int32), jax.ShapeDtypeStruct((B, rows, LANES), F32))
        out_specs = (rout(D), rout(D), rout(LANES), rout(LANES))
    return pl.pallas_call(
        functools.partial(_mixer_out_kernel, with_router=router is not None, n_stream=len(s_args), row_off=row_off),
        out_shape=out_shape,
        grid=(B, nt),
        in_specs=in_specs,
        out_specs=out_specs,
        compiler_params=_params(("parallel", "arbitrary")),
        name="mixer_out",
    )(*args)


MOE_TM = 512
MOE_FF = 1792
MOE_TD = 256
RANK_TILE = 1024
POS_TILE = 2048
ISSUE_UNROLL = 8


def _moe_rank_kernel(ids_ref, rank_ref, cnt_ref, carry_ref, before_ref):
    @pl.when(pl.program_id(0) == 0)
    def _():
        carry_ref[...] = jnp.zeros_like(carry_ref)
        r = lax.broadcasted_iota(jnp.int32, (RANK_TILE, RANK_TILE), 0)
        c = lax.broadcasted_iota(jnp.int32, (RANK_TILE, RANK_TILE), 1)
        before_ref[...] = jnp.where(c < r, 1.0, 0.0).astype(BF16)

    ids = ids_ref[...]
    lane = lax.broadcasted_iota(jnp.int32, ids.shape, 1)
    onehot = jnp.where(jnp.logical_or(lane == ids[:, 0:1], lane == ids[:, 1:2]), 1.0, 0.0)
    rank_ref[...] = jnp.dot(before_ref[...], onehot.astype(BF16), preferred_element_type=F32) + carry_ref[...]
    carry_ref[...] += jnp.sum(onehot, axis=0, keepdims=True)
    cnt_ref[...] = carry_ref[...]


def _moe_rank(ids):
    n = ids.shape[0]
    return pl.pallas_call(
        _moe_rank_kernel,
        out_shape=(jax.ShapeDtypeStruct((n, LANES), F32), jax.ShapeDtypeStruct((1, LANES), F32)),
        grid=(n // RANK_TILE,),
        in_specs=[pl.BlockSpec((RANK_TILE, LANES), lambda t: (t, 0))],
        out_specs=(pl.BlockSpec((RANK_TILE, LANES), lambda t: (t, 0)), pl.BlockSpec((1, LANES), lambda t: (0, 0))),
        scratch_shapes=[pltpu.VMEM((1, LANES), F32), pltpu.VMEM((RANK_TILE, RANK_TILE), BF16)],
        compiler_params=_params(("arbitrary",)),
        name="moe_rank",
    )(ids)


def _moe_pos_kernel(ids_ref, rank_ref, start_ref, pos_ref):
    ids = ids_ref[...]
    lane = lax.broadcasted_iota(jnp.int32, ids.shape, 1)
    tgt = start_ref[...] + rank_ref[...]
    p0 = jnp.sum(jnp.where(lane == ids[:, 0:1], tgt, 0.0), axis=-1, keepdims=True)
    p1 = jnp.sum(jnp.where(lane == ids[:, 1:2], tgt, 0.0), axis=-1, keepdims=True)
    pos_ref[...] = jnp.where(lane == 0, p0, jnp.where(lane == 1, p1, 0.0)).astype(jnp.int32)


def _moe_pos(ids, rank, start_row):
    n = ids.shape[0]
    blk = pl.BlockSpec((POS_TILE, LANES), lambda t: (t, 0))
    return pl.pallas_call(
        _moe_pos_kernel,
        out_shape=jax.ShapeDtypeStruct((n, LANES), jnp.int32),
        grid=(n // POS_TILE,),
        in_specs=[blk, blk, pl.BlockSpec((1, LANES), lambda t: (0, 0))],
        out_specs=blk,
        compiler_params=_params(("parallel",)),
        name="moe_pos",
    )(ids, rank, start_row)


def _row_copy(src, src_row, dst, dst_row, sem):
    return pltpu.make_async_copy(src.at[pl.ds(src_row, 1), :], dst.at[pl.ds(dst_row, 1), :], sem)


def _moe_dispatch_kernel(pad_ref, pos_ref, h_ref, xs_ref, zero_ref, sem):
    @pl.when(pl.program_id(0) == 0)
    def _():
        zero_ref[...] = jnp.zeros_like(zero_ref)
        fills = [pltpu.make_async_copy(
            zero_ref, xs_ref.at[pl.ds(pl.multiple_of(pad_ref[e], SUBLANES), MOE_TM + SUBLANES), :], sem)
            for e in range(N_EXPERTS)]
        for cp in fills:
            cp.start()
        for cp in fills:
            cp.wait()

        def fill_tile(j, carry):
            cp = pltpu.make_async_copy(zero_ref.at[pl.ds(0, MOE_TM), :],
                                       xs_ref.at[pl.ds(pl.multiple_of(j * MOE_TM, MOE_TM), MOE_TM), :], sem)
            cp.start()
            cp.wait()
            return carry

        lax.fori_loop(pad_ref[N_EXPERTS], xs_ref.shape[0] // MOE_TM, fill_tile, 0)

    def issue(r, carry):
        for k in range(2):
            _row_copy(h_ref, r, xs_ref, pos_ref[0, 0, 2 * r + k], sem).start(priority=k)
        return carry

    lax.fori_loop(0, MOE_TD, issue, 0, unroll=ISSUE_UNROLL)
    for k in range(2):
        pltpu.make_async_copy(h_ref, xs_ref.at[pl.ds(0, MOE_TD), :], sem).wait()


def _moe_dispatch(h, pos, fill_meta, ns):
    n, d = h.shape
    return pl.pallas_call(
        _moe_dispatch_kernel,
        out_shape=jax.ShapeDtypeStruct((ns, d), F32),
        grid_spec=pltpu.PrefetchScalarGridSpec(
            num_scalar_prefetch=1,
            grid=(n // MOE_TD,),
            in_specs=[pl.BlockSpec((1, 1, 2 * MOE_TD), lambda t, pad: (t, 0, 0), memory_space=pltpu.SMEM),
                      pl.BlockSpec((MOE_TD, d), lambda t, pad: (t, 0))],
            out_specs=pl.BlockSpec(memory_space=pl.ANY),
            scratch_shapes=[pltpu.VMEM((MOE_TM + SUBLANES, d), F32), pltpu.SemaphoreType.DMA(())]),
        compiler_params=_params(("arbitrary",)),
        name="moe_dispatch",
    )(fill_meta, pos, h)


def _moe_group_kernel(te_ref, nv_ref, xs_ref, wg_ref, wu_ref, wd_ref, ys_ref, acc_ref):
    i, f = pl.program_id(0), pl.program_id(1)
    last = pl.num_programs(1) - 1
    valid = i < nv_ref[0]

    @pl.when(valid)
    def _():
        @pl.when(f == 0)
        def _():
            acc_ref[...] = jnp.zeros_like(acc_ref)

        x = xs_ref[...].astype(BF16)
        g = jnp.dot(x, wg_ref[0], preferred_element_type=F32)
        u = jnp.dot(x, wu_ref[0], preferred_element_type=F32)
        act = (g * _sigmoid(g) * u).astype(BF16)
        acc_ref[...] += jnp.dot(act, wd_ref[0], preferred_element_type=F32)

        @pl.when(f == last)
        def _():
            ys_ref[...] = acc_ref[...]

    @pl.when(jnp.logical_and(jnp.logical_not(valid), f == last))
    def _():
        ys_ref[...] = jnp.zeros_like(ys_ref)


def _moe_group(xs, tile_expert, n_valid, wg, wu, wd):
    ns, d = xs.shape
    n_tiles = ns // MOE_TM - 1
    ff = wg.shape[2]
    live = lambda i, nv: i < nv[0]
    nf = ff // MOE_FF
    step = lambda i, f, nv: jnp.where(live(i, nv), f, nf - 1)
    return pl.pallas_call(
        _moe_group_kernel,
        out_shape=jax.ShapeDtypeStruct((n_tiles * MOE_TM, d), F32),
        grid_spec=pltpu.PrefetchScalarGridSpec(
            num_scalar_prefetch=2,
            grid=(n_tiles, nf),
            in_specs=[pl.BlockSpec((MOE_TM, d), lambda i, f, te, nv: (jnp.where(live(i, nv), i, 0), 0)),
                      pl.BlockSpec((1, d, MOE_FF), lambda i, f, te, nv: (te[i], 0, step(i, f, nv))),
                      pl.BlockSpec((1, d, MOE_FF), lambda i, f, te, nv: (te[i], 0, step(i, f, nv))),
                      pl.BlockSpec((1, MOE_FF, d), lambda i, f, te, nv: (te[i], step(i, f, nv), 0))],
            out_specs=pl.BlockSpec((MOE_TM, d), lambda i, f, te, nv: (i, 0)),
            scratch_shapes=[pltpu.VMEM((MOE_TM, d), F32)]),
        compiler_params=_params(("arbitrary", "arbitrary")),
        name="moe_group",
    )(tile_expert, n_valid, xs, wg, wu, wd)


def _moe_combine_kernel(pos_ref, x_ref, gates_ref, mod_ref, fw_ref, ys_ref, o_ref, ybuf, sem):
    def issue(r, carry):
        for k in range(2):
            _row_copy(ys_ref, pos_ref[0, 0, 2 * r + k], ybuf.at[k], r, sem).start(priority=k)
        return carry

    lax.fori_loop(0, MOE_TD, issue, 0, unroll=ISSUE_UNROLL)
    for k in range(2):
        pltpu.make_async_copy(ys_ref.at[pl.ds(0, MOE_TD), :], ybuf.at[k], sem).wait()
    gates = gates_ref[...]
    y = gates[:, 0:1] * ybuf[0] + gates[:, 1:2] * ybuf[1]
    x2 = x_ref[...] + mod_ref[0, 0][5:6] * y
    o_ref[...] = _rms_rows(x2) * fw_ref[...]


def _moe_combine(pos, x1, gates, mods, fw, ys, tokens_per_sample):
    n, d = x1.shape
    per = tokens_per_sample // MOE_TD
    return pl.pallas_call(
        _moe_combine_kernel,
        out_shape=jax.ShapeDtypeStruct((n, d), F32),
        grid=(n // MOE_TD,),
        in_specs=[pl.BlockSpec((1, 1, 2 * MOE_TD), lambda t: (t, 0, 0), memory_space=pltpu.SMEM),
                  pl.BlockSpec((MOE_TD, d), lambda t: (t, 0)),
                  pl.BlockSpec((MOE_TD, LANES), lambda t: (t, 0)),
                  pl.BlockSpec((1, 1, 6, d), lambda t: (t // per, 1, 0, 0)),
                  pl.BlockSpec((1, d), lambda t: (0, 0)),
                  pl.BlockSpec(memory_space=pl.ANY)],
        out_specs=pl.BlockSpec((MOE_TD, d), lambda t: (t, 0)),
        scratch_shapes=[pltpu.VMEM((2, MOE_TD, d), F32), pltpu.SemaphoreType.DMA(())],
        compiler_params=_params(("arbitrary",)),
        name="moe_combine",
    )(pos, x1, gates, mods, fw.reshape(1, d), ys)


def _moe(h2, ids, gates, x1, mods, wg, wu, wd, fw):
    B, T, D = x1.shape
    n = B * T
    ids, gates = ids.reshape(n, LANES), gates.reshape(n, LANES)
    rank, cnt = _moe_rank(ids)
    cnt = cnt[0, :N_EXPERTS].astype(jnp.int32)
    padded = (cnt + MOE_TM - 1) // MOE_TM * MOE_TM
    end = jnp.cumsum(padded)
    start = end - padded
    n_tiles = 2 * n // MOE_TM + N_EXPERTS
    tile_expert = jnp.minimum(jnp.sum(jnp.arange(n_tiles)[:, None] >= (end // MOE_TM)[None, :], axis=1),
                              N_EXPERTS - 1).astype(jnp.int32)
    n_valid = (end[-1:] // MOE_TM).astype(jnp.int32)
    start_row = jnp.zeros((1, LANES), F32).at[0, :N_EXPERTS].set(start.astype(F32))
    pos = _moe_pos(ids, rank, start_row)
    pos = pos[:, :2].reshape(n // MOE_TD, 1, 2 * MOE_TD)
    fill_meta = jnp.concatenate([(start + cnt) // SUBLANES * SUBLANES, n_valid]).astype(jnp.int32)
    xs = _moe_dispatch(h2.reshape(n, D), pos, fill_meta, (n_tiles + 1) * MOE_TM)
    ys = _moe_group(xs, tile_expert, n_valid, wg, wu, wd)
    return _moe_combine(pos, x1.reshape(n, D), gates, mods, fw, ys, T).reshape(B, T, D)


_ROT_PERM = np.concatenate([np.arange(0, A_DH, 2), np.arange(1, A_DH, 2)])


def _prep_w_in(w):
    o = np.cumsum([0, M_HEADS * M_DK, M_HEADS * M_DK, M_HEADS * M_DV, M_HEADS * M_DV, 4 * M_HEADS,
                   A_HEADS * A_DH, A_KV_HEADS * A_DH, A_KV_HEADS * A_DH])
    mq, mk, mv, mo, mg, aq, ak, av = [w[:, o[i]:o[i + 1]] for i in range(8)]
    qk = jnp.concatenate([jnp.concatenate([mq[:, h * M_DK:(h + 1) * M_DK] * (M_DK ** -0.5),
                                           mk[:, h * M_DK:(h + 1) * M_DK]], axis=1) for h in range(M_HEADS)], axis=1)
    perm_q = np.concatenate([h * A_DH + _ROT_PERM for h in range(A_HEADS)])
    perm_k = np.concatenate([h * A_DH + _ROT_PERM for h in range(A_KV_HEADS)])
    pad = jnp.zeros((w.shape[0], G_WIDTH - 4 * M_HEADS), w.dtype)
    return jnp.concatenate([qk, mv, mo, aq[:, perm_q], ak[:, perm_k], av, mg, pad], axis=1).astype(BF16)


def _rope_tables(n_tok, n_ctx):
    rows = n_tok // GRID_W
    row = jnp.broadcast_to(jnp.arange(rows, dtype=F32)[:, None], (rows, GRID_W)).reshape(n_tok)
    col = jnp.broadcast_to(jnp.arange(GRID_W, dtype=F32)[None, :], (rows, GRID_W)).reshape(n_tok)
    n_freq = A_DH // 4
    inv_freq = ROPE_THETA ** (-jnp.arange(n_freq, dtype=F32) / n_freq)
    ang = jnp.concatenate([row[:, None] * inv_freq, col[:, None] * inv_freq], axis=-1)
    cos, sin = jnp.cos(ang), jnp.sin(ang)
    cos = jnp.concatenate([jnp.ones((n_ctx, A_DH // 2), F32), cos], axis=0)
    sin = jnp.concatenate([jnp.zeros((n_ctx, A_DH // 2), F32), sin], axis=0)
    return jnp.tile(cos, (1, 4)), jnp.tile(jnp.concatenate([-sin, sin], axis=1), (1, 2))


def kernel(x, c, ctx, c_ctx, ada_w, ada_b, norm1_w, norm2_w, w_in, mlstm_gate_b, mlstm_norm_w, q_norm_w, k_norm_w,
           w_out, ffn_w_gate, ffn_w_up, ffn_w_down, moe_router, moe_w_gate, moe_w_up, moe_w_down, final_norm_w):
    B, T, D = x.shape
    n_ctx = ctx.shape[1]
    L = n_ctx + T
    depth = w_in.shape[0]
    assert D == D_MODEL and n_ctx == ROW_TILE and T % RANK_TILE == 0 and depth == 2
    ctx_tiles = n_ctx // ROW_TILE
    tq = ROW_TILE

    xa = (ctx, x)
    cvec =jnp.concatenate([c, c_ctx[None], jnp.zeros((8 - B - 1, D), F32)], axis=0)
    cos, sin = _rope_tables(T, n_ctx)
    out = None
    for i in range(depth):
        last = i == depth - 1
        modraw = _ada(cvec, ada_w[i], ada_b[i])
        mods = jnp.stack([jnp.broadcast_to(modraw[B].reshape(1, 6, D), (B, 6, D)),
                          modraw[:B].reshape(B, 6, D)], axis=1)
        qw = jnp.tile(q_norm_w[i][_ROT_PERM], 2).reshape(1, LANES)
        kw = jnp.tile(k_norm_w[i][_ROT_PERM], 2).reshape(1, LANES)
        p, g, qt, k, vt = _in_proj(xa, mods, norm1_w[i], _prep_w_in(w_in[i]), cos, sin, qw, kw)
        hf, hb = _mlstm(p, g, mlstm_gate_b[i], n_ctx // M_CHUNK)
        a = _attention(qt, k, vt, q_tile0=ctx_tiles if last else 0, n_ctx=n_ctx, tq=tq)
        wout = w_out[i].astype(BF16)
        if not last:
            j = i // 2
            ffn = (ffn_w_gate[j].astype(BF16), ffn_w_up[j].astype(BF16), ffn_w_down[j].astype(BF16))
            xa = _mixer_out(hf, hb, p, a, xa, mods, mlstm_norm_w[i], norm2_w[i], wout, ffn=ffn)
        else:
            j = i // 2
            router = jnp.zeros((D, LANES), F32).at[:, :N_EXPERTS].set(moe_router[j])
            router_hi = router.astype(BF16)
            router = jnp.stack([router_hi, (router - router_hi.astype(F32)).astype(BF16)])
            x1, h2, ids, gates = _mixer_out(hf, hb, p, a, xa, mods, mlstm_norm_w[i], norm2_w[i], wout,
                                            router=router, row_off=ctx_tiles)
            out = _moe(h2, ids, gates, x1, mods, moe_w_gate[j].astype(BF16), moe_w_up[j].astype(BF16),
                       moe_w_down[j].astype(BF16), final_norm_w)
    return out
```

```python
import functools
import math

import numpy as np
import jax
import jax.numpy as jnp
from jax import lax
from jax.experimental import pallas as pl
from jax.experimental.pallas import tpu as pltpu

F32 = jnp.float32
BF16 = jnp.bfloat16
HIGHEST = lax.Precision.HIGHEST

D_MODEL = 1024
GRID_W = 64
M_HEADS = 4
M_DV = 128
M_DK = 64
M_CHUNK = 128
A_HEADS = 8
A_KV_HEADS = 2
A_GROUP = A_HEADS // A_KV_HEADS
A_DH = 64
ROPE_THETA = 10000.0
N_EXPERTS = 8
EPS = 1e-6

LANES = 128
SUBLANES = 8
ROW_TILE = 256
VMEM_LIMIT = 56 * 1024 * 1024

P_QK = 0
P_MV = 512
P_MO = 1024
P_AQ = 1536
P_AKV = 2048
P_WIDTH = 2304
G_WIDTH = LANES


def _params(sem, vmem=VMEM_LIMIT, flags=None):
    return pltpu.CompilerParams(dimension_semantics=sem, vmem_limit_bytes=vmem, flags=flags)


def _sigmoid(x):
    return 1.0 / (1.0 + jnp.exp(-x))


def _rms_rows(x):
    return x * lax.rsqrt(jnp.mean(x * x, axis=-1, keepdims=True) + EPS)


def _ada_kernel(c_ref, w_ref, b_ref, o_ref):
    c = c_ref[...]
    s = c * _sigmoid(c)
    o_ref[...] = jnp.dot(s, w_ref[...], precision=HIGHEST, preferred_element_type=F32) + b_ref[...]


def _ada(cvec, w, b):
    n = w.shape[1]
    bn = 1536
    return pl.pallas_call(
        _ada_kernel,
        out_shape=jax.ShapeDtypeStruct((cvec.shape[0], n), F32),
        grid=(n // bn,),
        in_specs=[pl.BlockSpec(cvec.shape, lambda j: (0, 0)),
                  pl.BlockSpec((w.shape[0], bn), lambda j: (0, j)),
                  pl.BlockSpec((1, bn), lambda j: (0, j))],
        out_specs=pl.BlockSpec((cvec.shape[0], bn), lambda j: (0, j)),
        compiler_params=_params(("arbitrary",)),
        name="ada_mod",
    )(cvec, w, b.reshape(1, n))


def _mod_spec(off=0):
    return pl.BlockSpec((1, 1, 6, D_MODEL), lambda b, t: (b, jnp.minimum(t + off, 1), 0, 0))


def _stream_specs(stream, off=0):
    tile = (1, ROW_TILE, D_MODEL)
    if not isinstance(stream, tuple):
        return [pl.BlockSpec(tile, lambda b, t: (b, t + off, 0))], [stream]
    return ([pl.BlockSpec(tile, lambda b, t: (b, 0, 0)),
             pl.BlockSpec(tile, lambda b, t: (b, jnp.maximum(t + off - 1, 0), 0))], list(stream))


def _stream_tile(refs, off=0):
    if len(refs) == 1:
        return refs[0][0]
    return jnp.where(pl.program_id(1) + off == 0, refs[0][0], refs[1][0])


def _in_proj_kernel(*refs, n_stream):
    mod_ref, nw_ref, w_ref, cos_ref, sin_ref, qw_ref, kw_ref, p_ref, g_ref, qt_ref, k_ref, vt_ref = refs[n_stream:]
    mod = mod_ref[0, 0]
    h = _rms_rows(_stream_tile(refs[:n_stream])) * nw_ref[...] * (1.0 + mod[1:2]) + mod[0:1]
    hb = h.astype(BF16)
    r_att = jnp.dot(hb, w_ref[:, P_AQ:P_WIDTH], preferred_element_type=F32)
    _attn_prep_tile(r_att[:, :P_AKV - P_AQ], r_att[:, P_AKV - P_AQ:], cos_ref[...], sin_ref[...], qw_ref[...],
                    kw_ref[...], qt_ref, k_ref, vt_ref)
    p_ref[0] = jnp.dot(hb, w_ref[:, :P_AQ], preferred_element_type=F32).astype(BF16)
    g_ref[0] = jnp.dot(hb, w_ref[:, P_WIDTH:], preferred_element_type=F32)


def _in_proj(stream, mods, nw, wp, cos, sin, qw, kw):
    s_specs, s_args = _stream_specs(stream)
    B = s_args[0].shape[0]
    L = sum(s.shape[1] for s in s_args)
    D = D_MODEL
    nt = L // ROW_TILE
    row = lambda w: pl.BlockSpec((1, ROW_TILE, w), lambda b, t: (b, t, 0))
    vec = pl.BlockSpec((1, LANES), lambda b, t: (0, 0))
    table = pl.BlockSpec((ROW_TILE, LANES), lambda b, t: (t, 0))
    return pl.pallas_call(
        functools.partial(_in_proj_kernel, n_stream=len(s_args)),
        out_shape=(jax.ShapeDtypeStruct((B, L, P_AQ), BF16),
                   jax.ShapeDtypeStruct((B, L, G_WIDTH), F32),
                   jax.ShapeDtypeStruct((B, A_HEADS * A_DH, L), BF16),
                   jax.ShapeDtypeStruct((B, A_KV_HEADS, nt, ROW_TILE, A_DH), BF16),
                   jax.ShapeDtypeStruct((B, A_KV_HEADS, nt, VT_ROWS, ROW_TILE), BF16)),
        grid=(B, nt),
        in_specs=[*s_specs, _mod_spec(),
                  pl.BlockSpec((1, D), lambda b, t: (0, 0)),
                  pl.BlockSpec(wp.shape, lambda b, t: (0, 0), pipeline_mode=pl.Buffered(1)),
                  table, table, vec, vec],
        out_specs=(row(P_AQ), row(G_WIDTH),
                   pl.BlockSpec((1, A_HEADS * A_DH, ROW_TILE), lambda b, t: (b, 0, t)),
                   pl.BlockSpec((1, A_KV_HEADS, 1, ROW_TILE, A_DH), lambda b, t: (b, 0, t, 0, 0)),
                   pl.BlockSpec((1, A_KV_HEADS, 1, VT_ROWS, ROW_TILE), lambda b, t: (b, 0, t, 0, 0))),
        compiler_params=_params(("parallel", "arbitrary")),
        name="in_proj",
    )(*s_args, mods, nw.reshape(1, D), wp, cos, sin, qw, kw)


C_ROWS = M_DV + 16
VEC_ROWS = 24
INTRA_CHUNKS = 6

def _scan_lanes(x, reverse):
    lane = lax.broadcasted_iota(jnp.int32, x.shape, 1)
    k = 1
    while k < M_CHUNK:
        if reverse:
            x = x + jnp.where(lane < M_CHUNK - k, pltpu.roll(x, M_CHUNK - k, axis=1), 0.0)
        else:
            x = x + jnp.where(lane >= k, pltpu.roll(x, k, axis=1), 0.0)
        k *= 2
    return x


def _mlstm_intra_kernel(qk_ref, v_ref, g_ref, bias_ref, numf_ref, numb_ref, vecf_ref, vecb_ref, clf_ref, clb_ref):
    row = lax.broadcasted_iota(jnp.int32, (M_CHUNK, M_CHUNK), 0)
    col = lax.broadcasted_iota(jnp.int32, (M_CHUNK, M_CHUNK), 1)
    gate_row = lax.broadcasted_iota(jnp.int32, (16, M_CHUNK), 0)
    tail_row = lax.broadcasted_iota(jnp.int32, (C_ROWS - M_DV, M_CHUNK), 0)
    outs = ((numf_ref, vecf_ref, clf_ref, row <= col, M_CHUNK - 1),
            (numb_ref, vecb_ref, clb_ref, row >= col, 0))
    chunks = range(INTRA_CHUNKS)
    toks = [pl.ds(c * M_CHUNK, M_CHUNK) for c in chunks]

    g_row, scans, gap_cols = [], [], []
    for c in chunks:
        g = g_ref[0, toks[c], :] + bias_ref[...]
        gr = g.T[0:16, :]
        lf_row = jnp.minimum(gr, 0.0) - jnp.log1p(jnp.exp(-jnp.abs(gr)))
        sc = (_scan_lanes(lf_row, False), _scan_lanes(lf_row, True))
        gaps = gr - pltpu.roll(jnp.where(gate_row < 8, sc[0], sc[1]), 12, axis=0)
        g_row.append(gr)
        scans.append(sc)
        gap_cols.append(jnp.concatenate([gaps, jnp.zeros((M_CHUNK - 16, M_CHUNK), F32)], axis=0).T)
        for vec_ref in (vecf_ref, vecb_ref):
            vec_ref[0, c, 12 + 2 * M_HEADS:VEC_ROWS, :] = jnp.zeros((VEC_ROWS - 12 - 2 * M_HEADS, LANES), F32)

    heads = [(c, h) for c in chunks for h in range(M_HEADS)]
    ks, vts, s_raws = {}, {}, {}
    for c, h in heads:
        qk = qk_ref[0, toks[c], h * LANES:(h + 1) * LANES]
        q, ks[c, h] = qk[:, :M_DK], qk[:, M_DK:]
        vts[c, h] = v_ref[0, toks[c], h * M_DV:(h + 1) * M_DV].astype(F32).T
        s_raws[c, h] = lax.dot_general(ks[c, h], q, (((1,), (1,)), ((), ())), preferred_element_type=F32)

    units = [(c, h, d) for c, h in heads for d in range(2)]
    s_w, vws = {}, {}
    for c, h, d in units:
        _, vec_ref, _, allowed, last = outs[d]
        b_r = scans[c][d][8 * d + 4 + h:8 * d + 5 + h, :]
        i_r = g_row[c][8 * d + h:8 * d + h + 1, :]
        j = 8 * d + h
        b_end = b_r[:, last:last + 1]
        d_log = jnp.where(allowed, b_r + gap_cols[c][:, j:j + 1], -jnp.inf)
        m_intra = jnp.max(d_log, axis=0, keepdims=True)
        s = s_raws[c, h] * jnp.exp(d_log - m_intra)
        s_w[c, h, d] = s.astype(BF16)
        vec_ref[0, c, 3 * h:3 * h + 1, :] = jnp.sum(s, axis=0, keepdims=True)
        vec_ref[0, c, 3 * h + 1:3 * h + 2, :] = m_intra
        vec_ref[0, c, 3 * h + 2:3 * h + 3, :] = b_r
        w_log = b_end - b_r + i_r
        m_loc = jnp.max(w_log, axis=-1, keepdims=True)
        w_row = jnp.exp(w_log - m_loc)
        vws[c, h, d] = jnp.concatenate([vts[c, h] * w_row, jnp.where(tail_row == 0, w_row, 0.0)],
                                       axis=0).astype(BF16)
        vec_ref[0, c, 12 + 2 * h:13 + 2 * h, :] = jnp.broadcast_to(m_loc, (1, LANES))
        vec_ref[0, c, 13 + 2 * h:14 + 2 * h, :] = jnp.broadcast_to(b_end, (1, LANES))

    for c, h, d in units:
        num_ref, _, cl_ref, _, _ = outs[d]
        num_ref[0, c, h] = jnp.dot(vts[c, h].astype(BF16), s_w[c, h, d],
                                   preferred_element_type=F32).astype(BF16)
        cl_ref[0, c, h] = jnp.dot(vws[c, h, d], ks[c, h], preferred_element_type=F32)


def _mlstm_scan_kernel(*refs, n_batch):
    ins, (hf_ref, hb_ref, cn_ref, m_ref) = refs[:8], refs[8:]

    @pl.when(pl.program_id(0) == 0)
    def _():
        cn_ref[...] = jnp.zeros_like(cn_ref)
        m_ref[...] = jnp.zeros_like(m_ref)

    for d, h_ref in enumerate((hf_ref, hb_ref)):
        qk_ref, num_ref, vec_ref, cl_ref = ins[4 * d:4 * d + 4]
        for b in range(n_batch):
            for h in range(M_HEADS):
                idx = (d * n_batch + b) * M_HEADS + h
                q = qk_ref[b, :, h * LANES:h * LANES + M_DK]
                row = lambda r: vec_ref[b, 0, r:r + 1, :]
                den_i, m_i, b_r = row(3 * h), row(3 * h + 1), row(3 * h + 2)
                m_loc, b_end = row(12 + 2 * h), row(13 + 2 * h)
                m_prev = m_ref[idx]
                cn = cn_ref[idx]

                inter = b_r + m_prev
                m_t = jnp.maximum(inter, m_i)
                a = jnp.exp(inter - m_t)
                e = jnp.exp(m_i - m_t)
                cq = lax.dot_general(cn.astype(BF16), q, (((1,), (1,)), ((), ())),
                                     preferred_element_type=F32)
                den = e * den_i + a * cq[M_DV:M_DV + 1, :]
                scale = 1.0 / jnp.maximum(jnp.abs(den), jnp.exp(-m_t))
                ht = (e * num_ref[b, 0, h].astype(F32) + a * cq[0:M_DV, :]) * scale
                h_ref[b, :, h * M_DV:(h + 1) * M_DV] = ht.T.astype(BF16)

                m_new = jnp.maximum(b_end + m_prev, m_loc)
                a_s = jnp.exp(b_end + m_prev - m_new)
                s_s = jnp.exp(m_loc - m_new)
                cn_ref[idx] = a_s[:, :M_DK] * cn + s_s[:, :M_DK] * cl_ref[b, 0, h]
                m_ref[idx] = m_new


def _mlstm(p, g, gate_b, ctx_chunks):
    B, L, _ = p.shape
    nc = L // M_CHUNK
    width = M_HEADS * M_DV
    bias = jnp.zeros((1, G_WIDTH), F32).at[0, :16].set(gate_b)
    assert nc % INTRA_CHUNKS == 0
    tok = lambda w, cb=0: pl.BlockSpec((1, INTRA_CHUNKS * M_CHUNK, w), lambda b, c: (b, c, cb))
    num_shape, vec_shape, cl_shape = (M_HEADS, M_DV, M_CHUNK), (VEC_ROWS, LANES), (M_HEADS, C_ROWS, M_DK)
    per_chunk = lambda s: pl.BlockSpec((1, INTRA_CHUNKS) + s, lambda b, c: (b, c) + (0,) * len(s))
    f32 = lambda *s: jax.ShapeDtypeStruct(s, F32)
    numf, numb, vecf, vecb, clf, clb = pl.pallas_call(
        _mlstm_intra_kernel,
        out_shape=(jax.ShapeDtypeStruct((B, nc, *num_shape), BF16),) * 2 + (f32(B, nc, *vec_shape),) * 2
                  + (f32(B, nc, *cl_shape),) * 2,
        grid=(B, nc // INTRA_CHUNKS),
        in_specs=[tok(width, P_QK // width), tok(width, P_MV // width), tok(G_WIDTH),
                  pl.BlockSpec((1, G_WIDTH), lambda b, c: (0, 0))],
        out_specs=(per_chunk(num_shape),) * 2 + (per_chunk(vec_shape),) * 2 + (per_chunk(cl_shape),) * 2,
        compiler_params=_params(("parallel", "parallel")),
        name="mlstm_intra",
    )(p, p, g, bias)

    fwd = lambda j: j
    bwd = lambda j: jnp.where(j < ctx_chunks, ctx_chunks - 1 - j, nc - 1 + ctx_chunks - j)
    stok = lambda cm, w, cb=0: pl.BlockSpec((B, M_CHUNK, w), lambda j: (0, cm(j), cb))
    schunk = lambda cm, s: pl.BlockSpec((B, 1) + s, lambda j: (0, cm(j)) + (0,) * len(s))
    side = lambda cm: [stok(cm, width, P_QK // width), schunk(cm, num_shape), schunk(cm, vec_shape),
                       schunk(cm, cl_shape)]
    chains = 2 * B * M_HEADS
    return pl.pallas_call(
        functools.partial(_mlstm_scan_kernel, n_batch=B),
        out_shape=(jax.ShapeDtypeStruct((B, L, width), BF16),) * 2,
        grid=(nc,),
        in_specs=side(fwd) + side(bwd),
        out_specs=(stok(fwd, width), stok(bwd, width)),
        scratch_shapes=[pltpu.VMEM((chains, C_ROWS, M_DK), F32),
                        pltpu.VMEM((chains, 1, LANES), F32)],
        compiler_params=_params(("arbitrary",)),
        name="mlstm_scan",
    )(p, numf, vecf, clf, p, numb, vecb, clb)


def _head_norm_rope(xs, ws, cos, sin, bd):
    sqs = [x * x for x in xs]
    his = [sq.astype(BF16) for sq in sqs]
    los = [(sq - hi.astype(F32)).astype(BF16) for sq, hi in zip(sqs, his)]
    mss = [jnp.dot(hi, bd, preferred_element_type=F32) + jnp.dot(lo, bd, preferred_element_type=F32)
           for hi, lo in zip(his, los)]
    ys = [x * lax.rsqrt(ms + EPS) * w for x, ms, w in zip(xs, mss, ws)]
    lane = lax.broadcasted_iota(jnp.int32, ys[0].shape, 1)
    first_half = lane % A_DH < A_DH // 2
    partners = [jnp.where(first_half, pltpu.roll(y, LANES - A_DH // 2, axis=1), pltpu.roll(y, A_DH // 2, axis=1))
                for y in ys]
    return [y * cos + partner * sin for y, partner in zip(ys, partners)]


Q_SCALE = A_DH ** -0.5 * math.log2(math.e)


def _attn_prep_tile(q, kv, cos, sin, qw, kw, qt_ref, k_ref, vt_ref):
    r = lax.broadcasted_iota(jnp.int32, (LANES, LANES), 0) // A_DH
    c = lax.broadcasted_iota(jnp.int32, (LANES, LANES), 1) // A_DH
    bd = jnp.where(r == c, 1.0 / A_DH, 0.0).astype(BF16)
    n_pairs = A_HEADS // 2
    tiles = [q[:, pair * LANES:(pair + 1) * LANES] for pair in range(n_pairs)] + [kv[:, :LANES]]
    rotated = _head_norm_rope(tiles, [qw] * n_pairs + [kw], cos, sin, bd)
    for pair in range(n_pairs):
        qt_ref[0, pair * LANES:(pair + 1) * LANES, :] = (rotated[pair] * Q_SCALE).T.astype(BF16)
    k = rotated[n_pairs].astype(BF16)
    for kvh in range(A_KV_HEADS):
        k_ref[0, kvh, 0] = k[:, kvh * A_DH:(kvh + 1) * A_DH]
    vt = kv[:, LANES:].T.astype(BF16)
    ones = jnp.ones((VT_ROWS - A_DH, vt.shape[1]), BF16)
    for kvh in range(A_KV_HEADS):
        vt_ref[0, kvh, 0] = jnp.concatenate([vt[kvh * A_DH:(kvh + 1) * A_DH, :], ones], axis=0)


ATT_SUB = 256
ATT_PIECE = 128
VT_ROWS = A_DH + 16


def _attn_kernel(qt_ref, k_ref, vt_ref, o_ref, sa_ref, sb_ref, ma_ref, mb_ref, acc_ref,
                 *, blocks, tq, ctx_tiles, q_tile0):
    q_of = lambda g: qt_ref[0, g * A_DH:(g + 1) * A_DH, :]
    lanes = lambda g: slice(g * tq, (g + 1) * tq)
    head, mid, n_mid, tail = blocks

    def step(nxt, cur, ms):
        out = []
        for g in range(A_GROUP):
            if cur is not None:
                c0, c_subs, cs_ref, cm_ref = cur
                m_new = jnp.maximum(ms[g], cm_ref[:, lanes(g)])
                alpha = jnp.exp2(ms[g] - m_new)
            best, pv = None, None
            for r in range(max(nxt[1] if nxt else 0, cur[1] if cur else 0)):
                parts = []
                for piece in range(ATT_SUB // ATT_PIECE):
                    rows = pl.ds(piece * ATT_PIECE, ATT_PIECE)
                    buf_rows = pl.ds(r * ATT_SUB + piece * ATT_PIECE, ATT_PIECE)
                    if nxt is not None and r < nxt[1]:
                        s = jnp.dot(k_ref[0, 0, nxt[0] + r, rows, :], q_of(g), preferred_element_type=F32)
                        nxt[2][g, buf_rows, :] = s
                        top = jnp.max(s, axis=0, keepdims=True)
                        best = top if best is None else jnp.maximum(best, top)
                    if cur is not None and r < c_subs:
                        parts.append(jnp.exp2(cs_ref[g, buf_rows, :] - m_new).astype(BF16))
                if cur is not None and r < c_subs:
                    d = jnp.dot(vt_ref[0, 0, c0 + r], jnp.concatenate(parts, axis=0), preferred_element_type=F32)
                    pv = d if pv is None else pv + d
            if nxt is not None:
                nxt[3][:, lanes(g)] = best
            if cur is not None:
                acc_ref[:, lanes(g)] = alpha * acc_ref[:, lanes(g)] + pv
                out.append(m_new)
            else:
                out.append(ms[g])
        return tuple(out)

    def finish():
        o = acc_ref[0:A_DH, :] / acc_ref[A_DH:A_DH + 1, :]
        o = jnp.concatenate([o[:, lanes(g)] for g in range(A_GROUP)], axis=0)
        o_ref[0] = o.T.astype(BF16)

    acc_ref[...] = jnp.zeros_like(acc_ref)
    init = (jnp.full((1, tq), -jnp.inf, F32),) * A_GROUP
    is_ctx = pl.program_id(2) + q_tile0 < ctx_tiles
    buf_a, buf_b = (sa_ref, ma_ref), (sb_ref, mb_ref)

    mid_block = lambda i, buf: (head + mid * i, mid, *buf)

    @pl.when(is_ctx)
    def _():
        step((0, head, *buf_a), None, init)
        step(None, (0, head, *buf_a), init)
        finish()

    @pl.when(jnp.logical_not(is_ctx))
    def _():
        step((0, head, *buf_a), None, init)
        ms = step(mid_block(0, buf_b), (0, head, *buf_a), init)

        def pair(j, ms):
            ms = step(mid_block(2 * j + 1, buf_a), mid_block(2 * j, buf_b), ms)
            return step(mid_block(2 * j + 2, buf_b), mid_block(2 * j + 1, buf_a), ms)

        ms = lax.fori_loop(0, n_mid // 2 - 1, pair, ms)
        last = n_mid - 1
        ms = step(mid_block(last, buf_a), mid_block(last - 1, buf_b), ms)
        tail_block = (head + mid * n_mid, tail, *buf_b)
        ms = step(tail_block, mid_block(last, buf_a), ms)
        step(None, tail_block, ms)
        finish()


def _attention(qt, k, vt, *, q_tile0, n_ctx, tq):
    B, _, L = qt.shape
    n_sub = k.shape[2]
    head, mid = n_ctx // ATT_SUB, 3
    n_mid = (n_sub - head - 1) // mid // 2 * 2
    tail = n_sub - head - mid * n_mid
    assert k.shape[3] == ATT_SUB and n_ctx == tq == ATT_SUB and n_mid >= 2 and 1 <= tail <= mid
    width = A_GROUP * A_DH
    n = A_GROUP * tq
    s_buf, m_buf = pltpu.VMEM((A_GROUP, mid * ATT_SUB, tq), F32), pltpu.VMEM((1, n), F32)
    return pl.pallas_call(
        functools.partial(_attn_kernel, blocks=(head, mid, n_mid, tail), tq=tq, ctx_tiles=n_ctx // tq,
                          q_tile0=q_tile0),
        out_shape=jax.ShapeDtypeStruct((B, L - q_tile0 * tq, A_HEADS * A_DH), BF16),
        grid=(B, A_KV_HEADS, L // tq - q_tile0),
        in_specs=[pl.BlockSpec((1, width, tq), lambda b, kv, t: (b, kv, t + q_tile0)),
                  pl.BlockSpec((1, 1, n_sub, ATT_SUB, A_DH), lambda b, kv, t: (b, kv, 0, 0, 0)),
                  pl.BlockSpec((1, 1, n_sub, VT_ROWS, ATT_SUB), lambda b, kv, t: (b, kv, 0, 0, 0))],
        out_specs=pl.BlockSpec((1, tq, width), lambda b, kv, t: (b, t, kv)),
        scratch_shapes=[s_buf, s_buf, m_buf, m_buf, pltpu.VMEM((VT_ROWS, n), F32)],
        compiler_params=_params(("parallel", "parallel", "arbitrary")),
        name="attention",
    )(qt, k, vt)


def _mixer_out_kernel(*refs, with_router, n_stream, row_off):
    hf_ref, hb_ref, mo_ref, a_ref = refs[:4]
    mod_ref, mnw_ref, n2w_ref, wout_ref = refs[4 + n_stream:8 + n_stream]
    rest = refs[8 + n_stream:]
    hs = hf_ref[0].astype(F32) + hb_ref[0].astype(F32)
    hn = jnp.concatenate([_rms_rows(hs[:, h * M_DV:(h + 1) * M_DV]) for h in range(M_HEADS)], axis=1)
    m = hn * mnw_ref[...] * _sigmoid(mo_ref[0].astype(F32))
    y_in = jnp.concatenate([m.astype(BF16), a_ref[0]], axis=1)
    mod = mod_ref[0, 0]
    x1 = _stream_tile(refs[4:4 + n_stream], row_off) + mod[2:3] * jnp.dot(y_in, wout_ref[...],
                                                                          preferred_element_type=F32)
    h2 = _rms_rows(x1) * n2w_ref[...] * (1.0 + mod[4:5]) + mod[3:4]
    if not with_router:
        wg_ref, wu_ref, wd_ref, o_ref = rest
        hb16 = h2.astype(BF16)
        g = jnp.dot(hb16, wg_ref[...], preferred_element_type=F32)
        u = jnp.dot(hb16, wu_ref[...], preferred_element_type=F32)
        act = (g * _sigmoid(g) * u).astype(BF16)
        o_ref[0] = x1 + mod[5:6] * jnp.dot(act, wd_ref[...], preferred_element_type=F32)
        return
    router_ref, x1_ref, h2_ref, ids_ref, gates_ref = rest
    x1_ref[0] = x1
    h2_ref[0] = h2
    h_hi = h2.astype(BF16)
    h_lo = (h2 - h_hi.astype(F32)).astype(BF16)
    logits = (jnp.dot(h_hi, router_ref[0], preferred_element_type=F32)
              + jnp.dot(h_lo, router_ref[0], preferred_element_type=F32)
              + jnp.dot(h_hi, router_ref[1], preferred_element_type=F32))
    lane = lax.broadcasted_iota(jnp.int32, logits.shape, 1)
    logits = jnp.where(lane < N_EXPERTS, logits, -jnp.inf)
    m1 = jnp.max(logits, axis=-1, keepdims=True)
    i1 = jnp.min(jnp.where(logits == m1, lane, LANES), axis=-1, keepdims=True)
    rest = jnp.where(lane == i1, -jnp.inf, logits)
    m2 = jnp.max(rest, axis=-1, keepdims=True)
    i2 = jnp.min(jnp.where(rest == m2, lane, LANES), axis=-1, keepdims=True)
    e2 = jnp.exp(m2 - m1)
    g1 = 1.0 / (1.0 + e2)
    ids_ref[0] = jnp.where(lane == 0, i1, jnp.where(lane == 1, i2, -1))
    gates_ref[0] = jnp.where(lane == 0, g1, jnp.where(lane == 1, e2 * g1, 0.0))


def _mixer_out(hf, hb, p, a, stream, mods, mnw, n2w, wout, *, ffn=None, router=None, row_off=0):
    B, L = p.shape[:2]
    D = D_MODEL
    nt = L // ROW_TILE - row_off
    rin = lambda w, cb=0: pl.BlockSpec((1, ROW_TILE, w), lambda b, t: (b, t + row_off, cb))
    rout = lambda w: pl.BlockSpec((1, ROW_TILE, w), lambda b, t: (b, t, 0))
    const = lambda arr: pl.BlockSpec(arr.shape, lambda b, t: (0,) * arr.ndim, pipeline_mode=pl.Buffered(1))
    mw = M_HEADS * M_DV
    a_off = row_off - (L - a.shape[1]) // ROW_TILE
    a_spec = pl.BlockSpec((1, ROW_TILE, A_HEADS * A_DH), lambda b, t: (b, t + a_off, 0))
    s_specs, s_args = _stream_specs(stream, row_off)
    in_specs = [rin(mw), rin(mw), rin(mw, P_MO // mw), a_spec, *s_specs, _mod_spec(row_off),
                pl.BlockSpec((1, mw), lambda b, t: (0, 0)), pl.BlockSpec((1, D), lambda b, t: (0, 0)),
                const(wout)]
    args = [hf, hb, p, a, *s_args, mods, mnw.reshape(1, mw), n2w.reshape(1, D), wout]
    rows = nt * ROW_TILE
    if router is None:
        in_specs += [const(w) for w in ffn]
        args += list(ffn)
        out_shape, out_specs = jax.ShapeDtypeStruct((B, rows, D), F32), rout(D)
    else:
        in_specs.append(const(router))
        args.append(router)
        out_shape = (jax.ShapeDtypeStruct((B, rows, D), F32), jax.ShapeDtypeStruct((B, rows, D), F32),
                     jax.ShapeDtypeStruct((B, rows, LANES), jnp.int32), jax.ShapeDtypeStruct((B, rows, LANES), F32))
        out_specs = (rout(D), rout(D), rout(LANES), rout(LANES))
    return pl.pallas_call(
        functools.partial(_mixer_out_kernel, with_router=router is not None, n_stream=len(s_args), row_off=row_off),
        out_shape=out_shape,
        grid=(B, nt),
        in_specs=in_specs,
        out_specs=out_specs,
        compiler_params=_params(("parallel", "arbitrary")),
        name="mixer_out",
    )(*args)


MOE_TM = 512
MOE_FF = 1792
MOE_TD = 512
RANK_TILE = 1024
POS_TILE = 2048
ISSUE_UNROLL = 16


def _moe_rank_kernel(ids_ref, rank_ref, cnt_ref, carry_ref, before_ref):
    @pl.when(pl.program_id(0) == 0)
    def _():
        carry_ref[...] = jnp.zeros_like(carry_ref)
        r = lax.broadcasted_iota(jnp.int32, (RANK_TILE, RANK_TILE), 0)
        c = lax.broadcasted_iota(jnp.int32, (RANK_TILE, RANK_TILE), 1)
        before_ref[...] = jnp.where(c < r, 1.0, 0.0).astype(BF16)

    ids = ids_ref[...]
    lane = lax.broadcasted_iota(jnp.int32, ids.shape, 1)
    onehot = jnp.where(jnp.logical_or(lane == ids[:, 0:1], lane == ids[:, 1:2]), 1.0, 0.0)
    rank_ref[...] = jnp.dot(before_ref[...], onehot.astype(BF16), preferred_element_type=F32) + carry_ref[...]
    carry_ref[...] += jnp.sum(onehot, axis=0, keepdims=True)
    cnt_ref[...] = carry_ref[...]


def _moe_rank(ids):
    n = ids.shape[0]
    return pl.pallas_call(
        _moe_rank_kernel,
        out_shape=(jax.ShapeDtypeStruct((n, LANES), F32), jax.ShapeDtypeStruct((1, LANES), F32)),
        grid=(n // RANK_TILE,),
        in_specs=[pl.BlockSpec((RANK_TILE, LANES), lambda t: (t, 0))],
        out_specs=(pl.BlockSpec((RANK_TILE, LANES), lambda t: (t, 0)), pl.BlockSpec((1, LANES), lambda t: (0, 0))),
        scratch_shapes=[pltpu.VMEM((1, LANES), F32), pltpu.VMEM((RANK_TILE, RANK_TILE), BF16)],
        compiler_params=_params(("arbitrary",)),
        name="moe_rank",
    )(ids)


def _moe_pos_kernel(ids_ref, rank_ref, start_ref, pos_ref):
    ids = ids_ref[...]
    lane = lax.broadcasted_iota(jnp.int32, ids.shape, 1)
    tgt = start_ref[...] + rank_ref[...]
    p0 = jnp.sum(jnp.where(lane == ids[:, 0:1], tgt, 0.0), axis=-1, keepdims=True)
    p1 = jnp.sum(jnp.where(lane == ids[:, 1:2], tgt, 0.0), axis=-1, keepdims=True)
    pos_ref[...] = jnp.where(lane == 0, p0, jnp.where(lane == 1, p1, 0.0)).astype(jnp.int32)


def _moe_pos(ids, rank, start_row):
    n = ids.shape[0]
    blk = pl.BlockSpec((POS_TILE, LANES), lambda t: (t, 0))
    return pl.pallas_call(
        _moe_pos_kernel,
        out_shape=jax.ShapeDtypeStruct((n, LANES), jnp.int32),
        grid=(n // POS_TILE,),
        in_specs=[blk, blk, pl.BlockSpec((1, LANES), lambda t: (0, 0))],
        out_specs=blk,
        compiler_params=_params(("parallel",)),
        name="moe_pos",
    )(ids, rank, start_row)


def _row_copy(src, src_row, dst, dst_row, sem):
    return pltpu.make_async_copy(src.at[pl.ds(src_row, 1), :], dst.at[pl.ds(dst_row, 1), :], sem)


def _moe_dispatch_kernel(pad_ref, pos_ref, h_ref, xs_ref, zero_ref, sem):
    @pl.when(pl.program_id(0) == 0)
    def _():
        zero_ref[...] = jnp.zeros_like(zero_ref)
        fills = [pltpu.make_async_copy(
            zero_ref, xs_ref.at[pl.ds(pl.multiple_of(pad_ref[e], SUBLANES), MOE_TM + SUBLANES), :], sem)
            for e in range(N_EXPERTS)]
        for cp in fills:
            cp.start()
        for cp in fills:
            cp.wait()

        def fill_tile(j, carry):
            cp = pltpu.make_async_copy(zero_ref.at[pl.ds(0, MOE_TM), :],
                                       xs_ref.at[pl.ds(pl.multiple_of(j * MOE_TM, MOE_TM), MOE_TM), :], sem)
            cp.start()
            cp.wait()
            return carry

        lax.fori_loop(pad_ref[N_EXPERTS], xs_ref.shape[0] // MOE_TM, fill_tile, 0)

    def issue(r, carry):
        for k in range(2):
            _row_copy(h_ref, r, xs_ref, pos_ref[0, 0, 2 * r + k], sem).start(priority=k)
        return carry

    lax.fori_loop(0, MOE_TD, issue, 0, unroll=ISSUE_UNROLL)
    for k in range(2):
        pltpu.make_async_copy(h_ref, xs_ref.at[pl.ds(0, MOE_TD), :], sem).wait()


def _moe_dispatch(h, pos, fill_meta, ns):
    n, d = h.shape
    return pl.pallas_call(
        _moe_dispatch_kernel,
        out_shape=jax.ShapeDtypeStruct((ns, d), F32),
        grid_spec=pltpu.PrefetchScalarGridSpec(
            num_scalar_prefetch=1,
            grid=(n // MOE_TD,),
            in_specs=[pl.BlockSpec((1, 1, 2 * MOE_TD), lambda t, pad: (t, 0, 0), memory_space=pltpu.SMEM),
                      pl.BlockSpec((MOE_TD, d), lambda t, pad: (t, 0))],
            out_specs=pl.BlockSpec(memory_space=pl.ANY),
            scratch_shapes=[pltpu.VMEM((MOE_TM + SUBLANES, d), F32), pltpu.SemaphoreType.DMA(())]),
        compiler_params=_params(("arbitrary",)),
        name="moe_dispatch",
    )(fill_meta, pos, h)


def _moe_group_kernel(te_ref, nv_ref, xs_ref, wg_ref, wu_ref, wd_ref, ys_ref, acc_ref):
    i, f = pl.program_id(0), pl.program_id(1)
    last = pl.num_programs(1) - 1
    valid = i < nv_ref[0]

    @pl.when(valid)
    def _():
        @pl.when(f == 0)
        def _():
            acc_ref[...] = jnp.zeros_like(acc_ref)

        x = xs_ref[...].astype(BF16)
        g = jnp.dot(x, wg_ref[0], preferred_element_type=F32)
        u = jnp.dot(x, wu_ref[0], preferred_element_type=F32)
        act = (g * _sigmoid(g) * u).astype(BF16)
        acc_ref[...] += jnp.dot(act, wd_ref[0], preferred_element_type=F32)

        @pl.when(f == last)
        def _():
            ys_ref[...] = acc_ref[...]

    @pl.when(jnp.logical_and(jnp.logical_not(valid), f == last))
    def _():
        ys_ref[...] = jnp.zeros_like(ys_ref)


def _moe_group(xs, tile_expert, n_valid, wg, wu, wd):
    ns, d = xs.shape
    n_tiles = ns // MOE_TM - 1
    ff = wg.shape[2]
    live = lambda i, nv: i < nv[0]
    nf = ff // MOE_FF
    step = lambda i, f, nv: jnp.where(live(i, nv), f, nf - 1)
    return pl.pallas_call(
        _moe_group_kernel,
        out_shape=jax.ShapeDtypeStruct((n_tiles * MOE_TM, d), F32),
        grid_spec=pltpu.PrefetchScalarGridSpec(
            num_scalar_prefetch=2,
            grid=(n_tiles, nf),
            in_specs=[pl.BlockSpec((MOE_TM, d), lambda i, f, te, nv: (jnp.where(live(i, nv), i, 0), 0)),
                      pl.BlockSpec((1, d, MOE_FF), lambda i, f, te, nv: (te[i], 0, step(i, f, nv))),
                      pl.BlockSpec((1, d, MOE_FF), lambda i, f, te, nv: (te[i], 0, step(i, f, nv))),
                      pl.BlockSpec((1, MOE_FF, d), lambda i, f, te, nv: (te[i], step(i, f, nv), 0))],
            out_specs=pl.BlockSpec((MOE_TM, d), lambda i, f, te, nv: (i, 0)),
            scratch_shapes=[pltpu.VMEM((MOE_TM, d), F32)]),
        compiler_params=_params(("arbitrary", "arbitrary")),
        name="moe_group",
    )(tile_expert, n_valid, xs, wg, wu, wd)


def _moe_combine_kernel(pos_ref, x_ref, gates_ref, mod_ref, fw_ref, ys_ref, o_ref, ybuf, sem):
    def issue(r, carry):
        for k in range(2):
            _row_copy(ys_ref, pos_ref[0, 0, 2 * r + k], ybuf.at[k], r, sem).start(priority=k)
        return carry

    lax.fori_loop(0, MOE_TD, issue, 0, unroll=ISSUE_UNROLL)
    for k in range(2):
        pltpu.make_async_copy(ys_ref.at[pl.ds(0, MOE_TD), :], ybuf.at[k], sem).wait()
    gates = gates_ref[...]
    y = gates[:, 0:1] * ybuf[0] + gates[:, 1:2] * ybuf[1]
    x2 = x_ref[...] + mod_ref[0, 0][5:6] * y
    o_ref[...] = _rms_rows(x2) * fw_ref[...]


def _moe_combine(pos, x1, gates, mods, fw, ys, tokens_per_sample):
    n, d = x1.shape
    per = tokens_per_sample // MOE_TD
    return pl.pallas_call(
        _moe_combine_kernel,
        out_shape=jax.ShapeDtypeStruct((n, d), F32),
        grid=(n // MOE_TD,),
        in_specs=[pl.BlockSpec((1, 1, 2 * MOE_TD), lambda t: (t, 0, 0), memory_space=pltpu.SMEM),
                  pl.BlockSpec((MOE_TD, d), lambda t: (t, 0)),
                  pl.BlockSpec((MOE_TD, LANES), lambda t: (t, 0)),
                  pl.BlockSpec((1, 1, 6, d), lambda t: (t // per, 1, 0, 0)),
                  pl.BlockSpec((1, d), lambda t: (0, 0)),
                  pl.BlockSpec(memory_space=pl.ANY)],
        out_specs=pl.BlockSpec((MOE_TD, d), lambda t: (t, 0)),
        scratch_shapes=[pltpu.VMEM((2, MOE_TD, d), F32), pltpu.SemaphoreType.DMA(())],
        compiler_params=_params(("arbitrary",)),
        name="moe_combine",
    )(pos, x1, gates, mods, fw.reshape(1, d), ys)


def _moe(h2, ids, gates, x1, mods, wg, wu, wd, fw):
    B, T, D = x1.shape
    n = B * T
    ids, gates = ids.reshape(n, LANES), gates.reshape(n, LANES)
    rank, cnt = _moe_rank(ids)
    cnt = cnt[0, :N_EXPERTS].astype(jnp.int32)
    padded = (cnt + MOE_TM - 1) // MOE_TM * MOE_TM
    end = jnp.cumsum(padded)
    start = end - padded
    n_tiles = 2 * n // MOE_TM + N_EXPERTS
    tile_expert = jnp.minimum(jnp.sum(jnp.arange(n_tiles)[:, None] >= (end // MOE_TM)[None, :], axis=1),
                              N_EXPERTS - 1).astype(jnp.int32)
    n_valid = (end[-1:] // MOE_TM).astype(jnp.int32)
    start_row = jnp.zeros((1, LANES), F32).at[0, :N_EXPERTS].set(start.astype(F32))
    pos = _moe_pos(ids, rank, start_row)
    pos = pos[:, :2].reshape(n // MOE_TD, 1, 2 * MOE_TD)
    fill_meta = jnp.concatenate([(start + cnt) // SUBLANES * SUBLANES, n_valid]).astype(jnp.int32)
    xs = _moe_dispatch(h2.reshape(n, D), pos, fill_meta, (n_tiles + 1) * MOE_TM)
    ys = _moe_group(xs, tile_expert, n_valid, wg, wu, wd)
    return _moe_combine(pos, x1.reshape(n, D), gates, mods, fw, ys, T).reshape(B, T, D)


_ROT_PERM = np.concatenate([np.arange(0, A_DH, 2), np.arange(1, A_DH, 2)])


def _prep_w_in(w):
    o = np.cumsum([0, M_HEADS * M_DK, M_HEADS * M_DK, M_HEADS * M_DV, M_HEADS * M_DV, 4 * M_HEADS,
                   A_HEADS * A_DH, A_KV_HEADS * A_DH, A_KV_HEADS * A_DH])
    mq, mk, mv, mo, mg, aq, ak, av = [w[:, o[i]:o[i + 1]] for i in range(8)]
    qk = jnp.concatenate([jnp.concatenate([mq[:, h * M_DK:(h + 1) * M_DK] * (M_DK ** -0.5),
                                           mk[:, h * M_DK:(h + 1) * M_DK]], axis=1) for h in range(M_HEADS)], axis=1)
    perm_q = np.concatenate([h * A_DH + _ROT_PERM for h in range(A_HEADS)])
    perm_k = np.concatenate([h * A_DH + _ROT_PERM for h in range(A_KV_HEADS)])
    pad = jnp.zeros((w.shape[0], G_WIDTH - 4 * M_HEADS), w.dtype)
    return jnp.concatenate([qk, mv, mo, aq[:, perm_q], ak[:, perm_k], av, mg, pad], axis=1).astype(BF16)


def _rope_tables(n_tok, n_ctx):
    rows = n_tok // GRID_W
    row = jnp.broadcast_to(jnp.arange(rows, dtype=F32)[:, None], (rows, GRID_W)).reshape(n_tok)
    col = jnp.broadcast_to(jnp.arange(GRID_W, dtype=F32)[None, :], (rows, GRID_W)).reshape(n_tok)
    n_freq = A_DH // 4
    inv_freq = ROPE_THETA ** (-jnp.arange(n_freq, dtype=F32) / n_freq)
    ang = jnp.concatenate([row[:, None] * inv_freq, col[:, None] * inv_freq], axis=-1)
    cos, sin = jnp.cos(ang), jnp.sin(ang)
    cos = jnp.concatenate([jnp.ones((n_ctx, A_DH // 2), F32), cos], axis=0)
    sin = jnp.concatenate([jnp.zeros((n_ctx, A_DH // 2), F32), sin], axis=0)
    return jnp.tile(cos, (1, 4)), jnp.tile(jnp.concatenate([-sin, sin], axis=1), (1, 2))


def kernel(x, c, ctx, c_ctx, ada_w, ada_b, norm1_w, norm2_w, w_in, mlstm_gate_b, mlstm_norm_w, q_norm_w, k_norm_w,
           w_out, ffn_w_gate, ffn_w_up, ffn_w_down, moe_router, moe_w_gate, moe_w_up, moe_w_down, final_norm_w):
    B, T, D = x.shape
    n_ctx = ctx.shape[1]
    L = n_ctx + T
    depth = w_in.shape[0]
    assert D == D_MODEL and n_ctx == ROW_TILE and T % RANK_TILE == 0 and depth == 2
    ctx_tiles = n_ctx // ROW_TILE
    tq = ROW_TILE

    xa = (ctx, x)
    cvec =jnp.concatenate([c, c_ctx[None], jnp.zeros((8 - B - 1, D), F32)], axis=0)
    cos, sin = _rope_tables(T, n_ctx)
    out = None
    for i in range(depth):
        last = i == depth - 1
        modraw = _ada(cvec, ada_w[i], ada_b[i])
        mods = jnp.stack([jnp.broadcast_to(modraw[B].reshape(1, 6, D), (B, 6, D)),
                          modraw[:B].reshape(B, 6, D)], axis=1)
        qw = jnp.tile(q_norm_w[i][_ROT_PERM], 2).reshape(1, LANES)
        kw = jnp.tile(k_norm_w[i][_ROT_PERM], 2).reshape(1, LANES)
        p, g, qt, k, vt = _in_proj(xa, mods, norm1_w[i], _prep_w_in(w_in[i]), cos, sin, qw, kw)
        hf, hb = _mlstm(p, g, mlstm_gate_b[i], n_ctx // M_CHUNK)
        a = _attention(qt, k, vt, q_tile0=ctx_tiles if last else 0, n_ctx=n_ctx, tq=tq)
        wout = w_out[i].astype(BF16)
        if not last:
            j = i // 2
            ffn = (ffn_w_gate[j].astype(BF16), ffn_w_up[j].astype(BF16), ffn_w_down[j].astype(BF16))
            xa = _mixer_out(hf, hb, p, a, xa, mods, mlstm_norm_w[i], norm2_w[i], wout, ffn=ffn)
        else:
            j = i // 2
            router = jnp.zeros((D, LANES), F32).at[:, :N_EXPERTS].set(moe_router[j])
            router_hi = router.astype(BF16)
            router = jnp.stack([router_hi, (router - router_hi.astype(F32)).astype(BF16)])
            x1, h2, ids, gates = _mixer_out(hf, hb, p, a, xa, mods, mlstm_norm_w[i], norm2_w[i], wout,
                                            router=router, row_off=ctx_tiles)
            out = _moe(h2, ids, gates, x1, mods, moe_w_gate[j].astype(BF16), moe_w_up[j].astype(BF16),
                       moe_w_down[j].astype(BF16), final_norm_w)
    return out
```

```python
import functools
import math

import numpy as np
import jax
import jax.numpy as jnp
from jax import lax
from jax.experimental import pallas as pl
from jax.experimental.pallas import tpu as pltpu

F32 = jnp.float32
BF16 = jnp.bfloat16
HIGHEST = lax.Precision.HIGHEST

D_MODEL = 1024
GRID_W = 64
M_HEADS = 4
M_DV = 128
M_DK = 64
M_CHUNK = 128
A_HEADS = 8
A_KV_HEADS = 2
A_GROUP = A_HEADS // A_KV_HEADS
A_DH = 64
ROPE_THETA = 10000.0
N_EXPERTS = 8
EPS = 1e-6

LANES = 128
SUBLANES = 8
ROW_TILE = 256
VMEM_LIMIT = 56 * 1024 * 1024

P_QK = 0
P_MV = 512
P_MO = 1024
P_AQ = 1536
P_AKV = 2048
P_WIDTH = 2304
G_WIDTH = LANES


def _params(sem, vmem=VMEM_LIMIT, flags=None):
    return pltpu.CompilerParams(dimension_semantics=sem, vmem_limit_bytes=vmem, flags=flags)


def _sigmoid(x):
    return 1.0 / (1.0 + jnp.exp(-x))


def _rms_rows(x):
    return x * lax.rsqrt(jnp.mean(x * x, axis=-1, keepdims=True) + EPS)


def _ada_kernel(c_ref, w_ref, b_ref, o_ref):
    c = c_ref[...]
    s = c * _sigmoid(c)
    o_ref[...] = jnp.dot(s, w_ref[...], precision=HIGHEST, preferred_element_type=F32) + b_ref[...]


def _ada(cvec, w, b):
    n = w.shape[1]
    bn = 1536
    return pl.pallas_call(
        _ada_kernel,
        out_shape=jax.ShapeDtypeStruct((cvec.shape[0], n), F32),
        grid=(n // bn,),
        in_specs=[pl.BlockSpec(cvec.shape, lambda j: (0, 0)),
                  pl.BlockSpec((w.shape[0], bn), lambda j: (0, j)),
                  pl.BlockSpec((1, bn), lambda j: (0, j))],
        out_specs=pl.BlockSpec((cvec.shape[0], bn), lambda j: (0, j)),
        compiler_params=_params(("arbitrary",)),
        name="ada_mod",
    )(cvec, w, b.reshape(1, n))


def _mod_spec(off=0):
    return pl.BlockSpec((1, 1, 6, D_MODEL), lambda b, t: (b, jnp.minimum(t + off, 1), 0, 0))


def _stream_specs(stream, off=0):
    tile = (1, ROW_TILE, D_MODEL)
    if not isinstance(stream, tuple):
        return [pl.BlockSpec(tile, lambda b, t: (b, t + off, 0))], [stream]
    return ([pl.BlockSpec(tile, lambda b, t: (b, 0, 0)),
             pl.BlockSpec(tile, lambda b, t: (b, jnp.maximum(t + off - 1, 0), 0))], list(stream))


def _stream_tile(refs, off=0):
    if len(refs) == 1:
        return refs[0][0]
    return jnp.where(pl.program_id(1) + off == 0, refs[0][0], refs[1][0])


def _in_proj_kernel(*refs, n_stream):
    mod_ref, nw_ref, w_ref, cos_ref, sin_ref, qw_ref, kw_ref, p_ref, g_ref, qt_ref, k_ref, vt_ref = refs[n_stream:]
    mod = mod_ref[0, 0]
    h = _rms_rows(_stream_tile(refs[:n_stream])) * nw_ref[...] * (1.0 + mod[1:2]) + mod[0:1]
    hb = h.astype(BF16)
    r_att = jnp.dot(hb, w_ref[:, P_AQ:P_WIDTH], preferred_element_type=F32)
    _attn_prep_tile(r_att[:, :P_AKV - P_AQ], r_att[:, P_AKV - P_AQ:], cos_ref[...], sin_ref[...], qw_ref[...],
                    kw_ref[...], qt_ref, k_ref, vt_ref)
    p_ref[0] = jnp.dot(hb, w_ref[:, :P_AQ], preferred_element_type=F32).astype(BF16)
    g_ref[0] = jnp.dot(hb, w_ref[:, P_WIDTH:], preferred_element_type=F32)


def _in_proj(stream, mods, nw, wp, cos, sin, qw, kw):
    s_specs, s_args = _stream_specs(stream)
    B = s_args[0].shape[0]
    L = sum(s.shape[1] for s in s_args)
    D = D_MODEL
    nt = L // ROW_TILE
    row = lambda w: pl.BlockSpec((1, ROW_TILE, w), lambda b, t: (b, t, 0))
    vec = pl.BlockSpec((1, LANES), lambda b, t: (0, 0))
    table = pl.BlockSpec((ROW_TILE, LANES), lambda b, t: (t, 0))
    return pl.pallas_call(
        functools.partial(_in_proj_kernel, n_stream=len(s_args)),
        out_shape=(jax.ShapeDtypeStruct((B, L, P_AQ), BF16),
                   jax.ShapeDtypeStruct((B, L, G_WIDTH), F32),
                   jax.ShapeDtypeStruct((B, A_HEADS * A_DH, L), BF16),
                   jax.ShapeDtypeStruct((B, A_KV_HEADS, nt, ROW_TILE, A_DH), BF16),
                   jax.ShapeDtypeStruct((B, A_KV_HEADS, nt, VT_ROWS, ROW_TILE), BF16)),
        grid=(B, nt),
        in_specs=[*s_specs, _mod_spec(),
                  pl.BlockSpec((1, D), lambda b, t: (0, 0)),
                  pl.BlockSpec(wp.shape, lambda b, t: (0, 0), pipeline_mode=pl.Buffered(1)),
                  table, table, vec, vec],
        out_specs=(row(P_AQ), row(G_WIDTH),
                   pl.BlockSpec((1, A_HEADS * A_DH, ROW_TILE), lambda b, t: (b, 0, t)),
                   pl.BlockSpec((1, A_KV_HEADS, 1, ROW_TILE, A_DH), lambda b, t: (b, 0, t, 0, 0)),
                   pl.BlockSpec((1, A_KV_HEADS, 1, VT_ROWS, ROW_TILE), lambda b, t: (b, 0, t, 0, 0))),
        compiler_params=_params(("parallel", "arbitrary")),
        name="in_proj",
    )(*s_args, mods, nw.reshape(1, D), wp, cos, sin, qw, kw)


C_ROWS = M_DV + 16
VEC_ROWS = 24
INTRA_CHUNKS = 6

def _scan_lanes(x, reverse):
    lane = lax.broadcasted_iota(jnp.int32, x.shape, 1)
    k = 1
    while k < M_CHUNK:
        if reverse:
            x = x + jnp.where(lane < M_CHUNK - k, pltpu.roll(x, M_CHUNK - k, axis=1), 0.0)
        else:
            x = x + jnp.where(lane >= k, pltpu.roll(x, k, axis=1), 0.0)
        k *= 2
    return x


def _mlstm_intra_kernel(qk_ref, v_ref, g_ref, bias_ref, numf_ref, numb_ref, vecf_ref, vecb_ref, clf_ref, clb_ref):
    row = lax.broadcasted_iota(jnp.int32, (M_CHUNK, M_CHUNK), 0)
    col = lax.broadcasted_iota(jnp.int32, (M_CHUNK, M_CHUNK), 1)
    gate_row = lax.broadcasted_iota(jnp.int32, (16, M_CHUNK), 0)
    tail_row = lax.broadcasted_iota(jnp.int32, (C_ROWS - M_DV, M_CHUNK), 0)
    outs = ((numf_ref, vecf_ref, clf_ref, row <= col, M_CHUNK - 1),
            (numb_ref, vecb_ref, clb_ref, row >= col, 0))
    chunks = range(INTRA_CHUNKS)
    toks = [pl.ds(c * M_CHUNK, M_CHUNK) for c in chunks]

    g_row, scans, gap_cols = [], [], []
    for c in chunks:
        g = g_ref[0, toks[c], :] + bias_ref[...]
        gr = g.T[0:16, :]
        lf_row = jnp.minimum(gr, 0.0) - jnp.log1p(jnp.exp(-jnp.abs(gr)))
        sc = (_scan_lanes(lf_row, False), _scan_lanes(lf_row, True))
        gaps = gr - pltpu.roll(jnp.where(gate_row < 8, sc[0], sc[1]), 12, axis=0)
        g_row.append(gr)
        scans.append(sc)
        gap_cols.append(jnp.concatenate([gaps, jnp.zeros((M_CHUNK - 16, M_CHUNK), F32)], axis=0).T)
        for vec_ref in (vecf_ref, vecb_ref):
            vec_ref[0, c, 12 + 2 * M_HEADS:VEC_ROWS, :] = jnp.zeros((VEC_ROWS - 12 - 2 * M_HEADS, LANES), F32)

    heads = [(c, h) for c in chunks for h in range(M_HEADS)]
    ks, vts, s_raws = {}, {}, {}
    for c, h in heads:
        qk = qk_ref[0, toks[c], h * LANES:(h + 1) * LANES]
        q, ks[c, h] = qk[:, :M_DK], qk[:, M_DK:]
        vts[c, h] = v_ref[0, toks[c], h * M_DV:(h + 1) * M_DV].astype(F32).T
        s_raws[c, h] = lax.dot_general(ks[c, h], q, (((1,), (1,)), ((), ())), preferred_element_type=F32)

    units = [(c, h, d) for c, h in heads for d in range(2)]
    s_w, vws = {}, {}
    for c, h, d in units:
        _, vec_ref, _, allowed, last = outs[d]
        b_r = scans[c][d][8 * d + 4 + h:8 * d + 5 + h, :]
        i_r = g_row[c][8 * d + h:8 * d + h + 1, :]
        j = 8 * d + h
        b_end = b_r[:, last:last + 1]
        d_log = jnp.where(allowed, b_r + gap_cols[c][:, j:j + 1], -jnp.inf)
        m_intra = jnp.max(d_log, axis=0, keepdims=True)
        s = s_raws[c, h] * jnp.exp(d_log - m_intra)
        s_w[c, h, d] = s.astype(BF16)
        vec_ref[0, c, 3 * h:3 * h + 1, :] = jnp.sum(s, axis=0, keepdims=True)
        vec_ref[0, c, 3 * h + 1:3 * h + 2, :] = m_intra
        vec_ref[0, c, 3 * h + 2:3 * h + 3, :] = b_r
        w_log = b_end - b_r + i_r
        m_loc = jnp.max(w_log, axis=-1, keepdims=True)
        w_row = jnp.exp(w_log - m_loc)
        vws[c, h, d] = jnp.concatenate([vts[c, h] * w_row, jnp.where(tail_row == 0, w_row, 0.0)],
                                       axis=0).astype(BF16)
        vec_ref[0, c, 12 + 2 * h:13 + 2 * h, :] = jnp.broadcast_to(m_loc, (1, LANES))
        vec_ref[0, c, 13 + 2 * h:14 + 2 * h, :] = jnp.broadcast_to(b_end, (1, LANES))

    for c, h, d in units:
        num_ref, _, cl_ref, _, _ = outs[d]
        num_ref[0, c, h] = jnp.dot(vts[c, h].astype(BF16), s_w[c, h, d],
                                   preferred_element_type=F32).astype(BF16)
        cl_ref[0, c, h] = jnp.dot(vws[c, h, d], ks[c, h], preferred_element_type=F32)


def _mlstm_scan_kernel(*refs, n_batch):
    ins, (hf_ref, hb_ref, cn_ref, m_ref) = refs[:8], refs[8:]

    @pl.when(pl.program_id(0) == 0)
    def _():
        cn_ref[...] = jnp.zeros_like(cn_ref)
        m_ref[...] = jnp.zeros_like(m_ref)

    for d, h_ref in enumerate((hf_ref, hb_ref)):
        qk_ref, num_ref, vec_ref, cl_ref = ins[4 * d:4 * d + 4]
        for b in range(n_batch):
            for h in range(M_HEADS):
                idx = (d * n_batch + b) * M_HEADS + h
                q = qk_ref[b, :, h * LANES:h * LANES + M_DK]
                row = lambda r: vec_ref[b, 0, r:r + 1, :]
                den_i, m_i, b_r = row(3 * h), row(3 * h + 1), row(3 * h + 2)
                m_loc, b_end = row(12 + 2 * h), row(13 + 2 * h)
                m_prev = m_ref[idx]
                cn = cn_ref[idx]

                inter = b_r + m_prev
                m_t = jnp.maximum(inter, m_i)
                a = jnp.exp(inter - m_t)
                e = jnp.exp(m_i - m_t)
                cq = lax.dot_general(cn.astype(BF16), q, (((1,), (1,)), ((), ())),
                                     preferred_element_type=F32)
                den = e * den_i + a * cq[M_DV:M_DV + 1, :]
                scale = 1.0 / jnp.maximum(jnp.abs(den), jnp.exp(-m_t))
                ht = (e * num_ref[b, 0, h].astype(F32) + a * cq[0:M_DV, :]) * scale
                h_ref[b, :, h * M_DV:(h + 1) * M_DV] = ht.T.astype(BF16)

                m_new = jnp.maximum(b_end + m_prev, m_loc)
                a_s = jnp.exp(b_end + m_prev - m_new)
                s_s = jnp.exp(m_loc - m_new)
                cn_ref[idx] = a_s[:, :M_DK] * cn + s_s[:, :M_DK] * cl_ref[b, 0, h]
                m_ref[idx] = m_new


def _mlstm(p, g, gate_b, ctx_chunks):
    B, L, _ = p.shape
    nc = L // M_CHUNK
    width = M_HEADS * M_DV
    bias = jnp.zeros((1, G_WIDTH), F32).at[0, :16].set(gate_b)
    assert nc % INTRA_CHUNKS == 0
    tok = lambda w, cb=0: pl.BlockSpec((1, INTRA_CHUNKS * M_CHUNK, w), lambda b, c: (b, c, cb))
    num_shape, vec_shape, cl_shape = (M_HEADS, M_DV, M_CHUNK), (VEC_ROWS, LANES), (M_HEADS, C_ROWS, M_DK)
    per_chunk = lambda s: pl.BlockSpec((1, INTRA_CHUNKS) + s, lambda b, c: (b, c) + (0,) * len(s))
    f32 = lambda *s: jax.ShapeDtypeStruct(s, F32)
    numf, numb, vecf, vecb, clf, clb = pl.pallas_call(
        _mlstm_intra_kernel,
        out_shape=(jax.ShapeDtypeStruct((B, nc, *num_shape), BF16),) * 2 + (f32(B, nc, *vec_shape),) * 2
                  + (f32(B, nc, *cl_shape),) * 2,
        grid=(B, nc // INTRA_CHUNKS),
        in_specs=[tok(width, P_QK // width), tok(width, P_MV // width), tok(G_WIDTH),
                  pl.BlockSpec((1, G_WIDTH), lambda b, c: (0, 0))],
        out_specs=(per_chunk(num_shape),) * 2 + (per_chunk(vec_shape),) * 2 + (per_chunk(cl_shape),) * 2,
        compiler_params=_params(("parallel", "parallel")),
        name="mlstm_intra",
    )(p, p, g, bias)

    fwd = lambda j: j
    bwd = lambda j: jnp.where(j < ctx_chunks, ctx_chunks - 1 - j, nc - 1 + ctx_chunks - j)
    stok = lambda cm, w, cb=0: pl.BlockSpec((B, M_CHUNK, w), lambda j: (0, cm(j), cb))
    schunk = lambda cm, s: pl.BlockSpec((B, 1) + s, lambda j: (0, cm(j)) + (0,) * len(s))
    side = lambda cm: [stok(cm, width, P_QK // width), schunk(cm, num_shape), schunk(cm, vec_shape),
                       schunk(cm, cl_shape)]
    chains = 2 * B * M_HEADS
    return pl.pallas_call(
        functools.partial(_mlstm_scan_kernel, n_batch=B),
        out_shape=(jax.ShapeDtypeStruct((B, L, width), BF16),) * 2,
        grid=(nc,),
        in_specs=side(fwd) + side(bwd),
        out_specs=(stok(fwd, width), stok(bwd, width)),
        scratch_shapes=[pltpu.VMEM((chains, C_ROWS, M_DK), F32),
                        pltpu.VMEM((chains, 1, LANES), F32)],
        compiler_params=_params(("arbitrary",)),
        name="mlstm_scan",
    )(p, numf, vecf, clf, p, numb, vecb, clb)


def _head_norm_rope(xs, ws, cos, sin, bd):
    sqs = [x * x for x in xs]
    his = [sq.astype(BF16) for sq in sqs]
    los = [(sq - hi.astype(F32)).astype(BF16) for sq, hi in zip(sqs, his)]
    mss = [jnp.dot(hi, bd, preferred_element_type=F32) + jnp.dot(lo, bd, preferred_element_type=F32)
           for hi, lo in zip(his, los)]
    ys = [x * lax.rsqrt(ms + EPS) * w for x, ms, w in zip(xs, mss, ws)]
    lane = lax.broadcasted_iota(jnp.int32, ys[0].shape, 1)
    first_half = lane % A_DH < A_DH // 2
    partners = [jnp.where(first_half, pltpu.roll(y, LANES - A_DH // 2, axis=1), pltpu.roll(y, A_DH // 2, axis=1))
                for y in ys]
    return [y * cos + partner * sin for y, partner in zip(ys, partners)]


Q_SCALE = A_DH ** -0.5 * math.log2(math.e)


def _attn_prep_tile(q, kv, cos, sin, qw, kw, qt_ref, k_ref, vt_ref):
    r = lax.broadcasted_iota(jnp.int32, (LANES, LANES), 0) // A_DH
    c = lax.broadcasted_iota(jnp.int32, (LANES, LANES), 1) // A_DH
    bd = jnp.where(r == c, 1.0 / A_DH, 0.0).astype(BF16)
    n_pairs = A_HEADS // 2
    tiles = [q[:, pair * LANES:(pair + 1) * LANES] for pair in range(n_pairs)] + [kv[:, :LANES]]
    rotated = _head_norm_rope(tiles, [qw] * n_pairs + [kw], cos, sin, bd)
    for pair in range(n_pairs):
        qt_ref[0, pair * LANES:(pair + 1) * LANES, :] = (rotated[pair] * Q_SCALE).T.astype(BF16)
    k = rotated[n_pairs].astype(BF16)
    for kvh in range(A_KV_HEADS):
        k_ref[0, kvh, 0] = k[:, kvh * A_DH:(kvh + 1) * A_DH]
    vt = kv[:, LANES:].T.astype(BF16)
    ones = jnp.ones((VT_ROWS - A_DH, vt.shape[1]), BF16)
    for kvh in range(A_KV_HEADS):
        vt_ref[0, kvh, 0] = jnp.concatenate([vt[kvh * A_DH:(kvh + 1) * A_DH, :], ones], axis=0)


ATT_SUB = 256
ATT_PIECE = 128
VT_ROWS = A_DH + 16


def _attn_kernel(qt_ref, k_ref, vt_ref, o_ref, sa_ref, sb_ref, ma_ref, mb_ref, acc_ref,
                 *, blocks, tq, ctx_tiles, q_tile0):
    q_of = lambda g: qt_ref[0, g * A_DH:(g + 1) * A_DH, :]
    lanes = lambda g: slice(g * tq, (g + 1) * tq)
    head, mid, n_mid, tail = blocks

    def step(nxt, cur, ms):
        out = []
        for g in range(A_GROUP):
            if cur is not None:
                c0, c_subs, cs_ref, cm_ref = cur
                m_new = jnp.maximum(ms[g], cm_ref[:, lanes(g)])
                alpha = jnp.exp2(ms[g] - m_new)
            best, pv = None, None
            for r in range(max(nxt[1] if nxt else 0, cur[1] if cur else 0)):
                parts = []
                for piece in range(ATT_SUB // ATT_PIECE):
                    rows = pl.ds(piece * ATT_PIECE, ATT_PIECE)
                    buf_rows = pl.ds(r * ATT_SUB + piece * ATT_PIECE, ATT_PIECE)
                    if nxt is not None and r < nxt[1]:
                        s = jnp.dot(k_ref[0, 0, nxt[0] + r, rows, :], q_of(g), preferred_element_type=F32)
                        nxt[2][g, buf_rows, :] = s
                        top = jnp.max(s, axis=0, keepdims=True)
                        best = top if best is None else jnp.maximum(best, top)
                    if cur is not None and r < c_subs:
                        parts.append(jnp.exp2(cs_ref[g, buf_rows, :] - m_new).astype(BF16))
                if cur is not None and r < c_subs:
                    d = jnp.dot(vt_ref[0, 0, c0 + r], jnp.concatenate(parts, axis=0), preferred_element_type=F32)
                    pv = d if pv is None else pv + d
            if nxt is not None:
                nxt[3][:, lanes(g)] = best
            if cur is not None:
                acc_ref[:, lanes(g)] = alpha * acc_ref[:, lanes(g)] + pv
                out.append(m_new)
            else:
                out.append(ms[g])
        return tuple(out)

    def finish():
        o = acc_ref[0:A_DH, :] / acc_ref[A_DH:A_DH + 1, :]
        o = jnp.concatenate([o[:, lanes(g)] for g in range(A_GROUP)], axis=0)
        o_ref[0] = o.T.astype(BF16)

    acc_ref[...] = jnp.zeros_like(acc_ref)
    init = (jnp.full((1, tq), -jnp.inf, F32),) * A_GROUP
    is_ctx = pl.program_id(2) + q_tile0 < ctx_tiles
    buf_a, buf_b = (sa_ref, ma_ref), (sb_ref, mb_ref)

    mid_block = lambda i, buf: (head + mid * i, mid, *buf)

    @pl.when(is_ctx)
    def _():
        step((0, head, *buf_a), None, init)
        step(None, (0, head, *buf_a), init)
        finish()

    @pl.when(jnp.logical_not(is_ctx))
    def _():
        step((0, head, *buf_a), None, init)
        ms = step(mid_block(0, buf_b), (0, head, *buf_a), init)

        def pair(j, ms):
            ms = step(mid_block(2 * j + 1, buf_a), mid_block(2 * j, buf_b), ms)
            return step(mid_block(2 * j + 2, buf_b), mid_block(2 * j + 1, buf_a), ms)

        ms = lax.fori_loop(0, n_mid // 2 - 1, pair, ms)
        last = n_mid - 1
        ms = step(mid_block(last, buf_a), mid_block(last - 1, buf_b), ms)
        tail_block = (head + mid * n_mid, tail, *buf_b)
        ms = step(tail_block, mid_block(last, buf_a), ms)
        step(None, tail_block, ms)
        finish()


def _attention(qt, k, vt, *, q_tile0, n_ctx, tq):
    B, _, L = qt.shape
    n_sub = k.shape[2]
    head, mid = n_ctx // ATT_SUB, 3
    n_mid = (n_sub - head - 1) // mid // 2 * 2
    tail = n_sub - head - mid * n_mid
    assert k.shape[3] == ATT_SUB and n_ctx == tq == ATT_SUB and n_mid >= 2 and 1 <= tail <= mid
    width = A_GROUP * A_DH
    n = A_GROUP * tq
    s_buf, m_buf = pltpu.VMEM((A_GROUP, mid * ATT_SUB, tq), F32), pltpu.VMEM((1, n), F32)
    return pl.pallas_call(
        functools.partial(_attn_kernel, blocks=(head, mid, n_mid, tail), tq=tq, ctx_tiles=n_ctx // tq,
                          q_tile0=q_tile0),
        out_shape=jax.ShapeDtypeStruct((B, L - q_tile0 * tq, A_HEADS * A_DH), BF16),
        grid=(B, A_KV_HEADS, L // tq - q_tile0),
        in_specs=[pl.BlockSpec((1, width, tq), lambda b, kv, t: (b, kv, t + q_tile0)),
                  pl.BlockSpec((1, 1, n_sub, ATT_SUB, A_DH), lambda b, kv, t: (b, kv, 0, 0, 0)),
                  pl.BlockSpec((1, 1, n_sub, VT_ROWS, ATT_SUB), lambda b, kv, t: (b, kv, 0, 0, 0))],
        out_specs=pl.BlockSpec((1, tq, width), lambda b, kv, t: (b, t, kv)),
        scratch_shapes=[s_buf, s_buf, m_buf, m_buf, pltpu.VMEM((VT_ROWS, n), F32)],
        compiler_params=_params(("parallel", "parallel", "arbitrary")),
        name="attention",
    )(qt, k, vt)


def _mixer_out_kernel(*refs, with_router, n_stream, row_off):
    hf_ref, hb_ref, mo_ref, a_ref = refs[:4]
    mod_ref, mnw_ref, n2w_ref, wout_ref = refs[4 + n_stream:8 + n_stream]
    rest = refs[8 + n_stream:]
    hs = hf_ref[0].astype(F32) + hb_ref[0].astype(F32)
    hn = jnp.concatenate([_rms_rows(hs[:, h * M_DV:(h + 1) * M_DV]) for h in range(M_HEADS)], axis=1)
    m = hn * mnw_ref[...] * _sigmoid(mo_ref[0].astype(F32))
    y_in = jnp.concatenate([m.astype(BF16), a_ref[0]], axis=1)
    mod = mod_ref[0, 0]
    x1 = _stream_tile(refs[4:4 + n_stream], row_off) + mod[2:3] * jnp.dot(y_in, wout_ref[...],
                                                                          preferred_element_type=F32)
    h2 = _rms_rows(x1) * n2w_ref[...] * (1.0 + mod[4:5]) + mod[3:4]
    if not with_router:
        wg_ref, wu_ref, wd_ref, o_ref = rest
        hb16 = h2.astype(BF16)
        g = jnp.dot(hb16, wg_ref[...], preferred_element_type=F32)
        u = jnp.dot(hb16, wu_ref[...], preferred_element_type=F32)
        act = (g * _sigmoid(g) * u).astype(BF16)
        o_ref[0] = x1 + mod[5:6] * jnp.dot(act, wd_ref[...], preferred_element_type=F32)
        return
    router_ref, x1_ref, h2_ref, ids_ref, gates_ref = rest
    x1_ref[0] = x1
    h2_ref[0] = h2
    h_hi = h2.astype(BF16)
    h_lo = (h2 - h_hi.astype(F32)).astype(BF16)
    logits = (jnp.dot(h_hi, router_ref[0], preferred_element_type=F32)
              + jnp.dot(h_lo, router_ref[0], preferred_element_type=F32)
              + jnp.dot(h_hi, router_ref[1], preferred_element_type=F32))
    lane = lax.broadcasted_iota(jnp.int32, logits.shape, 1)
    logits = jnp.where(lane < N_EXPERTS, logits, -jnp.inf)
    m1 = jnp.max(logits, axis=-1, keepdims=True)
    i1 = jnp.min(jnp.where(logits == m1, lane, LANES), axis=-1, keepdims=True)
    rest = jnp.where(lane == i1, -jnp.inf, logits)
    m2 = jnp.max(rest, axis=-1, keepdims=True)
    i2 = jnp.min(jnp.where(rest == m2, lane, LANES), axis=-1, keepdims=True)
    e2 = jnp.exp(m2 - m1)
    g1 = 1.0 / (1.0 + e2)
    ids_ref[0] = jnp.where(lane == 0, i1, jnp.where(lane == 1, i2, -1))
    gates_ref[0] = jnp.where(lane == 0, g1, jnp.where(lane == 1, e2 * g1, 0.0))


def _mixer_out(hf, hb, p, a, stream, mods, mnw, n2w, wout, *, ffn=None, router=None, row_off=0):
    B, L = p.shape[:2]
    D = D_MODEL
    nt = L // ROW_TILE - row_off
    rin = lambda w, cb=0: pl.BlockSpec((1, ROW_TILE, w), lambda b, t: (b, t + row_off, cb))
    rout = lambda w: pl.BlockSpec((1, ROW_TILE, w), lambda b, t: (b, t, 0))
    const = lambda arr: pl.BlockSpec(arr.shape, lambda b, t: (0,) * arr.ndim, pipeline_mode=pl.Buffered(1))
    mw = M_HEADS * M_DV
    a_off = row_off - (L - a.shape[1]) // ROW_TILE
    a_spec = pl.BlockSpec((1, ROW_TILE, A_HEADS * A_DH), lambda b, t: (b, t + a_off, 0))
    s_specs, s_args = _stream_specs(stream, row_off)
    in_specs = [rin(mw), rin(mw), rin(mw, P_MO // mw), a_spec, *s_specs, _mod_spec(row_off),
                pl.BlockSpec((1, mw), lambda b, t: (0, 0)), pl.BlockSpec((1, D), lambda b, t: (0, 0)),
                const(wout)]
    args = [hf, hb, p, a, *s_args, mods, mnw.reshape(1, mw), n2w.reshape(1, D), wout]
    rows = nt * ROW_TILE
    if router is None:
        in_specs += [const(w) for w in ffn]
        args += list(ffn)
        out_shape, out_specs = jax.ShapeDtypeStruct((B, rows, D), F32), rout(D)
    else:
        in_specs.append(const(router))
        args.append(router)
        out_shape = (jax.ShapeDtypeStruct((B, rows, D), F32), jax.ShapeDtypeStruct((B, rows, D), F32),
                     jax.ShapeDtypeStruct((B, rows, LANES), jnp.int32), jax.ShapeDtypeStruct((B, rows, LANES), F32))
        out_specs = (rout(D), rout(D), rout(LANES), rout(LANES))
    return pl.pallas_call(
        functools.partial(_mixer_out_kernel, with_router=router is not None, n_stream=len(s_args), row_off=row_off),
        out_shape=out_shape,
        grid=(B, nt),
        in_specs=in_specs,
        out_specs=out_specs,
        compiler_params=_params(("parallel", "arbitrary")),
        name="mixer_out",
    )(*args)


MOE_TM = 512
MOE_FF = 1792
MOE_TD = 1024
RANK_TILE = 1024
POS_TILE = 2048
ISSUE_UNROLL = 16


def _moe_rank_kernel(ids_ref, rank_ref, cnt_ref, carry_ref, before_ref):
    @pl.when(pl.program_id(0) == 0)
    def _():
        carry_ref[...] = jnp.zeros_like(carry_ref)
        r = lax.broadcasted_iota(jnp.int32, (RANK_TILE, RANK_TILE), 0)
        c = lax.broadcasted_iota(jnp.int32, (RANK_TILE, RANK_TILE), 1)
        before_ref[...] = jnp.where(c < r, 1.0, 0.0).astype(BF16)

    ids = ids_ref[...]
    lane = lax.broadcasted_iota(jnp.int32, ids.shape, 1)
    onehot = jnp.where(jnp.logical_or(lane == ids[:, 0:1], lane == ids[:, 1:2]), 1.0, 0.0)
    rank_ref[...] = jnp.dot(before_ref[...], onehot.astype(BF16), preferred_element_type=F32) + carry_ref[...]
    carry_ref[...] += jnp.sum(onehot, axis=0, keepdims=True)
    cnt_ref[...] = carry_ref[...]


def _moe_rank(ids):
    n = ids.shape[0]
    return pl.pallas_call(
        _moe_rank_kernel,
        out_shape=(jax.ShapeDtypeStruct((n, LANES), F32), jax.ShapeDtypeStruct((1, LANES), F32)),
        grid=(n // RANK_TILE,),
        in_specs=[pl.BlockSpec((RANK_TILE, LANES), lambda t: (t, 0))],
        out_specs=(pl.BlockSpec((RANK_TILE, LANES), lambda t: (t, 0)), pl.BlockSpec((1, LANES), lambda t: (0, 0))),
        scratch_shapes=[pltpu.VMEM((1, LANES), F32), pltpu.VMEM((RANK_TILE, RANK_TILE), BF16)],
        compiler_params=_params(("arbitrary",)),
        name="moe_rank",
    )(ids)


def _moe_pos_kernel(ids_ref, rank_ref, start_ref, pos_ref):
    ids = ids_ref[...]
    lane = lax.broadcasted_iota(jnp.int32, ids.shape, 1)
    tgt = start_ref[...] + rank_ref[...]
    p0 = jnp.sum(jnp.where(lane == ids[:, 0:1], tgt, 0.0), axis=-1, keepdims=True)
    p1 = jnp.sum(jnp.where(lane == ids[:, 1:2], tgt, 0.0), axis=-1, keepdims=True)
    pos_ref[...] = jnp.where(lane == 0, p0, jnp.where(lane == 1, p1, 0.0)).astype(jnp.int32)


def _moe_pos(ids, rank, start_row):
    n = ids.shape[0]
    blk = pl.BlockSpec((POS_TILE, LANES), lambda t: (t, 0))
    return pl.pallas_call(
        _moe_pos_kernel,
        out_shape=jax.ShapeDtypeStruct((n, LANES), jnp.int32),
        grid=(n // POS_TILE,),
        in_specs=[blk, blk, pl.BlockSpec((1, LANES), lambda t: (0, 0))],
        out_specs=blk,
        compiler_params=_params(("parallel",)),
        name="moe_pos",
    )(ids, rank, start_row)


def _row_copy(src, src_row, dst, dst_row, sem):
    return pltpu.make_async_copy(src.at[pl.ds(src_row, 1), :], dst.at[pl.ds(dst_row, 1), :], sem)


def _moe_dispatch_kernel(pad_ref, pos_ref, h_ref, xs_ref, zero_ref, sem):
    @pl.when(pl.program_id(0) == 0)
    def _():
        zero_ref[...] = jnp.zeros_like(zero_ref)
        fills = [pltpu.make_async_copy(
            zero_ref, xs_ref.at[pl.ds(pl.multiple_of(pad_ref[e], SUBLANES), MOE_TM + SUBLANES), :], sem)
            for e in range(N_EXPERTS)]
        for cp in fills:
            cp.start()
        for cp in fills:
            cp.wait()

        def fill_tile(j, carry):
            cp = pltpu.make_async_copy(zero_ref.at[pl.ds(0, MOE_TM), :],
                                       xs_ref.at[pl.ds(pl.multiple_of(j * MOE_TM, MOE_TM), MOE_TM), :], sem)
            cp.start()
            cp.wait()
            return carry

        lax.fori_loop(pad_ref[N_EXPERTS], xs_ref.shape[0] // MOE_TM, fill_tile, 0)

    def issue(r, carry):
        for k in range(2):
            _row_copy(h_ref, r, xs_ref, pos_ref[0, 0, 2 * r + k], sem).start(priority=k)
        return carry

    lax.fori_loop(0, MOE_TD, issue, 0, unroll=ISSUE_UNROLL)
    for k in range(2):
        pltpu.make_async_copy(h_ref, xs_ref.at[pl.ds(0, MOE_TD), :], sem).wait()


def _moe_dispatch(h, pos, fill_meta, ns):
    n, d = h.shape
    return pl.pallas_call(
        _moe_dispatch_kernel,
        out_shape=jax.ShapeDtypeStruct((ns, d), F32),
        grid_spec=pltpu.PrefetchScalarGridSpec(
            num_scalar_prefetch=1,
            grid=(n // MOE_TD,),
            in_specs=[pl.BlockSpec((1, 1, 2 * MOE_TD), lambda t, pad: (t, 0, 0), memory_space=pltpu.SMEM),
                      pl.BlockSpec((MOE_TD, d), lambda t, pad: (t, 0))],
            out_specs=pl.BlockSpec(memory_space=pl.ANY),
            scratch_shapes=[pltpu.VMEM((MOE_TM + SUBLANES, d), F32), pltpu.SemaphoreType.DMA(())]),
        compiler_params=_params(("arbitrary",)),
        name="moe_dispatch",
    )(fill_meta, pos, h)


def _moe_group_kernel(te_ref, nv_ref, xs_ref, wg_ref, wu_ref, wd_ref, ys_ref, acc_ref):
    i, f = pl.program_id(0), pl.program_id(1)
    last = pl.num_programs(1) - 1
    valid = i < nv_ref[0]

    @pl.when(valid)
    def _():
        @pl.when(f == 0)
        def _():
            acc_ref[...] = jnp.zeros_like(acc_ref)

        x = xs_ref[...].astype(BF16)
        g = jnp.dot(x, wg_ref[0], preferred_element_type=F32)
        u = jnp.dot(x, wu_ref[0], preferred_element_type=F32)
        act = (g * _sigmoid(g) * u).astype(BF16)
        acc_ref[...] += jnp.dot(act, wd_ref[0], preferred_element_type=F32)

        @pl.when(f == last)
        def _():
            ys_ref[...] = acc_ref[...]

    @pl.when(jnp.logical_and(jnp.logical_not(valid), f == last))
    def _():
        ys_ref[...] = jnp.zeros_like(ys_ref)


def _moe_group(xs, tile_expert, n_valid, wg, wu, wd):
    ns, d = xs.shape
    n_tiles = ns // MOE_TM - 1
    ff = wg.shape[2]
    live = lambda i, nv: i < nv[0]
    nf = ff // MOE_FF
    step = lambda i, f, nv: jnp.where(live(i, nv), f, nf - 1)
    return pl.pallas_call(
        _moe_group_kernel,
        out_shape=jax.ShapeDtypeStruct((n_tiles * MOE_TM, d), F32),
        grid_spec=pltpu.PrefetchScalarGridSpec(
            num_scalar_prefetch=2,
            grid=(n_tiles, nf),
            in_specs=[pl.BlockSpec((MOE_TM, d), lambda i, f, te, nv: (jnp.where(live(i, nv), i, 0), 0)),
                      pl.BlockSpec((1, d, MOE_FF), lambda i, f, te, nv: (te[i], 0, step(i, f, nv))),
                      pl.BlockSpec((1, d, MOE_FF), lambda i, f, te, nv: (te[i], 0, step(i, f, nv))),
                      pl.BlockSpec((1, MOE_FF, d), lambda i, f, te, nv: (te[i], step(i, f, nv), 0))],
            out_specs=pl.BlockSpec((MOE_TM, d), lambda i, f, te, nv: (i, 0)),
            scratch_shapes=[pltpu.VMEM((MOE_TM, d), F32)]),
        compiler_params=_params(("arbitrary", "arbitrary")),
        name="moe_group",
    )(tile_expert, n_valid, xs, wg, wu, wd)


def _moe_combine_kernel(pos_ref, x_ref, gates_ref, mod_ref, fw_ref, ys_ref, o_ref, ybuf, sem):
    def issue(r, carry):
        for k in range(2):
            _row_copy(ys_ref, pos_ref[0, 0, 2 * r + k], ybuf.at[k], r, sem).start(priority=k)
        return carry

    lax.fori_loop(0, MOE_TD, issue, 0, unroll=ISSUE_UNROLL)
    for k in range(2):
        pltpu.make_async_copy(ys_ref.at[pl.ds(0, MOE_TD), :], ybuf.at[k], sem).wait()
    gates = gates_ref[...]
    y = gates[:, 0:1] * ybuf[0] + gates[:, 1:2] * ybuf[1]
    x2 = x_ref[...] + mod_ref[0, 0][5:6] * y
    o_ref[...] = _rms_rows(x2) * fw_ref[...]


def _moe_combine(pos, x1, gates, mods, fw, ys, tokens_per_sample):
    n, d = x1.shape
    per = tokens_per_sample // MOE_TD
    return pl.pallas_call(
        _moe_combine_kernel,
        out_shape=jax.ShapeDtypeStruct((n, d), F32),
        grid=(n // MOE_TD,),
        in_specs=[pl.BlockSpec((1, 1, 2 * MOE_TD), lambda t: (t, 0, 0), memory_space=pltpu.SMEM),
                  pl.BlockSpec((MOE_TD, d), lambda t: (t, 0)),
                  pl.BlockSpec((MOE_TD, LANES), lambda t: (t, 0)),
                  pl.BlockSpec((1, 1, 6, d), lambda t: (t // per, 1, 0, 0)),
                  pl.BlockSpec((1, d), lambda t: (0, 0)),
                  pl.BlockSpec(memory_space=pl.ANY)],
        out_specs=pl.BlockSpec((MOE_TD, d), lambda t: (t, 0)),
        scratch_shapes=[pltpu.VMEM((2, MOE_TD, d), F32), pltpu.SemaphoreType.DMA(())],
        compiler_params=_params(("arbitrary",)),
        name="moe_combine",
    )(pos, x1, gates, mods, fw.reshape(1, d), ys)


def _moe(h2, ids, gates, x1, mods, wg, wu, wd, fw):
    B, T, D = x1.shape
    n = B * T
    ids, gates = ids.reshape(n, LANES), gates.reshape(n, LANES)
    rank, cnt = _moe_rank(ids)
    cnt = cnt[0, :N_EXPERTS].astype(jnp.int32)
    padded = (cnt + MOE_TM - 1) // MOE_TM * MOE_TM
    end = jnp.cumsum(padded)
    start = end - padded
    n_tiles = 2 * n // MOE_TM + N_EXPERTS
    tile_expert = jnp.minimum(jnp.sum(jnp.arange(n_tiles)[:, None] >= (end // MOE_TM)[None, :], axis=1),
                              N_EXPERTS - 1).astype(jnp.int32)
    n_valid = (end[-1:] // MOE_TM).astype(jnp.int32)
    start_row = jnp.zeros((1, LANES), F32).at[0, :N_EXPERTS].set(start.astype(F32))
    pos = _moe_pos(ids, rank, start_row)
    pos = pos[:, :2].reshape(n // MOE_TD, 1, 2 * MOE_TD)
    fill_meta = jnp.concatenate([(start + cnt) // SUBLANES * SUBLANES, n_valid]).astype(jnp.int32)
    xs = _moe_dispatch(h2.reshape(n, D), pos, fill_meta, (n_tiles + 1) * MOE_TM)
    ys = _moe_group(xs, tile_expert, n_valid, wg, wu, wd)
    return _moe_combine(pos, x1.reshape(n, D), gates, mods, fw, ys, T).reshape(B, T, D)


_ROT_PERM = np.concatenate([np.arange(0, A_DH, 2), np.arange(1, A_DH, 2)])


def _prep_w_in(w):
    o = np.cumsum([0, M_HEADS * M_DK, M_HEADS * M_DK, M_HEADS * M_DV, M_HEADS * M_DV, 4 * M_HEADS,
                   A_HEADS * A_DH, A_KV_HEADS * A_DH, A_KV_HEADS * A_DH])
    mq, mk, mv, mo, mg, aq, ak, av = [w[:, o[i]:o[i + 1]] for i in range(8)]
    qk = jnp.concatenate([jnp.concatenate([mq[:, h * M_DK:(h + 1) * M_DK] * (M_DK ** -0.5),
                                           mk[:, h * M_DK:(h + 1) * M_DK]], axis=1) for h in range(M_HEADS)], axis=1)
    perm_q = np.concatenate([h * A_DH + _ROT_PERM for h in range(A_HEADS)])
    perm_k = np.concatenate([h * A_DH + _ROT_PERM for h in range(A_KV_HEADS)])
    pad = jnp.zeros((w.shape[0], G_WIDTH - 4 * M_HEADS), w.dtype)
    return jnp.concatenate([qk, mv, mo, aq[:, perm_q], ak[:, perm_k], av, mg, pad], axis=1).astype(BF16)


def _rope_tables(n_tok, n_ctx):
    rows = n_tok // GRID_W
    row = jnp.broadcast_to(jnp.arange(rows, dtype=F32)[:, None], (rows, GRID_W)).reshape(n_tok)
    col = jnp.broadcast_to(jnp.arange(GRID_W, dtype=F32)[None, :], (rows, GRID_W)).reshape(n_tok)
    n_freq = A_DH // 4
    inv_freq = ROPE_THETA ** (-jnp.arange(n_freq, dtype=F32) / n_freq)
    ang = jnp.concatenate([row[:, None] * inv_freq, col[:, None] * inv_freq], axis=-1)
    cos, sin = jnp.cos(ang), jnp.sin(ang)
    cos = jnp.concatenate([jnp.ones((n_ctx, A_DH // 2), F32), cos], axis=0)
    sin = jnp.concatenate([jnp.zeros((n_ctx, A_DH // 2), F32), sin], axis=0)
    return jnp.tile(cos, (1, 4)), jnp.tile(jnp.concatenate([-sin, sin], axis=1), (1, 2))


def kernel(x, c, ctx, c_ctx, ada_w, ada_b, norm1_w, norm2_w, w_in, mlstm_gate_b, mlstm_norm_w, q_norm_w, k_norm_w,
           w_out, ffn_w_gate, ffn_w_up, ffn_w_down, moe_router, moe_w_gate, moe_w_up, moe_w_down, final_norm_w):
    B, T, D = x.shape
    n_ctx = ctx.shape[1]
    L = n_ctx + T
    depth = w_in.shape[0]
    assert D == D_MODEL and n_ctx == ROW_TILE and T % RANK_TILE == 0 and depth == 2
    ctx_tiles = n_ctx // ROW_TILE
    tq = ROW_TILE

    xa = (ctx, x)
    cvec =jnp.concatenate([c, c_ctx[None], jnp.zeros((8 - B - 1, D), F32)], axis=0)
    cos, sin = _rope_tables(T, n_ctx)
    out = None
    for i in range(depth):
        last = i == depth - 1
        modraw = _ada(cvec, ada_w[i], ada_b[i])
        mods = jnp.stack([jnp.broadcast_to(modraw[B].reshape(1, 6, D), (B, 6, D)),
                          modraw[:B].reshape(B, 6, D)], axis=1)
        qw = jnp.tile(q_norm_w[i][_ROT_PERM], 2).reshape(1, LANES)
        kw = jnp.tile(k_norm_w[i][_ROT_PERM], 2).reshape(1, LANES)
        p, g, qt, k, vt = _in_proj(xa, mods, norm1_w[i], _prep_w_in(w_in[i]), cos, sin, qw, kw)
        hf, hb = _mlstm(p, g, mlstm_gate_b[i], n_ctx // M_CHUNK)
        a = _attention(qt, k, vt, q_tile0=ctx_tiles if last else 0, n_ctx=n_ctx, tq=tq)
        wout = w_out[i].astype(BF16)
        if not last:
            j = i // 2
            ffn = (ffn_w_gate[j].astype(BF16), ffn_w_up[j].astype(BF16), ffn_w_down[j].astype(BF16))
            xa = _mixer_out(hf, hb, p, a, xa, mods, mlstm_norm_w[i], norm2_w[i], wout, ffn=ffn)
        else:
            j = i // 2
            router = jnp.zeros((D, LANES), F32).at[:, :N_EXPERTS].set(moe_router[j])
            router_hi = router.astype(BF16)
            router = jnp.stack([router_hi, (router - router_hi.astype(F32)).astype(BF16)])
            x1, h2, ids, gates = _mixer_out(hf, hb, p, a, xa, mods, mlstm_norm_w[i], norm2_w[i], wout,
                                            router=router, row_off=ctx_tiles)
            out = _moe(h2, ids, gates, x1, mods, moe_w_gate[j].astype(BF16), moe_w_up[j].astype(BF16),
                       moe_w_down[j].astype(BF16), final_norm_w)
    return out
```

```python
import functools
import math

import numpy as np
import jax
import jax.numpy as jnp
from jax import lax
from jax.experimental import pallas as pl
from jax.experimental.pallas import tpu as pltpu

F32 = jnp.float32
BF16 = jnp.bfloat16
HIGHEST = lax.Precision.HIGHEST

D_MODEL = 1024
GRID_W = 64
M_HEADS = 4
M_DV = 128
M_DK = 64
M_CHUNK = 128
A_HEADS = 8
A_KV_HEADS = 2
A_GROUP = A_HEADS // A_KV_HEADS
A_DH = 64
ROPE_THETA = 10000.0
N_EXPERTS = 8
EPS = 1e-6

LANES = 128
SUBLANES = 8
ROW_TILE = 256
VMEM_LIMIT = 56 * 1024 * 1024

P_QK = 0
P_MV = 512
P_MO = 1024
P_AQ = 1536
P_AKV = 2048
P_WIDTH = 2304
G_WIDTH = LANES


def _params(sem, vmem=VMEM_LIMIT, flags=None):
    return pltpu.CompilerParams(dimension_semantics=sem, vmem_limit_bytes=vmem, flags=flags)


def _sigmoid(x):
    return 1.0 / (1.0 + jnp.exp(-x))


def _rms_rows(x):
    return x * lax.rsqrt(jnp.mean(x * x, axis=-1, keepdims=True) + EPS)


def _ada_kernel(c_ref, w_ref, b_ref, o_ref):
    c = c_ref[...]
    s = c * _sigmoid(c)
    o_ref[...] = jnp.dot(s, w_ref[...], precision=HIGHEST, preferred_element_type=F32) + b_ref[...]


def _ada(cvec, w, b):
    n = w.shape[1]
    bn = 1536
    return pl.pallas_call(
        _ada_kernel,
        out_shape=jax.ShapeDtypeStruct((cvec.shape[0], n), F32),
        grid=(n // bn,),
        in_specs=[pl.BlockSpec(cvec.shape, lambda j: (0, 0)),
                  pl.BlockSpec((w.shape[0], bn), lambda j: (0, j)),
                  pl.BlockSpec((1, bn), lambda j: (0, j))],
        out_specs=pl.BlockSpec((cvec.shape[0], bn), lambda j: (0, j)),
        compiler_params=_params(("arbitrary",)),
        name="ada_mod",
    )(cvec, w, b.reshape(1, n))


def _mod_spec(off=0):
    return pl.BlockSpec((1, 1, 6, D_MODEL), lambda b, t: (b, jnp.minimum(t + off, 1), 0, 0))


def _stream_specs(stream, off=0):
    tile = (1, ROW_TILE, D_MODEL)
    if not isinstance(stream, tuple):
        return [pl.BlockSpec(tile, lambda b, t: (b, t + off, 0))], [stream]
    return ([pl.BlockSpec(tile, lambda b, t: (b, 0, 0)),
             pl.BlockSpec(tile, lambda b, t: (b, jnp.maximum(t + off - 1, 0), 0))], list(stream))


def _stream_tile(refs, off=0):
    if len(refs) == 1:
        return refs[0][0]
    return jnp.where(pl.program_id(1) + off == 0, refs[0][0], refs[1][0])


def _in_proj_kernel(*refs, n_stream):
    mod_ref, nw_ref, w_ref, cos_ref, sin_ref, qw_ref, kw_ref, p_ref, g_ref, qt_ref, k_ref, vt_ref = refs[n_stream:]
    mod = mod_ref[0, 0]
    h = _rms_rows(_stream_tile(refs[:n_stream])) * nw_ref[...] * (1.0 + mod[1:2]) + mod[0:1]
    hb = h.astype(BF16)
    r_att = jnp.dot(hb, w_ref[:, P_AQ:P_WIDTH], preferred_element_type=F32)
    _attn_prep_tile(r_att[:, :P_AKV - P_AQ], r_att[:, P_AKV - P_AQ:], cos_ref[...], sin_ref[...], qw_ref[...],
                    kw_ref[...], qt_ref, k_ref, vt_ref)
    p_ref[0] = jnp.dot(hb, w_ref[:, :P_AQ], preferred_element_type=F32).astype(BF16)
    g_ref[0] = jnp.dot(hb, w_ref[:, P_WIDTH:], preferred_element_type=F32)


def _in_proj(stream, mods, nw, wp, cos, sin, qw, kw):
    s_specs, s_args = _stream_specs(stream)
    B = s_args[0].shape[0]
    L = sum(s.shape[1] for s in s_args)
    D = D_MODEL
    nt = L // ROW_TILE
    row = lambda w: pl.BlockSpec((1, ROW_TILE, w), lambda b, t: (b, t, 0))
    vec = pl.BlockSpec((1, LANES), lambda b, t: (0, 0))
    table = pl.BlockSpec((ROW_TILE, LANES), lambda b, t: (t, 0))
    return pl.pallas_call(
        functools.partial(_in_proj_kernel, n_stream=len(s_args)),
        out_shape=(jax.ShapeDtypeStruct((B, L, P_AQ), BF16),
                   jax.ShapeDtypeStruct((B, L, G_WIDTH), F32),
                   jax.ShapeDtypeStruct((B, A_HEADS * A_DH, L), BF16),
                   jax.ShapeDtypeStruct((B, A_KV_HEADS, nt, ROW_TILE, A_DH), BF16),
                   jax.ShapeDtypeStruct((B, A_KV_HEADS, nt, VT_ROWS, ROW_TILE), BF16)),
        grid=(B, nt),
        in_specs=[*s_specs, _mod_spec(),
                  pl.BlockSpec((1, D), lambda b, t: (0, 0)),
                  pl.BlockSpec(wp.shape, lambda b, t: (0, 0), pipeline_mode=pl.Buffered(1)),
                  table, table, vec, vec],
        out_specs=(row(P_AQ), row(G_WIDTH),
                   pl.BlockSpec((1, A_HEADS * A_DH, ROW_TILE), lambda b, t: (b, 0, t)),
                   pl.BlockSpec((1, A_KV_HEADS, 1, ROW_TILE, A_DH), lambda b, t: (b, 0, t, 0, 0)),
                   pl.BlockSpec((1, A_KV_HEADS, 1, VT_ROWS, ROW_TILE), lambda b, t: (b, 0, t, 0, 0))),
        compiler_params=_params(("parallel", "arbitrary")),
        name="in_proj",
    )(*s_args, mods, nw.reshape(1, D), wp, cos, sin, qw, kw)


C_ROWS = M_DV + 16
VEC_ROWS = 24
INTRA_CHUNKS = 6

def _scan_lanes(x, reverse):
    lane = lax.broadcasted_iota(jnp.int32, x.shape, 1)
    k = 1
    while k < M_CHUNK:
        if reverse:
            x = x + jnp.where(lane < M_CHUNK - k, pltpu.roll(x, M_CHUNK - k, axis=1), 0.0)
        else:
            x = x + jnp.where(lane >= k, pltpu.roll(x, k, axis=1), 0.0)
        k *= 2
    return x


def _mlstm_intra_kernel(qk_ref, v_ref, g_ref, bias_ref, numf_ref, numb_ref, vecf_ref, vecb_ref, clf_ref, clb_ref):
    row = lax.broadcasted_iota(jnp.int32, (M_CHUNK, M_CHUNK), 0)
    col = lax.broadcasted_iota(jnp.int32, (M_CHUNK, M_CHUNK), 1)
    gate_row = lax.broadcasted_iota(jnp.int32, (16, M_CHUNK), 0)
    tail_row = lax.broadcasted_iota(jnp.int32, (C_ROWS - M_DV, M_CHUNK), 0)
    outs = ((numf_ref, vecf_ref, clf_ref, row <= col, M_CHUNK - 1),
            (numb_ref, vecb_ref, clb_ref, row >= col, 0))
    chunks = range(INTRA_CHUNKS)
    toks = [pl.ds(c * M_CHUNK, M_CHUNK) for c in chunks]

    g_row, scans, gap_cols = [], [], []
    for c in chunks:
        g = g_ref[0, toks[c], :] + bias_ref[...]
        gr = g.T[0:16, :]
        lf_row = jnp.minimum(gr, 0.0) - jnp.log1p(jnp.exp(-jnp.abs(gr)))
        sc = (_scan_lanes(lf_row, False), _scan_lanes(lf_row, True))
        gaps = gr - pltpu.roll(jnp.where(gate_row < 8, sc[0], sc[1]), 12, axis=0)
        g_row.append(gr)
        scans.append(sc)
        gap_cols.append(jnp.concatenate([gaps, jnp.zeros((M_CHUNK - 16, M_CHUNK), F32)], axis=0).T)
        for vec_ref in (vecf_ref, vecb_ref):
            vec_ref[0, c, 12 + 2 * M_HEADS:VEC_ROWS, :] = jnp.zeros((VEC_ROWS - 12 - 2 * M_HEADS, LANES), F32)

    heads = [(c, h) for c in chunks for h in range(M_HEADS)]
    ks, vts, s_raws = {}, {}, {}
    for c, h in heads:
        qk = qk_ref[0, toks[c], h * LANES:(h + 1) * LANES]
        q, ks[c, h] = qk[:, :M_DK], qk[:, M_DK:]
        vts[c, h] = v_ref[0, toks[c], h * M_DV:(h + 1) * M_DV].astype(F32).T
        s_raws[c, h] = lax.dot_general(ks[c, h], q, (((1,), (1,)), ((), ())), preferred_element_type=F32)

    units = [(c, h, d) for c, h in heads for d in range(2)]
    s_w, vws = {}, {}
    for c, h, d in units:
        _, vec_ref, _, allowed, last = outs[d]
        b_r = scans[c][d][8 * d + 4 + h:8 * d + 5 + h, :]
        i_r = g_row[c][8 * d + h:8 * d + h + 1, :]
        j = 8 * d + h
        b_end = b_r[:, last:last + 1]
        d_log = jnp.where(allowed, b_r + gap_cols[c][:, j:j + 1], -jnp.inf)
        m_intra = jnp.max(d_log, axis=0, keepdims=True)
        s = s_raws[c, h] * jnp.exp(d_log - m_intra)
        s_w[c, h, d] = s.astype(BF16)
        vec_ref[0, c, 3 * h:3 * h + 1, :] = jnp.sum(s, axis=0, keepdims=True)
        vec_ref[0, c, 3 * h + 1:3 * h + 2, :] = m_intra
        vec_ref[0, c, 3 * h + 2:3 * h + 3, :] = b_r
        w_log = b_end - b_r + i_r
        m_loc = jnp.max(w_log, axis=-1, keepdims=True)
        w_row = jnp.exp(w_log - m_loc)
        vws[c, h, d] = jnp.concatenate([vts[c, h] * w_row, jnp.where(tail_row == 0, w_row, 0.0)],
                                       axis=0).astype(BF16)
        vec_ref[0, c, 12 + 2 * h:13 + 2 * h, :] = jnp.broadcast_to(m_loc, (1, LANES))
        vec_ref[0, c, 13 + 2 * h:14 + 2 * h, :] = jnp.broadcast_to(b_end, (1, LANES))

    for c, h, d in units:
        num_ref, _, cl_ref, _, _ = outs[d]
        num_ref[0, c, h] = jnp.dot(vts[c, h].astype(BF16), s_w[c, h, d],
                                   preferred_element_type=F32).astype(BF16)
        cl_ref[0, c, h] = jnp.dot(vws[c, h, d], ks[c, h], preferred_element_type=F32)


def _mlstm_scan_kernel(*refs, n_batch):
    ins, (hf_ref, hb_ref, cn_ref, m_ref) = refs[:8], refs[8:]

    @pl.when(pl.program_id(0) == 0)
    def _():
        cn_ref[...] = jnp.zeros_like(cn_ref)
        m_ref[...] = jnp.zeros_like(m_ref)

    for d, h_ref in enumerate((hf_ref, hb_ref)):
        qk_ref, num_ref, vec_ref, cl_ref = ins[4 * d:4 * d + 4]
        for b in range(n_batch):
            for h in range(M_HEADS):
                idx = (d * n_batch + b) * M_HEADS + h
                q = qk_ref[b, :, h * LANES:h * LANES + M_DK]
                row = lambda r: vec_ref[b, 0, r:r + 1, :]
                den_i, m_i, b_r = row(3 * h), row(3 * h + 1), row(3 * h + 2)
                m_loc, b_end = row(12 + 2 * h), row(13 + 2 * h)
                m_prev = m_ref[idx]
                cn = cn_ref[idx]

                inter = b_r + m_prev
                m_t = jnp.maximum(inter, m_i)
                a = jnp.exp(inter - m_t)
                e = jnp.exp(m_i - m_t)
                cq = lax.dot_general(cn.astype(BF16), q, (((1,), (1,)), ((), ())),
                                     preferred_element_type=F32)
                den = e * den_i + a * cq[M_DV:M_DV + 1, :]
                scale = 1.0 / jnp.maximum(jnp.abs(den), jnp.exp(-m_t))
                ht = (e * num_ref[b, 0, h].astype(F32) + a * cq[0:M_DV, :]) * scale
                h_ref[b, :, h * M_DV:(h + 1) * M_DV] = ht.T.astype(BF16)

                m_new = jnp.maximum(b_end + m_prev, m_loc)
                a_s = jnp.exp(b_end + m_prev - m_new)
                s_s = jnp.exp(m_loc - m_new)
                cn_ref[idx] = a_s[:, :M_DK] * cn + s_s[:, :M_DK] * cl_ref[b, 0, h]
                m_ref[idx] = m_new


def _mlstm(p, g, gate_b, ctx_chunks):
    B, L, _ = p.shape
    nc = L // M_CHUNK
    width = M_HEADS * M_DV
    bias = jnp.zeros((1, G_WIDTH), F32).at[0, :16].set(gate_b)
    assert nc % INTRA_CHUNKS == 0
    tok = lambda w, cb=0: pl.BlockSpec((1, INTRA_CHUNKS * M_CHUNK, w), lambda b, c: (b, c, cb))
    num_shape, vec_shape, cl_shape = (M_HEADS, M_DV, M_CHUNK), (VEC_ROWS, LANES), (M_HEADS, C_ROWS, M_DK)
    per_chunk = lambda s: pl.BlockSpec((1, INTRA_CHUNKS) + s, lambda b, c: (b, c) + (0,) * len(s))
    f32 = lambda *s: jax.ShapeDtypeStruct(s, F32)
    numf, numb, vecf, vecb, clf, clb = pl.pallas_call(
        _mlstm_intra_kernel,
        out_shape=(jax.ShapeDtypeStruct((B, nc, *num_shape), BF16),) * 2 + (f32(B, nc, *vec_shape),) * 2
                  + (f32(B, nc, *cl_shape),) * 2,
        grid=(B, nc // INTRA_CHUNKS),
        in_specs=[tok(width, P_QK // width), tok(width, P_MV // width), tok(G_WIDTH),
                  pl.BlockSpec((1, G_WIDTH), lambda b, c: (0, 0))],
        out_specs=(per_chunk(num_shape),) * 2 + (per_chunk(vec_shape),) * 2 + (per_chunk(cl_shape),) * 2,
        compiler_params=_params(("parallel", "parallel")),
        name="mlstm_intra",
    )(p, p, g, bias)

    fwd = lambda j: j
    bwd = lambda j: jnp.where(j < ctx_chunks, ctx_chunks - 1 - j, nc - 1 + ctx_chunks - j)
    stok = lambda cm, w, cb=0: pl.BlockSpec((B, M_CHUNK, w), lambda j: (0, cm(j), cb))
    schunk = lambda cm, s: pl.BlockSpec((B, 1) + s, lambda j: (0, cm(j)) + (0,) * len(s))
    side = lambda cm: [stok(cm, width, P_QK // width), schunk(cm, num_shape), schunk(cm, vec_shape),
                       schunk(cm, cl_shape)]
    chains = 2 * B * M_HEADS
    return pl.pallas_call(
        functools.partial(_mlstm_scan_kernel, n_batch=B),
        out_shape=(jax.ShapeDtypeStruct((B, L, width), BF16),) * 2,
        grid=(nc,),
        in_specs=side(fwd) + side(bwd),
        out_specs=(stok(fwd, width), stok(bwd, width)),
        scratch_shapes=[pltpu.VMEM((chains, C_ROWS, M_DK), F32),
                        pltpu.VMEM((chains, 1, LANES), F32)],
        compiler_params=_params(("arbitrary",)),
        name="mlstm_scan",
    )(p, numf, vecf, clf, p, numb, vecb, clb)


def _head_norm_rope(xs, ws, cos, sin, bd):
    sqs = [x * x for x in xs]
    his = [sq.astype(BF16) for sq in sqs]
    los = [(sq - hi.astype(F32)).astype(BF16) for sq, hi in zip(sqs, his)]
    mss = [jnp.dot(hi, bd, preferred_element_type=F32) + jnp.dot(lo, bd, preferred_element_type=F32)
           for hi, lo in zip(his, los)]
    ys = [x * lax.rsqrt(ms + EPS) * w for x, ms, w in zip(xs, mss, ws)]
    lane = lax.broadcasted_iota(jnp.int32, ys[0].shape, 1)
    first_half = lane % A_DH < A_DH // 2
    partners = [jnp.where(first_half, pltpu.roll(y, LANES - A_DH // 2, axis=1), pltpu.roll(y, A_DH // 2, axis=1))
                for y in ys]
    return [y * cos + partner * sin for y, partner in zip(ys, partners)]


Q_SCALE = A_DH ** -0.5 * math.log2(math.e)


def _attn_prep_tile(q, kv, cos, sin, qw, kw, qt_ref, k_ref, vt_ref):
    r = lax.broadcasted_iota(jnp.int32, (LANES, LANES), 0) // A_DH
    c = lax.broadcasted_iota(jnp.int32, (LANES, LANES), 1) // A_DH
    bd = jnp.where(r == c, 1.0 / A_DH, 0.0).astype(BF16)
    n_pairs = A_HEADS // 2
    tiles = [q[:, pair * LANES:(pair + 1) * LANES] for pair in range(n_pairs)] + [kv[:, :LANES]]
    rotated = _head_norm_rope(tiles, [qw] * n_pairs + [kw], cos, sin, bd)
    for pair in range(n_pairs):
        qt_ref[0, pair * LANES:(pair + 1) * LANES, :] = (rotated[pair] * Q_SCALE).T.astype(BF16)
    k = rotated[n_pairs].astype(BF16)
    for kvh in range(A_KV_HEADS):
        k_ref[0, kvh, 0] = k[:, kvh * A_DH:(kvh + 1) * A_DH]
    vt = kv[:, LANES:].T.astype(BF16)
    ones = jnp.ones((VT_ROWS - A_DH, vt.shape[1]), BF16)
    for kvh in range(A_KV_HEADS):
        vt_ref[0, kvh, 0] = jnp.concatenate([vt[kvh * A_DH:(kvh + 1) * A_DH, :], ones], axis=0)


ATT_SUB = 256
ATT_PIECE = 128
VT_ROWS = A_DH + 16


def _attn_kernel(qt_ref, k_ref, vt_ref, o_ref, sa_ref, sb_ref, ma_ref, mb_ref, acc_ref,
                 *, blocks, tq, ctx_tiles, q_tile0):
    q_of = lambda g: qt_ref[0, g * A_DH:(g + 1) * A_DH, :]
    lanes = lambda g: slice(g * tq, (g + 1) * tq)
    head, mid, n_mid, tail = blocks

    def step(nxt, cur, ms):
        out = []
        for g in range(A_GROUP):
            if cur is not None:
                c0, c_subs, cs_ref, cm_ref = cur
                m_new = jnp.maximum(ms[g], cm_ref[:, lanes(g)])
                alpha = jnp.exp2(ms[g] - m_new)
            best, pv = None, None
            for r in range(max(nxt[1] if nxt else 0, cur[1] if cur else 0)):
                parts = []
                for piece in range(ATT_SUB // ATT_PIECE):
                    rows = pl.ds(piece * ATT_PIECE, ATT_PIECE)
                    buf_rows = pl.ds(r * ATT_SUB + piece * ATT_PIECE, ATT_PIECE)
                    if nxt is not None and r < nxt[1]:
                        s = jnp.dot(k_ref[0, 0, nxt[0] + r, rows, :], q_of(g), preferred_element_type=F32)
                        nxt[2][g, buf_rows, :] = s
                        top = jnp.max(s, axis=0, keepdims=True)
                        best = top if best is None else jnp.maximum(best, top)
                    if cur is not None and r < c_subs:
                        parts.append(jnp.exp2(cs_ref[g, buf_rows, :] - m_new).astype(BF16))
                if cur is not None and r < c_subs:
                    d = jnp.dot(vt_ref[0, 0, c0 + r], jnp.concatenate(parts, axis=0), preferred_element_type=F32)
                    pv = d if pv is None else pv + d
            if nxt is not None:
                nxt[3][:, lanes(g)] = best
            if cur is not None:
                acc_ref[:, lanes(g)] = alpha * acc_ref[:, lanes(g)] + pv
                out.append(m_new)
            else:
                out.append(ms[g])
        return tuple(out)

    def finish():
        o = acc_ref[0:A_DH, :] / acc_ref[A_DH:A_DH + 1, :]
        o = jnp.concatenate([o[:, lanes(g)] for g in range(A_GROUP)], axis=0)
        o_ref[0] = o.T.astype(BF16)

    acc_ref[...] = jnp.zeros_like(acc_ref)
    init = (jnp.full((1, tq), -jnp.inf, F32),) * A_GROUP
    is_ctx = pl.program_id(2) + q_tile0 < ctx_tiles
    buf_a, buf_b = (sa_ref, ma_ref), (sb_ref, mb_ref)

    mid_block = lambda i, buf: (head + mid * i, mid, *buf)

    @pl.when(is_ctx)
    def _():
        step((0, head, *buf_a), None, init)
        step(None, (0, head, *buf_a), init)
        finish()

    @pl.when(jnp.logical_not(is_ctx))
    def _():
        step((0, head, *buf_a), None, init)
        ms = step(mid_block(0, buf_b), (0, head, *buf_a), init)

        def pair(j, ms):
            ms = step(mid_block(2 * j + 1, buf_a), mid_block(2 * j, buf_b), ms)
            return step(mid_block(2 * j + 2, buf_b), mid_block(2 * j + 1, buf_a), ms)

        ms = lax.fori_loop(0, n_mid // 2 - 1, pair, ms)
        last = n_mid - 1
        ms = step(mid_block(last, buf_a), mid_block(last - 1, buf_b), ms)
        tail_block = (head + mid * n_mid, tail, *buf_b)
        ms = step(tail_block, mid_block(last, buf_a), ms)
        step(None, tail_block, ms)
        finish()


def _attention(qt, k, vt, *, q_tile0, n_ctx, tq):
    B, _, L = qt.shape
    n_sub = k.shape[2]
    head, mid = n_ctx // ATT_SUB, 3
    n_mid = (n_sub - head - 1) // mid // 2 * 2
    tail = n_sub - head - mid * n_mid
    assert k.shape[3] == ATT_SUB and n_ctx == tq == ATT_SUB and n_mid >= 2 and 1 <= tail <= mid
    width = A_GROUP * A_DH
    n = A_GROUP * tq
    s_buf, m_buf = pltpu.VMEM((A_GROUP, mid * ATT_SUB, tq), F32), pltpu.VMEM((1, n), F32)
    return pl.pallas_call(
        functools.partial(_attn_kernel, blocks=(head, mid, n_mid, tail), tq=tq, ctx_tiles=n_ctx // tq,
                          q_tile0=q_tile0),
        out_shape=jax.ShapeDtypeStruct((B, L - q_tile0 * tq, A_HEADS * A_DH), BF16),
        grid=(B, A_KV_HEADS, L // tq - q_tile0),
        in_specs=[pl.BlockSpec((1, width, tq), lambda b, kv, t: (b, kv, t + q_tile0)),
                  pl.BlockSpec((1, 1, n_sub, ATT_SUB, A_DH), lambda b, kv, t: (b, kv, 0, 0, 0)),
                  pl.BlockSpec((1, 1, n_sub, VT_ROWS, ATT_SUB), lambda b, kv, t: (b, kv, 0, 0, 0))],
        out_specs=pl.BlockSpec((1, tq, width), lambda b, kv, t: (b, t, kv)),
        scratch_shapes=[s_buf, s_buf, m_buf, m_buf, pltpu.VMEM((VT_ROWS, n), F32)],
        compiler_params=_params(("parallel", "parallel", "arbitrary")),
        name="attention",
    )(qt, k, vt)


def _mixer_out_kernel(*refs, with_router, n_stream, row_off):
    hf_ref, hb_ref, mo_ref, a_ref = refs[:4]
    mod_ref, mnw_ref, n2w_ref, wout_ref = refs[4 + n_stream:8 + n_stream]
    rest = refs[8 + n_stream:]
    hs = hf_ref[0].astype(F32) + hb_ref[0].astype(F32)
    hn = jnp.concatenate([_rms_rows(hs[:, h * M_DV:(h + 1) * M_DV]) for h in range(M_HEADS)], axis=1)
    m = hn * mnw_ref[...] * _sigmoid(mo_ref[0].astype(F32))
    y_in = jnp.concatenate([m.astype(BF16), a_ref[0]], axis=1)
    mod = mod_ref[0, 0]
    x1 = _stream_tile(refs[4:4 + n_stream], row_off) + mod[2:3] * jnp.dot(y_in, wout_ref[...],
                                                                          preferred_element_type=F32)
    h2 = _rms_rows(x1) * n2w_ref[...] * (1.0 + mod[4:5]) + mod[3:4]
    if not with_router:
        wg_ref, wu_ref, wd_ref, o_ref = rest
        hb16 = h2.astype(BF16)
        g = jnp.dot(hb16, wg_ref[...], preferred_element_type=F32)
        u = jnp.dot(hb16, wu_ref[...], preferred_element_type=F32)
        act = (g * _sigmoid(g) * u).astype(BF16)
        o_ref[0] = x1 + mod[5:6] * jnp.dot(act, wd_ref[...], preferred_element_type=F32)
        return
    router_ref, x1_ref, h2_ref, ids_ref, gates_ref = rest
    x1_ref[0] = x1
    h2_ref[0] = h2.astype(BF16)
    h_hi = h2.astype(BF16)
    h_lo = (h2 - h_hi.astype(F32)).astype(BF16)
    logits = (jnp.dot(h_hi, router_ref[0], preferred_element_type=F32)
              + jnp.dot(h_lo, router_ref[0], preferred_element_type=F32)
              + jnp.dot(h_hi, router_ref[1], preferred_element_type=F32))
    lane = lax.broadcasted_iota(jnp.int32, logits.shape, 1)
    logits = jnp.where(lane < N_EXPERTS, logits, -jnp.inf)
    m1 = jnp.max(logits, axis=-1, keepdims=True)
    i1 = jnp.min(jnp.where(logits == m1, lane, LANES), axis=-1, keepdims=True)
    rest = jnp.where(lane == i1, -jnp.inf, logits)
    m2 = jnp.max(rest, axis=-1, keepdims=True)
    i2 = jnp.min(jnp.where(rest == m2, lane, LANES), axis=-1, keepdims=True)
    e2 = jnp.exp(m2 - m1)
    g1 = 1.0 / (1.0 + e2)
    ids_ref[0] = jnp.where(lane == 0, i1, jnp.where(lane == 1, i2, -1))
    gates_ref[0] = jnp.where(lane == 0, g1, jnp.where(lane == 1, e2 * g1, 0.0))


def _mixer_out(hf, hb, p, a, stream, mods, mnw, n2w, wout, *, ffn=None, router=None, row_off=0):
    B, L = p.shape[:2]
    D = D_MODEL
    nt = L // ROW_TILE - row_off
    rin = lambda w, cb=0: pl.BlockSpec((1, ROW_TILE, w), lambda b, t: (b, t + row_off, cb))
    rout = lambda w: pl.BlockSpec((1, ROW_TILE, w), lambda b, t: (b, t, 0))
    const = lambda arr: pl.BlockSpec(arr.shape, lambda b, t: (0,) * arr.ndim, pipeline_mode=pl.Buffered(1))
    mw = M_HEADS * M_DV
    a_off = row_off - (L - a.shape[1]) // ROW_TILE
    a_spec = pl.BlockSpec((1, ROW_TILE, A_HEADS * A_DH), lambda b, t: (b, t + a_off, 0))
    s_specs, s_args = _stream_specs(stream, row_off)
    in_specs = [rin(mw), rin(mw), rin(mw, P_MO // mw), a_spec, *s_specs, _mod_spec(row_off),
                pl.BlockSpec((1, mw), lambda b, t: (0, 0)), pl.BlockSpec((1, D), lambda b, t: (0, 0)),
                const(wout)]
    args = [hf, hb, p, a, *s_args, mods, mnw.reshape(1, mw), n2w.reshape(1, D), wout]
    rows = nt * ROW_TILE
    if router is None:
        in_specs += [const(w) for w in ffn]
        args += list(ffn)
        out_shape, out_specs = jax.ShapeDtypeStruct((B, rows, D), F32), rout(D)
    else:
        in_specs.append(const(router))
        args.append(router)
        out_shape = (jax.ShapeDtypeStruct((B, rows, D), F32), jax.ShapeDtypeStruct((B, rows, D), BF16),
                     jax.ShapeDtypeStruct((B, rows, LANES), jnp.int32), jax.ShapeDtypeStruct((B, rows, LANES), F32))
        out_specs = (rout(D), rout(D), rout(LANES), rout(LANES))
    return pl.pallas_call(
        functools.partial(_mixer_out_kernel, with_router=router is not None, n_stream=len(s_args), row_off=row_off),
        out_shape=out_shape,
        grid=(B, nt),
        in_specs=in_specs,
        out_specs=out_specs,
        compiler_params=_params(("parallel", "arbitrary")),
        name="mixer_out",
    )(*args)


MOE_TM = 512
MOE_FF = 1792
MOE_TD = 1024
RANK_TILE = 1024
POS_TILE = 2048
ISSUE_UNROLL = 16


def _moe_rank_kernel(ids_ref, rank_ref, cnt_ref, carry_ref, before_ref):
    @pl.when(pl.program_id(0) == 0)
    def _():
        carry_ref[...] = jnp.zeros_like(carry_ref)
        r = lax.broadcasted_iota(jnp.int32, (RANK_TILE, RANK_TILE), 0)
        c = lax.broadcasted_iota(jnp.int32, (RANK_TILE, RANK_TILE), 1)
        before_ref[...] = jnp.where(c < r, 1.0, 0.0).astype(BF16)

    ids = ids_ref[...]
    lane = lax.broadcasted_iota(jnp.int32, ids.shape, 1)
    onehot = jnp.where(jnp.logical_or(lane == ids[:, 0:1], lane == ids[:, 1:2]), 1.0, 0.0)
    rank_ref[...] = jnp.dot(before_ref[...], onehot.astype(BF16), preferred_element_type=F32) + carry_ref[...]
    carry_ref[...] += jnp.sum(onehot, axis=0, keepdims=True)
    cnt_ref[...] = carry_ref[...]


def _moe_rank(ids):
    n = ids.shape[0]
    return pl.pallas_call(
        _moe_rank_kernel,
        out_shape=(jax.ShapeDtypeStruct((n, LANES), F32), jax.ShapeDtypeStruct((1, LANES), F32)),
        grid=(n // RANK_TILE,),
        in_specs=[pl.BlockSpec((RANK_TILE, LANES), lambda t: (t, 0))],
        out_specs=(pl.BlockSpec((RANK_TILE, LANES), lambda t: (t, 0)), pl.BlockSpec((1, LANES), lambda t: (0, 0))),
        scratch_shapes=[pltpu.VMEM((1, LANES), F32), pltpu.VMEM((RANK_TILE, RANK_TILE), BF16)],
        compiler_params=_params(("arbitrary",)),
        name="moe_rank",
    )(ids)


def _moe_pos_kernel(ids_ref, rank_ref, start_ref, pos_ref):
    ids = ids_ref[...]
    lane = lax.broadcasted_iota(jnp.int32, ids.shape, 1)
    tgt = start_ref[...] + rank_ref[...]
    p0 = jnp.sum(jnp.where(lane == ids[:, 0:1], tgt, 0.0), axis=-1, keepdims=True)
    p1 = jnp.sum(jnp.where(lane == ids[:, 1:2], tgt, 0.0), axis=-1, keepdims=True)
    pos_ref[...] = jnp.where(lane == 0, p0, jnp.where(lane == 1, p1, 0.0)).astype(jnp.int32)


def _moe_pos(ids, rank, start_row):
    n = ids.shape[0]
    blk = pl.BlockSpec((POS_TILE, LANES), lambda t: (t, 0))
    return pl.pallas_call(
        _moe_pos_kernel,
        out_shape=jax.ShapeDtypeStruct((n, LANES), jnp.int32),
        grid=(n // POS_TILE,),
        in_specs=[blk, blk, pl.BlockSpec((1, LANES), lambda t: (0, 0))],
        out_specs=blk,
        compiler_params=_params(("parallel",)),
        name="moe_pos",
    )(ids, rank, start_row)


def _row_copy(src, src_row, dst, dst_row, sem):
    return pltpu.make_async_copy(src.at[pl.ds(src_row, 1), :], dst.at[pl.ds(dst_row, 1), :], sem)


def _moe_dispatch_kernel(pad_ref, pos_ref, hin_ref, xs_ref, zero_ref, h_ref, sem):
    h_ref[...] = hin_ref[...].astype(F32)

    @pl.when(pl.program_id(0) == 0)
    def _():
        zero_ref[...] = jnp.zeros_like(zero_ref)
        fills = [pltpu.make_async_copy(
            zero_ref, xs_ref.at[pl.ds(pl.multiple_of(pad_ref[e], SUBLANES), MOE_TM + SUBLANES), :], sem)
            for e in range(N_EXPERTS)]
        for cp in fills:
            cp.start()
        for cp in fills:
            cp.wait()

        def fill_tile(j, carry):
            cp = pltpu.make_async_copy(zero_ref.at[pl.ds(0, MOE_TM), :],
                                       xs_ref.at[pl.ds(pl.multiple_of(j * MOE_TM, MOE_TM), MOE_TM), :], sem)
            cp.start()
            cp.wait()
            return carry

        lax.fori_loop(pad_ref[N_EXPERTS], xs_ref.shape[0] // MOE_TM, fill_tile, 0)

    def issue(r, carry):
        for k in range(2):
            _row_copy(h_ref, r, xs_ref, pos_ref[0, 0, 2 * r + k], sem).start(priority=k)
        return carry

    lax.fori_loop(0, MOE_TD, issue, 0, unroll=ISSUE_UNROLL)
    for k in range(2):
        pltpu.make_async_copy(h_ref, xs_ref.at[pl.ds(0, MOE_TD), :], sem).wait()


def _moe_dispatch(h, pos, fill_meta, ns):
    n, d = h.shape
    return pl.pallas_call(
        _moe_dispatch_kernel,
        out_shape=jax.ShapeDtypeStruct((ns, d), F32),
        grid_spec=pltpu.PrefetchScalarGridSpec(
            num_scalar_prefetch=1,
            grid=(n // MOE_TD,),
            in_specs=[pl.BlockSpec((1, 1, 2 * MOE_TD), lambda t, pad: (t, 0, 0), memory_space=pltpu.SMEM),
                      pl.BlockSpec((MOE_TD, d), lambda t, pad: (t, 0))],
            out_specs=pl.BlockSpec(memory_space=pl.ANY),
            scratch_shapes=[pltpu.VMEM((MOE_TM + SUBLANES, d), F32), pltpu.VMEM((MOE_TD, d), F32),
                            pltpu.SemaphoreType.DMA(())]),
        compiler_params=_params(("arbitrary",)),
        name="moe_dispatch",
    )(fill_meta, pos, h)


def _moe_group_kernel(te_ref, nv_ref, xs_ref, wg_ref, wu_ref, wd_ref, ys_ref, acc_ref):
    i, f = pl.program_id(0), pl.program_id(1)
    last = pl.num_programs(1) - 1
    valid = i < nv_ref[0]

    @pl.when(valid)
    def _():
        @pl.when(f == 0)
        def _():
            acc_ref[...] = jnp.zeros_like(acc_ref)

        x = xs_ref[...].astype(BF16)
        g = jnp.dot(x, wg_ref[0], preferred_element_type=F32)
        u = jnp.dot(x, wu_ref[0], preferred_element_type=F32)
        act = (g * _sigmoid(g) * u).astype(BF16)
        acc_ref[...] += jnp.dot(act, wd_ref[0], preferred_element_type=F32)

        @pl.when(f == last)
        def _():
            ys_ref[...] = acc_ref[...]

    @pl.when(jnp.logical_and(jnp.logical_not(valid), f == last))
    def _():
        ys_ref[...] = jnp.zeros_like(ys_ref)


def _moe_group(xs, tile_expert, n_valid, wg, wu, wd):
    ns, d = xs.shape
    n_tiles = ns // MOE_TM - 1
    ff = wg.shape[2]
    live = lambda i, nv: i < nv[0]
    nf = ff // MOE_FF
    step = lambda i, f, nv: jnp.where(live(i, nv), f, nf - 1)
    return pl.pallas_call(
        _moe_group_kernel,
        out_shape=jax.ShapeDtypeStruct((n_tiles * MOE_TM, d), F32),
        grid_spec=pltpu.PrefetchScalarGridSpec(
            num_scalar_prefetch=2,
            grid=(n_tiles, nf),
            in_specs=[pl.BlockSpec((MOE_TM, d), lambda i, f, te, nv: (jnp.where(live(i, nv), i, 0), 0)),
                      pl.BlockSpec((1, d, MOE_FF), lambda i, f, te, nv: (te[i], 0, step(i, f, nv))),
                      pl.BlockSpec((1, d, MOE_FF), lambda i, f, te, nv: (te[i], 0, step(i, f, nv))),
                      pl.BlockSpec((1, MOE_FF, d), lambda i, f, te, nv: (te[i], step(i, f, nv), 0))],
            out_specs=pl.BlockSpec((MOE_TM, d), lambda i, f, te, nv: (i, 0)),
            scratch_shapes=[pltpu.VMEM((MOE_TM, d), F32)]),
        compiler_params=_params(("arbitrary", "arbitrary")),
        name="moe_group",
    )(tile_expert, n_valid, xs, wg, wu, wd)


def _moe_combine_kernel(pos_ref, x_ref, gates_ref, mod_ref, fw_ref, ys_ref, o_ref, ybuf, sem):
    def issue(r, carry):
        for k in range(2):
            _row_copy(ys_ref, pos_ref[0, 0, 2 * r + k], ybuf.at[k], r, sem).start(priority=k)
        return carry

    lax.fori_loop(0, MOE_TD, issue, 0, unroll=ISSUE_UNROLL)
    for k in range(2):
        pltpu.make_async_copy(ys_ref.at[pl.ds(0, MOE_TD), :], ybuf.at[k], sem).wait()
    gates = gates_ref[...]
    y = gates[:, 0:1] * ybuf[0] + gates[:, 1:2] * ybuf[1]
    x2 = x_ref[...] + mod_ref[0, 0][5:6] * y
    o_ref[...] = _rms_rows(x2) * fw_ref[...]


def _moe_combine(pos, x1, gates, mods, fw, ys, tokens_per_sample):
    n, d = x1.shape
    per = tokens_per_sample // MOE_TD
    return pl.pallas_call(
        _moe_combine_kernel,
        out_shape=jax.ShapeDtypeStruct((n, d), F32),
        grid=(n // MOE_TD,),
        in_specs=[pl.BlockSpec((1, 1, 2 * MOE_TD), lambda t: (t, 0, 0), memory_space=pltpu.SMEM),
                  pl.BlockSpec((MOE_TD, d), lambda t: (t, 0)),
                  pl.BlockSpec((MOE_TD, LANES), lambda t: (t, 0)),
                  pl.BlockSpec((1, 1, 6, d), lambda t: (t // per, 1, 0, 0)),
                  pl.BlockSpec((1, d), lambda t: (0, 0)),
                  pl.BlockSpec(memory_space=pl.ANY)],
        out_specs=pl.BlockSpec((MOE_TD, d), lambda t: (t, 0)),
        scratch_shapes=[pltpu.VMEM((2, MOE_TD, d), F32), pltpu.SemaphoreType.DMA(())],
        compiler_params=_params(("arbitrary",)),
        name="moe_combine",
    )(pos, x1, gates, mods, fw.reshape(1, d), ys)


def _moe(h2, ids, gates, x1, mods, wg, wu, wd, fw):
    B, T, D = x1.shape
    n = B * T
    ids, gates = ids.reshape(n, LANES), gates.reshape(n, LANES)
    rank, cnt = _moe_rank(ids)
    cnt = cnt[0, :N_EXPERTS].astype(jnp.int32)
    padded = (cnt + MOE_TM - 1) // MOE_TM * MOE_TM
    end = jnp.cumsum(padded)
    start = end - padded
    n_tiles = 2 * n // MOE_TM + N_EXPERTS
    tile_expert = jnp.minimum(jnp.sum(jnp.arange(n_tiles)[:, None] >= (end // MOE_TM)[None, :], axis=1),
                              N_EXPERTS - 1).astype(jnp.int32)
    n_valid = (end[-1:] // MOE_TM).astype(jnp.int32)
    start_row = jnp.zeros((1, LANES), F32).at[0, :N_EXPERTS].set(start.astype(F32))
    pos = _moe_pos(ids, rank, start_row)
    pos = pos[:, :2].reshape(n // MOE_TD, 1, 2 * MOE_TD)
    fill_meta = jnp.concatenate([(start + cnt) // SUBLANES * SUBLANES, n_valid]).astype(jnp.int32)
    xs = _moe_dispatch(h2.reshape(n, D), pos, fill_meta, (n_tiles + 1) * MOE_TM)
    ys = _moe_group(xs, tile_expert, n_valid, wg, wu, wd)
    return _moe_combine(pos, x1.reshape(n, D), gates, mods, fw, ys, T).reshape(B, T, D)


_ROT_PERM = np.concatenate([np.arange(0, A_DH, 2), np.arange(1, A_DH, 2)])


def _prep_w_in(w):
    o = np.cumsum([0, M_HEADS * M_DK, M_HEADS * M_DK, M_HEADS * M_DV, M_HEADS * M_DV, 4 * M_HEADS,
                   A_HEADS * A_DH, A_KV_HEADS * A_DH, A_KV_HEADS * A_DH])
    mq, mk, mv, mo, mg, aq, ak, av = [w[:, o[i]:o[i + 1]] for i in range(8)]
    qk = jnp.concatenate([jnp.concatenate([mq[:, h * M_DK:(h + 1) * M_DK] * (M_DK ** -0.5),
                                           mk[:, h * M_DK:(h + 1) * M_DK]], axis=1) for h in range(M_HEADS)], axis=1)
    perm_q = np.concatenate([h * A_DH + _ROT_PERM for h in range(A_HEADS)])
    perm_k = np.concatenate([h * A_DH + _ROT_PERM for h in range(A_KV_HEADS)])
    pad = jnp.zeros((w.shape[0], G_WIDTH - 4 * M_HEADS), w.dtype)
    return jnp.concatenate([qk, mv, mo, aq[:, perm_q], ak[:, perm_k], av, mg, pad], axis=1).astype(BF16)


def _rope_tables(n_tok, n_ctx):
    rows = n_tok // GRID_W
    row = jnp.broadcast_to(jnp.arange(rows, dtype=F32)[:, None], (rows, GRID_W)).reshape(n_tok)
    col = jnp.broadcast_to(jnp.arange(GRID_W, dtype=F32)[None, :], (rows, GRID_W)).reshape(n_tok)
    n_freq = A_DH // 4
    inv_freq = ROPE_THETA ** (-jnp.arange(n_freq, dtype=F32) / n_freq)
    ang = jnp.concatenate([row[:, None] * inv_freq, col[:, None] * inv_freq], axis=-1)
    cos, sin = jnp.cos(ang), jnp.sin(ang)
    cos = jnp.concatenate([jnp.ones((n_ctx, A_DH // 2), F32), cos], axis=0)
    sin = jnp.concatenate([jnp.zeros((n_ctx, A_DH // 2), F32), sin], axis=0)
    return jnp.tile(cos, (1, 4)), jnp.tile(jnp.concatenate([-sin, sin], axis=1), (1, 2))


def kernel(x, c, ctx, c_ctx, ada_w, ada_b, norm1_w, norm2_w, w_in, mlstm_gate_b, mlstm_norm_w, q_norm_w, k_norm_w,
           w_out, ffn_w_gate, ffn_w_up, ffn_w_down, moe_router, moe_w_gate, moe_w_up, moe_w_down, final_norm_w):
    B, T, D = x.shape
    n_ctx = ctx.shape[1]
    L = n_ctx + T
    depth = w_in.shape[0]
    assert D == D_MODEL and n_ctx == ROW_TILE and T % RANK_TILE == 0 and depth == 2
    ctx_tiles = n_ctx // ROW_TILE
    tq = ROW_TILE

    xa = (ctx, x)
    cvec =jnp.concatenate([c, c_ctx[None], jnp.zeros((8 - B - 1, D), F32)], axis=0)
    cos, sin = _rope_tables(T, n_ctx)
    out = None
    for i in range(depth):
        last = i == depth - 1
        modraw = _ada(cvec, ada_w[i], ada_b[i])
        mods = jnp.stack([jnp.broadcast_to(modraw[B].reshape(1, 6, D), (B, 6, D)),
                          modraw[:B].reshape(B, 6, D)], axis=1)
        qw = jnp.tile(q_norm_w[i][_ROT_PERM], 2).reshape(1, LANES)
        kw = jnp.tile(k_norm_w[i][_ROT_PERM], 2).reshape(1, LANES)
        p, g, qt, k, vt = _in_proj(xa, mods, norm1_w[i], _prep_w_in(w_in[i]), cos, sin, qw, kw)
        hf, hb = _mlstm(p, g, mlstm_gate_b[i], n_ctx // M_CHUNK)
        a = _attention(qt, k, vt, q_tile0=ctx_tiles if last else 0, n_ctx=n_ctx, tq=tq)
        wout = w_out[i].astype(BF16)
        if not last:
            j = i // 2
            ffn = (ffn_w_gate[j].astype(BF16), ffn_w_up[j].astype(BF16), ffn_w_down[j].astype(BF16))
            xa = _mixer_out(hf, hb, p, a, xa, mods, mlstm_norm_w[i], norm2_w[i], wout, ffn=ffn)
        else:
            j = i // 2
            router = jnp.zeros((D, LANES), F32).at[:, :N_EXPERTS].set(moe_router[j])
            router_hi = router.astype(BF16)
            router = jnp.stack([router_hi, (router - router_hi.astype(F32)).astype(BF16)])
            x1, h2, ids, gates = _mixer_out(hf, hb, p, a, xa, mods, mlstm_norm_w[i], norm2_w[i], wout,
                                            router=router, row_off=ctx_tiles)
            out = _moe(h2, ids, gates, x1, mods, moe_w_gate[j].astype(BF16), moe_w_up[j].astype(BF16),
                       moe_w_down[j].astype(BF16), final_norm_w)
    return out
```

```python
import functools
import math

import numpy as np
import jax
import jax.numpy as jnp
from jax import lax
from jax.experimental import pallas as pl
from jax.experimental.pallas import tpu as pltpu

F32 = jnp.float32
BF16 = jnp.bfloat16
HIGHEST = lax.Precision.HIGHEST

D_MODEL = 1024
GRID_W = 64
M_HEADS = 4
M_DV = 128
M_DK = 64
M_CHUNK = 128
A_HEADS = 8
A_KV_HEADS = 2
A_GROUP = A_HEADS // A_KV_HEADS
A_DH = 64
ROPE_THETA = 10000.0
N_EXPERTS = 8
EPS = 1e-6

LANES = 128
SUBLANES = 8
ROW_TILE = 256
VMEM_LIMIT = 56 * 1024 * 1024

P_QK = 0
P_MV = 512
P_MO = 1024
P_AQ = 1536
P_AKV = 2048
P_WIDTH = 2304
G_WIDTH = LANES


def _params(sem, vmem=VMEM_LIMIT, flags=None):
    return pltpu.CompilerParams(dimension_semantics=sem, vmem_limit_bytes=vmem, flags=flags)


def _sigmoid(x):
    return 1.0 / (1.0 + jnp.exp(-x))


def _rms_rows(x):
    return x * lax.rsqrt(jnp.mean(x * x, axis=-1, keepdims=True) + EPS)


def _ada_kernel(c_ref, w_ref, b_ref, o_ref):
    c = c_ref[...]
    s = c * _sigmoid(c)
    o_ref[...] = jnp.dot(s, w_ref[...], precision=HIGHEST, preferred_element_type=F32) + b_ref[...]


def _ada(cvec, w, b):
    n = w.shape[1]
    bn = 1536
    return pl.pallas_call(
        _ada_kernel,
        out_shape=jax.ShapeDtypeStruct((cvec.shape[0], n), F32),
        grid=(n // bn,),
        in_specs=[pl.BlockSpec(cvec.shape, lambda j: (0, 0)),
                  pl.BlockSpec((w.shape[0], bn), lambda j: (0, j)),
                  pl.BlockSpec((1, bn), lambda j: (0, j))],
        out_specs=pl.BlockSpec((cvec.shape[0], bn), lambda j: (0, j)),
        compiler_params=_params(("arbitrary",)),
        name="ada_mod",
    )(cvec, w, b.reshape(1, n))


def _mod_spec(off=0):
    return pl.BlockSpec((1, 1, 6, D_MODEL), lambda b, t: (b, jnp.minimum(t + off, 1), 0, 0))


def _stream_specs(stream, off=0):
    tile = (1, ROW_TILE, D_MODEL)
    if not isinstance(stream, tuple):
        return [pl.BlockSpec(tile, lambda b, t: (b, t + off, 0))], [stream]
    return ([pl.BlockSpec(tile, lambda b, t: (b, 0, 0)),
             pl.BlockSpec(tile, lambda b, t: (b, jnp.maximum(t + off - 1, 0), 0))], list(stream))


def _stream_tile(refs, off=0):
    if len(refs) == 1:
        return refs[0][0]
    return jnp.where(pl.program_id(1) + off == 0, refs[0][0], refs[1][0])


def _in_proj_kernel(*refs, n_stream):
    mod_ref, nw_ref, w_ref, cos_ref, sin_ref, qw_ref, kw_ref, p_ref, g_ref, qt_ref, k_ref, vt_ref = refs[n_stream:]
    mod = mod_ref[0, 0]
    h = _rms_rows(_stream_tile(refs[:n_stream])) * nw_ref[...] * (1.0 + mod[1:2]) + mod[0:1]
    hb = h.astype(BF16)
    r_att = jnp.dot(hb, w_ref[:, P_AQ:P_WIDTH], preferred_element_type=F32)
    _attn_prep_tile(r_att[:, :P_AKV - P_AQ], r_att[:, P_AKV - P_AQ:], cos_ref[...], sin_ref[...], qw_ref[...],
                    kw_ref[...], qt_ref, k_ref, vt_ref)
    p_ref[0] = jnp.dot(hb, w_ref[:, :P_AQ], preferred_element_type=F32).astype(BF16)
    g_ref[0] = jnp.dot(hb, w_ref[:, P_WIDTH:], preferred_element_type=F32)


def _in_proj(stream, mods, nw, wp, cos, sin, qw, kw):
    s_specs, s_args = _stream_specs(stream)
    B = s_args[0].shape[0]
    L = sum(s.shape[1] for s in s_args)
    D = D_MODEL
    nt = L // ROW_TILE
    row = lambda w: pl.BlockSpec((1, ROW_TILE, w), lambda b, t: (b, t, 0))
    vec = pl.BlockSpec((1, LANES), lambda b, t: (0, 0))
    table = pl.BlockSpec((ROW_TILE, LANES), lambda b, t: (t, 0))
    return pl.pallas_call(
        functools.partial(_in_proj_kernel, n_stream=len(s_args)),
        out_shape=(jax.ShapeDtypeStruct((B, L, P_AQ), BF16),
                   jax.ShapeDtypeStruct((B, L, G_WIDTH), F32),
                   jax.ShapeDtypeStruct((B, A_HEADS * A_DH, L), BF16),
                   jax.ShapeDtypeStruct((B, A_KV_HEADS, nt, ROW_TILE, A_DH), BF16),
                   jax.ShapeDtypeStruct((B, A_KV_HEADS, nt, VT_ROWS, ROW_TILE), BF16)),
        grid=(B, nt),
        in_specs=[*s_specs, _mod_spec(),
                  pl.BlockSpec((1, D), lambda b, t: (0, 0)),
                  pl.BlockSpec(wp.shape, lambda b, t: (0, 0), pipeline_mode=pl.Buffered(1)),
                  table, table, vec, vec],
        out_specs=(row(P_AQ), row(G_WIDTH),
                   pl.BlockSpec((1, A_HEADS * A_DH, ROW_TILE), lambda b, t: (b, 0, t)),
                   pl.BlockSpec((1, A_KV_HEADS, 1, ROW_TILE, A_DH), lambda b, t: (b, 0, t, 0, 0)),
                   pl.BlockSpec((1, A_KV_HEADS, 1, VT_ROWS, ROW_TILE), lambda b, t: (b, 0, t, 0, 0))),
        compiler_params=_params(("parallel", "arbitrary")),
        name="in_proj",
    )(*s_args, mods, nw.reshape(1, D), wp, cos, sin, qw, kw)


C_ROWS = M_DV + 16
VEC_ROWS = 24
INTRA_CHUNKS = 6

def _scan_lanes(x, reverse):
    lane = lax.broadcasted_iota(jnp.int32, x.shape, 1)
    k = 1
    while k < M_CHUNK:
        if reverse:
            x = x + jnp.where(lane < M_CHUNK - k, pltpu.roll(x, M_CHUNK - k, axis=1), 0.0)
        else:
            x = x + jnp.where(lane >= k, pltpu.roll(x, k, axis=1), 0.0)
        k *= 2
    return x


def _mlstm_intra_kernel(qk_ref, v_ref, g_ref, bias_ref, numf_ref, numb_ref, vecf_ref, vecb_ref, clf_ref, clb_ref):
    row = lax.broadcasted_iota(jnp.int32, (M_CHUNK, M_CHUNK), 0)
    col = lax.broadcasted_iota(jnp.int32, (M_CHUNK, M_CHUNK), 1)
    gate_row = lax.broadcasted_iota(jnp.int32, (16, M_CHUNK), 0)
    tail_row = lax.broadcasted_iota(jnp.int32, (C_ROWS - M_DV, M_CHUNK), 0)
    outs = ((numf_ref, vecf_ref, clf_ref, row <= col, M_CHUNK - 1),
            (numb_ref, vecb_ref, clb_ref, row >= col, 0))
    chunks = range(INTRA_CHUNKS)
    toks = [pl.ds(c * M_CHUNK, M_CHUNK) for c in chunks]

    g_row, scans, gap_cols = [], [], []
    for c in chunks:
        g = g_ref[0, toks[c], :] + bias_ref[...]
        gr = g.T[0:16, :]
        lf_row = jnp.minimum(gr, 0.0) - jnp.log1p(jnp.exp(-jnp.abs(gr)))
        sc = (_scan_lanes(lf_row, False), _scan_lanes(lf_row, True))
        gaps = gr - pltpu.roll(jnp.where(gate_row < 8, sc[0], sc[1]), 12, axis=0)
        g_row.append(gr)
        scans.append(sc)
        gap_cols.append(jnp.concatenate([gaps, jnp.zeros((M_CHUNK - 16, M_CHUNK), F32)], axis=0).T)
        for vec_ref in (vecf_ref, vecb_ref):
            vec_ref[0, c, 12 + 2 * M_HEADS:VEC_ROWS, :] = jnp.zeros((VEC_ROWS - 12 - 2 * M_HEADS, LANES), F32)

    heads = [(c, h) for c in chunks for h in range(M_HEADS)]
    ks, vts, s_raws = {}, {}, {}
    for c, h in heads:
        qk = qk_ref[0, toks[c], h * LANES:(h + 1) * LANES]
        q, ks[c, h] = qk[:, :M_DK], qk[:, M_DK:]
        vts[c, h] = v_ref[0, toks[c], h * M_DV:(h + 1) * M_DV].astype(F32).T
        s_raws[c, h] = lax.dot_general(ks[c, h], q, (((1,), (1,)), ((), ())), preferred_element_type=F32)

    units = [(c, h, d) for c, h in heads for d in range(2)]
    s_w, vws = {}, {}
    for c, h, d in units:
        _, vec_ref, _, allowed, last = outs[d]
        b_r = scans[c][d][8 * d + 4 + h:8 * d + 5 + h, :]
        i_r = g_row[c][8 * d + h:8 * d + h + 1, :]
        j = 8 * d + h
        b_end = b_r[:, last:last + 1]
        d_log = jnp.where(allowed, b_r + gap_cols[c][:, j:j + 1], -jnp.inf)
        m_intra = jnp.max(d_log, axis=0, keepdims=True)
        s = s_raws[c, h] * jnp.exp(d_log - m_intra)
        s_w[c, h, d] = s.astype(BF16)
        vec_ref[0, c, 3 * h:3 * h + 1, :] = jnp.sum(s, axis=0, keepdims=True)
        vec_ref[0, c, 3 * h + 1:3 * h + 2, :] = m_intra
        vec_ref[0, c, 3 * h + 2:3 * h + 3, :] = b_r
        w_log = b_end - b_r + i_r
        m_loc = jnp.max(w_log, axis=-1, keepdims=True)
        w_row = jnp.exp(w_log - m_loc)
        vws[c, h, d] = jnp.concatenate([vts[c, h] * w_row, jnp.where(tail_row == 0, w_row, 0.0)],
                                       axis=0).astype(BF16)
        vec_ref[0, c, 12 + 2 * h:13 + 2 * h, :] = jnp.broadcast_to(m_loc, (1, LANES))
        vec_ref[0, c, 13 + 2 * h:14 + 2 * h, :] = jnp.broadcast_to(b_end, (1, LANES))

    for c, h, d in units:
        num_ref, _, cl_ref, _, _ = outs[d]
        num_ref[0, c, h] = jnp.dot(vts[c, h].astype(BF16), s_w[c, h, d],
                                   preferred_element_type=F32).astype(BF16)
        cl_ref[0, c, h] = jnp.dot(vws[c, h, d], ks[c, h], preferred_element_type=F32)


def _mlstm_scan_kernel(*refs, n_batch):
    ins, (hf_ref, hb_ref, cn_ref, m_ref) = refs[:8], refs[8:]

    @pl.when(pl.program_id(0) == 0)
    def _():
        cn_ref[...] = jnp.zeros_like(cn_ref)
        m_ref[...] = jnp.zeros_like(m_ref)

    for d, h_ref in enumerate((hf_ref, hb_ref)):
        qk_ref, num_ref, vec_ref, cl_ref = ins[4 * d:4 * d + 4]
        for b in range(n_batch):
            for h in range(M_HEADS):
                idx = (d * n_batch + b) * M_HEADS + h
                q = qk_ref[b, :, h * LANES:h * LANES + M_DK]
                row = lambda r: vec_ref[b, 0, r:r + 1, :]
                den_i, m_i, b_r = row(3 * h), row(3 * h + 1), row(3 * h + 2)
                m_loc, b_end = row(12 + 2 * h), row(13 + 2 * h)
                m_prev = m_ref[idx]
                cn = cn_ref[idx]

                inter = b_r + m_prev
                m_t = jnp.maximum(inter, m_i)
                a = jnp.exp(inter - m_t)
                e = jnp.exp(m_i - m_t)
                cq = lax.dot_general(cn.astype(BF16), q, (((1,), (1,)), ((), ())),
                                     preferred_element_type=F32)
                den = e * den_i + a * cq[M_DV:M_DV + 1, :]
                scale = 1.0 / jnp.maximum(jnp.abs(den), jnp.exp(-m_t))
                ht = (e * num_ref[b, 0, h].astype(F32) + a * cq[0:M_DV, :]) * scale
                h_ref[b, :, h * M_DV:(h + 1) * M_DV] = ht.T.astype(BF16)

                m_new = jnp.maximum(b_end + m_prev, m_loc)
                a_s = jnp.exp(b_end + m_prev - m_new)
                s_s = jnp.exp(m_loc - m_new)
                cn_ref[idx] = a_s[:, :M_DK] * cn + s_s[:, :M_DK] * cl_ref[b, 0, h]
                m_ref[idx] = m_new


def _mlstm(p, g, gate_b, ctx_chunks):
    B, L, _ = p.shape
    nc = L // M_CHUNK
    width = M_HEADS * M_DV
    bias = jnp.zeros((1, G_WIDTH), F32).at[0, :16].set(gate_b)
    assert nc % INTRA_CHUNKS == 0
    tok = lambda w, cb=0: pl.BlockSpec((1, INTRA_CHUNKS * M_CHUNK, w), lambda b, c: (b, c, cb))
    num_shape, vec_shape, cl_shape = (M_HEADS, M_DV, M_CHUNK), (VEC_ROWS, LANES), (M_HEADS, C_ROWS, M_DK)
    per_chunk = lambda s: pl.BlockSpec((1, INTRA_CHUNKS) + s, lambda b, c: (b, c) + (0,) * len(s))
    f32 = lambda *s: jax.ShapeDtypeStruct(s, F32)
    numf, numb, vecf, vecb, clf, clb = pl.pallas_call(
        _mlstm_intra_kernel,
        out_shape=(jax.ShapeDtypeStruct((B, nc, *num_shape), BF16),) * 2 + (f32(B, nc, *vec_shape),) * 2
                  + (f32(B, nc, *cl_shape),) * 2,
        grid=(B, nc // INTRA_CHUNKS),
        in_specs=[tok(width, P_QK // width), tok(width, P_MV // width), tok(G_WIDTH),
                  pl.BlockSpec((1, G_WIDTH), lambda b, c: (0, 0))],
        out_specs=(per_chunk(num_shape),) * 2 + (per_chunk(vec_shape),) * 2 + (per_chunk(cl_shape),) * 2,
        compiler_params=_params(("parallel", "parallel")),
        name="mlstm_intra",
    )(p, p, g, bias)

    fwd = lambda j: j
    bwd = lambda j: jnp.where(j < ctx_chunks, ctx_chunks - 1 - j, nc - 1 + ctx_chunks - j)
    stok = lambda cm, w, cb=0: pl.BlockSpec((B, M_CHUNK, w), lambda j: (0, cm(j), cb))
    schunk = lambda cm, s: pl.BlockSpec((B, 1) + s, lambda j: (0, cm(j)) + (0,) * len(s))
    side = lambda cm: [stok(cm, width, P_QK // width), schunk(cm, num_shape), schunk(cm, vec_shape),
                       schunk(cm, cl_shape)]
    chains = 2 * B * M_HEADS
    return pl.pallas_call(
        functools.partial(_mlstm_scan_kernel, n_batch=B),
        out_shape=(jax.ShapeDtypeStruct((B, L, width), BF16),) * 2,
        grid=(nc,),
        in_specs=side(fwd) + side(bwd),
        out_specs=(stok(fwd, width), stok(bwd, width)),
        scratch_shapes=[pltpu.VMEM((chains, C_ROWS, M_DK), F32),
                        pltpu.VMEM((chains, 1, LANES), F32)],
        compiler_params=_params(("arbitrary",)),
        name="mlstm_scan",
    )(p, numf, vecf, clf, p, numb, vecb, clb)


def _head_norm_rope(xs, ws, cos, sin, bd):
    sqs = [x * x for x in xs]
    his = [sq.astype(BF16) for sq in sqs]
    los = [(sq - hi.astype(F32)).astype(BF16) for sq, hi in zip(sqs, his)]
    mss = [jnp.dot(hi, bd, preferred_element_type=F32) + jnp.dot(lo, bd, preferred_element_type=F32)
           for hi, lo in zip(his, los)]
    ys = [x * lax.rsqrt(ms + EPS) * w for x, ms, w in zip(xs, mss, ws)]
    lane = lax.broadcasted_iota(jnp.int32, ys[0].shape, 1)
    first_half = lane % A_DH < A_DH // 2
    partners = [jnp.where(first_half, pltpu.roll(y, LANES - A_DH // 2, axis=1), pltpu.roll(y, A_DH // 2, axis=1))
                for y in ys]
    return [y * cos + partner * sin for y, partner in zip(ys, partners)]


Q_SCALE = A_DH ** -0.5 * math.log2(math.e)


def _attn_prep_tile(q, kv, cos, sin, qw, kw, qt_ref, k_ref, vt_ref):
    r = lax.broadcasted_iota(jnp.int32, (LANES, LANES), 0) // A_DH
    c = lax.broadcasted_iota(jnp.int32, (LANES, LANES), 1) // A_DH
    bd = jnp.where(r == c, 1.0 / A_DH, 0.0).astype(BF16)
    n_pairs = A_HEADS // 2
    tiles = [q[:, pair * LANES:(pair + 1) * LANES] for pair in range(n_pairs)] + [kv[:, :LANES]]
    rotated = _head_norm_rope(tiles, [qw] * n_pairs + [kw], cos, sin, bd)
    for pair in range(n_pairs):
        qt_ref[0, pair * LANES:(pair + 1) * LANES, :] = (rotated[pair] * Q_SCALE).T.astype(BF16)
    k = rotated[n_pairs].astype(BF16)
    for kvh in range(A_KV_HEADS):
        k_ref[0, kvh, 0] = k[:, kvh * A_DH:(kvh + 1) * A_DH]
    vt = kv[:, LANES:].T.astype(BF16)
    ones = jnp.ones((VT_ROWS - A_DH, vt.shape[1]), BF16)
    for kvh in range(A_KV_HEADS):
        vt_ref[0, kvh, 0] = jnp.concatenate([vt[kvh * A_DH:(kvh + 1) * A_DH, :], ones], axis=0)


ATT_SUB = 256
ATT_PIECE = 128
VT_ROWS = A_DH + 16


def _attn_kernel(qt_ref, k_ref, vt_ref, o_ref, sa_ref, sb_ref, ma_ref, mb_ref, acc_ref,
                 *, blocks, tq, ctx_tiles, q_tile0):
    q_of = lambda g: qt_ref[0, g * A_DH:(g + 1) * A_DH, :]
    lanes = lambda g: slice(g * tq, (g + 1) * tq)
    head, mid, n_mid, tail = blocks

    def step(nxt, cur, ms):
        out = []
        for g in range(A_GROUP):
            if cur is not None:
                c0, c_subs, cs_ref, cm_ref = cur
                m_new = jnp.maximum(ms[g], cm_ref[:, lanes(g)])
                alpha = jnp.exp2(ms[g] - m_new)
            best, pv = None, None
            for r in range(max(nxt[1] if nxt else 0, cur[1] if cur else 0)):
                parts = []
                for piece in range(ATT_SUB // ATT_PIECE):
                    rows = pl.ds(piece * ATT_PIECE, ATT_PIECE)
                    buf_rows = pl.ds(r * ATT_SUB + piece * ATT_PIECE, ATT_PIECE)
                    if nxt is not None and r < nxt[1]:
                        s = jnp.dot(k_ref[0, 0, nxt[0] + r, rows, :], q_of(g), preferred_element_type=F32)
                        nxt[2][g, buf_rows, :] = s
                        top = jnp.max(s, axis=0, keepdims=True)
                        best = top if best is None else jnp.maximum(best, top)
                    if cur is not None and r < c_subs:
                        parts.append(jnp.exp2(cs_ref[g, buf_rows, :] - m_new).astype(BF16))
                if cur is not None and r < c_subs:
                    d = jnp.dot(vt_ref[0, 0, c0 + r], jnp.concatenate(parts, axis=0), preferred_element_type=F32)
                    pv = d if pv is None else pv + d
            if nxt is not None:
                nxt[3][:, lanes(g)] = best
            if cur is not None:
                acc_ref[:, lanes(g)] = alpha * acc_ref[:, lanes(g)] + pv
                out.append(m_new)
            else:
                out.append(ms[g])
        return tuple(out)

    def finish():
        o = acc_ref[0:A_DH, :] / acc_ref[A_DH:A_DH + 1, :]
        o = jnp.concatenate([o[:, lanes(g)] for g in range(A_GROUP)], axis=0)
        o_ref[0] = o.T.astype(BF16)

    acc_ref[...] = jnp.zeros_like(acc_ref)
    init = (jnp.full((1, tq), -jnp.inf, F32),) * A_GROUP
    is_ctx = pl.program_id(2) + q_tile0 < ctx_tiles
    buf_a, buf_b = (sa_ref, ma_ref), (sb_ref, mb_ref)

    mid_block = lambda i, buf: (head + mid * i, mid, *buf)

    @pl.when(is_ctx)
    def _():
        step((0, head, *buf_a), None, init)
        step(None, (0, head, *buf_a), init)
        finish()

    @pl.when(jnp.logical_not(is_ctx))
    def _():
        step((0, head, *buf_a), None, init)
        ms = step(mid_block(0, buf_b), (0, head, *buf_a), init)

        def pair(j, ms):
            ms = step(mid_block(2 * j + 1, buf_a), mid_block(2 * j, buf_b), ms)
            return step(mid_block(2 * j + 2, buf_b), mid_block(2 * j + 1, buf_a), ms)

        ms = lax.fori_loop(0, n_mid // 2 - 1, pair, ms)
        last = n_mid - 1
        ms = step(mid_block(last, buf_a), mid_block(last - 1, buf_b), ms)
        tail_block = (head + mid * n_mid, tail, *buf_b)
        ms = step(tail_block, mid_block(last, buf_a), ms)
        step(None, tail_block, ms)
        finish()


def _attention(qt, k, vt, *, q_tile0, n_ctx, tq):
    B, _, L = qt.shape
    n_sub = k.shape[2]
    head, mid = n_ctx // ATT_SUB, 3
    n_mid = (n_sub - head - 1) // mid // 2 * 2
    tail = n_sub - head - mid * n_mid
    assert k.shape[3] == ATT_SUB and n_ctx == tq == ATT_SUB and n_mid >= 2 and 1 <= tail <= mid
    width = A_GROUP * A_DH
    n = A_GROUP * tq
    s_buf, m_buf = pltpu.VMEM((A_GROUP, mid * ATT_SUB, tq), F32), pltpu.VMEM((1, n), F32)
    return pl.pallas_call(
        functools.partial(_attn_kernel, blocks=(head, mid, n_mid, tail), tq=tq, ctx_tiles=n_ctx // tq,
                          q_tile0=q_tile0),
        out_shape=jax.ShapeDtypeStruct((B, L - q_tile0 * tq, A_HEADS * A_DH), BF16),
        grid=(B, A_KV_HEADS, L // tq - q_tile0),
        in_specs=[pl.BlockSpec((1, width, tq), lambda b, kv, t: (b, kv, t + q_tile0)),
                  pl.BlockSpec((1, 1, n_sub, ATT_SUB, A_DH), lambda b, kv, t: (b, kv, 0, 0, 0)),
                  pl.BlockSpec((1, 1, n_sub, VT_ROWS, ATT_SUB), lambda b, kv, t: (b, kv, 0, 0, 0))],
        out_specs=pl.BlockSpec((1, tq, width), lambda b, kv, t: (b, t, kv)),
        scratch_shapes=[s_buf, s_buf, m_buf, m_buf, pltpu.VMEM((VT_ROWS, n), F32)],
        compiler_params=_params(("parallel", "parallel", "arbitrary")),
        name="attention",
    )(qt, k, vt)


def _mixer_out_kernel(*refs, with_router, n_stream, row_off):
    hf_ref, hb_ref, mo_ref, a_ref = refs[:4]
    mod_ref, mnw_ref, n2w_ref, wout_ref = refs[4 + n_stream:8 + n_stream]
    rest = refs[8 + n_stream:]
    hs = hf_ref[0].astype(F32) + hb_ref[0].astype(F32)
    hn = jnp.concatenate([_rms_rows(hs[:, h * M_DV:(h + 1) * M_DV]) for h in range(M_HEADS)], axis=1)
    m = hn * mnw_ref[...] * _sigmoid(mo_ref[0].astype(F32))
    y_in = jnp.concatenate([m.astype(BF16), a_ref[0]], axis=1)
    mod = mod_ref[0, 0]
    x1 = _stream_tile(refs[4:4 + n_stream], row_off) + mod[2:3] * jnp.dot(y_in, wout_ref[...],
                                                                          preferred_element_type=F32)
    h2 = _rms_rows(x1) * n2w_ref[...] * (1.0 + mod[4:5]) + mod[3:4]
    if not with_router:
        wg_ref, wu_ref, wd_ref, o_ref = rest
        hb16 = h2.astype(BF16)
        g = jnp.dot(hb16, wg_ref[...], preferred_element_type=F32)
        u = jnp.dot(hb16, wu_ref[...], preferred_element_type=F32)
        act = (g * _sigmoid(g) * u).astype(BF16)
        o_ref[0] = x1 + mod[5:6] * jnp.dot(act, wd_ref[...], preferred_element_type=F32)
        return
    router_ref, x1_ref, h2_ref, ids_ref, gates_ref = rest
    x1_ref[0] = x1
    h2_ref[0] = h2
    h_hi = h2.astype(BF16)
    h_lo = (h2 - h_hi.astype(F32)).astype(BF16)
    logits = (jnp.dot(h_hi, router_ref[0], preferred_element_type=F32)
              + jnp.dot(h_lo, router_ref[0], preferred_element_type=F32)
              + jnp.dot(h_hi, router_ref[1], preferred_element_type=F32))
    lane = lax.broadcasted_iota(jnp.int32, logits.shape, 1)
    logits = jnp.where(lane < N_EXPERTS, logits, -jnp.inf)
    m1 = jnp.max(logits, axis=-1, keepdims=True)
    i1 = jnp.min(jnp.where(logits == m1, lane, LANES), axis=-1, keepdims=True)
    rest = jnp.where(lane == i1, -jnp.inf, logits)
    m2 = jnp.max(rest, axis=-1, keepdims=True)
    i2 = jnp.min(jnp.where(rest == m2, lane, LANES), axis=-1, keepdims=True)
    e2 = jnp.exp(m2 - m1)
    g1 = 1.0 / (1.0 + e2)
    ids_ref[0] = jnp.where(lane == 0, i1, jnp.where(lane == 1, i2, -1))
    gates_ref[0] = jnp.where(lane == 0, g1, jnp.where(lane == 1, e2 * g1, 0.0))


def _mixer_out(hf, hb, p, a, stream, mods, mnw, n2w, wout, *, ffn=None, router=None, row_off=0):
    B, L = p.shape[:2]
    D = D_MODEL
    nt = L // ROW_TILE - row_off
    rin = lambda w, cb=0: pl.BlockSpec((1, ROW_TILE, w), lambda b, t: (b, t + row_off, cb))
    rout = lambda w: pl.BlockSpec((1, ROW_TILE, w), lambda b, t: (b, t, 0))
    const = lambda arr: pl.BlockSpec(arr.shape, lambda b, t: (0,) * arr.ndim, pipeline_mode=pl.Buffered(1))
    mw = M_HEADS * M_DV
    a_off = row_off - (L - a.shape[1]) // ROW_TILE
    a_spec = pl.BlockSpec((1, ROW_TILE, A_HEADS * A_DH), lambda b, t: (b, t + a_off, 0))
    s_specs, s_args = _stream_specs(stream, row_off)
    in_specs = [rin(mw), rin(mw), rin(mw, P_MO // mw), a_spec, *s_specs, _mod_spec(row_off),
                pl.BlockSpec((1, mw), lambda b, t: (0, 0)), pl.BlockSpec((1, D), lambda b, t: (0, 0)),
                const(wout)]
    args = [hf, hb, p, a, *s_args, mods, mnw.reshape(1, mw), n2w.reshape(1, D), wout]
    rows = nt * ROW_TILE
    if router is None:
        in_specs += [const(w) for w in ffn]
        args += list(ffn)
        out_shape, out_specs = jax.ShapeDtypeStruct((B, rows, D), F32), rout(D)
    else:
        in_specs.append(const(router))
        args.append(router)
        out_shape = (jax.ShapeDtypeStruct((B, rows, D), F32), jax.ShapeDtypeStruct((B, rows, D), F32),
                     jax.ShapeDtypeStruct((B, rows, LANES), jnp.int32), jax.ShapeDtypeStruct((B, rows, LANES), F32))
        out_specs = (rout(D), rout(D), rout(LANES), rout(LANES))
    return pl.pallas_call(
        functools.partial(_mixer_out_kernel, with_router=router is not None, n_stream=len(s_args), row_off=row_off),
        out_shape=out_shape,
        grid=(B, nt),
        in_specs=in_specs,
        out_specs=out_specs,
        compiler_params=_params(("parallel", "arbitrary")),
        name="mixer_out",
    )(*args)


MOE_TM = 512
MOE_FF = 1792
MOE_TD = 1024
RANK_TILE = 2048
POS_TILE = 4096
ISSUE_UNROLL = 16


def _moe_rank_kernel(ids_ref, rank_ref, cnt_ref, carry_ref, before_ref):
    @pl.when(pl.program_id(0) == 0)
    def _():
        carry_ref[...] = jnp.zeros_like(carry_ref)
        r = lax.broadcasted_iota(jnp.int32, (RANK_TILE, RANK_TILE), 0)
        c = lax.broadcasted_iota(jnp.int32, (RANK_TILE, RANK_TILE), 1)
        before_ref[...] = jnp.where(c < r, 1.0, 0.0).astype(BF16)

    ids = ids_ref[...]
    lane = lax.broadcasted_iota(jnp.int32, ids.shape, 1)
    onehot = jnp.where(jnp.logical_or(lane == ids[:, 0:1], lane == ids[:, 1:2]), 1.0, 0.0)
    rank_ref[...] = jnp.dot(before_ref[...], onehot.astype(BF16), preferred_element_type=F32) + carry_ref[...]
    carry_ref[...] += jnp.sum(onehot, axis=0, keepdims=True)
    cnt_ref[...] = carry_ref[...]


def _moe_rank(ids):
    n = ids.shape[0]
    return pl.pallas_call(
        _moe_rank_kernel,
        out_shape=(jax.ShapeDtypeStruct((n, LANES), F32), jax.ShapeDtypeStruct((1, LANES), F32)),
        grid=(n // RANK_TILE,),
        in_specs=[pl.BlockSpec((RANK_TILE, LANES), lambda t: (t, 0))],
        out_specs=(pl.BlockSpec((RANK_TILE, LANES), lambda t: (t, 0)), pl.BlockSpec((1, LANES), lambda t: (0, 0))),
        scratch_shapes=[pltpu.VMEM((1, LANES), F32), pltpu.VMEM((RANK_TILE, RANK_TILE), BF16)],
        compiler_params=_params(("arbitrary",)),
        name="moe_rank",
    )(ids)


def _moe_pos_kernel(ids_ref, rank_ref, start_ref, pos_ref):
    ids = ids_ref[...]
    lane = lax.broadcasted_iota(jnp.int32, ids.shape, 1)
    tgt = start_ref[...] + rank_ref[...]
    p0 = jnp.sum(jnp.where(lane == ids[:, 0:1], tgt, 0.0), axis=-1, keepdims=True)
    p1 = jnp.sum(jnp.where(lane == ids[:, 1:2], tgt, 0.0), axis=-1, keepdims=True)
    pos_ref[...] = jnp.where(lane == 0, p0, jnp.where(lane == 1, p1, 0.0)).astype(jnp.int32)


def _moe_pos(ids, rank, start_row):
    n = ids.shape[0]
    blk = pl.BlockSpec((POS_TILE, LANES), lambda t: (t, 0))
    return pl.pallas_call(
        _moe_pos_kernel,
        out_shape=jax.ShapeDtypeStruct((n, LANES), jnp.int32),
        grid=(n // POS_TILE,),
        in_specs=[blk, blk, pl.BlockSpec((1, LANES), lambda t: (0, 0))],
        out_specs=blk,
        compiler_params=_params(("parallel",)),
        name="moe_pos",
    )(ids, rank, start_row)


def _row_copy(src, src_row, dst, dst_row, sem):
    return pltpu.make_async_copy(src.at[pl.ds(src_row, 1), :], dst.at[pl.ds(dst_row, 1), :], sem)


def _moe_dispatch_kernel(pad_ref, pos_ref, h_ref, xs_ref, zero_ref, sem):
    @pl.when(pl.program_id(0) == 0)
    def _():
        zero_ref[...] = jnp.zeros_like(zero_ref)
        fills = [pltpu.make_async_copy(
            zero_ref, xs_ref.at[pl.ds(pl.multiple_of(pad_ref[e], SUBLANES), MOE_TM + SUBLANES), :], sem)
            for e in range(N_EXPERTS)]
        for cp in fills:
            cp.start()
        for cp in fills:
            cp.wait()

        def fill_tile(j, carry):
            cp = pltpu.make_async_copy(zero_ref.at[pl.ds(0, MOE_TM), :],
                                       xs_ref.at[pl.ds(pl.multiple_of(j * MOE_TM, MOE_TM), MOE_TM), :], sem)
            cp.start()
            cp.wait()
            return carry

        lax.fori_loop(pad_ref[N_EXPERTS], xs_ref.shape[0] // MOE_TM, fill_tile, 0)

    def issue(r, carry):
        for k in range(2):
            _row_copy(h_ref, r, xs_ref, pos_ref[0, 0, 2 * r + k], sem).start(priority=k)
        return carry

    lax.fori_loop(0, MOE_TD, issue, 0, unroll=ISSUE_UNROLL)
    for k in range(2):
        pltpu.make_async_copy(h_ref, xs_ref.at[pl.ds(0, MOE_TD), :], sem).wait()


def _moe_dispatch(h, pos, fill_meta, ns):
    n, d = h.shape
    return pl.pallas_call(
        _moe_dispatch_kernel,
        out_shape=jax.ShapeDtypeStruct((ns, d), F32),
        grid_spec=pltpu.PrefetchScalarGridSpec(
            num_scalar_prefetch=1,
            grid=(n // MOE_TD,),
            in_specs=[pl.BlockSpec((1, 1, 2 * MOE_TD), lambda t, pad: (t, 0, 0), memory_space=pltpu.SMEM),
                      pl.BlockSpec((MOE_TD, d), lambda t, pad: (t, 0))],
            out_specs=pl.BlockSpec(memory_space=pl.ANY),
            scratch_shapes=[pltpu.VMEM((MOE_TM + SUBLANES, d), F32), pltpu.SemaphoreType.DMA(())]),
        compiler_params=_params(("arbitrary",)),
        name="moe_dispatch",
    )(fill_meta, pos, h)


def _moe_group_kernel(te_ref, nv_ref, xs_ref, wg_ref, wu_ref, wd_ref, ys_ref, acc_ref):
    i, f = pl.program_id(0), pl.program_id(1)
    last = pl.num_programs(1) - 1
    valid = i < nv_ref[0]

    @pl.when(valid)
    def _():
        @pl.when(f == 0)
        def _():
            acc_ref[...] = jnp.zeros_like(acc_ref)

        x = xs_ref[...].astype(BF16)
        g = jnp.dot(x, wg_ref[0], preferred_element_type=F32)
        u = jnp.dot(x, wu_ref[0], preferred_element_type=F32)
        act = (g * _sigmoid(g) * u).astype(BF16)
        acc_ref[...] += jnp.dot(act, wd_ref[0], preferred_element_type=F32)

        @pl.when(f == last)
        def _():
            ys_ref[...] = acc_ref[...]

    @pl.when(jnp.logical_and(jnp.logical_not(valid), f == last))
    def _():
        ys_ref[...] = jnp.zeros_like(ys_ref)


def _moe_group(xs, tile_expert, n_valid, wg, wu, wd):
    ns, d = xs.shape
    n_tiles = ns // MOE_TM - 1
    ff = wg.shape[2]
    live = lambda i, nv: i < nv[0]
    nf = ff // MOE_FF
    step = lambda i, f, nv: jnp.where(live(i, nv), f, nf - 1)
    return pl.pallas_call(
        _moe_group_kernel,
        out_shape=jax.ShapeDtypeStruct((n_tiles * MOE_TM, d), F32),
        grid_spec=pltpu.PrefetchScalarGridSpec(
            num_scalar_prefetch=2,
            grid=(n_tiles, nf),
            in_specs=[pl.BlockSpec((MOE_TM, d), lambda i, f, te, nv: (jnp.where(live(i, nv), i, 0), 0)),
                      pl.BlockSpec((1, d, MOE_FF), lambda i, f, te, nv: (te[i], 0, step(i, f, nv))),
                      pl.BlockSpec((1, d, MOE_FF), lambda i, f, te, nv: (te[i], 0, step(i, f, nv))),
                      pl.BlockSpec((1, MOE_FF, d), lambda i, f, te, nv: (te[i], step(i, f, nv), 0))],
            out_specs=pl.BlockSpec((MOE_TM, d), lambda i, f, te, nv: (i, 0)),
            scratch_shapes=[pltpu.VMEM((MOE_TM, d), F32)]),
        compiler_params=_params(("arbitrary", "arbitrary")),
        name="moe_group",
    )(tile_expert, n_valid, xs, wg, wu, wd)


def _moe_combine_kernel(pos_ref, x_ref, gates_ref, mod_ref, fw_ref, ys_ref, o_ref, ybuf, sem):
    def issue(r, carry):
        for k in range(2):
            _row_copy(ys_ref, pos_ref[0, 0, 2 * r + k], ybuf.at[k], r, sem).start(priority=k)
        return carry

    lax.fori_loop(0, MOE_TD, issue, 0, unroll=ISSUE_UNROLL)
    for k in range(2):
        pltpu.make_async_copy(ys_ref.at[pl.ds(0, MOE_TD), :], ybuf.at[k], sem).wait()
    gates = gates_ref[...]
    y = gates[:, 0:1] * ybuf[0] + gates[:, 1:2] * ybuf[1]
    x2 = x_ref[...] + mod_ref[0, 0][5:6] * y
    o_ref[...] = _rms_rows(x2) * fw_ref[...]


def _moe_combine(pos, x1, gates, mods, fw, ys, tokens_per_sample):
    n, d = x1.shape
    per = tokens_per_sample // MOE_TD
    return pl.pallas_call(
        _moe_combine_kernel,
        out_shape=jax.ShapeDtypeStruct((n, d), F32),
        grid=(n // MOE_TD,),
        in_specs=[pl.BlockSpec((1, 1, 2 * MOE_TD), lambda t: (t, 0, 0), memory_space=pltpu.SMEM),
                  pl.BlockSpec((MOE_TD, d), lambda t: (t, 0)),
                  pl.BlockSpec((MOE_TD, LANES), lambda t: (t, 0)),
                  pl.BlockSpec((1, 1, 6, d), lambda t: (t // per, 1, 0, 0)),
                  pl.BlockSpec((1, d), lambda t: (0, 0)),
                  pl.BlockSpec(memory_space=pl.ANY)],
        out_specs=pl.BlockSpec((MOE_TD, d), lambda t: (t, 0)),
        scratch_shapes=[pltpu.VMEM((2, MOE_TD, d), F32), pltpu.SemaphoreType.DMA(())],
        compiler_params=_params(("arbitrary",)),
        name="moe_combine",
    )(pos, x1, gates, mods, fw.reshape(1, d), ys)


def _moe(h2, ids, gates, x1, mods, wg, wu, wd, fw):
    B, T, D = x1.shape
    n = B * T
    ids, gates = ids.reshape(n, LANES), gates.reshape(n, LANES)
    rank, cnt = _moe_rank(ids)
    cnt = cnt[0, :N_EXPERTS].astype(jnp.int32)
    padded = (cnt + MOE_TM - 1) // MOE_TM * MOE_TM
    end = jnp.cumsum(padded)
    start = end - padded
    n_tiles = 2 * n // MOE_TM + N_EXPERTS
    tile_expert = jnp.minimum(jnp.sum(jnp.arange(n_tiles)[:, None] >= (end // MOE_TM)[None, :], axis=1),
                              N_EXPERTS - 1).astype(jnp.int32)
    n_valid = (end[-1:] // MOE_TM).astype(jnp.int32)
    start_row = jnp.zeros((1, LANES), F32).at[0, :N_EXPERTS].set(start.astype(F32))
    pos = _moe_pos(ids, rank, start_row)
    pos = pos[:, :2].reshape(n // MOE_TD, 1, 2 * MOE_TD)
    fill_meta = jnp.concatenate([(start + cnt) // SUBLANES * SUBLANES, n_valid]).astype(jnp.int32)
    xs = _moe_dispatch(h2.reshape(n, D), pos, fill_meta, (n_tiles + 1) * MOE_TM)
    ys = _moe_group(xs, tile_expert, n_valid, wg, wu, wd)
    return _moe_combine(pos, x1.reshape(n, D), gates, mods, fw, ys, T).reshape(B, T, D)


_ROT_PERM = np.concatenate([np.arange(0, A_DH, 2), np.arange(1, A_DH, 2)])


def _prep_w_in(w):
    o = np.cumsum([0, M_HEADS * M_DK, M_HEADS * M_DK, M_HEADS * M_DV, M_HEADS * M_DV, 4 * M_HEADS,
                   A_HEADS * A_DH, A_KV_HEADS * A_DH, A_KV_HEADS * A_DH])
    mq, mk, mv, mo, mg, aq, ak, av = [w[:, o[i]:o[i + 1]] for i in range(8)]
    qk = jnp.concatenate([jnp.concatenate([mq[:, h * M_DK:(h + 1) * M_DK] * (M_DK ** -0.5),
                                           mk[:, h * M_DK:(h + 1) * M_DK]], axis=1) for h in range(M_HEADS)], axis=1)
    perm_q = np.concatenate([h * A_DH + _ROT_PERM for h in range(A_HEADS)])
    perm_k = np.concatenate([h * A_DH + _ROT_PERM for h in range(A_KV_HEADS)])
    pad = jnp.zeros((w.shape[0], G_WIDTH - 4 * M_HEADS), w.dtype)
    return jnp.concatenate([qk, mv, mo, aq[:, perm_q], ak[:, perm_k], av, mg, pad], axis=1).astype(BF16)


def _rope_tables(n_tok, n_ctx):
    rows = n_tok // GRID_W
    row = jnp.broadcast_to(jnp.arange(rows, dtype=F32)[:, None], (rows, GRID_W)).reshape(n_tok)
    col = jnp.broadcast_to(jnp.arange(GRID_W, dtype=F32)[None, :], (rows, GRID_W)).reshape(n_tok)
    n_freq = A_DH // 4
    inv_freq = ROPE_THETA ** (-jnp.arange(n_freq, dtype=F32) / n_freq)
    ang = jnp.concatenate([row[:, None] * inv_freq, col[:, None] * inv_freq], axis=-1)
    cos, sin = jnp.cos(ang), jnp.sin(ang)
    cos = jnp.concatenate([jnp.ones((n_ctx, A_DH // 2), F32), cos], axis=0)
    sin = jnp.concatenate([jnp.zeros((n_ctx, A_DH // 2), F32), sin], axis=0)
    return jnp.tile(cos, (1, 4)), jnp.tile(jnp.concatenate([-sin, sin], axis=1), (1, 2))


def kernel(x, c, ctx, c_ctx, ada_w, ada_b, norm1_w, norm2_w, w_in, mlstm_gate_b, mlstm_norm_w, q_norm_w, k_norm_w,
           w_out, ffn_w_gate, ffn_w_up, ffn_w_down, moe_router, moe_w_gate, moe_w_up, moe_w_down, final_norm_w):
    B, T, D = x.shape
    n_ctx = ctx.shape[1]
    L = n_ctx + T
    depth = w_in.shape[0]
    assert D == D_MODEL and n_ctx == ROW_TILE and T % RANK_TILE == 0 and depth == 2
    ctx_tiles = n_ctx // ROW_TILE
    tq = ROW_TILE

    xa = (ctx, x)
    cvec =jnp.concatenate([c, c_ctx[None], jnp.zeros((8 - B - 1, D), F32)], axis=0)
    cos, sin = _rope_tables(T, n_ctx)
    out = None
    for i in range(depth):
        last = i == depth - 1
        modraw = _ada(cvec, ada_w[i], ada_b[i])
        mods = jnp.stack([jnp.broadcast_to(modraw[B].reshape(1, 6, D), (B, 6, D)),
                          modraw[:B].reshape(B, 6, D)], axis=1)
        qw = jnp.tile(q_norm_w[i][_ROT_PERM], 2).reshape(1, LANES)
        kw = jnp.tile(k_norm_w[i][_ROT_PERM], 2).reshape(1, LANES)
        p, g, qt, k, vt = _in_proj(xa, mods, norm1_w[i], _prep_w_in(w_in[i]), cos, sin, qw, kw)
        hf, hb = _mlstm(p, g, mlstm_gate_b[i], n_ctx // M_CHUNK)
        a = _attention(qt, k, vt, q_tile0=ctx_tiles if last else 0, n_ctx=n_ctx, tq=tq)
        wout = w_out[i].astype(BF16)
        if not last:
            j = i // 2
            ffn = (ffn_w_gate[j].astype(BF16), ffn_w_up[j].astype(BF16), ffn_w_down[j].astype(BF16))
            xa = _mixer_out(hf, hb, p, a, xa, mods, mlstm_norm_w[i], norm2_w[i], wout, ffn=ffn)
        else:
            j = i // 2
            router = jnp.zeros((D, LANES), F32).at[:, :N_EXPERTS].set(moe_router[j])
            router_hi = router.astype(BF16)
            router = jnp.stack([router_hi, (router - router_hi.astype(F32)).astype(BF16)])
            x1, h2, ids, gates = _mixer_out(hf, hb, p, a, xa, mods, mlstm_norm_w[i], norm2_w[i], wout,
                                            router=router, row_off=ctx_tiles)
            out = _moe(h2, ids, gates, x1, mods, moe_w_gate[j].astype(BF16), moe_w_up[j].astype(BF16),
                       moe_w_down[j].astype(BF16), final_norm_w)
    return out
```

```python
import functools
import math

import numpy as np
import jax
import jax.numpy as jnp
from jax import lax
from jax.experimental import pallas as pl
from jax.experimental.pallas import tpu as pltpu

F32 = jnp.float32
BF16 = jnp.bfloat16
HIGHEST = lax.Precision.HIGHEST

D_MODEL = 1024
GRID_W = 64
M_HEADS = 4
M_DV = 128
M_DK = 64
M_CHUNK = 128
A_HEADS = 8
A_KV_HEADS = 2
A_GROUP = A_HEADS // A_KV_HEADS
A_DH = 64
ROPE_THETA = 10000.0
N_EXPERTS = 8
EPS = 1e-6

LANES = 128
SUBLANES = 8
ROW_TILE = 256
VMEM_LIMIT = 56 * 1024 * 1024

P_QK = 0
P_MV = 512
P_MO = 1024
P_AQ = 1536
P_AKV = 2048
P_WIDTH = 2304
G_WIDTH = LANES


def _params(sem, vmem=VMEM_LIMIT, flags=None):
    return pltpu.CompilerParams(dimension_semantics=sem, vmem_limit_bytes=vmem, flags=flags)


def _sigmoid(x):
    return 1.0 / (1.0 + jnp.exp(-x))


def _rms_rows(x):
    return x * lax.rsqrt(jnp.mean(x * x, axis=-1, keepdims=True) + EPS)


def _ada_kernel(c_ref, w_ref, b_ref, o_ref):
    c = c_ref[...]
    s = c * _sigmoid(c)
    o_ref[...] = jnp.dot(s, w_ref[...], precision=HIGHEST, preferred_element_type=F32) + b_ref[...]


def _ada(cvec, w, b):
    n = w.shape[1]
    bn = 1536
    return pl.pallas_call(
        _ada_kernel,
        out_shape=jax.ShapeDtypeStruct((cvec.shape[0], n), F32),
        grid=(n // bn,),
        in_specs=[pl.BlockSpec(cvec.shape, lambda j: (0, 0)),
                  pl.BlockSpec((w.shape[0], bn), lambda j: (0, j)),
                  pl.BlockSpec((1, bn), lambda j: (0, j))],
        out_specs=pl.BlockSpec((cvec.shape[0], bn), lambda j: (0, j)),
        compiler_params=_params(("arbitrary",)),
        name="ada_mod",
    )(cvec, w, b.reshape(1, n))


def _mod_spec(off=0):
    return pl.BlockSpec((1, 1, 6, D_MODEL), lambda b, t: (b, jnp.minimum(t + off, 1), 0, 0))


def _stream_specs(stream, off=0):
    tile = (1, ROW_TILE, D_MODEL)
    if not isinstance(stream, tuple):
        return [pl.BlockSpec(tile, lambda b, t: (b, t + off, 0))], [stream]
    return ([pl.BlockSpec(tile, lambda b, t: (b, 0, 0)),
             pl.BlockSpec(tile, lambda b, t: (b, jnp.maximum(t + off - 1, 0), 0))], list(stream))


def _stream_tile(refs, off=0):
    if len(refs) == 1:
        return refs[0][0]
    return jnp.where(pl.program_id(1) + off == 0, refs[0][0], refs[1][0])


def _in_proj_kernel(*refs, n_stream):
    mod_ref, nw_ref, w_ref, cos_ref, sin_ref, qw_ref, kw_ref, p_ref, g_ref, qt_ref, k_ref, vt_ref = refs[n_stream:]
    mod = mod_ref[0, 0]
    h = _rms_rows(_stream_tile(refs[:n_stream])) * nw_ref[...] * (1.0 + mod[1:2]) + mod[0:1]
    hb = h.astype(BF16)
    r_att = jnp.dot(hb, w_ref[:, P_AQ:P_WIDTH], preferred_element_type=F32)
    _attn_prep_tile(r_att[:, :P_AKV - P_AQ], r_att[:, P_AKV - P_AQ:], cos_ref[...], sin_ref[...], qw_ref[...],
                    kw_ref[...], qt_ref, k_ref, vt_ref)
    p_ref[0] = jnp.dot(hb, w_ref[:, :P_AQ], preferred_element_type=F32).astype(BF16)
    g_ref[0] = jnp.dot(hb, w_ref[:, P_WIDTH:], preferred_element_type=F32)


def _in_proj(stream, mods, nw, wp, cos, sin, qw, kw):
    s_specs, s_args = _stream_specs(stream)
    B = s_args[0].shape[0]
    L = sum(s.shape[1] for s in s_args)
    D = D_MODEL
    nt = L // ROW_TILE
    row = lambda w: pl.BlockSpec((1, ROW_TILE, w), lambda b, t: (b, t, 0))
    vec = pl.BlockSpec((1, LANES), lambda b, t: (0, 0))
    table = pl.BlockSpec((ROW_TILE, LANES), lambda b, t: (t, 0))
    return pl.pallas_call(
        functools.partial(_in_proj_kernel, n_stream=len(s_args)),
        out_shape=(jax.ShapeDtypeStruct((B, L, P_AQ), BF16),
                   jax.ShapeDtypeStruct((B, L, G_WIDTH), F32),
                   jax.ShapeDtypeStruct((B, A_HEADS * A_DH, L), BF16),
                   jax.ShapeDtypeStruct((B, A_KV_HEADS, nt, ROW_TILE, A_DH), BF16),
                   jax.ShapeDtypeStruct((B, A_KV_HEADS, nt, VT_ROWS, ROW_TILE), BF16)),
        grid=(B, nt),
        in_specs=[*s_specs, _mod_spec(),
                  pl.BlockSpec((1, D), lambda b, t: (0, 0)),
                  pl.BlockSpec(wp.shape, lambda b, t: (0, 0), pipeline_mode=pl.Buffered(1)),
                  table, table, vec, vec],
        out_specs=(row(P_AQ), row(G_WIDTH),
                   pl.BlockSpec((1, A_HEADS * A_DH, ROW_TILE), lambda b, t: (b, 0, t)),
                   pl.BlockSpec((1, A_KV_HEADS, 1, ROW_TILE, A_DH), lambda b, t: (b, 0, t, 0, 0)),
                   pl.BlockSpec((1, A_KV_HEADS, 1, VT_ROWS, ROW_TILE), lambda b, t: (b, 0, t, 0, 0))),
        compiler_params=_params(("parallel", "arbitrary")),
        name="in_proj",
    )(*s_args, mods, nw.reshape(1, D), wp, cos, sin, qw, kw)


C_ROWS = M_DV + 16
VEC_ROWS = 24
INTRA_CHUNKS = 6

def _scan_lanes(x, reverse):
    lane = lax.broadcasted_iota(jnp.int32, x.shape, 1)
    k = 1
    while k < M_CHUNK:
        if reverse:
            x = x + jnp.where(lane < M_CHUNK - k, pltpu.roll(x, M_CHUNK - k, axis=1), 0.0)
        else:
            x = x + jnp.where(lane >= k, pltpu.roll(x, k, axis=1), 0.0)
        k *= 2
    return x


def _mlstm_intra_kernel(qk_ref, v_ref, g_ref, bias_ref, numf_ref, numb_ref, vecf_ref, vecb_ref, clf_ref, clb_ref):
    row = lax.broadcasted_iota(jnp.int32, (M_CHUNK, M_CHUNK), 0)
    col = lax.broadcasted_iota(jnp.int32, (M_CHUNK, M_CHUNK), 1)
    gate_row = lax.broadcasted_iota(jnp.int32, (16, M_CHUNK), 0)
    tail_row = lax.broadcasted_iota(jnp.int32, (C_ROWS - M_DV, M_CHUNK), 0)
    outs = ((numf_ref, vecf_ref, clf_ref, row <= col, M_CHUNK - 1),
            (numb_ref, vecb_ref, clb_ref, row >= col, 0))
    chunks = range(INTRA_CHUNKS)
    toks = [pl.ds(c * M_CHUNK, M_CHUNK) for c in chunks]

    g_row, scans, gap_cols = [], [], []
    for c in chunks:
        g = g_ref[0, toks[c], :] + bias_ref[...]
        gr = g.T[0:16, :]
        lf_row = jnp.minimum(gr, 0.0) - jnp.log1p(jnp.exp(-jnp.abs(gr)))
        sc = (_scan_lanes(lf_row, False), _scan_lanes(lf_row, True))
        gaps = gr - pltpu.roll(jnp.where(gate_row < 8, sc[0], sc[1]), 12, axis=0)
        g_row.append(gr)
        scans.append(sc)
        gap_cols.append(jnp.concatenate([gaps, jnp.zeros((M_CHUNK - 16, M_CHUNK), F32)], axis=0).T)
        for vec_ref in (vecf_ref, vecb_ref):
            vec_ref[0, c, 12 + 2 * M_HEADS:VEC_ROWS, :] = jnp.zeros((VEC_ROWS - 12 - 2 * M_HEADS, LANES), F32)

    heads = [(c, h) for c in chunks for h in range(M_HEADS)]
    ks, vts, s_raws = {}, {}, {}
    for c, h in heads:
        qk = qk_ref[0, toks[c], h * LANES:(h + 1) * LANES]
        q, ks[c, h] = qk[:, :M_DK], qk[:, M_DK:]
        vts[c, h] = v_ref[0, toks[c], h * M_DV:(h + 1) * M_DV].astype(F32).T
        s_raws[c, h] = lax.dot_general(ks[c, h], q, (((1,), (1,)), ((), ())), preferred_element_type=F32)

    units = [(c, h, d) for c, h in heads for d in range(2)]
    s_w, vws = {}, {}
    for c, h, d in units:
        _, vec_ref, _, allowed, last = outs[d]
        b_r = scans[c][d][8 * d + 4 + h:8 * d + 5 + h, :]
        i_r = g_row[c][8 * d + h:8 * d + h + 1, :]
        j = 8 * d + h
        b_end = b_r[:, last:last + 1]
        d_log = jnp.where(allowed, b_r + gap_cols[c][:, j:j + 1], -jnp.inf)
        m_intra = jnp.max(d_log, axis=0, keepdims=True)
        s = s_raws[c, h] * jnp.exp(d_log - m_intra)
        s_w[c, h, d] = s.astype(BF16)
        vec_ref[0, c, 3 * h:3 * h + 1, :] = jnp.sum(s, axis=0, keepdims=True)
        vec_ref[0, c, 3 * h + 1:3 * h + 2, :] = m_intra
        vec_ref[0, c, 3 * h + 2:3 * h + 3, :] = b_r
        w_log = b_end - b_r + i_r
        m_loc = jnp.max(w_log, axis=-1, keepdims=True)
        w_row = jnp.exp(w_log - m_loc)
        vws[c, h, d] = jnp.concatenate([vts[c, h] * w_row, jnp.where(tail_row == 0, w_row, 0.0)],
                                       axis=0).astype(BF16)
        vec_ref[0, c, 12 + 2 * h:13 + 2 * h, :] = jnp.broadcast_to(m_loc, (1, LANES))
        vec_ref[0, c, 13 + 2 * h:14 + 2 * h, :] = jnp.broadcast_to(b_end, (1, LANES))

    for c, h, d in units:
        num_ref, _, cl_ref, _, _ = outs[d]
        num_ref[0, c, h] = jnp.dot(vts[c, h].astype(BF16), s_w[c, h, d],
                                   preferred_element_type=F32).astype(BF16)
        cl_ref[0, c, h] = jnp.dot(vws[c, h, d], ks[c, h], preferred_element_type=F32)


def _mlstm_scan_kernel(*refs, n_batch):
    ins, (hf_ref, hb_ref, cn_ref, m_ref) = refs[:8], refs[8:]

    @pl.when(pl.program_id(0) == 0)
    def _():
        cn_ref[...] = jnp.zeros_like(cn_ref)
        m_ref[...] = jnp.zeros_like(m_ref)

    for d, h_ref in enumerate((hf_ref, hb_ref)):
        qk_ref, num_ref, vec_ref, cl_ref = ins[4 * d:4 * d + 4]
        for b in range(n_batch):
            for h in range(M_HEADS):
                idx = (d * n_batch + b) * M_HEADS + h
                q = qk_ref[b, :, h * LANES:h * LANES + M_DK]
                row = lambda r: vec_ref[b, 0, r:r + 1, :]
                den_i, m_i, b_r = row(3 * h), row(3 * h + 1), row(3 * h + 2)
                m_loc, b_end = row(12 + 2 * h), row(13 + 2 * h)
                m_prev = m_ref[idx]
                cn = cn_ref[idx]

                inter = b_r + m_prev
                m_t = jnp.maximum(inter, m_i)
                a = jnp.exp(inter - m_t)
                e = jnp.exp(m_i - m_t)
                cq = lax.dot_general(cn.astype(BF16), q, (((1,), (1,)), ((), ())),
                                     preferred_element_type=F32)
                den = e * den_i + a * cq[M_DV:M_DV + 1, :]
                scale = 1.0 / jnp.maximum(jnp.abs(den), jnp.exp(-m_t))
                ht = (e * num_ref[b, 0, h].astype(F32) + a * cq[0:M_DV, :]) * scale
                h_ref[b, :, h * M_DV:(h + 1) * M_DV] = ht.T.astype(BF16)

                m_new = jnp.maximum(b_end + m_prev, m_loc)
                a_s = jnp.exp(b_end + m_prev - m_new)
                s_s = jnp.exp(m_loc - m_new)
                cn_ref[idx] = a_s[:, :M_DK] * cn + s_s[:, :M_DK] * cl_ref[b, 0, h]
                m_ref[idx] = m_new


def _mlstm(p, g, gate_b, ctx_chunks):
    B, L, _ = p.shape
    nc = L // M_CHUNK
    width = M_HEADS * M_DV
    bias = jnp.zeros((1, G_WIDTH), F32).at[0, :16].set(gate_b)
    assert nc % INTRA_CHUNKS == 0
    tok = lambda w, cb=0: pl.BlockSpec((1, INTRA_CHUNKS * M_CHUNK, w), lambda b, c: (b, c, cb))
    num_shape, vec_shape, cl_shape = (M_HEADS, M_DV, M_CHUNK), (VEC_ROWS, LANES), (M_HEADS, C_ROWS, M_DK)
    per_chunk = lambda s: pl.BlockSpec((1, INTRA_CHUNKS) + s, lambda b, c: (b, c) + (0,) * len(s))
    f32 = lambda *s: jax.ShapeDtypeStruct(s, F32)
    numf, numb, vecf, vecb, clf, clb = pl.pallas_call(
        _mlstm_intra_kernel,
        out_shape=(jax.ShapeDtypeStruct((B, nc, *num_shape), BF16),) * 2 + (f32(B, nc, *vec_shape),) * 2
                  + (f32(B, nc, *cl_shape),) * 2,
        grid=(B, nc // INTRA_CHUNKS),
        in_specs=[tok(width, P_QK // width), tok(width, P_MV // width), tok(G_WIDTH),
                  pl.BlockSpec((1, G_WIDTH), lambda b, c: (0, 0))],
        out_specs=(per_chunk(num_shape),) * 2 + (per_chunk(vec_shape),) * 2 + (per_chunk(cl_shape),) * 2,
        compiler_params=_params(("parallel", "parallel")),
        name="mlstm_intra",
    )(p, p, g, bias)

    fwd = lambda j: j
    bwd = lambda j: jnp.where(j < ctx_chunks, ctx_chunks - 1 - j, nc - 1 + ctx_chunks - j)
    stok = lambda cm, w, cb=0: pl.BlockSpec((B, M_CHUNK, w), lambda j: (0, cm(j), cb))
    schunk = lambda cm, s: pl.BlockSpec((B, 1) + s, lambda j: (0, cm(j)) + (0,) * len(s))
    side = lambda cm: [stok(cm, width, P_QK // width), schunk(cm, num_shape), schunk(cm, vec_shape),
                       schunk(cm, cl_shape)]
    chains = 2 * B * M_HEADS
    return pl.pallas_call(
        functools.partial(_mlstm_scan_kernel, n_batch=B),
        out_shape=(jax.ShapeDtypeStruct((B, L, width), BF16),) * 2,
        grid=(nc,),
        in_specs=side(fwd) + side(bwd),
        out_specs=(stok(fwd, width), stok(bwd, width)),
        scratch_shapes=[pltpu.VMEM((chains, C_ROWS, M_DK), F32),
                        pltpu.VMEM((chains, 1, LANES), F32)],
        compiler_params=_params(("arbitrary",)),
        name="mlstm_scan",
    )(p, numf, vecf, clf, p, numb, vecb, clb)


def _head_norm_rope(xs, ws, cos, sin, bd):
    sqs = [x * x for x in xs]
    his = [sq.astype(BF16) for sq in sqs]
    los = [(sq - hi.astype(F32)).astype(BF16) for sq, hi in zip(sqs, his)]
    mss = [jnp.dot(hi, bd, preferred_element_type=F32) + jnp.dot(lo, bd, preferred_element_type=F32)
           for hi, lo in zip(his, los)]
    ys = [x * lax.rsqrt(ms + EPS) * w for x, ms, w in zip(xs, mss, ws)]
    lane = lax.broadcasted_iota(jnp.int32, ys[0].shape, 1)
    first_half = lane % A_DH < A_DH // 2
    partners = [jnp.where(first_half, pltpu.roll(y, LANES - A_DH // 2, axis=1), pltpu.roll(y, A_DH // 2, axis=1))
                for y in ys]
    return [y * cos + partner * sin for y, partner in zip(ys, partners)]


Q_SCALE = A_DH ** -0.5 * math.log2(math.e)


def _attn_prep_tile(q, kv, cos, sin, qw, kw, qt_ref, k_ref, vt_ref):
    r = lax.broadcasted_iota(jnp.int32, (LANES, LANES), 0) // A_DH
    c = lax.broadcasted_iota(jnp.int32, (LANES, LANES), 1) // A_DH
    bd = jnp.where(r == c, 1.0 / A_DH, 0.0).astype(BF16)
    n_pairs = A_HEADS // 2
    tiles = [q[:, pair * LANES:(pair + 1) * LANES] for pair in range(n_pairs)] + [kv[:, :LANES]]
    rotated = _head_norm_rope(tiles, [qw] * n_pairs + [kw], cos, sin, bd)
    for pair in range(n_pairs):
        qt_ref[0, pair * LANES:(pair + 1) * LANES, :] = (rotated[pair] * Q_SCALE).T.astype(BF16)
    k = rotated[n_pairs].astype(BF16)
    for kvh in range(A_KV_HEADS):
        k_ref[0, kvh, 0] = k[:, kvh * A_DH:(kvh + 1) * A_DH]
    vt = kv[:, LANES:].T.astype(BF16)
    ones = jnp.ones((VT_ROWS - A_DH, vt.shape[1]), BF16)
    for kvh in range(A_KV_HEADS):
        vt_ref[0, kvh, 0] = jnp.concatenate([vt[kvh * A_DH:(kvh + 1) * A_DH, :], ones], axis=0)


ATT_SUB = 256
ATT_PIECE = 128
VT_ROWS = A_DH + 16


def _attn_kernel(qt_ref, k_ref, vt_ref, o_ref, sa_ref, sb_ref, ma_ref, mb_ref, acc_ref,
                 *, blocks, tq, ctx_tiles, q_tile0):
    q_of = lambda g: qt_ref[0, g * A_DH:(g + 1) * A_DH, :]
    lanes = lambda g: slice(g * tq, (g + 1) * tq)
    head, mid, n_mid, tail = blocks

    def step(nxt, cur, ms):
        out = []
        for g in range(A_GROUP):
            if cur is not None:
                c0, c_subs, cs_ref, cm_ref = cur
                m_new = jnp.maximum(ms[g], cm_ref[:, lanes(g)])
                alpha = jnp.exp2(ms[g] - m_new)
            best, pv = None, None
            for r in range(max(nxt[1] if nxt else 0, cur[1] if cur else 0)):
                parts = []
                for piece in range(ATT_SUB // ATT_PIECE):
                    rows = pl.ds(piece * ATT_PIECE, ATT_PIECE)
                    buf_rows = pl.ds(r * ATT_SUB + piece * ATT_PIECE, ATT_PIECE)
                    if nxt is not None and r < nxt[1]:
                        s = jnp.dot(k_ref[0, 0, nxt[0] + r, rows, :], q_of(g), preferred_element_type=F32)
                        nxt[2][g, buf_rows, :] = s
                        top = jnp.max(s, axis=0, keepdims=True)
                        best = top if best is None else jnp.maximum(best, top)
                    if cur is not None and r < c_subs:
                        parts.append(jnp.exp2(cs_ref[g, buf_rows, :] - m_new).astype(BF16))
                if cur is not None and r < c_subs:
                    d = jnp.dot(vt_ref[0, 0, c0 + r], jnp.concatenate(parts, axis=0), preferred_element_type=F32)
                    pv = d if pv is None else pv + d
            if nxt is not None:
                nxt[3][:, lanes(g)] = best
            if cur is not None:
                acc_ref[:, lanes(g)] = alpha * acc_ref[:, lanes(g)] + pv
                out.append(m_new)
            else:
                out.append(ms[g])
        return tuple(out)

    def finish():
        o = acc_ref[0:A_DH, :] / acc_ref[A_DH:A_DH + 1, :]
        o = jnp.concatenate([o[:, lanes(g)] for g in range(A_GROUP)], axis=0)
        o_ref[0] = o.T.astype(BF16)

    acc_ref[...] = jnp.zeros_like(acc_ref)
    init = (jnp.full((1, tq), -jnp.inf, F32),) * A_GROUP
    is_ctx = pl.program_id(2) + q_tile0 < ctx_tiles
    buf_a, buf_b = (sa_ref, ma_ref), (sb_ref, mb_ref)

    mid_block = lambda i, buf: (head + mid * i, mid, *buf)

    @pl.when(is_ctx)
    def _():
        step((0, head, *buf_a), None, init)
        step(None, (0, head, *buf_a), init)
        finish()

    @pl.when(jnp.logical_not(is_ctx))
    def _():
        step((0, head, *buf_a), None, init)
        ms = step(mid_block(0, buf_b), (0, head, *buf_a), init)

        def pair(j, ms):
            ms = step(mid_block(2 * j + 1, buf_a), mid_block(2 * j, buf_b), ms)
            return step(mid_block(2 * j + 2, buf_b), mid_block(2 * j + 1, buf_a), ms)

        ms = lax.fori_loop(0, n_mid // 2 - 1, pair, ms)
        last = n_mid - 1
        ms = step(mid_block(last, buf_a), mid_block(last - 1, buf_b), ms)
        tail_block = (head + mid * n_mid, tail, *buf_b)
        ms = step(tail_block, mid_block(last, buf_a), ms)
        step(None, tail_block, ms)
        finish()


def _attention(qt, k, vt, *, q_tile0, n_ctx, tq):
    B, _, L = qt.shape
    n_sub = k.shape[2]
    head, mid = n_ctx // ATT_SUB, 3
    n_mid = (n_sub - head - 1) // mid // 2 * 2
    tail = n_sub - head - mid * n_mid
    assert k.shape[3] == ATT_SUB and n_ctx == tq == ATT_SUB and n_mid >= 2 and 1 <= tail <= mid
    width = A_GROUP * A_DH
    n = A_GROUP * tq
    s_buf, m_buf = pltpu.VMEM((A_GROUP, mid * ATT_SUB, tq), F32), pltpu.VMEM((1, n), F32)
    return pl.pallas_call(
        functools.partial(_attn_kernel, blocks=(head, mid, n_mid, tail), tq=tq, ctx_tiles=n_ctx // tq,
                          q_tile0=q_tile0),
        out_shape=jax.ShapeDtypeStruct((B, L - q_tile0 * tq, A_HEADS * A_DH), BF16),
        grid=(B, A_KV_HEADS, L // tq - q_tile0),
        in_specs=[pl.BlockSpec((1, width, tq), lambda b, kv, t: (b, kv, t + q_tile0)),
                  pl.BlockSpec((1, 1, n_sub, ATT_SUB, A_DH), lambda b, kv, t: (b, kv, 0, 0, 0)),
                  pl.BlockSpec((1, 1, n_sub, VT_ROWS, ATT_SUB), lambda b, kv, t: (b, kv, 0, 0, 0))],
        out_specs=pl.BlockSpec((1, tq, width), lambda b, kv, t: (b, t, kv)),
        scratch_shapes=[s_buf, s_buf, m_buf, m_buf, pltpu.VMEM((VT_ROWS, n), F32)],
        compiler_params=_params(("parallel", "parallel", "arbitrary")),
        name="attention",
    )(qt, k, vt)


def _mixer_out_kernel(*refs, with_router, n_stream, row_off):
    hf_ref, hb_ref, mo_ref, a_ref = refs[:4]
    mod_ref, mnw_ref, n2w_ref, wout_ref = refs[4 + n_stream:8 + n_stream]
    rest = refs[8 + n_stream:]
    hs = hf_ref[0].astype(F32) + hb_ref[0].astype(F32)
    hn = jnp.concatenate([_rms_rows(hs[:, h * M_DV:(h + 1) * M_DV]) for h in range(M_HEADS)], axis=1)
    m = hn * mnw_ref[...] * _sigmoid(mo_ref[0].astype(F32))
    y_in = jnp.concatenate([m.astype(BF16), a_ref[0]], axis=1)
    mod = mod_ref[0, 0]
    x1 = _stream_tile(refs[4:4 + n_stream], row_off) + mod[2:3] * jnp.dot(y_in, wout_ref[...],
                                                                          preferred_element_type=F32)
    h2 = _rms_rows(x1) * n2w_ref[...] * (1.0 + mod[4:5]) + mod[3:4]
    if not with_router:
        wg_ref, wu_ref, wd_ref, o_ref = rest
        hb16 = h2.astype(BF16)
        g = jnp.dot(hb16, wg_ref[...], preferred_element_type=F32)
        u = jnp.dot(hb16, wu_ref[...], preferred_element_type=F32)
        act = (g * _sigmoid(g) * u).astype(BF16)
        o_ref[0] = x1 + mod[5:6] * jnp.dot(act, wd_ref[...], preferred_element_type=F32)
        return
    router_ref, x1_ref, h2_ref, ids_ref, gates_ref = rest
    x1_ref[0] = x1
    h2_ref[0] = h2
    h_hi = h2.astype(BF16)
    h_lo = (h2 - h_hi.astype(F32)).astype(BF16)
    logits = (jnp.dot(h_hi, router_ref[0], preferred_element_type=F32)
              + jnp.dot(h_lo, router_ref[0], preferred_element_type=F32)
              + jnp.dot(h_hi, router_ref[1], preferred_element_type=F32))
    lane = lax.broadcasted_iota(jnp.int32, logits.shape, 1)
    logits = jnp.where(lane < N_EXPERTS, logits, -jnp.inf)
    m1 = jnp.max(logits, axis=-1, keepdims=True)
    i1 = jnp.min(jnp.where(logits == m1, lane, LANES), axis=-1, keepdims=True)
    rest = jnp.where(lane == i1, -jnp.inf, logits)
    m2 = jnp.max(rest, axis=-1, keepdims=True)
    i2 = jnp.min(jnp.where(rest == m2, lane, LANES), axis=-1, keepdims=True)
    e2 = jnp.exp(m2 - m1)
    g1 = 1.0 / (1.0 + e2)
    ids_ref[0] = jnp.where(lane == 0, i1, jnp.where(lane == 1, i2, -1))
    gates_ref[0] = jnp.where(lane == 0, g1, jnp.where(lane == 1, e2 * g1, 0.0))


def _mixer_out(hf, hb, p, a, stream, mods, mnw, n2w, wout, *, ffn=None, router=None, row_off=0):
    B, L = p.shape[:2]
    D = D_MODEL
    nt = L // ROW_TILE - row_off
    rin = lambda w, cb=0: pl.BlockSpec((1, ROW_TILE, w), lambda b, t: (b, t + row_off, cb))
    rout = lambda w: pl.BlockSpec((1, ROW_TILE, w), lambda b, t: (b, t, 0))
    const = lambda arr: pl.BlockSpec(arr.shape, lambda b, t: (0,) * arr.ndim, pipeline_mode=pl.Buffered(1))
    mw = M_HEADS * M_DV
    a_off = row_off - (L - a.shape[1]) // ROW_TILE
    a_spec = pl.BlockSpec((1, ROW_TILE, A_HEADS * A_DH), lambda b, t: (b, t + a_off, 0))
    s_specs, s_args = _stream_specs(stream, row_off)
    in_specs = [rin(mw), rin(mw), rin(mw, P_MO // mw), a_spec, *s_specs, _mod_spec(row_off),
                pl.BlockSpec((1, mw), lambda b, t: (0, 0)), pl.BlockSpec((1, D), lambda b, t: (0, 0)),
                const(wout)]
    args = [hf, hb, p, a, *s_args, mods, mnw.reshape(1, mw), n2w.reshape(1, D), wout]
    rows = nt * ROW_TILE
    if router is None:
        in_specs += [const(w) for w in ffn]
        args += list(ffn)
        out_shape, out_specs = jax.ShapeDtypeStruct((B, rows, D), F32), rout(D)
    else:
        in_specs.append(const(router))
        args.append(router)
        out_shape = (jax.ShapeDtypeStruct((B, rows, D), F32), jax.ShapeDtypeStruct((B, rows, D), F32),
                     jax.ShapeDtypeStruct((B, rows, LANES), jnp.int32), jax.ShapeDtypeStruct((B, rows, LANES), F32))
        out_specs = (rout(D), rout(D), rout(LANES), rout(LANES))
    return pl.pallas_call(
        functools.partial(_mixer_out_kernel, with_router=router is not None, n_stream=len(s_args), row_off=row_off),
        out_shape=out_shape,
        grid=(B, nt),
        in_specs=in_specs,
        out_specs=out_specs,
        compiler_params=_params(("parallel", "arbitrary")),
        name="mixer_out",
    )(*args)


MOE_TM = 512
MOE_FF = 1792
MOE_TD = 1024
RANK_TILE = 1024
POS_TILE = 2048
ISSUE_UNROLL = 16


def _moe_rank_kernel(ids_ref, rank_ref, cnt_ref, carry_ref, before_ref):
    @pl.when(pl.program_id(0) == 0)
    def _():
        carry_ref[...] = jnp.zeros_like(carry_ref)
        r = lax.broadcasted_iota(jnp.int32, (RANK_TILE, RANK_TILE), 0)
        c = lax.broadcasted_iota(jnp.int32, (RANK_TILE, RANK_TILE), 1)
        before_ref[...] = jnp.where(c < r, 1.0, 0.0).astype(BF16)

    ids = ids_ref[...]
    lane = lax.broadcasted_iota(jnp.int32, ids.shape, 1)
    onehot = jnp.where(jnp.logical_or(lane == ids[:, 0:1], lane == ids[:, 1:2]), 1.0, 0.0)
    rank_ref[...] = jnp.dot(before_ref[...], onehot.astype(BF16), preferred_element_type=F32) + carry_ref[...]
    carry_ref[...] += jnp.sum(onehot, axis=0, keepdims=True)
    cnt_ref[...] = carry_ref[...]


def _moe_rank(ids):
    n = ids.shape[0]
    return pl.pallas_call(
        _moe_rank_kernel,
        out_shape=(jax.ShapeDtypeStruct((n, LANES), F32), jax.ShapeDtypeStruct((1, LANES), F32)),
        grid=(n // RANK_TILE,),
        in_specs=[pl.BlockSpec((RANK_TILE, LANES), lambda t: (t, 0))],
        out_specs=(pl.BlockSpec((RANK_TILE, LANES), lambda t: (t, 0)), pl.BlockSpec((1, LANES), lambda t: (0, 0))),
        scratch_shapes=[pltpu.VMEM((1, LANES), F32), pltpu.VMEM((RANK_TILE, RANK_TILE), BF16)],
        compiler_params=_params(("arbitrary",)),
        name="moe_rank",
    )(ids)


def _moe_pos_kernel(ids_ref, rank_ref, start_ref, pos_ref):
    ids = ids_ref[...]
    lane = lax.broadcasted_iota(jnp.int32, ids.shape, 1)
    tgt = start_ref[...] + rank_ref[...]
    p0 = jnp.sum(jnp.where(lane == ids[:, 0:1], tgt, 0.0), axis=-1, keepdims=True)
    p1 = jnp.sum(jnp.where(lane == ids[:, 1:2], tgt, 0.0), axis=-1, keepdims=True)
    pos_ref[...] = jnp.where(lane == 0, p0, jnp.where(lane == 1, p1, 0.0)).astype(jnp.int32)


def _moe_pos(ids, rank, start_row):
    n = ids.shape[0]
    blk = pl.BlockSpec((POS_TILE, LANES), lambda t: (t, 0))
    return pl.pallas_call(
        _moe_pos_kernel,
        out_shape=jax.ShapeDtypeStruct((n, LANES), jnp.int32),
        grid=(n // POS_TILE,),
        in_specs=[blk, blk, pl.BlockSpec((1, LANES), lambda t: (0, 0))],
        out_specs=blk,
        compiler_params=_params(("parallel",)),
        name="moe_pos",
    )(ids, rank, start_row)


def _row_copy(src, src_row, dst, dst_row, sem):
    return pltpu.make_async_copy(src.at[pl.ds(src_row, 1), :], dst.at[pl.ds(dst_row, 1), :], sem)


def _moe_dispatch_kernel(pad_ref, pos_ref, h_ref, xs_ref, zero_ref, sem):
    @pl.when(pl.program_id(0) == 0)
    def _():
        zero_ref[...] = jnp.zeros_like(zero_ref)
        fills = [pltpu.make_async_copy(
            zero_ref, xs_ref.at[pl.ds(pl.multiple_of(pad_ref[e], SUBLANES), MOE_TM + SUBLANES), :], sem)
            for e in range(N_EXPERTS)]
        for cp in fills:
            cp.start()
        for cp in fills:
            cp.wait()

        def fill_tile(j, carry):
            cp = pltpu.make_async_copy(zero_ref.at[pl.ds(0, MOE_TM), :],
                                       xs_ref.at[pl.ds(pl.multiple_of(j * MOE_TM, MOE_TM), MOE_TM), :], sem)
            cp.start()
            cp.wait()
            return carry

        lax.fori_loop(pad_ref[N_EXPERTS], xs_ref.shape[0] // MOE_TM, fill_tile, 0)

    def issue(r, carry):
        for k in range(2):
            _row_copy(h_ref, r, xs_ref, pos_ref[0, 0, 2 * r + k], sem).start(priority=k)
        return carry

    lax.fori_loop(0, MOE_TD, issue, 0, unroll=ISSUE_UNROLL)
    for k in range(2):
        pltpu.make_async_copy(h_ref, xs_ref.at[pl.ds(0, MOE_TD), :], sem).wait()


def _moe_dispatch(h, pos, fill_meta, ns):
    n, d = h.shape
    return pl.pallas_call(
        _moe_dispatch_kernel,
        out_shape=jax.ShapeDtypeStruct((ns, d), F32),
        grid_spec=pltpu.PrefetchScalarGridSpec(
            num_scalar_prefetch=1,
            grid=(n // MOE_TD,),
            in_specs=[pl.BlockSpec((1, 1, 2 * MOE_TD), lambda t, pad: (t, 0, 0), memory_space=pltpu.SMEM),
                      pl.BlockSpec((MOE_TD, d), lambda t, pad: (t, 0))],
            out_specs=pl.BlockSpec(memory_space=pl.ANY),
            scratch_shapes=[pltpu.VMEM((MOE_TM + SUBLANES, d), F32), pltpu.SemaphoreType.DMA(())]),
        compiler_params=_params(("arbitrary",)),
        name="moe_dispatch",
    )(fill_meta, pos, h)


def _moe_group_kernel(te_ref, nv_ref, xs_ref, wg_ref, wu_ref, wd_ref, ys_ref, acc_ref):
    i, f = pl.program_id(0), pl.program_id(1)
    last = pl.num_programs(1) - 1
    valid = i < nv_ref[0]

    @pl.when(valid)
    def _():
        @pl.when(f == 0)
        def _():
            acc_ref[...] = jnp.zeros_like(acc_ref)

        x = xs_ref[...].astype(BF16)
        g = jnp.dot(x, wg_ref[0], preferred_element_type=F32)
        u = jnp.dot(x, wu_ref[0], preferred_element_type=F32)
        act = (g * _sigmoid(g) * u).astype(BF16)
        acc_ref[...] += jnp.dot(act, wd_ref[0], preferred_element_type=F32)

        @pl.when(f == last)
        def _():
            ys_ref[...] = acc_ref[...]

    @pl.when(jnp.logical_and(jnp.logical_not(valid), f == last))
    def _():
        ys_ref[...] = jnp.zeros_like(ys_ref)


def _moe_group(xs, tile_expert, n_valid, wg, wu, wd):
    ns, d = xs.shape
    n_tiles = ns // MOE_TM - 1
    ff = wg.shape[2]
    live = lambda i, nv: i < nv[0]
    nf = ff // MOE_FF
    step = lambda i, f, nv: jnp.where(live(i, nv), f, nf - 1)
    return pl.pallas_call(
        _moe_group_kernel,
        out_shape=jax.ShapeDtypeStruct((n_tiles * MOE_TM, d), F32),
        grid_spec=pltpu.PrefetchScalarGridSpec(
            num_scalar_prefetch=2,
            grid=(n_tiles, nf),
            in_specs=[pl.BlockSpec((MOE_TM, d), lambda i, f, te, nv: (jnp.where(live(i, nv), i, 0), 0)),
                      pl.BlockSpec((1, d, MOE_FF), lambda i, f, te, nv: (te[i], 0, step(i, f, nv))),
                      pl.BlockSpec((1, d, MOE_FF), lambda i, f, te, nv: (te[i], 0, step(i, f, nv))),
                      pl.BlockSpec((1, MOE_FF, d), lambda i, f, te, nv: (te[i], step(i, f, nv), 0))],
            out_specs=pl.BlockSpec((MOE_TM, d), lambda i, f, te, nv: (i, 0)),
            scratch_shapes=[pltpu.VMEM((MOE_TM, d), F32)]),
        compiler_params=_params(("arbitrary", "arbitrary")),
        name="moe_group",
    )(tile_expert, n_valid, xs, wg, wu, wd)


def _moe_combine_kernel(pos_ref, posn_ref, x_ref, gates_ref, mod_ref, fw_ref, ys_ref, o_ref, ybuf, sems):
    t, nt = pl.program_id(0), pl.num_programs(0)

    def gather(p_ref, slot):
        def issue(r, carry):
            for k in range(2):
                _row_copy(ys_ref, p_ref[0, 0, 2 * r + k], ybuf.at[slot, k], r, sems.at[slot]).start(priority=k)
            return carry

        lax.fori_loop(0, MOE_TD, issue, 0, unroll=ISSUE_UNROLL)

    def combine(slot):
        for k in range(2):
            pltpu.make_async_copy(ys_ref.at[pl.ds(0, MOE_TD), :], ybuf.at[slot, k], sems.at[slot]).wait()
        gates = gates_ref[...]
        y = gates[:, 0:1] * ybuf[slot, 0] + gates[:, 1:2] * ybuf[slot, 1]
        x2 = x_ref[...] + mod_ref[0, 0][5:6] * y
        o_ref[...] = _rms_rows(x2) * fw_ref[...]

    @pl.when(t == 0)
    def _():
        gather(pos_ref, 0)

    for slot in range(2):
        @pl.when(t % 2 == slot)
        def _(slot=slot):
            @pl.when(t + 1 < nt)
            def _():
                gather(posn_ref, 1 - slot)

            combine(slot)


def _moe_combine(pos, x1, gates, mods, fw, ys, tokens_per_sample):
    n, d = x1.shape
    per = tokens_per_sample // MOE_TD
    nt = n // MOE_TD
    return pl.pallas_call(
        _moe_combine_kernel,
        out_shape=jax.ShapeDtypeStruct((n, d), F32),
        grid=(nt,),
        in_specs=[pl.BlockSpec((1, 1, 2 * MOE_TD), lambda t: (t, 0, 0), memory_space=pltpu.SMEM),
                  pl.BlockSpec((1, 1, 2 * MOE_TD), lambda t: (jnp.minimum(t + 1, nt - 1), 0, 0),
                               memory_space=pltpu.SMEM),
                  pl.BlockSpec((MOE_TD, d), lambda t: (t, 0)),
                  pl.BlockSpec((MOE_TD, LANES), lambda t: (t, 0)),
                  pl.BlockSpec((1, 1, 6, d), lambda t: (t // per, 1, 0, 0)),
                  pl.BlockSpec((1, d), lambda t: (0, 0)),
                  pl.BlockSpec(memory_space=pl.ANY)],
        out_specs=pl.BlockSpec((MOE_TD, d), lambda t: (t, 0)),
        scratch_shapes=[pltpu.VMEM((2, 2, MOE_TD, d), F32), pltpu.SemaphoreType.DMA((2,))],
        compiler_params=_params(("arbitrary",)),
        name="moe_combine",
    )(pos, pos, x1, gates, mods, fw.reshape(1, d), ys)


def _moe(h2, ids, gates, x1, mods, wg, wu, wd, fw):
    B, T, D = x1.shape
    n = B * T
    ids, gates = ids.reshape(n, LANES), gates.reshape(n, LANES)
    rank, cnt = _moe_rank(ids)
    cnt = cnt[0, :N_EXPERTS].astype(jnp.int32)
    padded = (cnt + MOE_TM - 1) // MOE_TM * MOE_TM
    end = jnp.cumsum(padded)
    start = end - padded
    n_tiles = 2 * n // MOE_TM + N_EXPERTS
    tile_expert = jnp.minimum(jnp.sum(jnp.arange(n_tiles)[:, None] >= (end // MOE_TM)[None, :], axis=1),
                              N_EXPERTS - 1).astype(jnp.int32)
    n_valid = (end[-1:] // MOE_TM).astype(jnp.int32)
    start_row = jnp.zeros((1, LANES), F32).at[0, :N_EXPERTS].set(start.astype(F32))
    pos = _moe_pos(ids, rank, start_row)
    pos = pos[:, :2].reshape(n // MOE_TD, 1, 2 * MOE_TD)
    fill_meta = jnp.concatenate([(start + cnt) // SUBLANES * SUBLANES, n_valid]).astype(jnp.int32)
    xs = _moe_dispatch(h2.reshape(n, D), pos, fill_meta, (n_tiles + 1) * MOE_TM)
    ys = _moe_group(xs, tile_expert, n_valid, wg, wu, wd)
    return _moe_combine(pos, x1.reshape(n, D), gates, mods, fw, ys, T).reshape(B, T, D)


_ROT_PERM = np.concatenate([np.arange(0, A_DH, 2), np.arange(1, A_DH, 2)])


def _prep_w_in(w):
    o = np.cumsum([0, M_HEADS * M_DK, M_HEADS * M_DK, M_HEADS * M_DV, M_HEADS * M_DV, 4 * M_HEADS,
                   A_HEADS * A_DH, A_KV_HEADS * A_DH, A_KV_HEADS * A_DH])
    mq, mk, mv, mo, mg, aq, ak, av = [w[:, o[i]:o[i + 1]] for i in range(8)]
    qk = jnp.concatenate([jnp.concatenate([mq[:, h * M_DK:(h + 1) * M_DK] * (M_DK ** -0.5),
                                           mk[:, h * M_DK:(h + 1) * M_DK]], axis=1) for h in range(M_HEADS)], axis=1)
    perm_q = np.concatenate([h * A_DH + _ROT_PERM for h in range(A_HEADS)])
    perm_k = np.concatenate([h * A_DH + _ROT_PERM for h in range(A_KV_HEADS)])
    pad = jnp.zeros((w.shape[0], G_WIDTH - 4 * M_HEADS), w.dtype)
    return jnp.concatenate([qk, mv, mo, aq[:, perm_q], ak[:, perm_k], av, mg, pad], axis=1).astype(BF16)


def _rope_tables(n_tok, n_ctx):
    rows = n_tok // GRID_W
    row = jnp.broadcast_to(jnp.arange(rows, dtype=F32)[:, None], (rows, GRID_W)).reshape(n_tok)
    col = jnp.broadcast_to(jnp.arange(GRID_W, dtype=F32)[None, :], (rows, GRID_W)).reshape(n_tok)
    n_freq = A_DH // 4
    inv_freq = ROPE_THETA ** (-jnp.arange(n_freq, dtype=F32) / n_freq)
    ang = jnp.concatenate([row[:, None] * inv_freq, col[:, None] * inv_freq], axis=-1)
    cos, sin = jnp.cos(ang), jnp.sin(ang)
    cos = jnp.concatenate([jnp.ones((n_ctx, A_DH // 2), F32), cos], axis=0)
    sin = jnp.concatenate([jnp.zeros((n_ctx, A_DH // 2), F32), sin], axis=0)
    return jnp.tile(cos, (1, 4)), jnp.tile(jnp.concatenate([-sin, sin], axis=1), (1, 2))


def kernel(x, c, ctx, c_ctx, ada_w, ada_b, norm1_w, norm2_w, w_in, mlstm_gate_b, mlstm_norm_w, q_norm_w, k_norm_w,
           w_out, ffn_w_gate, ffn_w_up, ffn_w_down, moe_router, moe_w_gate, moe_w_up, moe_w_down, final_norm_w):
    B, T, D = x.shape
    n_ctx = ctx.shape[1]
    L = n_ctx + T
    depth = w_in.shape[0]
    assert D == D_MODEL and n_ctx == ROW_TILE and T % RANK_TILE == 0 and depth == 2
    ctx_tiles = n_ctx // ROW_TILE
    tq = ROW_TILE

    xa = (ctx, x)
    cvec =jnp.concatenate([c, c_ctx[None], jnp.zeros((8 - B - 1, D), F32)], axis=0)
    cos, sin = _rope_tables(T, n_ctx)
    out = None
    for i in range(depth):
        last = i == depth - 1
        modraw = _ada(cvec, ada_w[i], ada_b[i])
        mods = jnp.stack([jnp.broadcast_to(modraw[B].reshape(1, 6, D), (B, 6, D)),
                          modraw[:B].reshape(B, 6, D)], axis=1)
        qw = jnp.tile(q_norm_w[i][_ROT_PERM], 2).reshape(1, LANES)
        kw = jnp.tile(k_norm_w[i][_ROT_PERM], 2).reshape(1, LANES)
        p, g, qt, k, vt = _in_proj(xa, mods, norm1_w[i], _prep_w_in(w_in[i]), cos, sin, qw, kw)
        hf, hb = _mlstm(p, g, mlstm_gate_b[i], n_ctx // M_CHUNK)
        a = _attention(qt, k, vt, q_tile0=ctx_tiles if last else 0, n_ctx=n_ctx, tq=tq)
        wout = w_out[i].astype(BF16)
        if not last:
            j = i // 2
            ffn = (ffn_w_gate[j].astype(BF16), ffn_w_up[j].astype(BF16), ffn_w_down[j].astype(BF16))
            xa = _mixer_out(hf, hb, p, a, xa, mods, mlstm_norm_w[i], norm2_w[i], wout, ffn=ffn)
        else:
            j = i // 2
            router = jnp.zeros((D, LANES), F32).at[:, :N_EXPERTS].set(moe_router[j])
            router_hi = router.astype(BF16)
            router = jnp.stack([router_hi, (router - router_hi.astype(F32)).astype(BF16)])
            x1, h2, ids, gates = _mixer_out(hf, hb, p, a, xa, mods, mlstm_norm_w[i], norm2_w[i], wout,
                                            router=router, row_off=ctx_tiles)
            out = _moe(h2, ids, gates, x1, mods, moe_w_gate[j].astype(BF16), moe_w_up[j].astype(BF16),
                       moe_w_down[j].astype(BF16), final_norm_w)
    return out
```
